```python
import math
import jax
import jax.numpy as jnp
from jax import lax
import numpy as np

D_MODEL = 1024
BATCH = 2
SEQ = 8192
DEPTH = 2

CTX_LEN = 256
GRID_W = 64
D_MIX = D_MODEL
N_DIR = 2
RWKV_WIDTH = D_MIX // 2
RWKV_HEAD = 64
RWKV_HEADS = RWKV_WIDTH // RWKV_HEAD
DECAY_RANK = 64
ICL_RANK = 64
GATE_RANK = 128
RWKV_SLAB = 3 * RWKV_WIDTH + N_DIR * (DECAY_RANK + ICL_RANK + GATE_RANK)
S5_WIDTH = D_MIX - RWKV_WIDTH
S5_GROUP = 16
S5_GROUPS = S5_WIDTH // S5_GROUP
S5_STATE = 64
D_IN = RWKV_SLAB + S5_WIDTH
D_FF = 2816
N_EXPERTS = 8
TOP_K = 2
D_FF_EXPERT = 3584
N_DENSE = (DEPTH + 1) // 2
N_MOE = DEPTH // 2
NORM_EPS = 1e-6
GN_EPS = 64e-5
L2_EPS = 1e-12
LAM_RE_MAX = -1e-4

kernel_name = "hybrid_rwkv7_s5_moe_diffusion_block"


def rmsnorm(x, g):
    xf = x.astype(jnp.float32)
    y = xf * lax.rsqrt(jnp.mean(xf * xf, axis=-1, keepdims=True) + NORM_EPS)
    return (y * g.astype(jnp.float32)).astype(x.dtype)


def modulate(h, shift, scale):
    return h * (1 + scale) + shift


def qshift_grid(x, rows):
    b, t, ch = x.shape
    g = x.reshape(b, rows, GRID_W, ch // 4, 4)
    left = jnp.pad(g[:, :, :-1, :, 0], ((0, 0), (0, 0), (1, 0), (0, 0)))
    right = jnp.pad(g[:, :, 1:, :, 1], ((0, 0), (0, 0), (0, 1), (0, 0)))
    up = jnp.pad(g[:, :-1, :, :, 2], ((0, 0), (1, 0), (0, 0), (0, 0)))
    down = jnp.pad(g[:, 1:, :, :, 3], ((0, 0), (0, 1), (0, 0), (0, 0)))
    return jnp.stack([left, right, up, down], axis=-1).reshape(b, t, ch)


def shift_seq(x):
    b, t, ch = x.shape
    g = x.reshape(b, t, ch // 2, 2)
    prev = jnp.pad(g[:, :-1, :, 0], ((0, 0), (1, 0), (0, 0)))
    nxt = jnp.pad(g[:, 1:, :, 1], ((0, 0), (0, 1), (0, 0)))
    return jnp.stack([prev, nxt], axis=-1).reshape(b, t, ch)


def heads(t):
    return t.reshape(t.shape[:-1] + (RWKV_HEADS, RWKV_HEAD))


def split_slab(slab):
    sizes = [RWKV_WIDTH] * 3 + [N_DIR * DECAY_RANK, N_DIR * ICL_RANK, N_DIR * GATE_RANK]
    return jnp.split(slab, np.cumsum(sizes)[:-1].tolist(), axis=-1)


def wkv_scan(r, w, k, v, kk, a, state0, reverse):
    def step(s, inp):
        r_t, w_t, k_t, v_t, kk_t, a_t = inp
        sa = jnp.einsum('bhvk,bhk->bhv', s, -kk_t)
        s = (s * w_t[:, :, None, :] + sa[..., None] * (kk_t * a_t)[:, :, None, :]
             + v_t[..., None] * k_t[:, :, None, :])
        return s, jnp.einsum('bhvk,bhk->bhv', s, r_t)
    xs = tuple(jnp.moveaxis(t, 1, 0) for t in (r, w, k, v, kk, a))
    s_final, y = lax.scan(step, state0, xs, reverse=reverse)
    return s_final, jnp.moveaxis(y, 0, 1)


def rwkv_stream(slab, w0, w_up, a0, a_up, k_k, k_a):
    r, k, v, wd, ad, gd = split_slab(slab.astype(jnp.float32))
    kk = heads(k * k_k)
    kk = kk / jnp.maximum(jnp.sqrt(jnp.sum(kk * kk, axis=-1, keepdims=True)), L2_EPS)
    per_dir = []
    for d in range(N_DIR):
        wd_d = wd[..., d * DECAY_RANK:(d + 1) * DECAY_RANK]
        ad_d = ad[..., d * ICL_RANK:(d + 1) * ICL_RANK]
        gd_d = gd[..., d * GATE_RANK:(d + 1) * GATE_RANK]
        w_log = -jax.nn.softplus(-(w0[d] + jnp.tanh(wd_d) @ w_up[d])) - 0.5
        w = jnp.exp(-jnp.exp(w_log))
        a = jax.nn.sigmoid(a0[d] + ad_d @ a_up[d])
        kt = k * (1 + (a - 1) * k_a)
        per_dir.append((heads(w), heads(a), heads(kt), gd_d))
    return heads(r), heads(v), kk, per_dir


def rwkv_readout(y, r, kt, v, gd, g_up, ln_w, ln_b, r_k):
    mean = jnp.mean(y, axis=-1, keepdims=True)
    var = jnp.mean(jnp.square(y - mean), axis=-1, keepdims=True)
    yn = (y - mean) * lax.rsqrt(var + GN_EPS) * ln_w + ln_b
    bonus = jnp.sum(r * kt * r_k, axis=-1, keepdims=True) * v
    g = heads(jax.nn.sigmoid(gd) @ g_up)
    return (yn + bonus) * g


def rwkv_group(slab_c, slab_l, w0, w_up, a0, a_up, g_up, k_k, k_a, r_k, ln_w, ln_b, need_ctx):
    r_c, v_c, kk_c, dir_c = rwkv_stream(slab_c, w0, w_up, a0, a_up, k_k, k_a)
    r_l, v_l, kk_l, dir_l = rwkv_stream(slab_l, w0, w_up, a0, a_up, k_k, k_a)
    state0 = jnp.zeros((slab_l.shape[0], RWKV_HEADS, RWKV_HEAD, RWKV_HEAD), jnp.float32)
    outs_c, outs_l = [], []
    for d in range(N_DIR):
        rev = d == 1
        w_c, a_c, kt_c, gd_c = dir_c[d]
        w_l, a_l, kt_l, gd_l = dir_l[d]
        s_c, y_c = wkv_scan(r_c, w_c, kt_c, v_c, kk_c, a_c, state0, rev)
        _, y_l = wkv_scan(r_l, w_l, kt_l, v_l, kk_l, a_l, s_c, rev)
        outs_l.append(rwkv_readout(y_l, r_l, kt_l, v_l, gd_l, g_up[d], ln_w, ln_b, r_k))
        if need_ctx:
            outs_c.append(rwkv_readout(y_c, r_c, kt_c, v_c, gd_c, g_up[d], ln_w, ln_b, r_k))
    out_l = outs_l[0] + outs_l[1]
    out_l = out_l.reshape(out_l.shape[:2] + (RWKV_WIDTH,))
    if not need_ctx:
        return None, out_l
    out_c = outs_c[0] + outs_c[1]
    return out_c.reshape(out_c.shape[:2] + (RWKV_WIDTH,)), out_l


def s5_discretize(lam_re, lam_im, log_dt, b_re, b_im):
    lr = jnp.minimum(lam_re.astype(jnp.float32), LAM_RE_MAX)
    li = lam_im.astype(jnp.float32)
    dt = jnp.exp(log_dt.astype(jnp.float32))[:, None]
    mag = jnp.exp(lr * dt)
    ar = mag * jnp.cos(li * dt)
    ai = mag * jnp.sin(li * dt)
    den = lr * lr + li * li
    xr = ar - 1.0
    cr = (xr * lr + ai * li) / den
    ci = (ai * lr - xr * li) / den
    br = cr[..., None] * b_re - ci[..., None] * b_im
    bi = cr[..., None] * b_im + ci[..., None] * b_re
    return ar, ai, br, bi


def complex_affine_combine(e1, e2):
    a1r, a1i, b1r, b1i = e1
    a2r, a2i, b2r, b2i = e2
    return (a1r * a2r - a1i * a2i, a1r * a2i + a1i * a2r,
            a2r * b1r - a2i * b1i + b2r, a2r * b1i + a2i * b1r + b2i)


def s5_scan(u, ar, ai, br, bi, s0r, s0i, reverse):
    bur = jnp.einsum('btgc,gpc->btgp', u, br)
    bui = jnp.einsum('btgc,gpc->btgp', u, bi)
    first = -1 if reverse else 0
    bur = bur.at[:, first].add(ar * s0r - ai * s0i)
    bui = bui.at[:, first].add(ar * s0i + ai * s0r)
    a_r = jnp.broadcast_to(ar, bur.shape)
    a_i = jnp.broadcast_to(ai, bur.shape)
    _, _, xr, xi = lax.associative_scan(complex_affine_combine, (a_r, a_i, bur, bui),
                                        reverse=reverse, axis=1)
    return xr, xi


def s5_readout(xr, xi, c_re, c_im):
    return jnp.einsum('btgp,gcp->btgc', xr, c_re) - jnp.einsum('btgp,gcp->btgc', xi, c_im)


def s5_group(u_c, u_l, lam_re, lam_im, log_dt, b_re, b_im, c_re, c_im, d_skip, glu_w, glu_b, need_ctx):
    def groups(u):
        u = u.astype(jnp.float32)
        return u.reshape(u.shape[:2] + (S5_GROUPS, S5_GROUP))
    uc, ul = groups(u_c), groups(u_l)
    zero = jnp.zeros((uc.shape[0], S5_GROUPS, S5_STATE), jnp.float32)
    ys_c, ys_l = [], []
    for d in range(N_DIR):
        rev = d == 1
        ar, ai, br, bi = s5_discretize(lam_re[d], lam_im[d], log_dt[d], b_re[d], b_im[d])
        xr_c, xi_c = s5_scan(uc, ar, ai, br, bi, zero, zero, rev)
        last = 0 if rev else -1
        xr_l, xi_l = s5_scan(ul, ar, ai, br, bi, xr_c[:, last], xi_c[:, last], rev)
        ys_l.append(s5_readout(xr_l, xi_l, c_re[d], c_im[d]))
        if need_ctx:
            ys_c.append(s5_readout(xr_c, xi_c, c_re[d], c_im[d]))

    def finish(ys, u):
        y = (ys[0] + ys[1]).reshape(u.shape[:2] + (S5_WIDTH,)) + d_skip * u.astype(jnp.float32)
        y = jax.nn.gelu(y)
        return y * jax.nn.sigmoid(y @ glu_w + glu_b)

    return (finish(ys_c, u_c) if need_ctx else None), finish(ys_l, u_l)


def swiglu(h, w_gate, w_up, w_down):
    return (jax.nn.silu(h @ w_gate) * (h @ w_up)) @ w_down


def moe_swiglu(h, router, w_gate, w_up, w_down):
    logits = (h @ router).astype(jnp.float32)
    top_val, top_idx = lax.top_k(logits, TOP_K)
    top_p = jax.nn.softmax(top_val, axis=-1)
    combine = jnp.einsum('btk,btke->bte', top_p,
                         jax.nn.one_hot(top_idx, N_EXPERTS, dtype=jnp.float32))
    out = jnp.zeros(h.shape, jnp.float32)
    for e in range(N_EXPERTS):
        out = out + combine[..., e:e + 1] * swiglu(h, w_gate[e], w_up[e], w_down[e]).astype(jnp.float32)
    return out.astype(h.dtype)


def setup_inputs(seed: int = 0) -> dict:
    key = jax.random.key(seed)
    keys = iter(jax.random.split(key, 64))
    f32 = jnp.float32

    def nrm(shape, scale):
        return jax.random.normal(next(keys), shape, f32) * scale

    def uni(shape, lo, hi):
        return jax.random.uniform(next(keys), shape, f32, lo, hi)

    lam_im0 = math.pi * jnp.arange(S5_STATE, dtype=f32)
    return {
        "x": nrm((BATCH, SEQ, D_MODEL), 1.0),
        "c": nrm((BATCH, D_MODEL), 1.0),
        "ctx": nrm((BATCH, CTX_LEN, D_MODEL), 1.0),
        "c_ctx": nrm((D_MODEL,), 1.0),
        "ada_w": nrm((DEPTH, D_MODEL, 6 * D_MODEL), 0.5 * D_MODEL ** -0.5),
        "ada_b": nrm((DEPTH, 6 * D_MODEL), 0.02),
        "norm1_g": 1.0 + nrm((DEPTH, D_MODEL), 0.1),
        "norm2_g": 1.0 + nrm((DEPTH, D_MODEL), 0.1),
        "w_in": nrm((DEPTH, D_MODEL, D_IN), D_MODEL ** -0.5),
        "w_out": nrm((DEPTH, D_MIX, D_MODEL), D_MIX ** -0.5),
        "shift_mu": uni((DEPTH, RWKV_SLAB), 0.0, 1.0),
        "rwkv_w0": uni((DEPTH, N_DIR, RWKV_WIDTH), -6.5, -1.5),
        "rwkv_w_up": nrm((DEPTH, N_DIR, DECAY_RANK, RWKV_WIDTH), 0.1 * DECAY_RANK ** -0.5),
        "rwkv_a0": nrm((DEPTH, N_DIR, RWKV_WIDTH), 0.1),
        "rwkv_a_up": nrm((DEPTH, N_DIR, ICL_RANK, RWKV_WIDTH), ICL_RANK ** -0.5),
        "rwkv_g_up": nrm((DEPTH, N_DIR, GATE_RANK, RWKV_WIDTH), GATE_RANK ** -0.5),
        "rwkv_k_k": 0.85 + nrm((DEPTH, RWKV_WIDTH), 0.05),
        "rwkv_k_a": 1.0 + nrm((DEPTH, RWKV_WIDTH), 0.05),
        "rwkv_r_k": nrm((DEPTH, RWKV_HEADS, RWKV_HEAD), 0.1),
        "rwkv_ln_w": 1.0 + nrm((DEPTH, RWKV_HEADS, RWKV_HEAD), 0.1),
        "rwkv_ln_b": nrm((DEPTH, RWKV_HEADS, RWKV_HEAD), 0.01),
        "s5_lam_re": -0.5 + nrm((DEPTH, N_DIR, S5_GROUPS, S5_STATE), 0.01),
        "s5_lam_im": lam_im0 + nrm((DEPTH, N_DIR, S5_GROUPS, S5_STATE), 0.01),
        "s5_log_dt": uni((DEPTH, N_DIR, S5_GROUPS), math.log(1e-3), math.log(1e-1)),
        "s5_b_re": nrm((DEPTH, N_DIR, S5_GROUPS, S5_STATE, S5_GROUP), (2 * S5_GROUP) ** -0.5),
        "s5_b_im": nrm((DEPTH, N_DIR, S5_GROUPS, S5_STATE, S5_GROUP), (2 * S5_GROUP) ** -0.5),
        "s5_c_re": nrm((DEPTH, N_DIR, S5_GROUPS, S5_GROUP, S5_STATE), S5_STATE ** -0.5),
        "s5_c_im": nrm((DEPTH, N_DIR, S5_GROUPS, S5_GROUP, S5_STATE), S5_STATE ** -0.5),
        "s5_d": nrm((DEPTH, S5_WIDTH), 1.0),
        "s5_glu_w": nrm((DEPTH, S5_WIDTH, S5_WIDTH), S5_WIDTH ** -0.5),
        "s5_glu_b": nrm((DEPTH, S5_WIDTH), 0.01),
        "ffn_w_gate": nrm((N_DENSE, D_MODEL, D_FF), D_MODEL ** -0.5),
        "ffn_w_up": nrm((N_DENSE, D_MODEL, D_FF), D_MODEL ** -0.5),
        "ffn_w_down": nrm((N_DENSE, D_FF, D_MODEL), D_FF ** -0.5),
        "moe_router": nrm((N_MOE, D_MODEL, N_EXPERTS), D_MODEL ** -0.5),
        "moe_w_gate": nrm((N_MOE, N_EXPERTS, D_MODEL, D_FF_EXPERT), D_MODEL ** -0.5),
        "moe_w_up": nrm((N_MOE, N_EXPERTS, D_MODEL, D_FF_EXPERT), D_MODEL ** -0.5),
        "moe_w_down": nrm((N_MOE, N_EXPERTS, D_FF_EXPERT, D_MODEL), D_FF_EXPERT ** -0.5),
        "final_g": 1.0 + nrm((D_MODEL,), 0.1),
    }


def reference(x, c, ctx, c_ctx, ada_w, ada_b, norm1_g, norm2_g, w_in, w_out, shift_mu,
              rwkv_w0, rwkv_w_up, rwkv_a0, rwkv_a_up, rwkv_g_up, rwkv_k_k, rwkv_k_a, rwkv_r_k,
              rwkv_ln_w, rwkv_ln_b, s5_lam_re, s5_lam_im, s5_log_dt, s5_b_re, s5_b_im,
              s5_c_re, s5_c_im, s5_d, s5_glu_w, s5_glu_b, ffn_w_gate, ffn_w_up, ffn_w_down,
              moe_router, moe_w_gate, moe_w_up, moe_w_down, final_g):
    rows = x.shape[1] // GRID_W
    x_l, x_c = x, ctx
    act_l = jax.nn.silu(c)
    act_c = jax.nn.silu(c_ctx)
    for i in range(DEPTH):
        need_ctx = i < DEPTH - 1
        mod_l = jnp.split((act_l @ ada_w[i] + ada_b[i])[:, None, :], 6, axis=-1)
        mod_c = jnp.split(act_c @ ada_w[i] + ada_b[i], 6, axis=-1)

        h_l = modulate(rmsnorm(x_l, norm1_g[i]), mod_l[0], mod_l[1])
        h_c = modulate(rmsnorm(x_c, norm1_g[i]), mod_c[0], mod_c[1])
        p_l = h_l @ w_in[i]
        p_c = h_c @ w_in[i]
        slab_l, u_l = p_l[..., :RWKV_SLAB], p_l[..., RWKV_SLAB:]
        slab_c, u_c = p_c[..., :RWKV_SLAB], p_c[..., RWKV_SLAB:]
        slab_l = slab_l + (qshift_grid(slab_l, rows) - slab_l) * shift_mu[i]
        slab_c = slab_c + (shift_seq(slab_c) - slab_c) * shift_mu[i]
        rw_c, rw_l = rwkv_group(slab_c, slab_l, rwkv_w0[i], rwkv_w_up[i], rwkv_a0[i], rwkv_a_up[i],
                                rwkv_g_up[i], rwkv_k_k[i], rwkv_k_a[i], rwkv_r_k[i],
                                rwkv_ln_w[i], rwkv_ln_b[i], need_ctx)
        ss_c, ss_l = s5_group(u_c, u_l, s5_lam_re[i], s5_lam_im[i], s5_log_dt[i], s5_b_re[i],
                              s5_b_im[i], s5_c_re[i], s5_c_im[i], s5_d[i], s5_glu_w[i],
                              s5_glu_b[i], need_ctx)
        mix_l = jnp.concatenate([rw_l, ss_l], axis=-1).astype(x_l.dtype) @ w_out[i]
        x_l = x_l + mod_l[2] * mix_l
        if need_ctx:
            mix_c = jnp.concatenate([rw_c, ss_c], axis=-1).astype(x_c.dtype) @ w_out[i]
            x_c = x_c + mod_c[2] * mix_c

        j = i // 2

        def channel_mix(h):
            if i % 2 == 0:
                return swiglu(h, ffn_w_gate[j], ffn_w_up[j], ffn_w_down[j])
            return moe_swiglu(h, moe_router[j], moe_w_gate[j], moe_w_up[j], moe_w_down[j])

        h2_l = modulate(rmsnorm(x_l, norm2_g[i]), mod_l[3], mod_l[4])
        x_l = x_l + mod_l[5] * channel_mix(h2_l)
        if need_ctx:
            h2_c = modulate(rmsnorm(x_c, norm2_g[i]), mod_c[3], mod_c[4])
            x_c = x_c + mod_c[5] * channel_mix(h2_c)
    return rmsnorm(x_l, final_g)
```

```python
import functools
import math

import jax
import jax.numpy as jnp
from jax import lax
from jax.experimental import pallas as pl
from jax.experimental.pallas import tpu as pltpu

F32 = jnp.float32
BF16 = jnp.bfloat16
HIGHEST = lax.Precision.HIGHEST

GRID_W = 64
HEAD = 64
DECAY_RANK = 64
ICL_RANK = 64
GATE_RANK = 128
S5_GROUP = 16
S5_STATE = 64
NORM_EPS = 1e-6
GN_EPS = 64e-5
L2_EPS = 1e-12
LAM_RE_MAX = -1e-4
TOP_K = 2

TM = 256
WKV_CHUNK = 64
WKV_HEADS = 4
S5_CHUNK = 16
VMEM_LIMIT = 56 * 1024 * 1024


def _cparams(*sem):
    return pltpu.CompilerParams(dimension_semantics=sem, vmem_limit_bytes=VMEM_LIMIT)


def _dot(a, b):
    return jnp.dot(a, b, preferred_element_type=F32)


def _dot32(a, b):
    return jnp.dot(a, b, precision=HIGHEST, preferred_element_type=F32)


def _dot_nt(a, b):
    return lax.dot_general(a, b, (((1,), (1,)), ((), ())), preferred_element_type=F32)


def _dot_tn(a, b):
    return lax.dot_general(a, b, (((0,), (0,)), ((), ())), preferred_element_type=F32)


def _ada_kernel(act_ref, w_ref, b_ref, o_ref):
    a = act_ref[...]
    a = a * jax.nn.sigmoid(a)
    o_ref[...] = _dot32(a, w_ref[...]) + b_ref[...]


def _ada_mod(act, ada_w, ada_b):
    depth, d, n = ada_w.shape
    tn = 1536
    return pl.pallas_call(
        _ada_kernel,
        grid=(depth, n // tn),
        in_specs=[
            pl.BlockSpec((8, d), lambda i, j: (0, 0)),
            pl.BlockSpec((None, d, tn), lambda i, j: (i, 0, j)),
            pl.BlockSpec((None, 1, tn), lambda i, j: (i, 0, j)),
        ],
        out_specs=pl.BlockSpec((None, 8, tn), lambda i, j: (i, 0, j)),
        out_shape=jax.ShapeDtypeStruct((depth, 8, n), F32),
        compiler_params=_cparams("arbitrary", "arbitrary"),
        name="ada_mod",
    )(act, ada_w, ada_b.reshape(depth, 1, n))


def _norm_mod(x, g, shift, scale):
    ms = jnp.mean(x * x, axis=-1, keepdims=True)
    y = x * lax.rsqrt(ms + NORM_EPS) * g
    return y * (1.0 + scale) + shift


def _inproj_kernel(x_ref, g_ref, sh_ref, sc_ref, w_ref, o_ref):
    h = _norm_mod(x_ref[...], g_ref[...], sh_ref[...], sc_ref[...])
    o_ref[...] = _dot(h.astype(BF16), w_ref[...])


def _inproj(xcat, g, shift, scale, w_bf, nct):
    b, tt, d = xcat.shape
    n = w_bf.shape[1]
    kind = lambda bi, i: (bi, jnp.where(i < nct, 0, 1), 0, 0)
    return pl.pallas_call(
        _inproj_kernel,
        grid=(b, tt // TM),
        in_specs=[
            pl.BlockSpec((None, TM, d), lambda bi, i: (bi, i, 0)),
            pl.BlockSpec((1, d), lambda bi, i: (0, 0)),
            pl.BlockSpec((None, None, 1, d), kind),
            pl.BlockSpec((None, None, 1, d), kind),
            pl.BlockSpec((d, n), lambda bi, i: (0, 0)),
        ],
        out_specs=pl.BlockSpec((None, TM, n), lambda bi, i: (bi, i, 0)),
        out_shape=jax.ShapeDtypeStruct((b, tt, n), F32),
        compiler_params=_cparams("parallel", "parallel"),
        name="inproj",
    )(xcat, g, shift, scale, w_bf)


def _prep_kernel(p_ref, up_ref, dn_ref, rm_ref, lc_ref, kk_ref, ka_ref, rk_ref, w0_ref, a0_ref,
                 wup_ref, aup_ref, gup_ref, bd_ref,
                 r_o, v_o, kkn_o, lw_o, kt_o, be_o, g_o, bo_o):
    x = p_ref[...]
    rm = rm_ref[...]
    lc = lc_ref[...]
    prev = pltpu.roll(x, 1, 0)
    nxt = pltpu.roll(x, TM - 1, 0)
    up = jnp.concatenate([up_ref[...], x[: TM - GRID_W]], axis=0)
    dn = jnp.concatenate([x[GRID_W:], dn_ref[...]], axis=0)
    slab = (x * lc[4:5]
            + rm[:, 0:1] * (prev * lc[0:1])
            + rm[:, 1:2] * (nxt * lc[1:2])
            + rm[:, 2:3] * (up * lc[2:3])
            + rm[:, 3:4] * (dn * lc[3:4]))
    w = kk_ref.shape[1]
    r = slab[:, 0:w]
    k = slab[:, w:2 * w]
    v = slab[:, 2 * w:3 * w]
    o = 3 * w
    wd = slab[:, o:o + 2 * DECAY_RANK]
    ad = slab[:, o + 2 * DECAY_RANK:o + 2 * DECAY_RANK + 2 * ICL_RANK]
    gd = slab[:, o + 2 * DECAY_RANK + 2 * ICL_RANK:]
    bd = bd_ref[...]
    kk = k * kk_ref[...]
    nrm = jnp.sqrt(_dot32(kk * kk, bd))
    kk = kk / jnp.maximum(nrm, L2_EPS)
    r_o[...] = r
    v_o[...] = v
    kkn_o[...] = kk
    twd = jnp.tanh(wd)
    sgd = jax.nn.sigmoid(gd)
    for d in range(2):
        z = w0_ref[d:d + 1, :] + _dot32(twd, wup_ref[d])
        w_log = -jax.nn.softplus(-z) - 0.5
        lw_o[d] = -jnp.exp(w_log)
        a = jax.nn.sigmoid(a0_ref[d:d + 1, :] + _dot32(ad, aup_ref[d]))
        kt = k * (1.0 + (a - 1.0) * ka_ref[...])
        kt_o[d] = kt
        be_o[d] = kk * a
        g_o[d] = _dot32(sgd, gup_ref[d])
        bo_o[d] = _dot32(r * kt * rk_ref[...], bd) * v


def _rwkv_prep(p, rowmask, lanec, k_k, k_a, r_k, w0, a0, wup, aup, gup, bd, nct, slab_w):
    b, tt, _ = p.shape
    w = k_k.shape[1]
    nt = tt // TM
    hb = TM // GRID_W
    nhb = tt // GRID_W
    full = lambda *s: pl.BlockSpec(s, lambda bi, i: (0,) * len(s))
    tok = pl.BlockSpec((None, TM, w), lambda bi, i: (bi, i, 0))
    tok2 = pl.BlockSpec((2, None, TM, w), lambda bi, i: (0, bi, i, 0))
    sh1 = jax.ShapeDtypeStruct((b, tt, w), F32)
    sh2 = jax.ShapeDtypeStruct((2, b, tt, w), F32)
    return pl.pallas_call(
        _prep_kernel,
        grid=(b, nt),
        in_specs=[
            pl.BlockSpec((None, TM, slab_w), lambda bi, i: (bi, i, 0)),
            pl.BlockSpec((None, GRID_W, slab_w), lambda bi, i: (bi, jnp.maximum(i * hb - 1, 0), 0)),
            pl.BlockSpec((None, GRID_W, slab_w),
                         lambda bi, i: (bi, jnp.minimum(i * hb + hb, nhb - 1), 0)),
            pl.BlockSpec((None, TM, 8), lambda bi, i: (i, 0, 0)),
            pl.BlockSpec((None, 8, slab_w), lambda bi, i: (jnp.where(i < nct, 0, 1), 0, 0)),
            full(1, w), full(1, w), full(1, w), full(2, w), full(2, w),
            full(2, 2 * DECAY_RANK, w), full(2, 2 * ICL_RANK, w), full(2, 2 * GATE_RANK, w),
            full(w, w),
        ],
        out_specs=[tok, tok, tok, tok2, tok2, tok2, tok2, tok2],
        out_shape=[sh1, sh1, sh1, sh2, sh2, sh2, sh2, sh2],
        compiler_params=_cparams("parallel", "parallel"),
        name="rwkv_prep",
    )(p, p, p, rowmask, lanec, k_k, k_a, r_k, w0, a0, wup, aup, gup, bd)


def _wkv_kernel(nctc, ntot, r_ref, v_ref, kk_ref, lw_ref, kt_ref, be_ref, y_ref, ht_ref):
    d = pl.program_id(1)
    j = pl.program_id(2)
    c = WKV_CHUNK
    gw = WKV_HEADS * HEAD
    gn = WKV_HEADS * c
    ngrp = r_ref.shape[1] // gw

    @pl.when(j == 0)
    def _():
        ht_ref[...] = jnp.zeros_like(ht_ref)

    rev = d == 1
    flip = lambda t: jnp.where(rev, c - 1 - t, t)
    ti = flip(lax.broadcasted_iota(jnp.int32, (c, c), 0))
    si = flip(lax.broadcasted_iota(jnp.int32, (c, c), 1))
    incl_t = si <= ti
    row = lax.broadcasted_iota(jnp.int32, (gn, gn), 0)
    col = lax.broadcasted_iota(jnp.int32, (gn, gn), 1)
    sh = int(math.log2(c))
    same = (row >> sh) == (col >> sh)
    tt_ = flip(row & (c - 1))
    ss_ = flip(col & (c - 1))
    before_bd = same & (ss_ < tt_)
    incl_bd = same & (ss_ <= tt_)
    eye = (row == col).astype(F32)

    lw = lw_ref[...]
    lg_in = _dot32(incl_t.astype(F32), lw)
    lg_ex = lg_in - lw
    g_end = jnp.where(rev, lg_in[0:1, :], lg_in[c - 1:c, :])
    e_in = jnp.exp(lg_in)
    e_ex = jnp.exp(lg_ex)
    e_neg = jnp.exp(-lg_in)
    e_end = jnp.exp(g_end)
    kk = kk_ref[...]
    at = -kk * e_ex
    rt = r_ref[...] * e_in
    bg = be_ref[...] * e_neg
    kg = kt_ref[...] * e_neg
    bge = bg * e_end
    kge = kg * e_end
    v = v_ref[...]

    def stack(x):
        return jnp.where(same, jnp.concatenate([x] * WKV_HEADS, axis=0), 0.0).astype(BF16)

    for q in range(ngrp):
        sl = slice(q * gw, (q + 1) * gw)
        at_s, rt_s, bg_s, kg_s = stack(at[:, sl]), stack(rt[:, sl]), stack(bg[:, sl]), stack(kg[:, sl])
        v_s, bge_s, kge_s = stack(v[:, sl]), stack(bge[:, sl]), stack(kge[:, sl])
        ht = ht_ref[q]
        ht_b = ht.astype(BF16)
        a_ab = jnp.where(before_bd, _dot_nt(at_s, bg_s), 0.0)
        a_ak = jnp.where(before_bd, _dot_nt(at_s, kg_s), 0.0)
        a_rb = jnp.where(incl_bd, _dot_nt(rt_s, bg_s), 0.0)
        a_rk = jnp.where(incl_bd, _dot_nt(rt_s, kg_s), 0.0)
        tm = eye + a_ab
        pw = a_ab
        for _ in range(int(math.log2(c)) - 1):
            pb = pw.astype(BF16)
            pw = _dot(pb, pb)
            tm = tm + _dot(tm.astype(BF16), pw.astype(BF16))
        wv = _dot(a_ak.astype(BF16), v_s) + _dot_nt(at_s, ht_b)
        u = _dot(tm.astype(BF16), wv.astype(BF16))
        u_b = u.astype(BF16)
        yb = _dot(a_rb.astype(BF16), u_b) + _dot(a_rk.astype(BF16), v_s) + _dot_nt(rt_s, ht_b)
        y = yb[0:c]
        for h in range(1, WKV_HEADS):
            y = y + yb[h * c:(h + 1) * c]
        y_ref[:, sl] = y
        ht_ref[q] = ht * e_end[:, sl] + _dot_tn(u_b, bge_s) + _dot_tn(v_s, kge_s)


def _wkv_scan(r, v, kk, lw, kt, be, nctc):
    b, tt, w = r.shape
    ntot = tt // WKV_CHUNK
    ngrp = w // (WKV_HEADS * HEAD)

    def cidx(d, j):
        rev_idx = jnp.where(j < nctc, nctc - 1 - j, ntot - 1 + nctc - j)
        return jnp.where(d == 0, j, rev_idx)

    tok = pl.BlockSpec((None, WKV_CHUNK, w), lambda bi, d, j: (bi, cidx(d, j), 0))
    tok2 = pl.BlockSpec((None, None, WKV_CHUNK, w), lambda bi, d, j: (d, bi, cidx(d, j), 0))
    return pl.pallas_call(
        functools.partial(_wkv_kernel, nctc, ntot),
        grid=(b, 2, ntot),
        in_specs=[tok, tok, tok, tok2, tok2, tok2],
        out_specs=tok2,
        out_shape=jax.ShapeDtypeStruct((2, b, tt, w), F32),
        scratch_shapes=[pltpu.VMEM((ngrp, WKV_HEADS * HEAD, WKV_HEADS * HEAD), F32)],
        compiler_params=_cparams("parallel", "parallel", "arbitrary"),
        name="wkv_scan",
    )(r, v, kk, lw, kt, be)


def _s5_weights(lam_re, lam_im, log_dt, b_re, b_im, c_re, c_im):
    tc = S5_CHUNK
    lr = jnp.minimum(lam_re.astype(F32), LAM_RE_MAX)
    li = lam_im.astype(F32)
    dt = jnp.exp(log_dt.astype(F32))[..., None]
    mag = jnp.exp(lr * dt)
    ar = mag * jnp.cos(li * dt)
    ai = mag * jnp.sin(li * dt)
    den = lr * lr + li * li
    xr = ar - 1.0
    cr = (xr * lr + ai * li) / den
    ci = (ai * lr - xr * li) / den
    br = cr[..., None] * b_re - ci[..., None] * b_im
    bi = cr[..., None] * b_im + ci[..., None] * b_re
    pr, pi = [jnp.ones_like(ar)], [jnp.zeros_like(ar)]
    for _ in range(tc):
        pr_n = pr[-1] * ar - pi[-1] * ai
        pi_n = pr[-1] * ai + pi[-1] * ar
        pr.append(pr_n)
        pi.append(pi_n)
    pr = jnp.stack(pr)
    pi = jnp.stack(pi)
    lbr = pr[..., None] * br - pi[..., None] * bi
    lbi = pr[..., None] * bi + pi[..., None] * br
    clr = c_re * pr[:, :, :, None, :] - c_im * pi[:, :, :, None, :]
    cli = c_re * pi[:, :, :, None, :] + c_im * pr[:, :, :, None, :]
    kern = (jnp.einsum('dgop,tdgpi->tdgoi', c_re, lbr, precision=HIGHEST)
            - jnp.einsum('dgop,tdgpi->tdgoi', c_im, lbi, precision=HIGHEST))
    g = ar.shape[1]
    p = ar.shape[2]
    cg = b_re.shape[-1]
    s = jnp.arange(tc)[:, None]
    t = jnp.arange(tc)[None, :]
    toes, pouts, qins = [], [], []
    for d in range(2):
        lag = (t - s) if d == 0 else (s - t)
        valid = (lag >= 0)
        kd = kern[:, d][jnp.where(valid, lag, 0)]
        kd = jnp.where(valid[:, :, None, None, None], kd, 0.0)
        toes.append(jnp.transpose(kd, (2, 0, 4, 1, 3)).reshape(g, tc * cg, tc * cg))
        e_out = (tc - 1 - jnp.arange(tc)) if d == 0 else jnp.arange(tc)
        por = jnp.transpose(lbr[:, d][e_out], (1, 0, 3, 2)).reshape(g, tc * cg, p)
        poi = jnp.transpose(lbi[:, d][e_out], (1, 0, 3, 2)).reshape(g, tc * cg, p)
        pouts.append(jnp.concatenate([por, poi], axis=-1))
        e_in = (jnp.arange(tc) + 1) if d == 0 else (tc - jnp.arange(tc))
        qr = jnp.transpose(clr[:, d][e_in], (1, 3, 0, 2)).reshape(g, p, tc * cg)
        qi = jnp.transpose(cli[:, d][e_in], (1, 3, 0, 2)).reshape(g, p, tc * cg)
        qins.append(jnp.concatenate([qr, -qi], axis=1))
    la = jnp.concatenate([pr[tc], pr[tc]], axis=-1)
    lb = jnp.concatenate([-pi[tc], pi[tc]], axis=-1)
    return jnp.stack(toes), jnp.stack(pouts), jnp.stack(qins), la, lb


def _s5_local_kernel(u_ref, p_ref, e_ref):
    e_ref[...] = _dot(u_ref[...], p_ref[...])


def _s5_local(ug, pout):
    g, rows, kc = ug.shape
    n = pout.shape[-1]
    return pl.pallas_call(
        _s5_local_kernel,
        grid=(2, g),
        in_specs=[
            pl.BlockSpec((None, rows, kc), lambda d, gi: (gi, 0, 0)),
            pl.BlockSpec((None, None, kc, n), lambda d, gi: (d, gi, 0, 0)),
        ],
        out_specs=pl.BlockSpec((None, rows, n), lambda d, gi: (d, 0, gi)),
        out_shape=jax.ShapeDtypeStruct((2, rows, g * n), F32),
        compiler_params=_cparams("parallel", "parallel"),
        name="s5_local",
    )(ug, pout)


S5_STATE_ROWS = 8
S5_STATE_LANES = 256


def _s5_state_kernel(nctc, ntot, e_ref, la_ref, lb_ref, x_ref):
    d = pl.program_id(0)
    la = la_ref[...]
    lb = lb_ref[...]
    lane = lax.broadcasted_iota(jnp.int32, la.shape, 1)
    first_half = (lane & (2 * S5_STATE - 1)) < S5_STATE
    wl = la.shape[1]

    def body(j, x):
        rev_idx = jnp.where(j < nctc, nctc - 1 - j, ntot - 1 + nctc - j)
        c = jnp.where(d == 0, j, rev_idx)
        x_ref[c] = x
        sw = jnp.where(first_half, pltpu.roll(x, wl - S5_STATE, 1), pltpu.roll(x, S5_STATE, 1))
        return la * x + lb * sw + e_ref[c]

    lax.fori_loop(0, ntot, body, jnp.zeros(la.shape, F32))


def _s5_state(e, la, lb, nctc, ntot):
    _, nch, nr, lanes = e.shape
    wl = S5_STATE_LANES
    blk = pl.BlockSpec((None, nch, nr, wl), lambda d, i: (d, 0, 0, i))
    cf = pl.BlockSpec((None, nr, wl), lambda d, i: (d, 0, i))
    return pl.pallas_call(
        functools.partial(_s5_state_kernel, nctc, ntot),
        grid=(2, lanes // wl),
        in_specs=[blk, cf, cf],
        out_specs=blk,
        out_shape=jax.ShapeDtypeStruct(e.shape, F32),
        compiler_params=_cparams("parallel", "parallel"),
        name="s5_state",
    )(e, la, lb)


def _s5_out_kernel(u_ref, t_ref, x_ref, q_ref, y_ref):
    d = pl.program_id(1)
    y = _dot(u_ref[...], t_ref[...]) + _dot(x_ref[...].astype(BF16), q_ref[...])

    @pl.when(d == 0)
    def _():
        y_ref[...] = y

    @pl.when(d != 0)
    def _():
        y_ref[...] += y


def _s5_out(ug, toe, xin, qin):
    g, rows, kc = ug.shape
    n2 = qin.shape[2]
    return pl.pallas_call(
        _s5_out_kernel,
        grid=(g, 2),
        in_specs=[
            pl.BlockSpec((None, rows, kc), lambda gi, d: (gi, 0, 0)),
            pl.BlockSpec((None, None, kc, kc), lambda gi, d: (d, gi, 0, 0)),
            pl.BlockSpec((None, rows, n2), lambda gi, d: (d, 0, gi)),
            pl.BlockSpec((None, None, n2, kc), lambda gi, d: (d, gi, 0, 0)),
        ],
        out_specs=pl.BlockSpec((None, rows, kc), lambda gi, d: (gi, 0, 0)),
        out_shape=jax.ShapeDtypeStruct((g, rows, kc), F32),
        compiler_params=_cparams("parallel", "arbitrary"),
        name="s5_out",
    )(ug, toe, xin, qin)


def _s5_mix(u, weights, nctc16):
    toe, pout, qin, la, lb = weights
    b, tt, w = u.shape
    g = w // S5_GROUP
    nch = tt // S5_CHUNK
    kc = S5_CHUNK * S5_GROUP
    ug = u.reshape(b, nch, S5_CHUNK, g, S5_GROUP)
    ug = jnp.transpose(ug, (3, 1, 0, 2, 4)).reshape(g, nch * b, kc).astype(BF16)
    e = _s5_local(ug, pout.astype(BF16))
    fold = S5_STATE_ROWS // b
    fl = e.shape[2] // fold
    coef = lambda t: jnp.tile(t.reshape(2, fold, fl), (1, b, 1))
    xin = _s5_state(e.reshape(2, nch, S5_STATE_ROWS, fl), coef(la), coef(lb), nctc16, nch)
    xin = xin.reshape(2, nch * b, e.shape[2])
    ys = _s5_out(ug, toe.astype(BF16), xin, qin.astype(BF16))
    ys = ys.reshape(g, nch, b, S5_CHUNK, S5_GROUP)
    return jnp.transpose(ys, (2, 1, 3, 0, 4)).reshape(b, tt, w)


def _mixout_kernel(x_ref, y_ref, g_ref, bo_ref, ys_ref, u_ref, lnw_ref, lnb_ref, bd_ref, dsk_ref,
                   gluw_ref, glub_ref, wout_ref, gate_ref, o_ref):
    bd = bd_ref[...]
    inv = 1.0 / HEAD
    rw = None
    for d in range(2):
        y = y_ref[d]
        mean = _dot32(y, bd) * inv
        yc = y - mean
        var = _dot32(yc * yc, bd) * inv
        yn = yc * lax.rsqrt(var + GN_EPS) * lnw_ref[...] + lnb_ref[...]
        o = (yn + bo_ref[d]) * g_ref[d]
        rw = o if rw is None else rw + o
    u = u_ref[...]
    ss = ys_ref[...] + dsk_ref[...] * u
    ss = jax.nn.gelu(ss)
    ss = ss * jax.nn.sigmoid(_dot(ss.astype(BF16), gluw_ref[...]) + glub_ref[...])
    w = rw.shape[1]
    mix = _dot(rw.astype(BF16), wout_ref[0:w, :]) + _dot(ss.astype(BF16), wout_ref[w:, :])
    o_ref[...] = x_ref[...] + gate_ref[...] * mix


def _mixout(xcat, y, g, bo, ys, p, ln_w, ln_b, bd, d_skip, glu_w, glu_b, w_out, gate, nct, t0):
    b, tt, d = xcat.shape
    w = ln_w.shape[1]
    sw = ys.shape[2]
    ublk = (p.shape[2] - sw) // sw
    nt = tt // TM - t0
    full = lambda *s: pl.BlockSpec(s, lambda bi, i: (0,) * len(s))
    tok2 = pl.BlockSpec((2, None, TM, w), lambda bi, i: (0, bi, i + t0, 0))
    return pl.pallas_call(
        _mixout_kernel,
        grid=(b, nt),
        in_specs=[
            pl.BlockSpec((None, TM, d), lambda bi, i: (bi, i + t0, 0)),
            tok2, tok2, tok2,
            pl.BlockSpec((None, TM, sw), lambda bi, i: (bi, i + t0, 0)),
            pl.BlockSpec((None, TM, sw), lambda bi, i: (bi, i + t0, ublk)),
            full(1, w), full(1, w), full(w, w), full(1, sw), full(sw, sw), full(1, sw),
            full(w + sw, d),
            pl.BlockSpec((None, None, 1, d), lambda bi, i: (bi, jnp.where(i + t0 < nct, 0, 1), 0, 0)),
        ],
        out_specs=pl.BlockSpec((None, TM, d), lambda bi, i: (bi, i, 0)),
        out_shape=jax.ShapeDtypeStruct((b, nt * TM, d), F32),
        compiler_params=_cparams("parallel", "parallel"),
        name="mix_out",
    )(xcat, y, g, bo, ys, p, ln_w, ln_b, bd, d_skip, glu_w, glu_b, w_out, gate)


def _ffn_kernel(x_ref, g_ref, sh_ref, sc_ref, gate_ref, wg_ref, wu_ref, wd_ref, o_ref):
    x = x_ref[...]
    h = _norm_mod(x, g_ref[...], sh_ref[...], sc_ref[...]).astype(BF16)
    a = _dot(h, wg_ref[...])
    a = a * jax.nn.sigmoid(a) * _dot(h, wu_ref[...])
    o_ref[...] = x + gate_ref[...] * _dot(a.astype(BF16), wd_ref[...])


def _ffn(xcat, g, shift, scale, gate, wg, wu, wd, nct):
    b, tt, d = xcat.shape
    ff = wg.shape[1]
    kind = lambda bi, i: (bi, jnp.where(i < nct, 0, 1), 0, 0)
    mod = pl.BlockSpec((None, None, 1, d), kind)
    return pl.pallas_call(
        _ffn_kernel,
        grid=(b, tt // TM),
        in_specs=[
            pl.BlockSpec((None, TM, d), lambda bi, i: (bi, i, 0)),
            pl.BlockSpec((1, d), lambda bi, i: (0, 0)),
            mod, mod, mod,
            pl.BlockSpec((d, ff), lambda bi, i: (0, 0)),
            pl.BlockSpec((d, ff), lambda bi, i: (0, 0)),
            pl.BlockSpec((ff, d), lambda bi, i: (0, 0)),
        ],
        out_specs=pl.BlockSpec((None, TM, d), lambda bi, i: (bi, i, 0)),
        out_shape=jax.ShapeDtypeStruct((b, tt, d), F32),
        compiler_params=_cparams("parallel", "parallel"),
        name="ffn",
    )(xcat, g, shift, scale, gate, wg, wu, wd)


MOE_TM = 1024
MOE_TF = 512
ROUTER_LANES = 128


def _moe_kernel(ne, x_ref, g_ref, sh_ref, sc_ref, gate_ref, rt_ref, wg_ref, wu_ref, wd_ref, fg_ref,
                o_ref, h_ref, cmb_ref, acc_ref):
    e = pl.program_id(1)
    j = pl.program_id(2)
    lane = lax.broadcasted_iota(jnp.int32, cmb_ref.shape, 1)

    @pl.when((e == 0) & (j == 0))
    def _():
        h = _norm_mod(x_ref[...], g_ref[...], sh_ref[...], sc_ref[...])
        h_ref[...] = h.astype(BF16)
        logits = jnp.where(lane < ne, _dot32(h, rt_ref[...]), -jnp.inf)
        m1 = jnp.max(logits, axis=-1, keepdims=True)
        lanef = lane.astype(F32)
        i1 = jnp.min(jnp.where(logits == m1, lanef, float(ROUTER_LANES)), axis=-1, keepdims=True)
        rest = jnp.where(lanef == i1, -jnp.inf, logits)
        m2 = jnp.max(rest, axis=-1, keepdims=True)
        i2 = jnp.min(jnp.where(rest == m2, lanef, float(ROUTER_LANES)), axis=-1, keepdims=True)
        e2 = jnp.exp(m2 - m1)
        p1 = 1.0 / (1.0 + e2)
        p2 = e2 / (1.0 + e2)
        cmb_ref[...] = jnp.where(lanef == i1, p1, 0.0) + jnp.where(lanef == i2, p2, 0.0)
        acc_ref[...] = jnp.zeros_like(acc_ref)

    h = h_ref[...]
    ce = jnp.sum(jnp.where(lane == e, cmb_ref[...], 0.0), axis=-1, keepdims=True)
    a = _dot(h, wg_ref[...])
    a = a * jax.nn.sigmoid(a) * _dot(h, wu_ref[...])
    acc_ref[...] += _dot((a * ce).astype(BF16), wd_ref[...])

    @pl.when((e == ne - 1) & (j == pl.num_programs(2) - 1))
    def _():
        y = x_ref[...] + gate_ref[...] * acc_ref[...]
        ms = jnp.mean(y * y, axis=-1, keepdims=True)
        o_ref[...] = y * lax.rsqrt(ms + NORM_EPS) * fg_ref[...]


def _moe(x, g, shift, scale, gate, router_pad, wg, wu, wd, final_g):
    b, l, d = x.shape
    ne, _, ff = wg.shape
    nt = l // MOE_TM
    mod = pl.BlockSpec((None, 1, d), lambda t, e, j: (t // nt, 0, 0))
    return pl.pallas_call(
        functools.partial(_moe_kernel, ne),
        grid=(b * nt, ne, ff // MOE_TF),
        in_specs=[
            pl.BlockSpec((None, MOE_TM, d), lambda t, e, j: (t // nt, t % nt, 0)),
            pl.BlockSpec((1, d), lambda t, e, j: (0, 0)),
            mod, mod, mod,
            pl.BlockSpec((d, ROUTER_LANES), lambda t, e, j: (0, 0)),
            pl.BlockSpec((None, d, MOE_TF), lambda t, e, j: (e, 0, j)),
            pl.BlockSpec((None, d, MOE_TF), lambda t, e, j: (e, 0, j)),
            pl.BlockSpec((None, MOE_TF, d), lambda t, e, j: (e, j, 0)),
            pl.BlockSpec((1, d), lambda t, e, j: (0, 0)),
        ],
        out_specs=pl.BlockSpec((None, MOE_TM, d), lambda t, e, j: (t // nt, t % nt, 0)),
        out_shape=jax.ShapeDtypeStruct((b, l, d), F32),
        scratch_shapes=[
            pltpu.VMEM((MOE_TM, d), BF16),
            pltpu.VMEM((MOE_TM, ROUTER_LANES), F32),
            pltpu.VMEM((MOE_TM, d), F32),
        ],
        compiler_params=_cparams("parallel", "arbitrary", "arbitrary"),
        name="moe",
    )(x, g, shift, scale, gate, router_pad, wg, wu, wd, final_g)


def _shift_masks(mu, ctx_len, seq_len):
    slab = mu.shape[0]
    nct = ctx_len // TM
    tt = ctx_len + seq_len
    t = jnp.arange(tt)
    is_ctx = t < ctx_len
    tl = t - ctx_len
    col = tl % GRID_W
    rows = seq_len // GRID_W
    grow = tl // GRID_W
    left = jnp.where(is_ctx, t != 0, col != 0)
    right = jnp.where(is_ctx, t != ctx_len - 1, col != GRID_W - 1)
    upv = jnp.where(is_ctx, False, grow != 0)
    dnv = jnp.where(is_ctx, False, grow != rows - 1)
    zero = jnp.zeros_like(left)
    rowmask = jnp.stack([left, right, upv, dnv, zero, zero, zero, zero], axis=-1).astype(F32)
    rowmask = rowmask.reshape(tt // TM, TM, 8)
    c = jnp.arange(slab)
    z = jnp.zeros_like(mu)
    lat = jnp.stack([mu * (c % 4 == 0), mu * (c % 4 == 1), mu * (c % 4 == 2), mu * (c % 4 == 3),
                     1.0 - mu, z, z, z])
    ctx = jnp.stack([mu * (c % 2 == 0), mu * (c % 2 == 1), z, z, 1.0 - mu, z, z, z])
    return rowmask, jnp.stack([ctx, lat]).astype(F32)


def _pad_rows(wt, total):
    r = wt.shape[1]
    z = jnp.zeros_like(wt[0])
    return jnp.stack([jnp.concatenate([wt[0], z], axis=0), jnp.concatenate([z, wt[1]], axis=0)])


def kernel(x, c, ctx, c_ctx, ada_w, ada_b, norm1_g, norm2_g, w_in, w_out, shift_mu, rwkv_w0, rwkv_w_up, rwkv_a0, rwkv_a_up, rwkv_g_up, rwkv_k_k, rwkv_k_a, rwkv_r_k, rwkv_ln_w, rwkv_ln_b, s5_lam_re, s5_lam_im, s5_log_dt, s5_b_re, s5_b_im, s5_c_re, s5_c_im, s5_d, s5_glu_w, s5_glu_b, ffn_w_gate, ffn_w_up, ffn_w_down, moe_router, moe_w_gate, moe_w_up, moe_w_down, final_g):
    b, l, d = x.shape
    ctx_len = ctx.shape[1]
    depth = ada_w.shape[0]
    slab_w = shift_mu.shape[1]
    rw_w = rwkv_k_k.shape[1]
    assert ctx_len == TM and l % TM == 0 and b + 1 <= 8
    assert rw_w % (WKV_HEADS * HEAD) == 0 and depth == 2
    nct = ctx_len // TM
    nctc = ctx_len // WKV_CHUNK
    nctc16 = ctx_len // S5_CHUNK

    act = jnp.zeros((8, d), F32).at[:b].set(c).at[b].set(c_ctx)
    mods = _ada_mod(act, ada_w, ada_b).reshape(depth, 8, 6, d)

    def mod(i, k):
        cm = jnp.broadcast_to(mods[i, b, k][None, :], (b, d))
        return jnp.stack([cm, mods[i, :b, k]], axis=1)[:, :, None, :]

    hi = lax.broadcasted_iota(jnp.int32, (rw_w, rw_w), 0) // HEAD
    hj = lax.broadcasted_iota(jnp.int32, (rw_w, rw_w), 1) // HEAD
    bd = (hi == hj).astype(F32)

    xcat = jnp.concatenate([ctx, x], axis=1)
    out = None
    for i in range(depth):
        last = i == depth - 1
        p = _inproj(xcat, norm1_g[i][None], mod(i, 0), mod(i, 1), w_in[i].astype(BF16), nct)
        rowmask, lanec = _shift_masks(shift_mu[i], ctx_len, l)
        r, v, kk, lw, kt, be, g, bo = _rwkv_prep(
            p, rowmask, lanec, rwkv_k_k[i][None], rwkv_k_a[i][None], rwkv_r_k[i].reshape(1, -1),
            rwkv_w0[i], rwkv_a0[i], _pad_rows(rwkv_w_up[i], 2 * DECAY_RANK),
            _pad_rows(rwkv_a_up[i], 2 * ICL_RANK), _pad_rows(rwkv_g_up[i], 2 * GATE_RANK), bd,
            nct, slab_w)
        y = _wkv_scan(r, v, kk, lw, kt, be, nctc)
        s5w = _s5_weights(s5_lam_re[i], s5_lam_im[i], s5_log_dt[i], s5_b_re[i], s5_b_im[i],
                          s5_c_re[i], s5_c_im[i])
        ys = _s5_mix(p[:, :, slab_w:], s5w, nctc16)
        t0 = nct if last else 0
        xm = _mixout(xcat, y, g, bo, ys, p, rwkv_ln_w[i].reshape(1, -1), rwkv_ln_b[i].reshape(1, -1),
                     bd, s5_d[i][None], s5_glu_w[i].astype(BF16), s5_glu_b[i][None],
                     w_out[i].astype(BF16), mod(i, 2), nct, t0)
        if not last:
            j = i // 2
            xcat = _ffn(xm, norm2_g[i][None], mod(i, 3), mod(i, 4), mod(i, 5),
                        ffn_w_gate[j].astype(BF16), ffn_w_up[j].astype(BF16),
                        ffn_w_down[j].astype(BF16), nct)
        else:
            j = i // 2
            ne = moe_router.shape[2]
            router_pad = jnp.zeros((d, ROUTER_LANES), F32).at[:, :ne].set(moe_router[j])
            lat = lambda k: mods[i, :b, k][:, None, :]
            out = _moe(xm, norm2_g[i][None], lat(3), lat(4), lat(5), router_pad,
                       moe_w_gate[j].astype(BF16), moe_w_up[j].astype(BF16),
                       moe_w_down[j].astype(BF16), final_g[None])
    return out
```

```python
import functools
import math

import jax
import jax.numpy as jnp
from jax import lax
from jax.experimental import pallas as pl
from jax.experimental.pallas import tpu as pltpu

F32 = jnp.float32
BF16 = jnp.bfloat16
HIGHEST = lax.Precision.HIGHEST

GRID_W = 64
HEAD = 64
DECAY_RANK = 64
ICL_RANK = 64
GATE_RANK = 128
S5_GROUP = 16
S5_STATE = 64
NORM_EPS = 1e-6
GN_EPS = 64e-5
L2_EPS = 1e-12
LAM_RE_MAX = -1e-4
TOP_K = 2

TM = 256
WKV_CHUNK = 64
WKV_HEADS = 4
S5_CHUNK = 16
VMEM_LIMIT = 56 * 1024 * 1024


def _cparams(*sem):
    return pltpu.CompilerParams(dimension_semantics=sem, vmem_limit_bytes=VMEM_LIMIT)


def _dot(a, b):
    return jnp.dot(a, b, preferred_element_type=F32)


def _dot32(a, b):
    return jnp.dot(a, b, precision=HIGHEST, preferred_element_type=F32)


def _dot_nt(a, b):
    return lax.dot_general(a, b, (((1,), (1,)), ((), ())), preferred_element_type=F32)


def _dot_tn(a, b):
    return lax.dot_general(a, b, (((0,), (0,)), ((), ())), preferred_element_type=F32)


def _ada_kernel(act_ref, w_ref, b_ref, o_ref):
    a = act_ref[...]
    a = a * jax.nn.sigmoid(a)
    o_ref[...] = _dot32(a, w_ref[...]) + b_ref[...]


def _ada_mod(act, ada_w, ada_b):
    depth, d, n = ada_w.shape
    tn = 1536
    return pl.pallas_call(
        _ada_kernel,
        grid=(depth, n // tn),
        in_specs=[
            pl.BlockSpec((8, d), lambda i, j: (0, 0)),
            pl.BlockSpec((None, d, tn), lambda i, j: (i, 0, j)),
            pl.BlockSpec((None, 1, tn), lambda i, j: (i, 0, j)),
        ],
        out_specs=pl.BlockSpec((None, 8, tn), lambda i, j: (i, 0, j)),
        out_shape=jax.ShapeDtypeStruct((depth, 8, n), F32),
        compiler_params=_cparams("arbitrary", "arbitrary"),
        name="ada_mod",
    )(act, ada_w, ada_b.reshape(depth, 1, n))


def _norm_mod(x, g, shift, scale):
    ms = jnp.mean(x * x, axis=-1, keepdims=True)
    y = x * lax.rsqrt(ms + NORM_EPS) * g
    return y * (1.0 + scale) + shift


def _inproj_kernel(x_ref, g_ref, sh_ref, sc_ref, w_ref, o_ref):
    h = _norm_mod(x_ref[...], g_ref[...], sh_ref[...], sc_ref[...])
    o_ref[...] = _dot(h.astype(BF16), w_ref[...])


def _inproj(xcat, g, shift, scale, w_bf, nct):
    b, tt, d = xcat.shape
    n = w_bf.shape[1]
    kind = lambda bi, i: (bi, jnp.where(i < nct, 0, 1), 0, 0)
    return pl.pallas_call(
        _inproj_kernel,
        grid=(b, tt // TM),
        in_specs=[
            pl.BlockSpec((None, TM, d), lambda bi, i: (bi, i, 0)),
            pl.BlockSpec((1, d), lambda bi, i: (0, 0)),
            pl.BlockSpec((None, None, 1, d), kind),
            pl.BlockSpec((None, None, 1, d), kind),
            pl.BlockSpec((d, n), lambda bi, i: (0, 0)),
        ],
        out_specs=pl.BlockSpec((None, TM, n), lambda bi, i: (bi, i, 0)),
        out_shape=jax.ShapeDtypeStruct((b, tt, n), F32),
        compiler_params=_cparams("parallel", "parallel"),
        name="inproj",
    )(xcat, g, shift, scale, w_bf)


def _split3(x):
    hi = x.astype(BF16)
    r1 = x - hi.astype(F32)
    mid = r1.astype(BF16)
    lo = (r1 - mid.astype(F32)).astype(BF16)
    return hi, mid, lo


def _prep_kernel(p_ref, up_ref, dn_ref, rm_ref, lc_ref, kk_ref, ka_ref, rk_ref, w0_ref, a0_ref,
                 wup_ref, aup_ref, gup_ref, bd_ref, tri_ref,
                 v_o, at_o, rt_o, bg_o, kg_o, ee_o, g_o, bo_o):
    x = p_ref[...]
    rm = rm_ref[...]
    lc = lc_ref[...]
    prev = pltpu.roll(x, 1, 0)
    nxt = pltpu.roll(x, TM - 1, 0)
    up = jnp.concatenate([up_ref[...], x[: TM - GRID_W]], axis=0)
    dn = jnp.concatenate([x[GRID_W:], dn_ref[...]], axis=0)
    slab = (x * lc[4:5]
            + rm[:, 0:1] * (prev * lc[0:1])
            + rm[:, 1:2] * (nxt * lc[1:2])
            + rm[:, 2:3] * (up * lc[2:3])
            + rm[:, 3:4] * (dn * lc[3:4]))
    w = kk_ref.shape[1]
    r = slab[:, 0:w]
    k = slab[:, w:2 * w]
    v = slab[:, 2 * w:3 * w]
    o = 3 * w
    wd = slab[:, o:o + 2 * DECAY_RANK]
    ad = slab[:, o + 2 * DECAY_RANK:o + 2 * DECAY_RANK + 2 * ICL_RANK]
    gd = slab[:, o + 2 * DECAY_RANK + 2 * ICL_RANK:]
    bd = bd_ref[...]
    kk = k * kk_ref[...]
    nrm = jnp.sqrt(_dot32(kk * kk, bd))
    kk = kk / jnp.maximum(nrm, L2_EPS)
    v_o[...] = v.astype(BF16)
    twd = jnp.tanh(wd)
    sgd = jax.nn.sigmoid(gd)
    c = WKV_CHUNK
    for d in range(2):
        z = w0_ref[d:d + 1, :] + _dot32(twd, wup_ref[d])
        w_log = -jax.nn.softplus(-z) - 0.5
        lw = -jnp.exp(w_log)
        a = jax.nn.sigmoid(a0_ref[d:d + 1, :] + _dot32(ad, aup_ref[d]))
        kt = k * (1.0 + (a - 1.0) * ka_ref[...])
        g_o[d] = _dot32(sgd, gup_ref[d])
        bo_o[d] = _dot32(r * kt * rk_ref[...], bd) * v
        tri = tri_ref[d]
        hi, mid, lo = _split3(lw)
        lg_in = _dot(tri, hi) + _dot(tri, mid) + _dot(tri, lo)
        e_neg = jnp.exp(-lg_in)
        at_o[d] = (-kk * jnp.exp(lg_in - lw)).astype(BF16)
        rt_o[d] = (r * jnp.exp(lg_in)).astype(BF16)
        bg_o[d] = (kk * a * e_neg).astype(BF16)
        kg_o[d] = (kt * e_neg).astype(BF16)
        for ci in range(TM // c):
            last = ci * c + (c - 1 if d == 0 else 0)
            ee_o[d, ci] = jnp.exp(lg_in[last:last + 1, :])


def _rwkv_prep(p, rowmask, lanec, k_k, k_a, r_k, w0, a0, wup, aup, gup, bd, tri, nct, slab_w):
    b, tt, _ = p.shape
    w = k_k.shape[1]
    nt = tt // TM
    cpt = TM // WKV_CHUNK
    hb = TM // GRID_W
    nhb = tt // GRID_W
    full = lambda *s: pl.BlockSpec(s, lambda bi, i: (0,) * len(s))
    tok = pl.BlockSpec((None, TM, w), lambda bi, i: (bi, i, 0))
    tok2 = pl.BlockSpec((2, None, TM, w), lambda bi, i: (0, bi, i, 0))
    bf1 = jax.ShapeDtypeStruct((b, tt, w), BF16)
    bf2 = jax.ShapeDtypeStruct((2, b, tt, w), BF16)
    sh2 = jax.ShapeDtypeStruct((2, b, tt, w), F32)
    return pl.pallas_call(
        _prep_kernel,
        grid=(b, nt),
        in_specs=[
            pl.BlockSpec((None, TM, slab_w), lambda bi, i: (bi, i, 0)),
            pl.BlockSpec((None, GRID_W, slab_w), lambda bi, i: (bi, jnp.maximum(i * hb - 1, 0), 0)),
            pl.BlockSpec((None, GRID_W, slab_w),
                         lambda bi, i: (bi, jnp.minimum(i * hb + hb, nhb - 1), 0)),
            pl.BlockSpec((None, TM, 8), lambda bi, i: (i, 0, 0)),
            pl.BlockSpec((None, 8, slab_w), lambda bi, i: (jnp.where(i < nct, 0, 1), 0, 0)),
            full(1, w), full(1, w), full(1, w), full(2, w), full(2, w),
            full(2, 2 * DECAY_RANK, w), full(2, 2 * ICL_RANK, w), full(2, 2 * GATE_RANK, w),
            full(w, w), full(2, TM, TM),
        ],
        out_specs=[tok, tok2, tok2, tok2, tok2,
                   pl.BlockSpec((2, None, cpt, 1, w), lambda bi, i: (0, bi, i, 0, 0)),
                   tok2, tok2],
        out_shape=[bf1, bf2, bf2, bf2, bf2,
                   jax.ShapeDtypeStruct((2, b, tt // WKV_CHUNK, 1, w), F32), sh2, sh2],
        compiler_params=_cparams("parallel", "parallel"),
        name="rwkv_prep",
    )(p, p, p, rowmask, lanec, k_k, k_a, r_k, w0, a0, wup, aup, gup, bd, tri)


def _wkv_kernel(nb, ngrp, v_f, v_r, at_f, at_r, rt_f, rt_r, bg_f, bg_r, kg_f, kg_r, ee_f, ee_r,
                y_f, y_r, ht_ref):
    j = pl.program_id(0)
    c = WKV_CHUNK
    gw = WKV_HEADS * HEAD
    gn = WKV_HEADS * c

    @pl.when(j == 0)
    def _():
        ht_ref[...] = jnp.zeros_like(ht_ref)

    sh = int(math.log2(c))
    row = lax.broadcasted_iota(jnp.int32, (gn, gw), 0)
    col = lax.broadcasted_iota(jnp.int32, (gn, gw), 1)
    same = (row >> sh) == (col >> sh)
    tf = lax.broadcasted_iota(jnp.int32, (c, gn), 0)
    sf = lax.broadcasted_iota(jnp.int32, (c, gn), 1) & (c - 1)
    eye = (tf == sf).astype(F32)

    def stack(x):
        xb = jnp.concatenate([x.astype(BF16)] * WKV_HEADS, axis=0)
        return jnp.where(same, xb, jnp.zeros_like(xb))

    dirs = ((v_f, at_f, rt_f, bg_f, kg_f, ee_f, y_f, sf < tf, sf <= tf),
            (v_r, at_r, rt_r, bg_r, kg_r, ee_r, y_r, sf > tf, sf >= tf))
    chains = [(d, bi, q) for d in range(2) for bi in range(nb) for q in range(ngrp)]
    sl = lambda q: slice(q * gw, (q + 1) * gw)
    rd = lambda k: [dirs[d][k][bi, :, sl(q)] for d, bi, q in chains]
    cat0 = lambda xs: jnp.concatenate(xs, axis=0)
    v, at, rt, bg, kg, ee = rd(0), rd(1), rd(2), rd(3), rd(4), rd(5)
    before = [dirs[d][7] for d, _, _ in chains]
    incl = [dirs[d][8] for d, _, _ in chains]
    n_ch = range(len(chains))

    v_bd = [stack(x) for x in v]
    at_bd = [stack(x) for x in at]
    bk_bd = [cat0([stack(bg[i]), stack(kg[i])]) for i in n_ch]
    a = [_dot_nt(cat0([at[i], rt[i]]), bk_bd[i]) for i in n_ch]
    n = [jnp.where(before[i], a[i][0:c, 0:gn], 0.0) for i in n_ch]
    a_kk = [cat0([jnp.where(before[i], a[i][0:c, gn:], 0.0),
                  jnp.where(incl[i], a[i][c:, gn:], 0.0)]).astype(BF16) for i in n_ch]
    a_rb = [jnp.where(incl[i], a[i][c:, 0:gn], 0.0).astype(BF16) for i in n_ch]
    tm = [eye + x for x in n]
    pw = [_dot(x.astype(BF16), stack(x)) for x in n]
    for lvl in range(1, sh):
        pw_bd = [stack(x) for x in pw]
        if lvl < sh - 1:
            tp = [_dot(cat0([tm[i].astype(BF16), pw[i].astype(BF16)]), pw_bd[i]) for i in n_ch]
            tm = [tm[i] + tp[i][0:c] for i in n_ch]
            pw = [tp[i][c:] for i in n_ch]
        else:
            tm = [tm[i] + _dot(tm[i].astype(BF16), pw_bd[i]) for i in n_ch]
    tm_b = [x.astype(BF16) for x in tm]
    atp = [_dot(tm_b[i], at_bd[i]) for i in n_ch]
    av = [_dot(a_kk[i], v_bd[i]) for i in n_ch]
    wv = [_dot(tm_b[i], stack(av[i][0:c])) for i in n_ch]
    wv_bd = [stack(x) for x in wv]
    atp_bd = [stack(x) for x in atp]
    ar = [_dot(a_rb[i], jnp.concatenate([wv_bd[i], atp_bd[i]], axis=1)) for i in n_ch]
    y0 = [ar[i][:, 0:gw] + av[i][c:] for i in n_ch]
    rtp = [(ar[i][:, gw:] + rt[i].astype(F32)).astype(BF16) for i in n_ch]
    bge_bd = [stack(bg[i].astype(F32) * ee[i]) for i in n_ch]
    kge_bd = [stack(kg[i].astype(F32) * ee[i]) for i in n_ch]
    g = [_dot_tn(bge_bd[i], atp_bd[i]).astype(BF16) for i in n_ch]
    hloc_t = [_dot_tn(cat0([wv_bd[i], v_bd[i]]), cat0([bge_bd[i], kge_bd[i]])) for i in n_ch]
    for i, (d, bi, q) in enumerate(chains):
        ht = ht_ref[d, bi, q]
        ht_b = ht.astype(BF16)
        dirs[d][6][bi, :, sl(q)] = y0[i] + _dot_nt(rtp[i], ht_b)
        ht_ref[d, bi, q] = ht * ee[i] + _dot_nt(ht_b, g[i]) + hloc_t[i]


def _wkv_scan(v, at, rt, bg, kg, ee, nctc):
    b, tt, w = v.shape
    ntot = tt // WKV_CHUNK
    ngrp = w // (WKV_HEADS * HEAD)
    fwd = lambda j: j
    rev = lambda j: jnp.where(j < nctc, nctc - 1 - j, ntot - 1 + nctc - j)
    tok = lambda cm: pl.BlockSpec((b, WKV_CHUNK, w), lambda j: (0, cm(j), 0))
    tok2 = lambda d, cm: pl.BlockSpec((None, b, WKV_CHUNK, w), lambda j: (d, 0, cm(j), 0))
    eesp = lambda d, cm: pl.BlockSpec((None, b, None, 1, w), lambda j: (d, 0, cm(j), 0, 0))
    pair = lambda f: [f(0, fwd), f(1, rev)]
    ysh = jax.ShapeDtypeStruct((b, tt, w), F32)
    return pl.pallas_call(
        functools.partial(_wkv_kernel, b, ngrp),
        grid=(ntot,),
        in_specs=[tok(fwd), tok(rev)] + pair(tok2) + pair(tok2) + pair(tok2) + pair(tok2) + pair(eesp),
        out_specs=[tok(fwd), tok(rev)],
        out_shape=[ysh, ysh],
        scratch_shapes=[pltpu.VMEM((2, b, ngrp, WKV_HEADS * HEAD, WKV_HEADS * HEAD), F32)],
        compiler_params=_cparams("arbitrary"),
        name="wkv_scan",
    )(v, v, at, at, rt, rt, bg, bg, kg, kg, ee, ee)


def _s5_weights(lam_re, lam_im, log_dt, b_re, b_im, c_re, c_im):
    tc = S5_CHUNK
    lr = jnp.minimum(lam_re.astype(F32), LAM_RE_MAX)
    li = lam_im.astype(F32)
    dt = jnp.exp(log_dt.astype(F32))[..., None]
    mag = jnp.exp(lr * dt)
    ar = mag * jnp.cos(li * dt)
    ai = mag * jnp.sin(li * dt)
    den = lr * lr + li * li
    xr = ar - 1.0
    cr = (xr * lr + ai * li) / den
    ci = (ai * lr - xr * li) / den
    br = cr[..., None] * b_re - ci[..., None] * b_im
    bi = cr[..., None] * b_im + ci[..., None] * b_re
    pr, pi = [jnp.ones_like(ar)], [jnp.zeros_like(ar)]
    for _ in range(tc):
        pr_n = pr[-1] * ar - pi[-1] * ai
        pi_n = pr[-1] * ai + pi[-1] * ar
        pr.append(pr_n)
        pi.append(pi_n)
    pr = jnp.stack(pr)
    pi = jnp.stack(pi)
    lbr = pr[..., None] * br - pi[..., None] * bi
    lbi = pr[..., None] * bi + pi[..., None] * br
    clr = c_re * pr[:, :, :, None, :] - c_im * pi[:, :, :, None, :]
    cli = c_re * pi[:, :, :, None, :] + c_im * pr[:, :, :, None, :]
    kern = (jnp.einsum('dgop,tdgpi->tdgoi', c_re, lbr, precision=HIGHEST)
            - jnp.einsum('dgop,tdgpi->tdgoi', c_im, lbi, precision=HIGHEST))
    g = ar.shape[1]
    p = ar.shape[2]
    cg = b_re.shape[-1]
    s = jnp.arange(tc)[:, None]
    t = jnp.arange(tc)[None, :]
    toes, pouts, qins = [], [], []
    for d in range(2):
        lag = (t - s) if d == 0 else (s - t)
        valid = (lag >= 0)
        kd = kern[:, d][jnp.where(valid, lag, 0)]
        kd = jnp.where(valid[:, :, None, None, None], kd, 0.0)
        toes.append(jnp.transpose(kd, (2, 0, 4, 1, 3)).reshape(g, tc * cg, tc * cg))
        e_out = (tc - 1 - jnp.arange(tc)) if d == 0 else jnp.arange(tc)
        por = jnp.transpose(lbr[:, d][e_out], (1, 0, 3, 2)).reshape(g, tc * cg, p)
        poi = jnp.transpose(lbi[:, d][e_out], (1, 0, 3, 2)).reshape(g, tc * cg, p)
        pouts.append(jnp.concatenate([por, poi], axis=-1))
        e_in = (jnp.arange(tc) + 1) if d == 0 else (tc - jnp.arange(tc))
        qr = jnp.transpose(clr[:, d][e_in], (1, 3, 0, 2)).reshape(g, p, tc * cg)
        qi = jnp.transpose(cli[:, d][e_in], (1, 3, 0, 2)).reshape(g, p, tc * cg)
        qins.append(jnp.concatenate([qr, -qi], axis=1))
    la = jnp.concatenate([pr[tc], pr[tc]], axis=-1)
    lb = jnp.concatenate([-pi[tc], pi[tc]], axis=-1)
    return jnp.stack(toes), jnp.stack(pouts), jnp.stack(qins), la, lb


def _s5_local_kernel(u_ref, p_ref, e_ref):
    e_ref[...] = _dot(u_ref[...], p_ref[...])


def _s5_local(ug, pout):
    g, rows, kc = ug.shape
    n = pout.shape[-1]
    return pl.pallas_call(
        _s5_local_kernel,
        grid=(2, g),
        in_specs=[
            pl.BlockSpec((None, rows, kc), lambda d, gi: (gi, 0, 0)),
            pl.BlockSpec((None, None, kc, n), lambda d, gi: (d, gi, 0, 0)),
        ],
        out_specs=pl.BlockSpec((None, rows, n), lambda d, gi: (d, 0, gi)),
        out_shape=jax.ShapeDtypeStruct((2, rows, g * n), F32),
        compiler_params=_cparams("parallel", "parallel"),
        name="s5_local",
    )(ug, pout)


S5_STATE_ROWS = 8
S5_STATE_LANES = 256


def _s5_state_kernel(nctc, ntot, e_ref, la_ref, lb_ref, x_ref):
    d = pl.program_id(0)
    la = la_ref[...]
    lb = lb_ref[...]
    lane = lax.broadcasted_iota(jnp.int32, la.shape, 1)
    first_half = (lane & (2 * S5_STATE - 1)) < S5_STATE
    wl = la.shape[1]

    def body(j, x):
        rev_idx = jnp.where(j < nctc, nctc - 1 - j, ntot - 1 + nctc - j)
        c = jnp.where(d == 0, j, rev_idx)
        x_ref[c] = x
        sw = jnp.where(first_half, pltpu.roll(x, wl - S5_STATE, 1), pltpu.roll(x, S5_STATE, 1))
        return la * x + lb * sw + e_ref[c]

    lax.fori_loop(0, ntot, body, jnp.zeros(la.shape, F32))


def _s5_state(e, la, lb, nctc, ntot):
    _, nch, nr, lanes = e.shape
    wl = S5_STATE_LANES
    blk = pl.BlockSpec((None, nch, nr, wl), lambda d, i: (d, 0, 0, i))
    cf = pl.BlockSpec((None, nr, wl), lambda d, i: (d, 0, i))
    return pl.pallas_call(
        functools.partial(_s5_state_kernel, nctc, ntot),
        grid=(2, lanes // wl),
        in_specs=[blk, cf, cf],
        out_specs=blk,
        out_shape=jax.ShapeDtypeStruct(e.shape, F32),
        compiler_params=_cparams("parallel", "parallel"),
        name="s5_state",
    )(e, la, lb)


def _s5_out_kernel(u_ref, t_ref, x_ref, q_ref, y_ref):
    d = pl.program_id(1)
    y = _dot(u_ref[...], t_ref[...]) + _dot(x_ref[...].astype(BF16), q_ref[...])

    @pl.when(d == 0)
    def _():
        y_ref[...] = y

    @pl.when(d != 0)
    def _():
        y_ref[...] += y


def _s5_out(ug, toe, xin, qin):
    g, rows, kc = ug.shape
    n2 = qin.shape[2]
    return pl.pallas_call(
        _s5_out_kernel,
        grid=(g, 2),
        in_specs=[
            pl.BlockSpec((None, rows, kc), lambda gi, d: (gi, 0, 0)),
            pl.BlockSpec((None, None, kc, kc), lambda gi, d: (d, gi, 0, 0)),
            pl.BlockSpec((None, rows, n2), lambda gi, d: (d, 0, gi)),
            pl.BlockSpec((None, None, n2, kc), lambda gi, d: (d, gi, 0, 0)),
        ],
        out_specs=pl.BlockSpec((None, rows, kc), lambda gi, d: (gi, 0, 0)),
        out_shape=jax.ShapeDtypeStruct((g, rows, kc), F32),
        compiler_params=_cparams("parallel", "arbitrary"),
        name="s5_out",
    )(ug, toe, xin, qin)


def _s5_mix(u, weights, nctc16):
    toe, pout, qin, la, lb = weights
    b, tt, w = u.shape
    g = w // S5_GROUP
    nch = tt // S5_CHUNK
    kc = S5_CHUNK * S5_GROUP
    ug = u.reshape(b, nch, S5_CHUNK, g, S5_GROUP)
    ug = jnp.transpose(ug, (3, 1, 0, 2, 4)).reshape(g, nch * b, kc).astype(BF16)
    e = _s5_local(ug, pout.astype(BF16))
    fold = S5_STATE_ROWS // b
    fl = e.shape[2] // fold
    coef = lambda t: jnp.tile(t.reshape(2, fold, fl), (1, b, 1))
    xin = _s5_state(e.reshape(2, nch, S5_STATE_ROWS, fl), coef(la), coef(lb), nctc16, nch)
    xin = xin.reshape(2, nch * b, e.shape[2])
    ys = _s5_out(ug, toe.astype(BF16), xin, qin.astype(BF16))
    ys = ys.reshape(g, nch, b, S5_CHUNK, S5_GROUP)
    return jnp.transpose(ys, (2, 1, 3, 0, 4)).reshape(b, tt, w)


def _mixout_kernel(x_ref, yf_ref, yr_ref, g_ref, bo_ref, ys_ref, u_ref, lnw_ref, lnb_ref, bd_ref,
                   dsk_ref, gluw_ref, glub_ref, wout_ref, gate_ref, o_ref):
    bd = bd_ref[...]
    inv = 1.0 / HEAD
    rw = None
    for d, y_ref in enumerate((yf_ref, yr_ref)):
        y = y_ref[...]
        mean = _dot32(y, bd) * inv
        yc = y - mean
        var = _dot32(yc * yc, bd) * inv
        yn = yc * lax.rsqrt(var + GN_EPS) * lnw_ref[...] + lnb_ref[...]
        o = (yn + bo_ref[d]) * g_ref[d]
        rw = o if rw is None else rw + o
    u = u_ref[...]
    ss = ys_ref[...] + dsk_ref[...] * u
    ss = jax.nn.gelu(ss)
    ss = ss * jax.nn.sigmoid(_dot(ss.astype(BF16), gluw_ref[...]) + glub_ref[...])
    w = rw.shape[1]
    mix = _dot(rw.astype(BF16), wout_ref[0:w, :]) + _dot(ss.astype(BF16), wout_ref[w:, :])
    o_ref[...] = x_ref[...] + gate_ref[...] * mix


def _mixout(xcat, yf, yr, g, bo, ys, p, ln_w, ln_b, bd, d_skip, glu_w, glu_b, w_out, gate, nct, t0):
    b, tt, d = xcat.shape
    w = ln_w.shape[1]
    sw = ys.shape[2]
    ublk = (p.shape[2] - sw) // sw
    nt = tt // TM - t0
    full = lambda *s: pl.BlockSpec(s, lambda bi, i: (0,) * len(s))
    tok = pl.BlockSpec((None, TM, w), lambda bi, i: (bi, i + t0, 0))
    tok2 = pl.BlockSpec((2, None, TM, w), lambda bi, i: (0, bi, i + t0, 0))
    return pl.pallas_call(
        _mixout_kernel,
        grid=(b, nt),
        in_specs=[
            pl.BlockSpec((None, TM, d), lambda bi, i: (bi, i + t0, 0)),
            tok, tok, tok2, tok2,
            pl.BlockSpec((None, TM, sw), lambda bi, i: (bi, i + t0, 0)),
            pl.BlockSpec((None, TM, sw), lambda bi, i: (bi, i + t0, ublk)),
            full(1, w), full(1, w), full(w, w), full(1, sw), full(sw, sw), full(1, sw),
            full(w + sw, d),
            pl.BlockSpec((None, None, 1, d), lambda bi, i: (bi, jnp.where(i + t0 < nct, 0, 1), 0, 0)),
        ],
        out_specs=pl.BlockSpec((None, TM, d), lambda bi, i: (bi, i, 0)),
        out_shape=jax.ShapeDtypeStruct((b, nt * TM, d), F32),
        compiler_params=_cparams("parallel", "parallel"),
        name="mix_out",
    )(xcat, yf, yr, g, bo, ys, p, ln_w, ln_b, bd, d_skip, glu_w, glu_b, w_out, gate)


def _ffn_kernel(x_ref, g_ref, sh_ref, sc_ref, gate_ref, wg_ref, wu_ref, wd_ref, o_ref):
    x = x_ref[...]
    h = _norm_mod(x, g_ref[...], sh_ref[...], sc_ref[...]).astype(BF16)
    a = _dot(h, wg_ref[...])
    a = a * jax.nn.sigmoid(a) * _dot(h, wu_ref[...])
    o_ref[...] = x + gate_ref[...] * _dot(a.astype(BF16), wd_ref[...])


def _ffn(xcat, g, shift, scale, gate, wg, wu, wd, nct):
    b, tt, d = xcat.shape
    ff = wg.shape[1]
    kind = lambda bi, i: (bi, jnp.where(i < nct, 0, 1), 0, 0)
    mod = pl.BlockSpec((None, None, 1, d), kind)
    return pl.pallas_call(
        _ffn_kernel,
        grid=(b, tt // TM),
        in_specs=[
            pl.BlockSpec((None, TM, d), lambda bi, i: (bi, i, 0)),
            pl.BlockSpec((1, d), lambda bi, i: (0, 0)),
            mod, mod, mod,
            pl.BlockSpec((d, ff), lambda bi, i: (0, 0)),
            pl.BlockSpec((d, ff), lambda bi, i: (0, 0)),
            pl.BlockSpec((ff, d), lambda bi, i: (0, 0)),
        ],
        out_specs=pl.BlockSpec((None, TM, d), lambda bi, i: (bi, i, 0)),
        out_shape=jax.ShapeDtypeStruct((b, tt, d), F32),
        compiler_params=_cparams("parallel", "parallel"),
        name="ffn",
    )(xcat, g, shift, scale, gate, wg, wu, wd)


MOE_TM = 1024
MOE_TF = 512
ROUTER_LANES = 128


def _moe_kernel(ne, x_ref, g_ref, sh_ref, sc_ref, gate_ref, rt_ref, wg_ref, wu_ref, wd_ref, fg_ref,
                o_ref, h_ref, cmb_ref, acc_ref):
    e = pl.program_id(1)
    j = pl.program_id(2)
    lane = lax.broadcasted_iota(jnp.int32, cmb_ref.shape, 1)

    @pl.when((e == 0) & (j == 0))
    def _():
        h = _norm_mod(x_ref[...], g_ref[...], sh_ref[...], sc_ref[...])
        h_ref[...] = h.astype(BF16)
        logits = jnp.where(lane < ne, _dot32(h, rt_ref[...]), -jnp.inf)
        m1 = jnp.max(logits, axis=-1, keepdims=True)
        lanef = lane.astype(F32)
        i1 = jnp.min(jnp.where(logits == m1, lanef, float(ROUTER_LANES)), axis=-1, keepdims=True)
        rest = jnp.where(lanef == i1, -jnp.inf, logits)
        m2 = jnp.max(rest, axis=-1, keepdims=True)
        i2 = jnp.min(jnp.where(rest == m2, lanef, float(ROUTER_LANES)), axis=-1, keepdims=True)
        e2 = jnp.exp(m2 - m1)
        p1 = 1.0 / (1.0 + e2)
        p2 = e2 / (1.0 + e2)
        cmb_ref[...] = jnp.where(lanef == i1, p1, 0.0) + jnp.where(lanef == i2, p2, 0.0)
        acc_ref[...] = jnp.zeros_like(acc_ref)

    h = h_ref[...]
    ce = jnp.sum(jnp.where(lane == e, cmb_ref[...], 0.0), axis=-1, keepdims=True)
    a = _dot(h, wg_ref[...])
    a = a * jax.nn.sigmoid(a) * _dot(h, wu_ref[...])
    acc_ref[...] += _dot((a * ce).astype(BF16), wd_ref[...])

    @pl.when((e == ne - 1) & (j == pl.num_programs(2) - 1))
    def _():
        y = x_ref[...] + gate_ref[...] * acc_ref[...]
        ms = jnp.mean(y * y, axis=-1, keepdims=True)
        o_ref[...] = y * lax.rsqrt(ms + NORM_EPS) * fg_ref[...]


def _moe(x, g, shift, scale, gate, router_pad, wg, wu, wd, final_g):
    b, l, d = x.shape
    ne, _, ff = wg.shape
    nt = l // MOE_TM
    mod = pl.BlockSpec((None, 1, d), lambda t, e, j: (t // nt, 0, 0))
    return pl.pallas_call(
        functools.partial(_moe_kernel, ne),
        grid=(b * nt, ne, ff // MOE_TF),
        in_specs=[
            pl.BlockSpec((None, MOE_TM, d), lambda t, e, j: (t // nt, t % nt, 0)),
            pl.BlockSpec((1, d), lambda t, e, j: (0, 0)),
            mod, mod, mod,
            pl.BlockSpec((d, ROUTER_LANES), lambda t, e, j: (0, 0)),
            pl.BlockSpec((None, d, MOE_TF), lambda t, e, j: (e, 0, j)),
            pl.BlockSpec((None, d, MOE_TF), lambda t, e, j: (e, 0, j)),
            pl.BlockSpec((None, MOE_TF, d), lambda t, e, j: (e, j, 0)),
            pl.BlockSpec((1, d), lambda t, e, j: (0, 0)),
        ],
        out_specs=pl.BlockSpec((None, MOE_TM, d), lambda t, e, j: (t // nt, t % nt, 0)),
        out_shape=jax.ShapeDtypeStruct((b, l, d), F32),
        scratch_shapes=[
            pltpu.VMEM((MOE_TM, d), BF16),
            pltpu.VMEM((MOE_TM, ROUTER_LANES), F32),
            pltpu.VMEM((MOE_TM, d), F32),
        ],
        compiler_params=_cparams("parallel", "arbitrary", "arbitrary"),
        name="moe",
    )(x, g, shift, scale, gate, router_pad, wg, wu, wd, final_g)


def _shift_masks(mu, ctx_len, seq_len):
    slab = mu.shape[0]
    nct = ctx_len // TM
    tt = ctx_len + seq_len
    t = jnp.arange(tt)
    is_ctx = t < ctx_len
    tl = t - ctx_len
    col = tl % GRID_W
    rows = seq_len // GRID_W
    grow = tl // GRID_W
    left = jnp.where(is_ctx, t != 0, col != 0)
    right = jnp.where(is_ctx, t != ctx_len - 1, col != GRID_W - 1)
    upv = jnp.where(is_ctx, False, grow != 0)
    dnv = jnp.where(is_ctx, False, grow != rows - 1)
    zero = jnp.zeros_like(left)
    rowmask = jnp.stack([left, right, upv, dnv, zero, zero, zero, zero], axis=-1).astype(F32)
    rowmask = rowmask.reshape(tt // TM, TM, 8)
    c = jnp.arange(slab)
    z = jnp.zeros_like(mu)
    lat = jnp.stack([mu * (c % 4 == 0), mu * (c % 4 == 1), mu * (c % 4 == 2), mu * (c % 4 == 3),
                     1.0 - mu, z, z, z])
    ctx = jnp.stack([mu * (c % 2 == 0), mu * (c % 2 == 1), z, z, 1.0 - mu, z, z, z])
    return rowmask, jnp.stack([ctx, lat]).astype(F32)


def _pad_rows(wt, total):
    r = wt.shape[1]
    z = jnp.zeros_like(wt[0])
    return jnp.stack([jnp.concatenate([wt[0], z], axis=0), jnp.concatenate([z, wt[1]], axis=0)])


def kernel(x, c, ctx, c_ctx, ada_w, ada_b, norm1_g, norm2_g, w_in, w_out, shift_mu, rwkv_w0, rwkv_w_up, rwkv_a0, rwkv_a_up, rwkv_g_up, rwkv_k_k, rwkv_k_a, rwkv_r_k, rwkv_ln_w, rwkv_ln_b, s5_lam_re, s5_lam_im, s5_log_dt, s5_b_re, s5_b_im, s5_c_re, s5_c_im, s5_d, s5_glu_w, s5_glu_b, ffn_w_gate, ffn_w_up, ffn_w_down, moe_router, moe_w_gate, moe_w_up, moe_w_down, final_g):
    b, l, d = x.shape
    ctx_len = ctx.shape[1]
    depth = ada_w.shape[0]
    slab_w = shift_mu.shape[1]
    rw_w = rwkv_k_k.shape[1]
    assert ctx_len == TM and l % TM == 0 and b + 1 <= 8
    assert rw_w % (WKV_HEADS * HEAD) == 0 and depth == 2
    nct = ctx_len // TM
    nctc = ctx_len // WKV_CHUNK
    nctc16 = ctx_len // S5_CHUNK

    act = jnp.zeros((8, d), F32).at[:b].set(c).at[b].set(c_ctx)
    mods = _ada_mod(act, ada_w, ada_b).reshape(depth, 8, 6, d)

    def mod(i, k):
        cm = jnp.broadcast_to(mods[i, b, k][None, :], (b, d))
        return jnp.stack([cm, mods[i, :b, k]], axis=1)[:, :, None, :]

    hi = lax.broadcasted_iota(jnp.int32, (rw_w, rw_w), 0) // HEAD
    hj = lax.broadcasted_iota(jnp.int32, (rw_w, rw_w), 1) // HEAD
    bd = (hi == hj).astype(F32)

    ti = lax.broadcasted_iota(jnp.int32, (TM, TM), 0)
    si = lax.broadcasted_iota(jnp.int32, (TM, TM), 1)
    same_chunk = (ti // WKV_CHUNK) == (si // WKV_CHUNK)
    tri = jnp.stack([same_chunk & (si <= ti), same_chunk & (si >= ti)]).astype(BF16)

    xcat = jnp.concatenate([ctx, x], axis=1)
    out = None
    for i in range(depth):
        last = i == depth - 1
        p = _inproj(xcat, norm1_g[i][None], mod(i, 0), mod(i, 1), w_in[i].astype(BF16), nct)
        rowmask, lanec = _shift_masks(shift_mu[i], ctx_len, l)
        v, at, rt, bg, kg, ee, g, bo = _rwkv_prep(
            p, rowmask, lanec, rwkv_k_k[i][None], rwkv_k_a[i][None], rwkv_r_k[i].reshape(1, -1),
            rwkv_w0[i], rwkv_a0[i], _pad_rows(rwkv_w_up[i], 2 * DECAY_RANK),
            _pad_rows(rwkv_a_up[i], 2 * ICL_RANK), _pad_rows(rwkv_g_up[i], 2 * GATE_RANK), bd, tri,
            nct, slab_w)
        yf, yr = _wkv_scan(v, at, rt, bg, kg, ee, nctc)
        s5w = _s5_weights(s5_lam_re[i], s5_lam_im[i], s5_log_dt[i], s5_b_re[i], s5_b_im[i],
                          s5_c_re[i], s5_c_im[i])
        ys = _s5_mix(p[:, :, slab_w:], s5w, nctc16)
        t0 = nct if last else 0
        xm = _mixout(xcat, yf, yr, g, bo, ys, p, rwkv_ln_w[i].reshape(1, -1), rwkv_ln_b[i].reshape(1, -1),
                     bd, s5_d[i][None], s5_glu_w[i].astype(BF16), s5_glu_b[i][None],
                     w_out[i].astype(BF16), mod(i, 2), nct, t0)
        if not last:
            j = i // 2
            xcat = _ffn(xm, norm2_g[i][None], mod(i, 3), mod(i, 4), mod(i, 5),
                        ffn_w_gate[j].astype(BF16), ffn_w_up[j].astype(BF16),
                        ffn_w_down[j].astype(BF16), nct)
        else:
            j = i // 2
            ne = moe_router.shape[2]
            router_pad = jnp.zeros((d, ROUTER_LANES), F32).at[:, :ne].set(moe_router[j])
            lat = lambda k: mods[i, :b, k][:, None, :]
            out = _moe(xm, norm2_g[i][None], lat(3), lat(4), lat(5), router_pad,
                       moe_w_gate[j].astype(BF16), moe_w_up[j].astype(BF16),
                       moe_w_down[j].astype(BF16), final_g[None])
    return out
```

```python
import functools
import math

import jax
import jax.numpy as jnp
from jax import lax
from jax.experimental import pallas as pl
from jax.experimental.pallas import tpu as pltpu

F32 = jnp.float32
BF16 = jnp.bfloat16
HIGHEST = lax.Precision.HIGHEST

GRID_W = 64
HEAD = 64
DECAY_RANK = 64
ICL_RANK = 64
GATE_RANK = 128
S5_GROUP = 16
S5_STATE = 64
NORM_EPS = 1e-6
GN_EPS = 64e-5
L2_EPS = 1e-12
LAM_RE_MAX = -1e-4
TOP_K = 2

TM = 256
WKV_CHUNK = 64
WKV_HEADS = 4
S5_CHUNK = 16
VMEM_LIMIT = 56 * 1024 * 1024


def _cparams(*sem):
    return pltpu.CompilerParams(dimension_semantics=sem, vmem_limit_bytes=VMEM_LIMIT)


def _dot(a, b):
    return jnp.dot(a, b, preferred_element_type=F32)


def _dot32(a, b):
    return jnp.dot(a, b, precision=HIGHEST, preferred_element_type=F32)


def _dot_nt(a, b):
    return lax.dot_general(a, b, (((1,), (1,)), ((), ())), preferred_element_type=F32)


def _dot_tn(a, b):
    return lax.dot_general(a, b, (((0,), (0,)), ((), ())), preferred_element_type=F32)


def _ada_kernel(act_ref, w_ref, b_ref, o_ref):
    a = act_ref[...]
    a = a * jax.nn.sigmoid(a)
    o_ref[...] = _dot32(a, w_ref[...]) + b_ref[...]


def _ada_mod(act, ada_w, ada_b):
    depth, d, n = ada_w.shape
    tn = 1536
    return pl.pallas_call(
        _ada_kernel,
        grid=(depth, n // tn),
        in_specs=[
            pl.BlockSpec((8, d), lambda i, j: (0, 0)),
            pl.BlockSpec((None, d, tn), lambda i, j: (i, 0, j)),
            pl.BlockSpec((None, 1, tn), lambda i, j: (i, 0, j)),
        ],
        out_specs=pl.BlockSpec((None, 8, tn), lambda i, j: (i, 0, j)),
        out_shape=jax.ShapeDtypeStruct((depth, 8, n), F32),
        compiler_params=_cparams("arbitrary", "arbitrary"),
        name="ada_mod",
    )(act, ada_w, ada_b.reshape(depth, 1, n))


def _norm_mod(x, g, shift, scale):
    ms = jnp.mean(x * x, axis=-1, keepdims=True)
    y = x * lax.rsqrt(ms + NORM_EPS) * g
    return y * (1.0 + scale) + shift


def _inproj_kernel(x_ref, g_ref, sh_ref, sc_ref, w_ref, o_ref):
    h = _norm_mod(x_ref[...], g_ref[...], sh_ref[...], sc_ref[...])
    o_ref[...] = _dot(h.astype(BF16), w_ref[...])


def _inproj(xcat, g, shift, scale, w_bf, nct):
    b, tt, d = xcat.shape
    n = w_bf.shape[1]
    kind = lambda bi, i: (bi, jnp.where(i < nct, 0, 1), 0, 0)
    return pl.pallas_call(
        _inproj_kernel,
        grid=(b, tt // TM),
        in_specs=[
            pl.BlockSpec((None, TM, d), lambda bi, i: (bi, i, 0)),
            pl.BlockSpec((1, d), lambda bi, i: (0, 0)),
            pl.BlockSpec((None, None, 1, d), kind),
            pl.BlockSpec((None, None, 1, d), kind),
            pl.BlockSpec((d, n), lambda bi, i: (0, 0)),
        ],
        out_specs=pl.BlockSpec((None, TM, n), lambda bi, i: (bi, i, 0)),
        out_shape=jax.ShapeDtypeStruct((b, tt, n), F32),
        compiler_params=_cparams("parallel", "parallel"),
        name="inproj",
    )(xcat, g, shift, scale, w_bf)


def _split3(x):
    hi = x.astype(BF16)
    r1 = x - hi.astype(F32)
    mid = r1.astype(BF16)
    lo = (r1 - mid.astype(F32)).astype(BF16)
    return hi, mid, lo


def _prep_kernel(p_ref, up_ref, dn_ref, rm_ref, lc_ref, kk_ref, ka_ref, rk_ref, w0_ref, a0_ref,
                 wup_ref, aup_ref, gup_ref, bd_ref, tri_ref,
                 v_o, at_o, rt_o, bg_o, kg_o, ee_o, g_o, bo_o):
    x = p_ref[...]
    rm = rm_ref[...]
    lc = lc_ref[...]
    prev = pltpu.roll(x, 1, 0)
    nxt = pltpu.roll(x, TM - 1, 0)
    up = jnp.concatenate([up_ref[...], x[: TM - GRID_W]], axis=0)
    dn = jnp.concatenate([x[GRID_W:], dn_ref[...]], axis=0)
    slab = (x * lc[4:5]
            + rm[:, 0:1] * (prev * lc[0:1])
            + rm[:, 1:2] * (nxt * lc[1:2])
            + rm[:, 2:3] * (up * lc[2:3])
            + rm[:, 3:4] * (dn * lc[3:4]))
    w = kk_ref.shape[1]
    r = slab[:, 0:w]
    k = slab[:, w:2 * w]
    v = slab[:, 2 * w:3 * w]
    o = 3 * w
    wd = slab[:, o:o + 2 * DECAY_RANK]
    ad = slab[:, o + 2 * DECAY_RANK:o + 2 * DECAY_RANK + 2 * ICL_RANK]
    gd = slab[:, o + 2 * DECAY_RANK + 2 * ICL_RANK:]
    bd = bd_ref[...]
    kk = k * kk_ref[...]
    nrm = jnp.sqrt(_dot32(kk * kk, bd))
    kk = kk / jnp.maximum(nrm, L2_EPS)
    v_o[...] = v.astype(BF16)
    twd = jnp.tanh(wd)
    sgd = jax.nn.sigmoid(gd)
    c = WKV_CHUNK
    for d in range(2):
        z = w0_ref[d:d + 1, :] + _dot32(twd, wup_ref[d])
        w_log = -jax.nn.softplus(-z) - 0.5
        lw = -jnp.exp(w_log)
        a = jax.nn.sigmoid(a0_ref[d:d + 1, :] + _dot32(ad, aup_ref[d]))
        kt = k * (1.0 + (a - 1.0) * ka_ref[...])
        g_o[d] = _dot32(sgd, gup_ref[d])
        bo_o[d] = _dot32(r * kt * rk_ref[...], bd) * v
        tri = tri_ref[d]
        hi, mid, lo = _split3(lw)
        lg_in = _dot(tri, hi) + _dot(tri, mid) + _dot(tri, lo)
        e_neg = jnp.exp(-lg_in)
        at_o[d] = (-kk * jnp.exp(lg_in - lw)).astype(BF16)
        rt_o[d] = (r * jnp.exp(lg_in)).astype(BF16)
        bg_o[d] = (kk * a * e_neg).astype(BF16)
        kg_o[d] = (kt * e_neg).astype(BF16)
        for ci in range(TM // c):
            last = ci * c + (c - 1 if d == 0 else 0)
            ee_o[d, ci] = jnp.exp(lg_in[last:last + 1, :])


def _rwkv_prep(p, rowmask, lanec, k_k, k_a, r_k, w0, a0, wup, aup, gup, bd, tri, nct, slab_w):
    b, tt, _ = p.shape
    w = k_k.shape[1]
    nt = tt // TM
    cpt = TM // WKV_CHUNK
    hb = TM // GRID_W
    nhb = tt // GRID_W
    full = lambda *s: pl.BlockSpec(s, lambda bi, i: (0,) * len(s))
    tok = pl.BlockSpec((None, TM, w), lambda bi, i: (bi, i, 0))
    tok2 = pl.BlockSpec((2, None, TM, w), lambda bi, i: (0, bi, i, 0))
    bf1 = jax.ShapeDtypeStruct((b, tt, w), BF16)
    bf2 = jax.ShapeDtypeStruct((2, b, tt, w), BF16)
    sh2 = jax.ShapeDtypeStruct((2, b, tt, w), F32)
    return pl.pallas_call(
        _prep_kernel,
        grid=(b, nt),
        in_specs=[
            pl.BlockSpec((None, TM, slab_w), lambda bi, i: (bi, i, 0)),
            pl.BlockSpec((None, GRID_W, slab_w), lambda bi, i: (bi, jnp.maximum(i * hb - 1, 0), 0)),
            pl.BlockSpec((None, GRID_W, slab_w),
                         lambda bi, i: (bi, jnp.minimum(i * hb + hb, nhb - 1), 0)),
            pl.BlockSpec((None, TM, 8), lambda bi, i: (i, 0, 0)),
            pl.BlockSpec((None, 8, slab_w), lambda bi, i: (jnp.where(i < nct, 0, 1), 0, 0)),
            full(1, w), full(1, w), full(1, w), full(2, w), full(2, w),
            full(2, 2 * DECAY_RANK, w), full(2, 2 * ICL_RANK, w), full(2, 2 * GATE_RANK, w),
            full(w, w), full(2, TM, TM),
        ],
        out_specs=[tok, tok2, tok2, tok2, tok2,
                   pl.BlockSpec((2, None, cpt, 1, w), lambda bi, i: (0, bi, i, 0, 0)),
                   tok2, tok2],
        out_shape=[bf1, bf2, bf2, bf2, bf2,
                   jax.ShapeDtypeStruct((2, b, tt // WKV_CHUNK, 1, w), F32), sh2, sh2],
        compiler_params=_cparams("parallel", "parallel"),
        name="rwkv_prep",
    )(p, p, p, rowmask, lanec, k_k, k_a, r_k, w0, a0, wup, aup, gup, bd, tri)


def _wkv_kernel(nb, ngrp, v_f, v_r, at_f, at_r, rt_f, rt_r, bg_f, bg_r, kg_f, kg_r, ee_f, ee_r,
                y_f, y_r, ht_ref):
    j = pl.program_id(0)
    c = WKV_CHUNK
    gw = WKV_HEADS * HEAD
    gn = WKV_HEADS * c

    @pl.when(j == 0)
    def _():
        ht_ref[...] = jnp.zeros_like(ht_ref)

    sh = int(math.log2(c))
    row = lax.broadcasted_iota(jnp.int32, (gn, gw), 0)
    col = lax.broadcasted_iota(jnp.int32, (gn, gw), 1)
    same = (row >> sh) == (col >> sh)
    tf = lax.broadcasted_iota(jnp.int32, (c, gn), 0)
    sf = lax.broadcasted_iota(jnp.int32, (c, gn), 1) & (c - 1)
    eye = (tf == sf).astype(F32)

    def stack(x):
        xb = jnp.concatenate([x.astype(BF16)] * WKV_HEADS, axis=0)
        return jnp.where(same, xb, jnp.zeros_like(xb))

    dirs = ((v_f, at_f, rt_f, bg_f, kg_f, ee_f, y_f, sf < tf, sf <= tf),
            (v_r, at_r, rt_r, bg_r, kg_r, ee_r, y_r, sf > tf, sf >= tf))
    chains = [(d, bi, q) for d in range(2) for bi in range(nb) for q in range(ngrp)]
    sl = lambda q: slice(q * gw, (q + 1) * gw)
    rd = lambda k: [dirs[d][k][bi, :, sl(q)] for d, bi, q in chains]
    cat0 = lambda xs: jnp.concatenate(xs, axis=0)
    v, at, rt, bg, kg, ee = rd(0), rd(1), rd(2), rd(3), rd(4), rd(5)
    before = [dirs[d][7] for d, _, _ in chains]
    incl = [dirs[d][8] for d, _, _ in chains]
    n_ch = range(len(chains))

    v_bd = [stack(x) for x in v]
    at_bd = [stack(x) for x in at]
    bk_bd = [cat0([stack(bg[i]), stack(kg[i])]) for i in n_ch]
    a = [_dot_nt(cat0([at[i], rt[i]]), bk_bd[i]) for i in n_ch]
    n = [jnp.where(before[i], a[i][0:c, 0:gn], 0.0) for i in n_ch]
    a_kk = [cat0([jnp.where(before[i], a[i][0:c, gn:], 0.0),
                  jnp.where(incl[i], a[i][c:, gn:], 0.0)]).astype(BF16) for i in n_ch]
    a_rb = [jnp.where(incl[i], a[i][c:, 0:gn], 0.0).astype(BF16) for i in n_ch]
    tm = [eye + x for x in n]
    pw = [_dot(x.astype(BF16), stack(x)) for x in n]
    for lvl in range(1, sh):
        pw_bd = [stack(x) for x in pw]
        if lvl < sh - 1:
            tp = [_dot(cat0([tm[i].astype(BF16), pw[i].astype(BF16)]), pw_bd[i]) for i in n_ch]
            tm = [tm[i] + tp[i][0:c] for i in n_ch]
            pw = [tp[i][c:] for i in n_ch]
        else:
            tm = [tm[i] + _dot(tm[i].astype(BF16), pw_bd[i]) for i in n_ch]
    tm_b = [x.astype(BF16) for x in tm]
    atp = [_dot(tm_b[i], at_bd[i]) for i in n_ch]
    av = [_dot(a_kk[i], v_bd[i]) for i in n_ch]
    wv = [_dot(tm_b[i], stack(av[i][0:c])) for i in n_ch]
    wv_bd = [stack(x) for x in wv]
    atp_bd = [stack(x) for x in atp]
    ar = [_dot(a_rb[i], jnp.concatenate([wv_bd[i], atp_bd[i]], axis=1)) for i in n_ch]
    y0 = [ar[i][:, 0:gw] + av[i][c:] for i in n_ch]
    rtp = [(ar[i][:, gw:] + rt[i].astype(F32)).astype(BF16) for i in n_ch]
    bge_bd = [stack(bg[i].astype(F32) * ee[i]) for i in n_ch]
    kge_bd = [stack(kg[i].astype(F32) * ee[i]) for i in n_ch]
    g = [_dot_tn(bge_bd[i], atp_bd[i]).astype(BF16) for i in n_ch]
    hloc_t = [_dot_tn(cat0([wv_bd[i], v_bd[i]]), cat0([bge_bd[i], kge_bd[i]])) for i in n_ch]
    for i, (d, bi, q) in enumerate(chains):
        ht = ht_ref[d, bi, q]
        ht_b = ht.astype(BF16)
        dirs[d][6][bi, :, sl(q)] = y0[i] + _dot_nt(rtp[i], ht_b)
        ht_ref[d, bi, q] = ht * ee[i] + _dot_nt(ht_b, g[i]) + hloc_t[i]


def _wkv_scan(v, at, rt, bg, kg, ee, nctc):
    b, tt, w = v.shape
    ntot = tt // WKV_CHUNK
    ngrp = w // (WKV_HEADS * HEAD)
    fwd = lambda j: j
    rev = lambda j: jnp.where(j < nctc, nctc - 1 - j, ntot - 1 + nctc - j)
    tok = lambda cm: pl.BlockSpec((b, WKV_CHUNK, w), lambda j: (0, cm(j), 0))
    tok2 = lambda d, cm: pl.BlockSpec((None, b, WKV_CHUNK, w), lambda j: (d, 0, cm(j), 0))
    eesp = lambda d, cm: pl.BlockSpec((None, b, None, 1, w), lambda j: (d, 0, cm(j), 0, 0))
    pair = lambda f: [f(0, fwd), f(1, rev)]
    ysh = jax.ShapeDtypeStruct((b, tt, w), F32)
    return pl.pallas_call(
        functools.partial(_wkv_kernel, b, ngrp),
        grid=(ntot,),
        in_specs=[tok(fwd), tok(rev)] + pair(tok2) + pair(tok2) + pair(tok2) + pair(tok2) + pair(eesp),
        out_specs=[tok(fwd), tok(rev)],
        out_shape=[ysh, ysh],
        scratch_shapes=[pltpu.VMEM((2, b, ngrp, WKV_HEADS * HEAD, WKV_HEADS * HEAD), F32)],
        compiler_params=_cparams("arbitrary"),
        name="wkv_scan",
    )(v, v, at, at, rt, rt, bg, bg, kg, kg, ee, ee)


def _s5_weights(lam_re, lam_im, log_dt, b_re, b_im, c_re, c_im):
    tc = S5_CHUNK
    lr = jnp.minimum(lam_re.astype(F32), LAM_RE_MAX)
    li = lam_im.astype(F32)
    dt = jnp.exp(log_dt.astype(F32))[..., None]
    mag = jnp.exp(lr * dt)
    ar = mag * jnp.cos(li * dt)
    ai = mag * jnp.sin(li * dt)
    den = lr * lr + li * li
    xr = ar - 1.0
    cr = (xr * lr + ai * li) / den
    ci = (ai * lr - xr * li) / den
    br = cr[..., None] * b_re - ci[..., None] * b_im
    bi = cr[..., None] * b_im + ci[..., None] * b_re
    pr, pi = [jnp.ones_like(ar)], [jnp.zeros_like(ar)]
    for _ in range(tc):
        pr_n = pr[-1] * ar - pi[-1] * ai
        pi_n = pr[-1] * ai + pi[-1] * ar
        pr.append(pr_n)
        pi.append(pi_n)
    pr = jnp.stack(pr)
    pi = jnp.stack(pi)
    lbr = pr[..., None] * br - pi[..., None] * bi
    lbi = pr[..., None] * bi + pi[..., None] * br
    clr = c_re * pr[:, :, :, None, :] - c_im * pi[:, :, :, None, :]
    cli = c_re * pi[:, :, :, None, :] + c_im * pr[:, :, :, None, :]
    kern = (jnp.einsum('dgop,tdgpi->tdgoi', c_re, lbr, precision=HIGHEST)
            - jnp.einsum('dgop,tdgpi->tdgoi', c_im, lbi, precision=HIGHEST))
    g = ar.shape[1]
    p = ar.shape[2]
    cg = b_re.shape[-1]
    s = jnp.arange(tc)[:, None]
    t = jnp.arange(tc)[None, :]
    toes, pouts, qins = [], [], []
    for d in range(2):
        lag = (t - s) if d == 0 else (s - t)
        valid = (lag >= 0)
        kd = kern[:, d][jnp.where(valid, lag, 0)]
        kd = jnp.where(valid[:, :, None, None, None], kd, 0.0)
        toes.append(jnp.transpose(kd, (2, 0, 4, 1, 3)).reshape(g, tc * cg, tc * cg))
        e_out = (tc - 1 - jnp.arange(tc)) if d == 0 else jnp.arange(tc)
        por = jnp.transpose(lbr[:, d][e_out], (1, 0, 3, 2)).reshape(g, tc * cg, p)
        poi = jnp.transpose(lbi[:, d][e_out], (1, 0, 3, 2)).reshape(g, tc * cg, p)
        pouts.append(jnp.concatenate([por, poi], axis=-1))
        e_in = (jnp.arange(tc) + 1) if d == 0 else (tc - jnp.arange(tc))
        qr = jnp.transpose(clr[:, d][e_in], (1, 3, 0, 2)).reshape(g, p, tc * cg)
        qi = jnp.transpose(cli[:, d][e_in], (1, 3, 0, 2)).reshape(g, p, tc * cg)
        qins.append(jnp.concatenate([qr, -qi], axis=1))
    la = jnp.concatenate([pr[tc], pr[tc]], axis=-1)
    lb = jnp.concatenate([-pi[tc], pi[tc]], axis=-1)
    return jnp.stack(toes), jnp.stack(pouts), jnp.stack(qins), la, lb


def _s5_local_kernel(u_ref, p_ref, e_ref):
    e_ref[...] = _dot(u_ref[...], p_ref[...])


def _s5_local(ug, pout):
    g, rows, kc = ug.shape
    n = pout.shape[-1]
    return pl.pallas_call(
        _s5_local_kernel,
        grid=(2, g),
        in_specs=[
            pl.BlockSpec((None, rows, kc), lambda d, gi: (gi, 0, 0)),
            pl.BlockSpec((None, None, kc, n), lambda d, gi: (d, gi, 0, 0)),
        ],
        out_specs=pl.BlockSpec((None, rows, n), lambda d, gi: (d, 0, gi)),
        out_shape=jax.ShapeDtypeStruct((2, rows, g * n), F32),
        compiler_params=_cparams("parallel", "parallel"),
        name="s5_local",
    )(ug, pout)


S5_STATE_ROWS = 8
S5_STATE_LANES = 256


def _s5_state_kernel(nctc, ntot, e_ref, la_ref, lb_ref, x_ref):
    d = pl.program_id(0)
    la = la_ref[...]
    lb = lb_ref[...]
    lane = lax.broadcasted_iota(jnp.int32, la.shape, 1)
    first_half = (lane & (2 * S5_STATE - 1)) < S5_STATE
    wl = la.shape[1]

    def body(j, x):
        rev_idx = jnp.where(j < nctc, nctc - 1 - j, ntot - 1 + nctc - j)
        c = jnp.where(d == 0, j, rev_idx)
        x_ref[c] = x
        sw = jnp.where(first_half, pltpu.roll(x, wl - S5_STATE, 1), pltpu.roll(x, S5_STATE, 1))
        return la * x + lb * sw + e_ref[c]

    lax.fori_loop(0, ntot, body, jnp.zeros(la.shape, F32))


def _s5_state(e, la, lb, nctc, ntot):
    _, nch, nr, lanes = e.shape
    wl = S5_STATE_LANES
    blk = pl.BlockSpec((None, nch, nr, wl), lambda d, i: (d, 0, 0, i))
    cf = pl.BlockSpec((None, nr, wl), lambda d, i: (d, 0, i))
    return pl.pallas_call(
        functools.partial(_s5_state_kernel, nctc, ntot),
        grid=(2, lanes // wl),
        in_specs=[blk, cf, cf],
        out_specs=blk,
        out_shape=jax.ShapeDtypeStruct(e.shape, F32),
        compiler_params=_cparams("parallel", "parallel"),
        name="s5_state",
    )(e, la, lb)


def _s5_out_kernel(u_ref, t_ref, x_ref, q_ref, y_ref):
    d = pl.program_id(1)
    y = _dot(u_ref[...], t_ref[...]) + _dot(x_ref[...].astype(BF16), q_ref[...])

    @pl.when(d == 0)
    def _():
        y_ref[...] = y

    @pl.when(d != 0)
    def _():
        y_ref[...] += y


def _s5_out(ug, toe, xin, qin):
    g, rows, kc = ug.shape
    n2 = qin.shape[2]
    return pl.pallas_call(
        _s5_out_kernel,
        grid=(g, 2),
        in_specs=[
            pl.BlockSpec((None, rows, kc), lambda gi, d: (gi, 0, 0)),
            pl.BlockSpec((None, None, kc, kc), lambda gi, d: (d, gi, 0, 0)),
            pl.BlockSpec((None, rows, n2), lambda gi, d: (d, 0, gi)),
            pl.BlockSpec((None, None, n2, kc), lambda gi, d: (d, gi, 0, 0)),
        ],
        out_specs=pl.BlockSpec((None, rows, kc), lambda gi, d: (gi, 0, 0)),
        out_shape=jax.ShapeDtypeStruct((g, rows, kc), F32),
        compiler_params=_cparams("parallel", "arbitrary"),
        name="s5_out",
    )(ug, toe, xin, qin)


def _s5_mix(u, weights, nctc16):
    toe, pout, qin, la, lb = weights
    b, tt, w = u.shape
    g = w // S5_GROUP
    nch = tt // S5_CHUNK
    kc = S5_CHUNK * S5_GROUP
    ug = u.reshape(b, nch, S5_CHUNK, g, S5_GROUP)
    ug = jnp.transpose(ug, (3, 1, 0, 2, 4)).reshape(g, nch * b, kc).astype(BF16)
    e = _s5_local(ug, pout.astype(BF16))
    fold = S5_STATE_ROWS // b
    fl = e.shape[2] // fold
    coef = lambda t: jnp.tile(t.reshape(2, fold, fl), (1, b, 1))
    xin = _s5_state(e.reshape(2, nch, S5_STATE_ROWS, fl), coef(la), coef(lb), nctc16, nch)
    xin = xin.reshape(2, nch * b, e.shape[2])
    ys = _s5_out(ug, toe.astype(BF16), xin, qin.astype(BF16))
    ys = ys.reshape(g, nch, b, S5_CHUNK, S5_GROUP)
    return jnp.transpose(ys, (2, 1, 3, 0, 4)).reshape(b, tt, w)


def _mixout_kernel(x_ref, yf_ref, yr_ref, g_ref, bo_ref, ys_ref, u_ref, lnw_ref, lnb_ref, bd_ref,
                   dsk_ref, gluw_ref, glub_ref, wout_ref, gate_ref, o_ref):
    bd = bd_ref[...]
    inv = 1.0 / HEAD
    rw = None
    for d, y_ref in enumerate((yf_ref, yr_ref)):
        y = y_ref[...]
        mean = _dot32(y, bd) * inv
        yc = y - mean
        var = _dot32(yc * yc, bd) * inv
        yn = yc * lax.rsqrt(var + GN_EPS) * lnw_ref[...] + lnb_ref[...]
        o = (yn + bo_ref[d]) * g_ref[d]
        rw = o if rw is None else rw + o
    u = u_ref[...]
    ss = ys_ref[...] + dsk_ref[...] * u
    ss = jax.nn.gelu(ss)
    ss = ss * jax.nn.sigmoid(_dot(ss.astype(BF16), gluw_ref[...]) + glub_ref[...])
    w = rw.shape[1]
    mix = _dot(rw.astype(BF16), wout_ref[0:w, :]) + _dot(ss.astype(BF16), wout_ref[w:, :])
    o_ref[...] = x_ref[...] + gate_ref[...] * mix


def _mixout(xcat, yf, yr, g, bo, ys, p, ln_w, ln_b, bd, d_skip, glu_w, glu_b, w_out, gate, nct, t0):
    b, tt, d = xcat.shape
    w = ln_w.shape[1]
    sw = ys.shape[2]
    ublk = (p.shape[2] - sw) // sw
    nt = tt // TM - t0
    full = lambda *s: pl.BlockSpec(s, lambda bi, i: (0,) * len(s))
    tok = pl.BlockSpec((None, TM, w), lambda bi, i: (bi, i + t0, 0))
    tok2 = pl.BlockSpec((2, None, TM, w), lambda bi, i: (0, bi, i + t0, 0))
    return pl.pallas_call(
        _mixout_kernel,
        grid=(b, nt),
        in_specs=[
            pl.BlockSpec((None, TM, d), lambda bi, i: (bi, i + t0, 0)),
            tok, tok, tok2, tok2,
            pl.BlockSpec((None, TM, sw), lambda bi, i: (bi, i + t0, 0)),
            pl.BlockSpec((None, TM, sw), lambda bi, i: (bi, i + t0, ublk)),
            full(1, w), full(1, w), full(w, w), full(1, sw), full(sw, sw), full(1, sw),
            full(w + sw, d),
            pl.BlockSpec((None, None, 1, d), lambda bi, i: (bi, jnp.where(i + t0 < nct, 0, 1), 0, 0)),
        ],
        out_specs=pl.BlockSpec((None, TM, d), lambda bi, i: (bi, i, 0)),
        out_shape=jax.ShapeDtypeStruct((b, nt * TM, d), F32),
        compiler_params=_cparams("parallel", "parallel"),
        name="mix_out",
    )(xcat, yf, yr, g, bo, ys, p, ln_w, ln_b, bd, d_skip, glu_w, glu_b, w_out, gate)


def _ffn_kernel(x_ref, g_ref, sh_ref, sc_ref, gate_ref, wg_ref, wu_ref, wd_ref, o_ref):
    x = x_ref[...]
    h = _norm_mod(x, g_ref[...], sh_ref[...], sc_ref[...]).astype(BF16)
    a = _dot(h, wg_ref[...])
    a = a * jax.nn.sigmoid(a) * _dot(h, wu_ref[...])
    o_ref[...] = x + gate_ref[...] * _dot(a.astype(BF16), wd_ref[...])


def _ffn(xcat, g, shift, scale, gate, wg, wu, wd, nct):
    b, tt, d = xcat.shape
    ff = wg.shape[1]
    kind = lambda bi, i: (bi, jnp.where(i < nct, 0, 1), 0, 0)
    mod = pl.BlockSpec((None, None, 1, d), kind)
    return pl.pallas_call(
        _ffn_kernel,
        grid=(b, tt // TM),
        in_specs=[
            pl.BlockSpec((None, TM, d), lambda bi, i: (bi, i, 0)),
            pl.BlockSpec((1, d), lambda bi, i: (0, 0)),
            mod, mod, mod,
            pl.BlockSpec((d, ff), lambda bi, i: (0, 0)),
            pl.BlockSpec((d, ff), lambda bi, i: (0, 0)),
            pl.BlockSpec((ff, d), lambda bi, i: (0, 0)),
        ],
        out_specs=pl.BlockSpec((None, TM, d), lambda bi, i: (bi, i, 0)),
        out_shape=jax.ShapeDtypeStruct((b, tt, d), F32),
        compiler_params=_cparams("parallel", "parallel"),
        name="ffn",
    )(xcat, g, shift, scale, gate, wg, wu, wd)


MOE_TR = 1024
MOE_TM = 2048
MOE_TF = 512
MOE_BLK = 256
MOE_SUB = 256


def _route_kernel(ne, x_ref, g_ref, sh_ref, sc_ref, rt_ref, h_o, cmb_o, cnt_o):
    h = _norm_mod(x_ref[...], g_ref[...], sh_ref[...], sc_ref[...])
    h_o[...] = h.astype(BF16)
    logits = lax.dot_general(rt_ref[...], h, (((1,), (1,)), ((), ())), precision=HIGHEST,
                             preferred_element_type=F32)
    sub = lax.broadcasted_iota(jnp.int32, logits.shape, 0).astype(F32)
    none = float(logits.shape[0])
    logits = jnp.where(sub < ne, logits, -jnp.inf)
    m1 = jnp.max(logits, axis=0, keepdims=True)
    i1 = jnp.min(jnp.where(logits == m1, sub, none), axis=0, keepdims=True)
    rest = jnp.where(sub == i1, -jnp.inf, logits)
    m2 = jnp.max(rest, axis=0, keepdims=True)
    i2 = jnp.min(jnp.where(rest == m2, sub, none), axis=0, keepdims=True)
    e2 = jnp.exp(m2 - m1)
    p1 = 1.0 / (1.0 + e2)
    p2 = e2 / (1.0 + e2)
    cmb = jnp.where(sub == i1, p1, 0.0) + jnp.where(sub == i2, p2, 0.0)
    cmb_o[...] = cmb
    cnt = jnp.sum((cmb > 0.0).astype(F32), axis=1, keepdims=True)
    cnt_o[...] = jnp.broadcast_to(cnt, cnt_o.shape).astype(jnp.int32)


def _route(x, g, shift, scale, router_t, ne):
    b, l, d = x.shape
    nr = router_t.shape[0]
    nt = l // MOE_TR
    mod = pl.BlockSpec((None, 1, d), lambda bi, i: (bi, 0, 0))
    return pl.pallas_call(
        functools.partial(_route_kernel, ne),
        grid=(b, nt),
        in_specs=[
            pl.BlockSpec((None, MOE_TR, d), lambda bi, i: (bi, i, 0)),
            pl.BlockSpec((1, d), lambda bi, i: (0, 0)),
            mod, mod,
            pl.BlockSpec((nr, d), lambda bi, i: (0, 0)),
        ],
        out_specs=[
            pl.BlockSpec((MOE_TR, d), lambda bi, i: (bi * nt + i, 0)),
            pl.BlockSpec((nr, MOE_TR), lambda bi, i: (0, bi * nt + i)),
            pl.BlockSpec((None, nr, 128), lambda bi, i: (bi * nt + i, 0, 0)),
        ],
        out_shape=[
            jax.ShapeDtypeStruct((b * l, d), BF16),
            jax.ShapeDtypeStruct((nr, b * l), F32),
            jax.ShapeDtypeStruct((b * nt, nr, 128), jnp.int32),
        ],
        compiler_params=_cparams("parallel", "parallel"),
        name="moe_route",
    )(x, g, shift, scale, router_t)


def _moe_kernel(cnt_ref, h_ref, cmb_ref, tri_ref, wg_ref, wu_ref, wd_ref, o_ref,
                pos_ref, hg_ref, ya_ref):
    t = pl.program_id(0)
    e = pl.program_id(1)
    j = pl.program_id(2)
    tm = h_ref.shape[0]
    nblk = jnp.right_shift(cnt_ref[t, e] + (MOE_BLK - 1), int(math.log2(MOE_BLK)))

    @pl.when((e == 0) & (j == 0))
    def _():
        o_ref[...] = jnp.zeros_like(o_ref)
        asg = (cmb_ref[...] > 0.0).astype(BF16)
        off = jnp.zeros((asg.shape[0], 1), F32)
        for k in range(tm // MOE_SUB):
            blk = asg[:, k * MOE_SUB:(k + 1) * MOE_SUB]
            pos_ref[:, k * MOE_SUB:(k + 1) * MOE_SUB] = _dot(blk, tri_ref[...]) + off
            off = off + jnp.sum(blk.astype(F32), axis=1, keepdims=True)

    sel = lax.broadcasted_iota(jnp.int32, pos_ref.shape, 0) == e
    posrow = jnp.sum(jnp.where(sel, pos_ref[...], 0.0), axis=0, keepdims=True)
    cwrow = jnp.sum(jnp.where(sel, cmb_ref[...], 0.0), axis=0, keepdims=True)
    rowi = lax.broadcasted_iota(jnp.int32, (MOE_BLK, tm), 0).astype(F32)

    def onehot(b):
        slot = rowi + (b * MOE_BLK).astype(F32)
        return (posrow == slot) & (cwrow > 0.0)

    @pl.when(j == 0)
    def _():
        def gather(b, carry):
            sel_b = jnp.where(onehot(b), 1.0, 0.0).astype(BF16)
            hg_ref[b] = _dot(sel_b, h_ref[...]).astype(BF16)
            ya_ref[b] = jnp.zeros(ya_ref.shape[1:], F32)
            return carry
        lax.fori_loop(0, nblk, gather, 0)

    def ffn(b, carry):
        hb = hg_ref[b]
        a = _dot(hb, wg_ref[...])
        a = a * jax.nn.sigmoid(a) * _dot(hb, wu_ref[...])
        ya_ref[b] += _dot(a.astype(BF16), wd_ref[...])
        return carry
    lax.fori_loop(0, nblk, ffn, 0)

    @pl.when(j == pl.num_programs(2) - 1)
    def _():
        def scatter(b, carry):
            wsel = jnp.where(onehot(b), cwrow, 0.0).astype(BF16)
            o_ref[...] += _dot_tn(wsel, ya_ref[b].astype(BF16))
            return carry
        lax.fori_loop(0, nblk, scatter, 0)


def _moe(h, cmb, cnt, tri, wg, wu, wd):
    n, d = h.shape
    nr = cmb.shape[0]
    ne, _, ff = wg.shape
    nbmax = MOE_TM // MOE_BLK
    grid_spec = pltpu.PrefetchScalarGridSpec(
        num_scalar_prefetch=1,
        grid=(n // MOE_TM, ne, ff // MOE_TF),
        in_specs=[
            pl.BlockSpec((MOE_TM, d), lambda t, e, j, c: (t, 0)),
            pl.BlockSpec((nr, MOE_TM), lambda t, e, j, c: (0, t)),
            pl.BlockSpec((MOE_SUB, MOE_SUB), lambda t, e, j, c: (0, 0)),
            pl.BlockSpec((None, d, MOE_TF), lambda t, e, j, c: (e, 0, j)),
            pl.BlockSpec((None, d, MOE_TF), lambda t, e, j, c: (e, 0, j)),
            pl.BlockSpec((None, MOE_TF, d), lambda t, e, j, c: (e, j, 0)),
        ],
        out_specs=pl.BlockSpec((MOE_TM, d), lambda t, e, j, c: (t, 0)),
        scratch_shapes=[
            pltpu.VMEM((nr, MOE_TM), F32),
            pltpu.VMEM((nbmax, MOE_BLK, d), BF16),
            pltpu.VMEM((nbmax, MOE_BLK, d), F32),
        ],
    )
    return pl.pallas_call(
        _moe_kernel,
        grid_spec=grid_spec,
        out_shape=jax.ShapeDtypeStruct((n, d), F32),
        compiler_params=_cparams("parallel", "arbitrary", "arbitrary"),
        name="moe",
    )(cnt, h, cmb, tri, wg, wu, wd)


def _final_kernel(x_ref, m_ref, gate_ref, fg_ref, o_ref):
    y = x_ref[...] + gate_ref[...] * m_ref[...]
    ms = jnp.mean(y * y, axis=-1, keepdims=True)
    o_ref[...] = y * lax.rsqrt(ms + NORM_EPS) * fg_ref[...]


def _final(x, m, gate, final_g):
    b, l, d = x.shape
    nt = l // MOE_TR
    return pl.pallas_call(
        _final_kernel,
        grid=(b, nt),
        in_specs=[
            pl.BlockSpec((None, MOE_TR, d), lambda bi, i: (bi, i, 0)),
            pl.BlockSpec((MOE_TR, d), lambda bi, i: (bi * nt + i, 0)),
            pl.BlockSpec((None, 1, d), lambda bi, i: (bi, 0, 0)),
            pl.BlockSpec((1, d), lambda bi, i: (0, 0)),
        ],
        out_specs=pl.BlockSpec((None, MOE_TR, d), lambda bi, i: (bi, i, 0)),
        out_shape=jax.ShapeDtypeStruct((b, l, d), F32),
        compiler_params=_cparams("parallel", "parallel"),
        name="moe_final",
    )(x, m, gate, final_g)


def _shift_masks(mu, ctx_len, seq_len):
    slab = mu.shape[0]
    nct = ctx_len // TM
    tt = ctx_len + seq_len
    t = jnp.arange(tt)
    is_ctx = t < ctx_len
    tl = t - ctx_len
    col = tl % GRID_W
    rows = seq_len // GRID_W
    grow = tl // GRID_W
    left = jnp.where(is_ctx, t != 0, col != 0)
    right = jnp.where(is_ctx, t != ctx_len - 1, col != GRID_W - 1)
    upv = jnp.where(is_ctx, False, grow != 0)
    dnv = jnp.where(is_ctx, False, grow != rows - 1)
    zero = jnp.zeros_like(left)
    rowmask = jnp.stack([left, right, upv, dnv, zero, zero, zero, zero], axis=-1).astype(F32)
    rowmask = rowmask.reshape(tt // TM, TM, 8)
    c = jnp.arange(slab)
    z = jnp.zeros_like(mu)
    lat = jnp.stack([mu * (c % 4 == 0), mu * (c % 4 == 1), mu * (c % 4 == 2), mu * (c % 4 == 3),
                     1.0 - mu, z, z, z])
    ctx = jnp.stack([mu * (c % 2 == 0), mu * (c % 2 == 1), z, z, 1.0 - mu, z, z, z])
    return rowmask, jnp.stack([ctx, lat]).astype(F32)


def _pad_rows(wt, total):
    r = wt.shape[1]
    z = jnp.zeros_like(wt[0])
    return jnp.stack([jnp.concatenate([wt[0], z], axis=0), jnp.concatenate([z, wt[1]], axis=0)])


def kernel(x, c, ctx, c_ctx, ada_w, ada_b, norm1_g, norm2_g, w_in, w_out, shift_mu, rwkv_w0, rwkv_w_up, rwkv_a0, rwkv_a_up, rwkv_g_up, rwkv_k_k, rwkv_k_a, rwkv_r_k, rwkv_ln_w, rwkv_ln_b, s5_lam_re, s5_lam_im, s5_log_dt, s5_b_re, s5_b_im, s5_c_re, s5_c_im, s5_d, s5_glu_w, s5_glu_b, ffn_w_gate, ffn_w_up, ffn_w_down, moe_router, moe_w_gate, moe_w_up, moe_w_down, final_g):
    b, l, d = x.shape
    ctx_len = ctx.shape[1]
    depth = ada_w.shape[0]
    slab_w = shift_mu.shape[1]
    rw_w = rwkv_k_k.shape[1]
    assert ctx_len == TM and l % TM == 0 and b + 1 <= 8
    assert rw_w % (WKV_HEADS * HEAD) == 0 and depth == 2
    nct = ctx_len // TM
    nctc = ctx_len // WKV_CHUNK
    nctc16 = ctx_len // S5_CHUNK

    act = jnp.zeros((8, d), F32).at[:b].set(c).at[b].set(c_ctx)
    mods = _ada_mod(act, ada_w, ada_b).reshape(depth, 8, 6, d)

    def mod(i, k):
        cm = jnp.broadcast_to(mods[i, b, k][None, :], (b, d))
        return jnp.stack([cm, mods[i, :b, k]], axis=1)[:, :, None, :]

    hi = lax.broadcasted_iota(jnp.int32, (rw_w, rw_w), 0) // HEAD
    hj = lax.broadcasted_iota(jnp.int32, (rw_w, rw_w), 1) // HEAD
    bd = (hi == hj).astype(F32)

    ti = lax.broadcasted_iota(jnp.int32, (TM, TM), 0)
    si = lax.broadcasted_iota(jnp.int32, (TM, TM), 1)
    same_chunk = (ti // WKV_CHUNK) == (si // WKV_CHUNK)
    tri = jnp.stack([same_chunk & (si <= ti), same_chunk & (si >= ti)]).astype(BF16)

    xcat = jnp.concatenate([ctx, x], axis=1)
    out = None
    for i in range(depth):
        last = i == depth - 1
        p = _inproj(xcat, norm1_g[i][None], mod(i, 0), mod(i, 1), w_in[i].astype(BF16), nct)
        rowmask, lanec = _shift_masks(shift_mu[i], ctx_len, l)
        v, at, rt, bg, kg, ee, g, bo = _rwkv_prep(
            p, rowmask, lanec, rwkv_k_k[i][None], rwkv_k_a[i][None], rwkv_r_k[i].reshape(1, -1),
            rwkv_w0[i], rwkv_a0[i], _pad_rows(rwkv_w_up[i], 2 * DECAY_RANK),
            _pad_rows(rwkv_a_up[i], 2 * ICL_RANK), _pad_rows(rwkv_g_up[i], 2 * GATE_RANK), bd, tri,
            nct, slab_w)
        yf, yr = _wkv_scan(v, at, rt, bg, kg, ee, nctc)
        s5w = _s5_weights(s5_lam_re[i], s5_lam_im[i], s5_log_dt[i], s5_b_re[i], s5_b_im[i],
                          s5_c_re[i], s5_c_im[i])
        ys = _s5_mix(p[:, :, slab_w:], s5w, nctc16)
        t0 = nct if last else 0
        xm = _mixout(xcat, yf, yr, g, bo, ys, p, rwkv_ln_w[i].reshape(1, -1), rwkv_ln_b[i].reshape(1, -1),
                     bd, s5_d[i][None], s5_glu_w[i].astype(BF16), s5_glu_b[i][None],
                     w_out[i].astype(BF16), mod(i, 2), nct, t0)
        if not last:
            j = i // 2
            xcat = _ffn(xm, norm2_g[i][None], mod(i, 3), mod(i, 4), mod(i, 5),
                        ffn_w_gate[j].astype(BF16), ffn_w_up[j].astype(BF16),
                        ffn_w_down[j].astype(BF16), nct)
        else:
            j = i // 2
            ne = moe_router.shape[2]
            nr = -(-ne // 8) * 8
            router_t = jnp.zeros((nr, d), F32).at[:ne].set(moe_router[j].T)
            lat = lambda k: mods[i, :b, k][:, None, :]
            h, cmb, cnt = _route(xm, norm2_g[i][None], lat(3), lat(4), router_t, ne)
            cnt = cnt[:, :ne, 0].reshape(-1, MOE_TM // MOE_TR, ne).sum(axis=1)
            ui = lax.broadcasted_iota(jnp.int32, (MOE_SUB, MOE_SUB), 0)
            uj = lax.broadcasted_iota(jnp.int32, (MOE_SUB, MOE_SUB), 1)
            moe = _moe(h, cmb, cnt, (ui < uj).astype(BF16), moe_w_gate[j].astype(BF16),
                       moe_w_up[j].astype(BF16), moe_w_down[j].astype(BF16))
            out = _final(xm, moe, lat(5), final_g[None])
    return out
```

```python
import functools
import math

import jax
import jax.numpy as jnp
from jax import lax
from jax.experimental import pallas as pl
from jax.experimental.pallas import tpu as pltpu

F32 = jnp.float32
BF16 = jnp.bfloat16
HIGHEST = lax.Precision.HIGHEST

GRID_W = 64
HEAD = 64
DECAY_RANK = 64
ICL_RANK = 64
GATE_RANK = 128
S5_GROUP = 16
S5_STATE = 64
NORM_EPS = 1e-6
GN_EPS = 64e-5
L2_EPS = 1e-12
LAM_RE_MAX = -1e-4
TOP_K = 2

TM = 256
WKV_CHUNK = 64
WKV_HEADS = 4
S5_CHUNK = 16
VMEM_LIMIT = 56 * 1024 * 1024


def _cparams(*sem):
    return pltpu.CompilerParams(dimension_semantics=sem, vmem_limit_bytes=VMEM_LIMIT)


def _dot(a, b):
    return jnp.dot(a, b, preferred_element_type=F32)


def _dot32(a, b):
    return jnp.dot(a, b, precision=HIGHEST, preferred_element_type=F32)


def _split2(x):
    hi = x.astype(BF16)
    return hi, (x - hi.astype(F32)).astype(BF16)


def _dot_ones(x, ones_bf):
    hi, lo = _split2(x)
    return _dot(hi, ones_bf) + _dot(lo, ones_bf)


def _dot_w2(x, w2_ref):
    hi, lo = _split2(x)
    return _dot(hi, w2_ref[0]) + _dot(lo, w2_ref[0]) + _dot(hi, w2_ref[1])


def _dot_nt(a, b):
    return lax.dot_general(a, b, (((1,), (1,)), ((), ())), preferred_element_type=F32)


def _dot_tn(a, b):
    return lax.dot_general(a, b, (((0,), (0,)), ((), ())), preferred_element_type=F32)


def _ada_kernel(act_ref, w_ref, b_ref, o_ref):
    a = act_ref[...]
    a = a * jax.nn.sigmoid(a)
    o_ref[...] = _dot32(a, w_ref[...]) + b_ref[...]


def _ada_mod(act, ada_w, ada_b):
    depth, d, n = ada_w.shape
    tn = 1536
    return pl.pallas_call(
        _ada_kernel,
        grid=(depth, n // tn),
        in_specs=[
            pl.BlockSpec((8, d), lambda i, j: (0, 0)),
            pl.BlockSpec((None, d, tn), lambda i, j: (i, 0, j)),
            pl.BlockSpec((None, 1, tn), lambda i, j: (i, 0, j)),
        ],
        out_specs=pl.BlockSpec((None, 8, tn), lambda i, j: (i, 0, j)),
        out_shape=jax.ShapeDtypeStruct((depth, 8, n), F32),
        compiler_params=_cparams("arbitrary", "arbitrary"),
        name="ada_mod",
    )(act, ada_w, ada_b.reshape(depth, 1, n))


def _norm_mod(x, g, shift, scale):
    ms = jnp.mean(x * x, axis=-1, keepdims=True)
    y = x * lax.rsqrt(ms + NORM_EPS) * g
    return y * (1.0 + scale) + shift


def _inproj_kernel(x_ref, g_ref, sh_ref, sc_ref, w_ref, o_ref):
    h = _norm_mod(x_ref[...], g_ref[...], sh_ref[...], sc_ref[...])
    o_ref[...] = _dot(h.astype(BF16), w_ref[...])


def _inproj(xcat, g, shift, scale, w_bf, nct):
    b, tt, d = xcat.shape
    n = w_bf.shape[1]
    kind = lambda bi, i: (bi, jnp.where(i < nct, 0, 1), 0, 0)
    return pl.pallas_call(
        _inproj_kernel,
        grid=(b, tt // TM),
        in_specs=[
            pl.BlockSpec((None, TM, d), lambda bi, i: (bi, i, 0)),
            pl.BlockSpec((1, d), lambda bi, i: (0, 0)),
            pl.BlockSpec((None, None, 1, d), kind),
            pl.BlockSpec((None, None, 1, d), kind),
            pl.BlockSpec((d, n), lambda bi, i: (0, 0)),
        ],
        out_specs=pl.BlockSpec((None, TM, n), lambda bi, i: (bi, i, 0)),
        out_shape=jax.ShapeDtypeStruct((b, tt, n), F32),
        compiler_params=_cparams("parallel", "parallel"),
        name="inproj",
    )(xcat, g, shift, scale, w_bf)


def _split3(x):
    hi = x.astype(BF16)
    r1 = x - hi.astype(F32)
    mid = r1.astype(BF16)
    lo = (r1 - mid.astype(F32)).astype(BF16)
    return hi, mid, lo


def _prep_kernel(p_ref, up_ref, dn_ref, rm_ref, lc_ref, kk_ref, ka_ref, rk_ref, w0_ref, a0_ref,
                 wup_ref, aup_ref, gup_ref, bd_ref, tri_ref,
                 v_o, at_o, rt_o, bg_o, kg_o, ee_o, g_o, bo_o):
    x = p_ref[...]
    rm = rm_ref[...]
    lc = lc_ref[...]
    prev = pltpu.roll(x, 1, 0)
    nxt = pltpu.roll(x, TM - 1, 0)
    up = jnp.concatenate([up_ref[...], x[: TM - GRID_W]], axis=0)
    dn = jnp.concatenate([x[GRID_W:], dn_ref[...]], axis=0)
    slab = (x * lc[4:5]
            + rm[:, 0:1] * (prev * lc[0:1])
            + rm[:, 1:2] * (nxt * lc[1:2])
            + rm[:, 2:3] * (up * lc[2:3])
            + rm[:, 3:4] * (dn * lc[3:4]))
    w = kk_ref.shape[1]
    r = slab[:, 0:w]
    k = slab[:, w:2 * w]
    v = slab[:, 2 * w:3 * w]
    o = 3 * w
    wd = slab[:, o:o + 2 * DECAY_RANK]
    ad = slab[:, o + 2 * DECAY_RANK:o + 2 * DECAY_RANK + 2 * ICL_RANK]
    gd = slab[:, o + 2 * DECAY_RANK + 2 * ICL_RANK:]
    bd = bd_ref[...]
    kk = k * kk_ref[...]
    nrm = jnp.sqrt(_dot_ones(kk * kk, bd))
    kk = kk / jnp.maximum(nrm, L2_EPS)
    v_o[...] = v.astype(BF16)
    twd = jnp.tanh(wd)
    sgd = jax.nn.sigmoid(gd)
    c = WKV_CHUNK
    for d in range(2):
        z = w0_ref[d:d + 1, :] + _dot_w2(twd, wup_ref.at[d])
        w_log = -jax.nn.softplus(-z) - 0.5
        lw = -jnp.exp(w_log)
        a = jax.nn.sigmoid(a0_ref[d:d + 1, :] + _dot_w2(ad, aup_ref.at[d]))
        kt = k * (1.0 + (a - 1.0) * ka_ref[...])
        g_o[d] = _dot_w2(sgd, gup_ref.at[d])
        bo_o[d] = _dot_ones(r * kt * rk_ref[...], bd) * v
        tri = tri_ref[d]
        hi, mid, lo = _split3(lw)
        lg_in = _dot(tri, hi) + _dot(tri, mid) + _dot(tri, lo)
        e_neg = jnp.exp(-lg_in)
        at_o[d] = (-kk * jnp.exp(lg_in - lw)).astype(BF16)
        rt_o[d] = (r * jnp.exp(lg_in)).astype(BF16)
        bg_o[d] = (kk * a * e_neg).astype(BF16)
        kg_o[d] = (kt * e_neg).astype(BF16)
        for ci in range(TM // c):
            last = ci * c + (c - 1 if d == 0 else 0)
            ee_o[d, ci] = jnp.exp(lg_in[last:last + 1, :])


def _rwkv_prep(p, rowmask, lanec, k_k, k_a, r_k, w0, a0, wup, aup, gup, bd, tri, nct, slab_w):
    b, tt, _ = p.shape
    w = k_k.shape[1]
    nt = tt // TM
    cpt = TM // WKV_CHUNK
    hb = TM // GRID_W
    nhb = tt // GRID_W
    full = lambda *s: pl.BlockSpec(s, lambda bi, i: (0,) * len(s))
    tok = pl.BlockSpec((None, TM, w), lambda bi, i: (bi, i, 0))
    tok2 = pl.BlockSpec((2, None, TM, w), lambda bi, i: (0, bi, i, 0))
    bf1 = jax.ShapeDtypeStruct((b, tt, w), BF16)
    bf2 = jax.ShapeDtypeStruct((2, b, tt, w), BF16)
    sh2 = jax.ShapeDtypeStruct((2, b, tt, w), F32)
    return pl.pallas_call(
        _prep_kernel,
        grid=(b, nt),
        in_specs=[
            pl.BlockSpec((None, TM, slab_w), lambda bi, i: (bi, i, 0)),
            pl.BlockSpec((None, GRID_W, slab_w), lambda bi, i: (bi, jnp.maximum(i * hb - 1, 0), 0)),
            pl.BlockSpec((None, GRID_W, slab_w),
                         lambda bi, i: (bi, jnp.minimum(i * hb + hb, nhb - 1), 0)),
            pl.BlockSpec((None, TM, 8), lambda bi, i: (i, 0, 0)),
            pl.BlockSpec((None, 8, slab_w), lambda bi, i: (jnp.where(i < nct, 0, 1), 0, 0)),
            full(1, w), full(1, w), full(1, w), full(2, w), full(2, w),
            full(2, 2, 2 * DECAY_RANK, w), full(2, 2, 2 * ICL_RANK, w), full(2, 2, 2 * GATE_RANK, w),
            full(w, w), full(2, TM, TM),
        ],
        out_specs=[tok, tok2, tok2, tok2, tok2,
                   pl.BlockSpec((2, None, cpt, 1, w), lambda bi, i: (0, bi, i, 0, 0)),
                   tok2, tok2],
        out_shape=[bf1, bf2, bf2, bf2, bf2,
                   jax.ShapeDtypeStruct((2, b, tt // WKV_CHUNK, 1, w), F32), sh2, sh2],
        compiler_params=_cparams("parallel", "parallel"),
        name="rwkv_prep",
    )(p, p, p, rowmask, lanec, k_k, k_a, r_k, w0, a0, wup, aup, gup, bd, tri)


def _wkv_kernel(nb, ngrp, v_f, v_r, at_f, at_r, rt_f, rt_r, bg_f, bg_r, kg_f, kg_r, ee_f, ee_r,
                y_f, y_r, ht_ref):
    j = pl.program_id(0)
    c = WKV_CHUNK
    gw = WKV_HEADS * HEAD
    gn = WKV_HEADS * c

    @pl.when(j == 0)
    def _():
        ht_ref[...] = jnp.zeros_like(ht_ref)

    sh = int(math.log2(c))
    row = lax.broadcasted_iota(jnp.int32, (gn, gw), 0)
    col = lax.broadcasted_iota(jnp.int32, (gn, gw), 1)
    same = (row >> sh) == (col >> sh)
    tf = lax.broadcasted_iota(jnp.int32, (c, gn), 0)
    sf = lax.broadcasted_iota(jnp.int32, (c, gn), 1) & (c - 1)
    eye = (tf == sf).astype(F32)

    def stack(x):
        xb = jnp.concatenate([x.astype(BF16)] * WKV_HEADS, axis=0)
        return jnp.where(same, xb, jnp.zeros_like(xb))

    dirs = ((v_f, at_f, rt_f, bg_f, kg_f, ee_f, y_f, sf < tf, sf <= tf),
            (v_r, at_r, rt_r, bg_r, kg_r, ee_r, y_r, sf > tf, sf >= tf))
    chains = [(d, bi, q) for d in range(2) for bi in range(nb) for q in range(ngrp)]
    sl = lambda q: slice(q * gw, (q + 1) * gw)
    rd = lambda k: [dirs[d][k][bi, :, sl(q)] for d, bi, q in chains]
    cat0 = lambda xs: jnp.concatenate(xs, axis=0)
    v, at, rt, bg, kg, ee = rd(0), rd(1), rd(2), rd(3), rd(4), rd(5)
    before = [dirs[d][7] for d, _, _ in chains]
    incl = [dirs[d][8] for d, _, _ in chains]
    n_ch = range(len(chains))

    v_bd = [stack(x) for x in v]
    at_bd = [stack(x) for x in at]
    bk_bd = [cat0([stack(bg[i]), stack(kg[i])]) for i in n_ch]
    a = [_dot_nt(cat0([at[i], rt[i]]), bk_bd[i]) for i in n_ch]
    n = [jnp.where(before[i], a[i][0:c, 0:gn], 0.0) for i in n_ch]
    a_kk = [cat0([jnp.where(before[i], a[i][0:c, gn:], 0.0),
                  jnp.where(incl[i], a[i][c:, gn:], 0.0)]).astype(BF16) for i in n_ch]
    a_rb = [jnp.where(incl[i], a[i][c:, 0:gn], 0.0).astype(BF16) for i in n_ch]
    tm = [eye + x for x in n]
    pw = [_dot(x.astype(BF16), stack(x)) for x in n]
    for lvl in range(1, sh):
        pw_bd = [stack(x) for x in pw]
        if lvl < sh - 1:
            tp = [_dot(cat0([tm[i].astype(BF16), pw[i].astype(BF16)]), pw_bd[i]) for i in n_ch]
            tm = [tm[i] + tp[i][0:c] for i in n_ch]
            pw = [tp[i][c:] for i in n_ch]
        else:
            tm = [tm[i] + _dot(tm[i].astype(BF16), pw_bd[i]) for i in n_ch]
    tm_b = [x.astype(BF16) for x in tm]
    atp = [_dot(tm_b[i], at_bd[i]) for i in n_ch]
    av = [_dot(a_kk[i], v_bd[i]) for i in n_ch]
    wv = [_dot(tm_b[i], stack(av[i][0:c])) for i in n_ch]
    wv_bd = [stack(x) for x in wv]
    atp_bd = [stack(x) for x in atp]
    ar = [_dot(a_rb[i], jnp.concatenate([wv_bd[i], atp_bd[i]], axis=1)) for i in n_ch]
    y0 = [ar[i][:, 0:gw] + av[i][c:] for i in n_ch]
    rtp = [(ar[i][:, gw:] + rt[i].astype(F32)).astype(BF16) for i in n_ch]
    bge_bd = [stack(bg[i].astype(F32) * ee[i]) for i in n_ch]
    kge_bd = [stack(kg[i].astype(F32) * ee[i]) for i in n_ch]
    g = [_dot_tn(bge_bd[i], atp_bd[i]).astype(BF16) for i in n_ch]
    hloc_t = [_dot_tn(cat0([wv_bd[i], v_bd[i]]), cat0([bge_bd[i], kge_bd[i]])) for i in n_ch]
    for i, (d, bi, q) in enumerate(chains):
        ht = ht_ref[d, bi, q]
        ht_b = ht.astype(BF16)
        dirs[d][6][bi, :, sl(q)] = y0[i] + _dot_nt(rtp[i], ht_b)
        ht_ref[d, bi, q] = ht * ee[i] + _dot_nt(ht_b, g[i]) + hloc_t[i]


def _wkv_scan(v, at, rt, bg, kg, ee, nctc):
    b, tt, w = v.shape
    ntot = tt // WKV_CHUNK
    ngrp = w // (WKV_HEADS * HEAD)
    fwd = lambda j: j
    rev = lambda j: jnp.where(j < nctc, nctc - 1 - j, ntot - 1 + nctc - j)
    tok = lambda cm: pl.BlockSpec((b, WKV_CHUNK, w), lambda j: (0, cm(j), 0))
    tok2 = lambda d, cm: pl.BlockSpec((None, b, WKV_CHUNK, w), lambda j: (d, 0, cm(j), 0))
    eesp = lambda d, cm: pl.BlockSpec((None, b, None, 1, w), lambda j: (d, 0, cm(j), 0, 0))
    pair = lambda f: [f(0, fwd), f(1, rev)]
    ysh = jax.ShapeDtypeStruct((b, tt, w), F32)
    return pl.pallas_call(
        functools.partial(_wkv_kernel, b, ngrp),
        grid=(ntot,),
        in_specs=[tok(fwd), tok(rev)] + pair(tok2) + pair(tok2) + pair(tok2) + pair(tok2) + pair(eesp),
        out_specs=[tok(fwd), tok(rev)],
        out_shape=[ysh, ysh],
        scratch_shapes=[pltpu.VMEM((2, b, ngrp, WKV_HEADS * HEAD, WKV_HEADS * HEAD), F32)],
        compiler_params=_cparams("arbitrary"),
        name="wkv_scan",
    )(v, v, at, at, rt, rt, bg, bg, kg, kg, ee, ee)


def _s5_weights(lam_re, lam_im, log_dt, b_re, b_im, c_re, c_im):
    tc = S5_CHUNK
    lr = jnp.minimum(lam_re.astype(F32), LAM_RE_MAX)
    li = lam_im.astype(F32)
    dt = jnp.exp(log_dt.astype(F32))[..., None]
    mag = jnp.exp(lr * dt)
    ar = mag * jnp.cos(li * dt)
    ai = mag * jnp.sin(li * dt)
    den = lr * lr + li * li
    xr = ar - 1.0
    cr = (xr * lr + ai * li) / den
    ci = (ai * lr - xr * li) / den
    br = cr[..., None] * b_re - ci[..., None] * b_im
    bi = cr[..., None] * b_im + ci[..., None] * b_re
    pr, pi = [jnp.ones_like(ar)], [jnp.zeros_like(ar)]
    for _ in range(tc):
        pr_n = pr[-1] * ar - pi[-1] * ai
        pi_n = pr[-1] * ai + pi[-1] * ar
        pr.append(pr_n)
        pi.append(pi_n)
    pr = jnp.stack(pr)
    pi = jnp.stack(pi)
    lbr = pr[..., None] * br - pi[..., None] * bi
    lbi = pr[..., None] * bi + pi[..., None] * br
    clr = c_re * pr[:, :, :, None, :] - c_im * pi[:, :, :, None, :]
    cli = c_re * pi[:, :, :, None, :] + c_im * pr[:, :, :, None, :]
    kern = (jnp.einsum('dgop,tdgpi->tdgoi', c_re, lbr, precision=HIGHEST)
            - jnp.einsum('dgop,tdgpi->tdgoi', c_im, lbi, precision=HIGHEST))
    g = ar.shape[1]
    p = ar.shape[2]
    cg = b_re.shape[-1]
    s = jnp.arange(tc)[:, None]
    t = jnp.arange(tc)[None, :]
    toes, pouts, qins = [], [], []
    for d in range(2):
        lag = (t - s) if d == 0 else (s - t)
        valid = (lag >= 0)
        kd = kern[:, d][jnp.where(valid, lag, 0)]
        kd = jnp.where(valid[:, :, None, None, None], kd, 0.0)
        toes.append(jnp.transpose(kd, (2, 0, 4, 1, 3)).reshape(g, tc * cg, tc * cg))
        e_out = (tc - 1 - jnp.arange(tc)) if d == 0 else jnp.arange(tc)
        por = jnp.transpose(lbr[:, d][e_out], (1, 0, 3, 2)).reshape(g, tc * cg, p)
        poi = jnp.transpose(lbi[:, d][e_out], (1, 0, 3, 2)).reshape(g, tc * cg, p)
        pouts.append(jnp.concatenate([por, poi], axis=-1))
        e_in = (jnp.arange(tc) + 1) if d == 0 else (tc - jnp.arange(tc))
        qr = jnp.transpose(clr[:, d][e_in], (1, 3, 0, 2)).reshape(g, p, tc * cg)
        qi = jnp.transpose(cli[:, d][e_in], (1, 3, 0, 2)).reshape(g, p, tc * cg)
        qins.append(jnp.concatenate([qr, -qi], axis=1))
    la = jnp.concatenate([pr[tc], pr[tc]], axis=-1)
    lb = jnp.concatenate([-pi[tc], pi[tc]], axis=-1)
    return jnp.stack(toes), jnp.stack(pouts), jnp.stack(qins), la, lb


def _s5_local_kernel(u_ref, p_ref, e_ref):
    e_ref[...] = _dot(u_ref[...], p_ref[...])


def _s5_local(ug, pout):
    g, rows, kc = ug.shape
    n = pout.shape[-1]
    return pl.pallas_call(
        _s5_local_kernel,
        grid=(2, g),
        in_specs=[
            pl.BlockSpec((None, rows, kc), lambda d, gi: (gi, 0, 0)),
            pl.BlockSpec((None, None, kc, n), lambda d, gi: (d, gi, 0, 0)),
        ],
        out_specs=pl.BlockSpec((None, rows, n), lambda d, gi: (d, 0, gi)),
        out_shape=jax.ShapeDtypeStruct((2, rows, g * n), F32),
        compiler_params=_cparams("parallel", "parallel"),
        name="s5_local",
    )(ug, pout)


S5_STATE_ROWS = 8
S5_STATE_LANES = 256


def _s5_state_kernel(nctc, ntot, e_ref, la_ref, lb_ref, x_ref, es_ref):
    d = pl.program_id(0)
    la = la_ref[...]
    lb = lb_ref[...]
    nr, wl = la.shape

    def swap(t):
        lane = lax.broadcasted_iota(jnp.int32, t.shape, 1)
        first_half = (lane & (2 * S5_STATE - 1)) < S5_STATE
        return jnp.where(first_half, pltpu.roll(t, wl - S5_STATE, 1), pltpu.roll(t, S5_STATE, 1))

    es_ref[...] = swap(e_ref[...].reshape(ntot * nr, wl)).reshape(ntot, nr, wl)
    lbs = swap(lb)

    def body(j, carry):
        x, xs = carry
        rev_idx = jnp.where(j < nctc, nctc - 1 - j, ntot - 1 + nctc - j)
        c = jnp.where(d == 0, j, rev_idx)
        x_ref[c] = x
        return la * x + lb * xs + e_ref[c], la * xs + lbs * x + es_ref[c]

    zero = jnp.zeros(la.shape, F32)
    lax.fori_loop(0, ntot, body, (zero, zero))


def _s5_state(e, la, lb, nctc, ntot):
    _, nch, nr, lanes = e.shape
    wl = S5_STATE_LANES
    blk = pl.BlockSpec((None, nch, nr, wl), lambda d, i: (d, 0, 0, i))
    cf = pl.BlockSpec((None, nr, wl), lambda d, i: (d, 0, i))
    return pl.pallas_call(
        functools.partial(_s5_state_kernel, nctc, ntot),
        grid=(2, lanes // wl),
        in_specs=[blk, cf, cf],
        out_specs=blk,
        out_shape=jax.ShapeDtypeStruct(e.shape, F32),
        scratch_shapes=[pltpu.VMEM((nch, nr, wl), F32)],
        compiler_params=_cparams("parallel", "parallel"),
        name="s5_state",
    )(e, la, lb)


def _s5_out_kernel(u_ref, t_ref, x_ref, q_ref, y_ref):
    d = pl.program_id(1)
    y = _dot(u_ref[...], t_ref[...]) + _dot(x_ref[...].astype(BF16), q_ref[...])

    @pl.when(d == 0)
    def _():
        y_ref[...] = y

    @pl.when(d != 0)
    def _():
        y_ref[...] += y


def _s5_out(ug, toe, xin, qin):
    g, rows, kc = ug.shape
    n2 = qin.shape[2]
    return pl.pallas_call(
        _s5_out_kernel,
        grid=(g, 2),
        in_specs=[
            pl.BlockSpec((None, rows, kc), lambda gi, d: (gi, 0, 0)),
            pl.BlockSpec((None, None, kc, kc), lambda gi, d: (d, gi, 0, 0)),
            pl.BlockSpec((None, rows, n2), lambda gi, d: (d, 0, gi)),
            pl.BlockSpec((None, None, n2, kc), lambda gi, d: (d, gi, 0, 0)),
        ],
        out_specs=pl.BlockSpec((None, rows, kc), lambda gi, d: (gi, 0, 0)),
        out_shape=jax.ShapeDtypeStruct((g, rows, kc), F32),
        compiler_params=_cparams("parallel", "arbitrary"),
        name="s5_out",
    )(ug, toe, xin, qin)


def _s5_mix(u, weights, nctc16):
    toe, pout, qin, la, lb = weights
    b, tt, w = u.shape
    g = w // S5_GROUP
    nch = tt // S5_CHUNK
    kc = S5_CHUNK * S5_GROUP
    ug = u.reshape(b, nch, S5_CHUNK, g, S5_GROUP)
    ug = jnp.transpose(ug, (3, 1, 0, 2, 4)).reshape(g, nch * b, kc).astype(BF16)
    e = _s5_local(ug, pout.astype(BF16))
    fold = S5_STATE_ROWS // b
    fl = e.shape[2] // fold
    coef = lambda t: jnp.tile(t.reshape(2, fold, fl), (1, b, 1))
    xin = _s5_state(e.reshape(2, nch, S5_STATE_ROWS, fl), coef(la), coef(lb), nctc16, nch)
    xin = xin.reshape(2, nch * b, e.shape[2])
    ys = _s5_out(ug, toe.astype(BF16), xin, qin.astype(BF16))
    ys = ys.reshape(g, nch, b, S5_CHUNK, S5_GROUP)
    return jnp.transpose(ys, (2, 1, 3, 0, 4)).reshape(b, tt, w)


def _mixout_kernel(x_ref, yf_ref, yr_ref, g_ref, bo_ref, ys_ref, u_ref, lnw_ref, lnb_ref, bd_ref,
                   dsk_ref, gluw_ref, glub_ref, wout_ref, gate_ref, o_ref):
    bd = bd_ref[...]
    inv = 1.0 / HEAD
    rw = None
    for d, y_ref in enumerate((yf_ref, yr_ref)):
        y = y_ref[...]
        mean = _dot_ones(y, bd) * inv
        yc = y - mean
        var = _dot_ones(yc * yc, bd) * inv
        yn = yc * lax.rsqrt(var + GN_EPS) * lnw_ref[...] + lnb_ref[...]
        o = (yn + bo_ref[d]) * g_ref[d]
        rw = o if rw is None else rw + o
    u = u_ref[...]
    ss = ys_ref[...] + dsk_ref[...] * u
    ss = jax.nn.gelu(ss)
    ss = ss * jax.nn.sigmoid(_dot(ss.astype(BF16), gluw_ref[...]) + glub_ref[...])
    w = rw.shape[1]
    mix = _dot(rw.astype(BF16), wout_ref[0:w, :]) + _dot(ss.astype(BF16), wout_ref[w:, :])
    o_ref[...] = x_ref[...] + gate_ref[...] * mix


def _mixout(xcat, yf, yr, g, bo, ys, p, ln_w, ln_b, bd, d_skip, glu_w, glu_b, w_out, gate, nct, t0):
    b, tt, d = xcat.shape
    w = ln_w.shape[1]
    sw = ys.shape[2]
    ublk = (p.shape[2] - sw) // sw
    nt = tt // TM - t0
    full = lambda *s: pl.BlockSpec(s, lambda bi, i: (0,) * len(s))
    tok = pl.BlockSpec((None, TM, w), lambda bi, i: (bi, i + t0, 0))
    tok2 = pl.BlockSpec((2, None, TM, w), lambda bi, i: (0, bi, i + t0, 0))
    return pl.pallas_call(
        _mixout_kernel,
        grid=(b, nt),
        in_specs=[
            pl.BlockSpec((None, TM, d), lambda bi, i: (bi, i + t0, 0)),
            tok, tok, tok2, tok2,
            pl.BlockSpec((None, TM, sw), lambda bi, i: (bi, i + t0, 0)),
            pl.BlockSpec((None, TM, sw), lambda bi, i: (bi, i + t0, ublk)),
            full(1, w), full(1, w), full(w, w), full(1, sw), full(sw, sw), full(1, sw),
            full(w + sw, d),
            pl.BlockSpec((None, None, 1, d), lambda bi, i: (bi, jnp.where(i + t0 < nct, 0, 1), 0, 0)),
        ],
        out_specs=pl.BlockSpec((None, TM, d), lambda bi, i: (bi, i, 0)),
        out_shape=jax.ShapeDtypeStruct((b, nt * TM, d), F32),
        compiler_params=_cparams("parallel", "parallel"),
        name="mix_out",
    )(xcat, yf, yr, g, bo, ys, p, ln_w, ln_b, bd, d_skip, glu_w, glu_b, w_out, gate)


def _ffn_kernel(x_ref, g_ref, sh_ref, sc_ref, gate_ref, wg_ref, wu_ref, wd_ref, o_ref):
    x = x_ref[...]
    h = _norm_mod(x, g_ref[...], sh_ref[...], sc_ref[...]).astype(BF16)
    a = _dot(h, wg_ref[...])
    a = a * jax.nn.sigmoid(a) * _dot(h, wu_ref[...])
    o_ref[...] = x + gate_ref[...] * _dot(a.astype(BF16), wd_ref[...])


def _ffn(xcat, g, shift, scale, gate, wg, wu, wd, nct):
    b, tt, d = xcat.shape
    ff = wg.shape[1]
    kind = lambda bi, i: (bi, jnp.where(i < nct, 0, 1), 0, 0)
    mod = pl.BlockSpec((None, None, 1, d), kind)
    return pl.pallas_call(
        _ffn_kernel,
        grid=(b, tt // TM),
        in_specs=[
            pl.BlockSpec((None, TM, d), lambda bi, i: (bi, i, 0)),
            pl.BlockSpec((1, d), lambda bi, i: (0, 0)),
            mod, mod, mod,
            pl.BlockSpec((d, ff), lambda bi, i: (0, 0)),
            pl.BlockSpec((d, ff), lambda bi, i: (0, 0)),
            pl.BlockSpec((ff, d), lambda bi, i: (0, 0)),
        ],
        out_specs=pl.BlockSpec((None, TM, d), lambda bi, i: (bi, i, 0)),
        out_shape=jax.ShapeDtypeStruct((b, tt, d), F32),
        compiler_params=_cparams("parallel", "parallel"),
        name="ffn",
    )(xcat, g, shift, scale, gate, wg, wu, wd)


MOE_TR = 1024
MOE_TM = 2048
MOE_TF = 512
MOE_BLK = 256
MOE_SUB = 256


def _route_kernel(ne, x_ref, g_ref, sh_ref, sc_ref, rt_ref, h_o, cmb_o, cnt_o):
    h = _norm_mod(x_ref[...], g_ref[...], sh_ref[...], sc_ref[...])
    h_o[...] = h.astype(BF16)
    logits = lax.dot_general(rt_ref[...], h, (((1,), (1,)), ((), ())), precision=HIGHEST,
                             preferred_element_type=F32)
    sub = lax.broadcasted_iota(jnp.int32, logits.shape, 0).astype(F32)
    none = float(logits.shape[0])
    logits = jnp.where(sub < ne, logits, -jnp.inf)
    m1 = jnp.max(logits, axis=0, keepdims=True)
    i1 = jnp.min(jnp.where(logits == m1, sub, none), axis=0, keepdims=True)
    rest = jnp.where(sub == i1, -jnp.inf, logits)
    m2 = jnp.max(rest, axis=0, keepdims=True)
    i2 = jnp.min(jnp.where(rest == m2, sub, none), axis=0, keepdims=True)
    e2 = jnp.exp(m2 - m1)
    p1 = 1.0 / (1.0 + e2)
    p2 = e2 / (1.0 + e2)
    cmb = jnp.where(sub == i1, p1, 0.0) + jnp.where(sub == i2, p2, 0.0)
    cmb_o[...] = cmb
    cnt = jnp.sum((cmb > 0.0).astype(F32), axis=1, keepdims=True)
    cnt_o[...] = jnp.broadcast_to(cnt, cnt_o.shape).astype(jnp.int32)


def _route(x, g, shift, scale, router_t, ne):
    b, l, d = x.shape
    nr = router_t.shape[0]
    nt = l // MOE_TR
    mod = pl.BlockSpec((None, 1, d), lambda bi, i: (bi, 0, 0))
    return pl.pallas_call(
        functools.partial(_route_kernel, ne),
        grid=(b, nt),
        in_specs=[
            pl.BlockSpec((None, MOE_TR, d), lambda bi, i: (bi, i, 0)),
            pl.BlockSpec((1, d), lambda bi, i: (0, 0)),
            mod, mod,
            pl.BlockSpec((nr, d), lambda bi, i: (0, 0)),
        ],
        out_specs=[
            pl.BlockSpec((MOE_TR, d), lambda bi, i: (bi * nt + i, 0)),
            pl.BlockSpec((nr, MOE_TR), lambda bi, i: (0, bi * nt + i)),
            pl.BlockSpec((None, nr, 128), lambda bi, i: (bi * nt + i, 0, 0)),
        ],
        out_shape=[
            jax.ShapeDtypeStruct((b * l, d), BF16),
            jax.ShapeDtypeStruct((nr, b * l), F32),
            jax.ShapeDtypeStruct((b * nt, nr, 128), jnp.int32),
        ],
        compiler_params=_cparams("parallel", "parallel"),
        name="moe_route",
    )(x, g, shift, scale, router_t)


def _moe_kernel(cnt_ref, h_ref, cmb_ref, tri_ref, wg_ref, wu_ref, wd_ref, o_ref,
                pos_ref, hg_ref, ya_ref):
    t = pl.program_id(0)
    e = pl.program_id(1)
    j = pl.program_id(2)
    tm = h_ref.shape[0]
    nblk = jnp.right_shift(cnt_ref[t, e] + (MOE_BLK - 1), int(math.log2(MOE_BLK)))

    @pl.when((e == 0) & (j == 0))
    def _():
        o_ref[...] = jnp.zeros_like(o_ref)
        asg = (cmb_ref[...] > 0.0).astype(BF16)
        off = jnp.zeros((asg.shape[0], 1), F32)
        for k in range(tm // MOE_SUB):
            blk = asg[:, k * MOE_SUB:(k + 1) * MOE_SUB]
            pos_ref[:, k * MOE_SUB:(k + 1) * MOE_SUB] = _dot(blk, tri_ref[...]) + off
            off = off + jnp.sum(blk.astype(F32), axis=1, keepdims=True)

    sel = lax.broadcasted_iota(jnp.int32, pos_ref.shape, 0) == e
    posrow = jnp.sum(jnp.where(sel, pos_ref[...], 0.0), axis=0, keepdims=True)
    cwrow = jnp.sum(jnp.where(sel, cmb_ref[...], 0.0), axis=0, keepdims=True)
    rowi = lax.broadcasted_iota(jnp.int32, (MOE_BLK, tm), 0).astype(F32)

    def onehot(b):
        slot = rowi + (b * MOE_BLK).astype(F32)
        return (posrow == slot) & (cwrow > 0.0)

    @pl.when(j == 0)
    def _():
        def gather(b, carry):
            sel_b = jnp.where(onehot(b), 1.0, 0.0).astype(BF16)
            hg_ref[b] = _dot(sel_b, h_ref[...]).astype(BF16)
            ya_ref[b] = jnp.zeros(ya_ref.shape[1:], F32)
            return carry
        lax.fori_loop(0, nblk, gather, 0)

    def ffn(b, carry):
        hb = hg_ref[b]
        a = _dot(hb, wg_ref[...])
        a = a * jax.nn.sigmoid(a) * _dot(hb, wu_ref[...])
        ya_ref[b] += _dot(a.astype(BF16), wd_ref[...])
        return carry
    lax.fori_loop(0, nblk, ffn, 0)

    @pl.when(j == pl.num_programs(2) - 1)
    def _():
        def scatter(b, carry):
            wsel = jnp.where(onehot(b), cwrow, 0.0).astype(BF16)
            o_ref[...] += _dot_tn(wsel, ya_ref[b].astype(BF16))
            return carry
        lax.fori_loop(0, nblk, scatter, 0)


def _moe(h, cmb, cnt, tri, wg, wu, wd):
    n, d = h.shape
    nr = cmb.shape[0]
    ne, _, ff = wg.shape
    nbmax = MOE_TM // MOE_BLK
    grid_spec = pltpu.PrefetchScalarGridSpec(
        num_scalar_prefetch=1,
        grid=(n // MOE_TM, ne, ff // MOE_TF),
        in_specs=[
            pl.BlockSpec((MOE_TM, d), lambda t, e, j, c: (t, 0)),
            pl.BlockSpec((nr, MOE_TM), lambda t, e, j, c: (0, t)),
            pl.BlockSpec((MOE_SUB, MOE_SUB), lambda t, e, j, c: (0, 0)),
            pl.BlockSpec((None, d, MOE_TF), lambda t, e, j, c: (e, 0, j)),
            pl.BlockSpec((None, d, MOE_TF), lambda t, e, j, c: (e, 0, j)),
            pl.BlockSpec((None, MOE_TF, d), lambda t, e, j, c: (e, j, 0)),
        ],
        out_specs=pl.BlockSpec((MOE_TM, d), lambda t, e, j, c: (t, 0)),
        scratch_shapes=[
            pltpu.VMEM((nr, MOE_TM), F32),
            pltpu.VMEM((nbmax, MOE_BLK, d), BF16),
            pltpu.VMEM((nbmax, MOE_BLK, d), F32),
        ],
    )
    return pl.pallas_call(
        _moe_kernel,
        grid_spec=grid_spec,
        out_shape=jax.ShapeDtypeStruct((n, d), F32),
        compiler_params=_cparams("parallel", "arbitrary", "arbitrary"),
        name="moe",
    )(cnt, h, cmb, tri, wg, wu, wd)


def _final_kernel(x_ref, m_ref, gate_ref, fg_ref, o_ref):
    y = x_ref[...] + gate_ref[...] * m_ref[...]
    ms = jnp.mean(y * y, axis=-1, keepdims=True)
    o_ref[...] = y * lax.rsqrt(ms + NORM_EPS) * fg_ref[...]


def _final(x, m, gate, final_g):
    b, l, d = x.shape
    nt = l // MOE_TR
    return pl.pallas_call(
        _final_kernel,
        grid=(b, nt),
        in_specs=[
            pl.BlockSpec((None, MOE_TR, d), lambda bi, i: (bi, i, 0)),
            pl.BlockSpec((MOE_TR, d), lambda bi, i: (bi * nt + i, 0)),
            pl.BlockSpec((None, 1, d), lambda bi, i: (bi, 0, 0)),
            pl.BlockSpec((1, d), lambda bi, i: (0, 0)),
        ],
        out_specs=pl.BlockSpec((None, MOE_TR, d), lambda bi, i: (bi, i, 0)),
        out_shape=jax.ShapeDtypeStruct((b, l, d), F32),
        compiler_params=_cparams("parallel", "parallel"),
        name="moe_final",
    )(x, m, gate, final_g)


def _shift_masks(mu, ctx_len, seq_len):
    slab = mu.shape[0]
    nct = ctx_len // TM
    tt = ctx_len + seq_len
    t = jnp.arange(tt)
    is_ctx = t < ctx_len
    tl = t - ctx_len
    col = tl % GRID_W
    rows = seq_len // GRID_W
    grow = tl // GRID_W
    left = jnp.where(is_ctx, t != 0, col != 0)
    right = jnp.where(is_ctx, t != ctx_len - 1, col != GRID_W - 1)
    upv = jnp.where(is_ctx, False, grow != 0)
    dnv = jnp.where(is_ctx, False, grow != rows - 1)
    zero = jnp.zeros_like(left)
    rowmask = jnp.stack([left, right, upv, dnv, zero, zero, zero, zero], axis=-1).astype(F32)
    rowmask = rowmask.reshape(tt // TM, TM, 8)
    c = jnp.arange(slab)
    z = jnp.zeros_like(mu)
    lat = jnp.stack([mu * (c % 4 == 0), mu * (c % 4 == 1), mu * (c % 4 == 2), mu * (c % 4 == 3),
                     1.0 - mu, z, z, z])
    ctx = jnp.stack([mu * (c % 2 == 0), mu * (c % 2 == 1), z, z, 1.0 - mu, z, z, z])
    return rowmask, jnp.stack([ctx, lat]).astype(F32)


def _pad_rows(wt):
    z = jnp.zeros_like(wt[0])
    wp = jnp.stack([jnp.concatenate([wt[0], z], axis=0), jnp.concatenate([z, wt[1]], axis=0)])
    hi = wp.astype(BF16)
    lo = (wp - hi.astype(F32)).astype(BF16)
    return jnp.stack([hi, lo], axis=1)


def kernel(x, c, ctx, c_ctx, ada_w, ada_b, norm1_g, norm2_g, w_in, w_out, shift_mu, rwkv_w0, rwkv_w_up, rwkv_a0, rwkv_a_up, rwkv_g_up, rwkv_k_k, rwkv_k_a, rwkv_r_k, rwkv_ln_w, rwkv_ln_b, s5_lam_re, s5_lam_im, s5_log_dt, s5_b_re, s5_b_im, s5_c_re, s5_c_im, s5_d, s5_glu_w, s5_glu_b, ffn_w_gate, ffn_w_up, ffn_w_down, moe_router, moe_w_gate, moe_w_up, moe_w_down, final_g):
    b, l, d = x.shape
    ctx_len = ctx.shape[1]
    depth = ada_w.shape[0]
    slab_w = shift_mu.shape[1]
    rw_w = rwkv_k_k.shape[1]
    assert ctx_len == TM and l % TM == 0 and b + 1 <= 8
    assert rw_w % (WKV_HEADS * HEAD) == 0 and depth == 2
    nct = ctx_len // TM
    nctc = ctx_len // WKV_CHUNK
    nctc16 = ctx_len // S5_CHUNK

    act = jnp.zeros((8, d), F32).at[:b].set(c).at[b].set(c_ctx)
    mods = _ada_mod(act, ada_w, ada_b).reshape(depth, 8, 6, d)

    def mod(i, k):
        cm = jnp.broadcast_to(mods[i, b, k][None, :], (b, d))
        return jnp.stack([cm, mods[i, :b, k]], axis=1)[:, :, None, :]

    hi = lax.broadcasted_iota(jnp.int32, (rw_w, rw_w), 0) // HEAD
    hj = lax.broadcasted_iota(jnp.int32, (rw_w, rw_w), 1) // HEAD
    bd = (hi == hj).astype(BF16)

    ti = lax.broadcasted_iota(jnp.int32, (TM, TM), 0)
    si = lax.broadcasted_iota(jnp.int32, (TM, TM), 1)
    same_chunk = (ti // WKV_CHUNK) == (si // WKV_CHUNK)
    tri = jnp.stack([same_chunk & (si <= ti), same_chunk & (si >= ti)]).astype(BF16)

    xcat = jnp.concatenate([ctx, x], axis=1)
    out = None
    for i in range(depth):
        last = i == depth - 1
        p = _inproj(xcat, norm1_g[i][None], mod(i, 0), mod(i, 1), w_in[i].astype(BF16), nct)
        rowmask, lanec = _shift_masks(shift_mu[i], ctx_len, l)
        v, at, rt, bg, kg, ee, g, bo = _rwkv_prep(
            p, rowmask, lanec, rwkv_k_k[i][None], rwkv_k_a[i][None], rwkv_r_k[i].reshape(1, -1),
            rwkv_w0[i], rwkv_a0[i], _pad_rows(rwkv_w_up[i]), _pad_rows(rwkv_a_up[i]),
            _pad_rows(rwkv_g_up[i]), bd, tri, nct, slab_w)
        yf, yr = _wkv_scan(v, at, rt, bg, kg, ee, nctc)
        s5w = _s5_weights(s5_lam_re[i], s5_lam_im[i], s5_log_dt[i], s5_b_re[i], s5_b_im[i],
                          s5_c_re[i], s5_c_im[i])
        ys = _s5_mix(p[:, :, slab_w:], s5w, nctc16)
        t0 = nct if last else 0
        xm = _mixout(xcat, yf, yr, g, bo, ys, p, rwkv_ln_w[i].reshape(1, -1), rwkv_ln_b[i].reshape(1, -1),
                     bd, s5_d[i][None], s5_glu_w[i].astype(BF16), s5_glu_b[i][None],
                     w_out[i].astype(BF16), mod(i, 2), nct, t0)
        if not last:
            j = i // 2
            xcat = _ffn(xm, norm2_g[i][None], mod(i, 3), mod(i, 4), mod(i, 5),
                        ffn_w_gate[j].astype(BF16), ffn_w_up[j].astype(BF16),
                        ffn_w_down[j].astype(BF16), nct)
        else:
            j = i // 2
            ne = moe_router.shape[2]
            nr = -(-ne // 8) * 8
            router_t = jnp.zeros((nr, d), F32).at[:ne].set(moe_router[j].T)
            lat = lambda k: mods[i, :b, k][:, None, :]
            h, cmb, cnt = _route(xm, norm2_g[i][None], lat(3), lat(4), router_t, ne)
            cnt = cnt[:, :ne, 0].reshape(-1, MOE_TM // MOE_TR, ne).sum(axis=1)
            ui = lax.broadcasted_iota(jnp.int32, (MOE_SUB, MOE_SUB), 0)
            uj = lax.broadcasted_iota(jnp.int32, (MOE_SUB, MOE_SUB), 1)
            moe = _moe(h, cmb, cnt, (ui < uj).astype(BF16), moe_w_gate[j].astype(BF16),
                       moe_w_up[j].astype(BF16), moe_w_down[j].astype(BF16))
            out = _final(xm, moe, lat(5), final_g[None])
    return out
```

```python
import functools
import math

import jax
import jax.numpy as jnp
from jax import lax
from jax.experimental import pallas as pl
from jax.experimental.pallas import tpu as pltpu

F32 = jnp.float32
BF16 = jnp.bfloat16
HIGHEST = lax.Precision.HIGHEST

GRID_W = 64
HEAD = 64
DECAY_RANK = 64
ICL_RANK = 64
GATE_RANK = 128
S5_GROUP = 16
S5_STATE = 64
NORM_EPS = 1e-6
GN_EPS = 64e-5
L2_EPS = 1e-12
LAM_RE_MAX = -1e-4
TOP_K = 2

TM = 256
WKV_CHUNK = 64
WKV_HEADS = 4
S5_CHUNK = 16
S5_OCT = 128
VMEM_LIMIT = 56 * 1024 * 1024


def _cparams(*sem):
    return pltpu.CompilerParams(dimension_semantics=sem, vmem_limit_bytes=VMEM_LIMIT)


def _dot(a, b):
    return jnp.dot(a, b, preferred_element_type=F32)


def _dot32(a, b):
    return jnp.dot(a, b, precision=HIGHEST, preferred_element_type=F32)


def _split2(x):
    hi = x.astype(BF16)
    return hi, (x - hi.astype(F32)).astype(BF16)


def _dot_ones(x, ones_bf):
    hi, lo = _split2(x)
    return _dot(hi, ones_bf) + _dot(lo, ones_bf)


def _dot_w2(x, w2_ref):
    hi, lo = _split2(x)
    return _dot(hi, w2_ref[0]) + _dot(lo, w2_ref[0]) + _dot(hi, w2_ref[1])


def _dot_nt(a, b):
    return lax.dot_general(a, b, (((1,), (1,)), ((), ())), preferred_element_type=F32)


def _dot_tn(a, b):
    return lax.dot_general(a, b, (((0,), (0,)), ((), ())), preferred_element_type=F32)


def _ada_kernel(act_ref, w_ref, b_ref, o_ref):
    a = act_ref[...]
    a = a * jax.nn.sigmoid(a)
    o_ref[...] = _dot32(a, w_ref[...]) + b_ref[...]


def _ada_mod(act, ada_w, ada_b):
    depth, d, n = ada_w.shape
    tn = 1536
    return pl.pallas_call(
        _ada_kernel,
        grid=(depth, n // tn),
        in_specs=[
            pl.BlockSpec((8, d), lambda i, j: (0, 0)),
            pl.BlockSpec((None, d, tn), lambda i, j: (i, 0, j)),
            pl.BlockSpec((None, 1, tn), lambda i, j: (i, 0, j)),
        ],
        out_specs=pl.BlockSpec((None, 8, tn), lambda i, j: (i, 0, j)),
        out_shape=jax.ShapeDtypeStruct((depth, 8, n), F32),
        compiler_params=_cparams("arbitrary", "arbitrary"),
        name="ada_mod",
    )(act, ada_w, ada_b.reshape(depth, 1, n))


def _norm_mod(x, g, shift, scale):
    ms = jnp.mean(x * x, axis=-1, keepdims=True)
    y = x * lax.rsqrt(ms + NORM_EPS) * g
    return y * (1.0 + scale) + shift


def _inproj_kernel(sw, x_ref, g_ref, sh_ref, sc_ref, w_ref, o_ref, u8_ref, us_ref):
    h = _norm_mod(x_ref[...], g_ref[...], sh_ref[...], sc_ref[...])
    p = _dot(h.astype(BF16), w_ref[...])
    o_ref[...] = p
    base = p.shape[1] - sw
    cpt = TM // S5_CHUNK
    for o8 in range(sw // S5_OCT):
        us_ref[o8] = p[:, base + o8 * S5_OCT:base + (o8 + 1) * S5_OCT]
    for o8 in range(sw // S5_OCT):
        for s in range(S5_CHUNK):
            u8_ref[o8, :, s * S5_OCT:(s + 1) * S5_OCT] = (
                us_ref[o8, pl.ds(s, cpt, stride=S5_CHUNK), :].astype(BF16))


def _inproj(xcat, g, shift, scale, w_bf, nct, sw):
    b, tt, d = xcat.shape
    n = w_bf.shape[1]
    noct = sw // S5_OCT
    cpt = TM // S5_CHUNK
    kw = S5_CHUNK * S5_OCT
    kind = lambda bi, i: (bi, jnp.where(i < nct, 0, 1), 0, 0)
    return pl.pallas_call(
        functools.partial(_inproj_kernel, sw),
        grid=(b, tt // TM),
        in_specs=[
            pl.BlockSpec((None, TM, d), lambda bi, i: (bi, i, 0)),
            pl.BlockSpec((1, d), lambda bi, i: (0, 0)),
            pl.BlockSpec((None, None, 1, d), kind),
            pl.BlockSpec((None, None, 1, d), kind),
            pl.BlockSpec((d, n), lambda bi, i: (0, 0)),
        ],
        out_specs=[pl.BlockSpec((None, TM, n), lambda bi, i: (bi, i, 0)),
                   pl.BlockSpec((noct, cpt, kw), lambda bi, i: (0, i, bi))],
        out_shape=[jax.ShapeDtypeStruct((b, tt, n), F32),
                   jax.ShapeDtypeStruct((noct, tt // S5_CHUNK, b * kw), BF16)],
        scratch_shapes=[pltpu.VMEM((noct, TM, S5_OCT), F32)],
        compiler_params=_cparams("parallel", "parallel"),
        name="inproj",
    )(xcat, g, shift, scale, w_bf)


def _split3(x):
    hi = x.astype(BF16)
    r1 = x - hi.astype(F32)
    mid = r1.astype(BF16)
    lo = (r1 - mid.astype(F32)).astype(BF16)
    return hi, mid, lo


def _prep_kernel(p_ref, up_ref, dn_ref, rm_ref, lc_ref, kk_ref, ka_ref, rk_ref, w0_ref, a0_ref,
                 wup_ref, aup_ref, gup_ref, bd_ref, tri_ref,
                 v_o, at_o, rt_o, bg_o, kg_o, ee_o, g_o, bo_o):
    x = p_ref[...]
    rm = rm_ref[...]
    lc = lc_ref[...]
    prev = pltpu.roll(x, 1, 0)
    nxt = pltpu.roll(x, TM - 1, 0)
    up = jnp.concatenate([up_ref[...], x[: TM - GRID_W]], axis=0)
    dn = jnp.concatenate([x[GRID_W:], dn_ref[...]], axis=0)
    slab = (x * lc[4:5]
            + rm[:, 0:1] * (prev * lc[0:1])
            + rm[:, 1:2] * (nxt * lc[1:2])
            + rm[:, 2:3] * (up * lc[2:3])
            + rm[:, 3:4] * (dn * lc[3:4]))
    w = kk_ref.shape[1]
    r = slab[:, 0:w]
    k = slab[:, w:2 * w]
    v = slab[:, 2 * w:3 * w]
    o = 3 * w
    wd = slab[:, o:o + 2 * DECAY_RANK]
    ad = slab[:, o + 2 * DECAY_RANK:o + 2 * DECAY_RANK + 2 * ICL_RANK]
    gd = slab[:, o + 2 * DECAY_RANK + 2 * ICL_RANK:]
    bd = bd_ref[...]
    kk = k * kk_ref[...]
    nrm = jnp.sqrt(_dot_ones(kk * kk, bd))
    kk = kk / jnp.maximum(nrm, L2_EPS)
    v_o[...] = v.astype(BF16)
    twd = jnp.tanh(wd)
    sgd = jax.nn.sigmoid(gd)
    c = WKV_CHUNK
    for d in range(2):
        z = w0_ref[d:d + 1, :] + _dot_w2(twd, wup_ref.at[d])
        w_log = -jax.nn.softplus(-z) - 0.5
        lw = -jnp.exp(w_log)
        a = jax.nn.sigmoid(a0_ref[d:d + 1, :] + _dot_w2(ad, aup_ref.at[d]))
        kt = k * (1.0 + (a - 1.0) * ka_ref[...])
        g_o[d] = _dot_w2(sgd, gup_ref.at[d])
        bo_o[d] = _dot_ones(r * kt * rk_ref[...], bd) * v
        tri = tri_ref[d]
        hi, mid, lo = _split3(lw)
        lg_in = _dot(tri, hi) + _dot(tri, mid) + _dot(tri, lo)
        e_neg = jnp.exp(-lg_in)
        at_o[d] = (-kk * jnp.exp(lg_in - lw)).astype(BF16)
        rt_o[d] = (r * jnp.exp(lg_in)).astype(BF16)
        bg_o[d] = (kk * a * e_neg).astype(BF16)
        kg_o[d] = (kt * e_neg).astype(BF16)
        for ci in range(TM // c):
            last = ci * c + (c - 1 if d == 0 else 0)
            ee_o[d, ci] = jnp.exp(lg_in[last:last + 1, :])


def _rwkv_prep(p, rowmask, lanec, k_k, k_a, r_k, w0, a0, wup, aup, gup, bd, tri, nct, slab_w):
    b, tt, _ = p.shape
    w = k_k.shape[1]
    nt = tt // TM
    cpt = TM // WKV_CHUNK
    hb = TM // GRID_W
    nhb = tt // GRID_W
    full = lambda *s: pl.BlockSpec(s, lambda bi, i: (0,) * len(s))
    tok = pl.BlockSpec((None, TM, w), lambda bi, i: (bi, i, 0))
    tok2 = pl.BlockSpec((2, None, TM, w), lambda bi, i: (0, bi, i, 0))
    bf1 = jax.ShapeDtypeStruct((b, tt, w), BF16)
    bf2 = jax.ShapeDtypeStruct((2, b, tt, w), BF16)
    sh2 = jax.ShapeDtypeStruct((2, b, tt, w), F32)
    return pl.pallas_call(
        _prep_kernel,
        grid=(b, nt),
        in_specs=[
            pl.BlockSpec((None, TM, slab_w), lambda bi, i: (bi, i, 0)),
            pl.BlockSpec((None, GRID_W, slab_w), lambda bi, i: (bi, jnp.maximum(i * hb - 1, 0), 0)),
            pl.BlockSpec((None, GRID_W, slab_w),
                         lambda bi, i: (bi, jnp.minimum(i * hb + hb, nhb - 1), 0)),
            pl.BlockSpec((None, TM, 8), lambda bi, i: (i, 0, 0)),
            pl.BlockSpec((None, 8, slab_w), lambda bi, i: (jnp.where(i < nct, 0, 1), 0, 0)),
            full(1, w), full(1, w), full(1, w), full(2, w), full(2, w),
            full(2, 2, 2 * DECAY_RANK, w), full(2, 2, 2 * ICL_RANK, w), full(2, 2, 2 * GATE_RANK, w),
            full(w, w), full(2, TM, TM),
        ],
        out_specs=[tok, tok2, tok2, tok2, tok2,
                   pl.BlockSpec((2, None, cpt, 1, w), lambda bi, i: (0, bi, i, 0, 0)),
                   tok2, tok2],
        out_shape=[bf1, bf2, bf2, bf2, bf2,
                   jax.ShapeDtypeStruct((2, b, tt // WKV_CHUNK, 1, w), F32), sh2, sh2],
        compiler_params=_cparams("parallel", "parallel"),
        name="rwkv_prep",
    )(p, p, p, rowmask, lanec, k_k, k_a, r_k, w0, a0, wup, aup, gup, bd, tri)


def _wkv_kernel(nb, ngrp, v_f, v_r, at_f, at_r, rt_f, rt_r, bg_f, bg_r, kg_f, kg_r, ee_f, ee_r,
                y_f, y_r, ht_ref):
    j = pl.program_id(0)
    c = WKV_CHUNK
    gw = WKV_HEADS * HEAD
    gn = WKV_HEADS * c

    @pl.when(j == 0)
    def _():
        ht_ref[...] = jnp.zeros_like(ht_ref)

    sh = int(math.log2(c))
    row = lax.broadcasted_iota(jnp.int32, (gn, gw), 0)
    col = lax.broadcasted_iota(jnp.int32, (gn, gw), 1)
    same = (row >> sh) == (col >> sh)
    tf = lax.broadcasted_iota(jnp.int32, (c, gn), 0)
    sf = lax.broadcasted_iota(jnp.int32, (c, gn), 1) & (c - 1)
    eye = (tf == sf).astype(F32)

    def stack(x):
        xb = jnp.concatenate([x.astype(BF16)] * WKV_HEADS, axis=0)
        return jnp.where(same, xb, jnp.zeros_like(xb))

    dirs = ((v_f, at_f, rt_f, bg_f, kg_f, ee_f, y_f, sf < tf, sf <= tf),
            (v_r, at_r, rt_r, bg_r, kg_r, ee_r, y_r, sf > tf, sf >= tf))
    chains = [(d, bi, q) for d in range(2) for bi in range(nb) for q in range(ngrp)]
    sl = lambda q: slice(q * gw, (q + 1) * gw)
    rd = lambda k: [dirs[d][k][bi, :, sl(q)] for d, bi, q in chains]
    cat0 = lambda xs: jnp.concatenate(xs, axis=0)
    v, at, rt, bg, kg, ee = rd(0), rd(1), rd(2), rd(3), rd(4), rd(5)
    before = [dirs[d][7] for d, _, _ in chains]
    incl = [dirs[d][8] for d, _, _ in chains]
    n_ch = range(len(chains))

    v_bd = [stack(x) for x in v]
    at_bd = [stack(x) for x in at]
    bk_bd = [cat0([stack(bg[i]), stack(kg[i])]) for i in n_ch]
    a = [_dot_nt(cat0([at[i], rt[i]]), bk_bd[i]) for i in n_ch]
    n = [jnp.where(before[i], a[i][0:c, 0:gn], 0.0) for i in n_ch]
    a_kk = [cat0([jnp.where(before[i], a[i][0:c, gn:], 0.0),
                  jnp.where(incl[i], a[i][c:, gn:], 0.0)]).astype(BF16) for i in n_ch]
    a_rb = [jnp.where(incl[i], a[i][c:, 0:gn], 0.0).astype(BF16) for i in n_ch]
    tm = [eye + x for x in n]
    pw = [_dot(x.astype(BF16), stack(x)) for x in n]
    for lvl in range(1, sh):
        pw_bd = [stack(x) for x in pw]
        if lvl < sh - 1:
            tp = [_dot(cat0([tm[i].astype(BF16), pw[i].astype(BF16)]), pw_bd[i]) for i in n_ch]
            tm = [tm[i] + tp[i][0:c] for i in n_ch]
            pw = [tp[i][c:] for i in n_ch]
        else:
            tm = [tm[i] + _dot(tm[i].astype(BF16), pw_bd[i]) for i in n_ch]
    tm_b = [x.astype(BF16) for x in tm]
    atp = [_dot(tm_b[i], at_bd[i]) for i in n_ch]
    av = [_dot(a_kk[i], v_bd[i]) for i in n_ch]
    wv = [_dot(tm_b[i], stack(av[i][0:c])) for i in n_ch]
    wv_bd = [stack(x) for x in wv]
    atp_bd = [stack(x) for x in atp]
    ar = [_dot(a_rb[i], jnp.concatenate([wv_bd[i], atp_bd[i]], axis=1)) for i in n_ch]
    y0 = [ar[i][:, 0:gw] + av[i][c:] for i in n_ch]
    rtp = [(ar[i][:, gw:] + rt[i].astype(F32)).astype(BF16) for i in n_ch]
    bge_bd = [stack(bg[i].astype(F32) * ee[i]) for i in n_ch]
    kge_bd = [stack(kg[i].astype(F32) * ee[i]) for i in n_ch]
    g = [_dot_tn(bge_bd[i], atp_bd[i]).astype(BF16) for i in n_ch]
    hloc_t = [_dot_tn(cat0([wv_bd[i], v_bd[i]]), cat0([bge_bd[i], kge_bd[i]])) for i in n_ch]
    for i, (d, bi, q) in enumerate(chains):
        ht = ht_ref[d, bi, q]
        ht_b = ht.astype(BF16)
        dirs[d][6][bi, :, sl(q)] = y0[i] + _dot_nt(rtp[i], ht_b)
        ht_ref[d, bi, q] = ht * ee[i] + _dot_nt(ht_b, g[i]) + hloc_t[i]


def _wkv_scan(v, at, rt, bg, kg, ee, nctc):
    b, tt, w = v.shape
    ntot = tt // WKV_CHUNK
    ngrp = w // (WKV_HEADS * HEAD)
    fwd = lambda j: j
    rev = lambda j: jnp.where(j < nctc, nctc - 1 - j, ntot - 1 + nctc - j)
    tok = lambda cm: pl.BlockSpec((b, WKV_CHUNK, w), lambda j: (0, cm(j), 0))
    tok2 = lambda d, cm: pl.BlockSpec((None, b, WKV_CHUNK, w), lambda j: (d, 0, cm(j), 0))
    eesp = lambda d, cm: pl.BlockSpec((None, b, None, 1, w), lambda j: (d, 0, cm(j), 0, 0))
    pair = lambda f: [f(0, fwd), f(1, rev)]
    ysh = jax.ShapeDtypeStruct((b, tt, w), F32)
    return pl.pallas_call(
        functools.partial(_wkv_kernel, b, ngrp),
        grid=(ntot,),
        in_specs=[tok(fwd), tok(rev)] + pair(tok2) + pair(tok2) + pair(tok2) + pair(tok2) + pair(eesp),
        out_specs=[tok(fwd), tok(rev)],
        out_shape=[ysh, ysh],
        scratch_shapes=[pltpu.VMEM((2, b, ngrp, WKV_HEADS * HEAD, WKV_HEADS * HEAD), F32)],
        compiler_params=_cparams("arbitrary"),
        name="wkv_scan",
    )(v, v, at, at, rt, rt, bg, bg, kg, kg, ee, ee)


def _s5_weights(lam_re, lam_im, log_dt, b_re, b_im, c_re, c_im):
    tc = S5_CHUNK
    lr = jnp.minimum(lam_re.astype(F32), LAM_RE_MAX)
    li = lam_im.astype(F32)
    dt = jnp.exp(log_dt.astype(F32))[..., None]
    mag = jnp.exp(lr * dt)
    ar = mag * jnp.cos(li * dt)
    ai = mag * jnp.sin(li * dt)
    den = lr * lr + li * li
    xr = ar - 1.0
    cr = (xr * lr + ai * li) / den
    ci = (ai * lr - xr * li) / den
    br = cr[..., None] * b_re - ci[..., None] * b_im
    bi = cr[..., None] * b_im + ci[..., None] * b_re
    pr, pi = [jnp.ones_like(ar)], [jnp.zeros_like(ar)]
    for _ in range(tc):
        pr_n = pr[-1] * ar - pi[-1] * ai
        pi_n = pr[-1] * ai + pi[-1] * ar
        pr.append(pr_n)
        pi.append(pi_n)
    pr = jnp.stack(pr)
    pi = jnp.stack(pi)
    lbr = pr[..., None] * br - pi[..., None] * bi
    lbi = pr[..., None] * bi + pi[..., None] * br
    clr = c_re * pr[:, :, :, None, :] - c_im * pi[:, :, :, None, :]
    cli = c_re * pi[:, :, :, None, :] + c_im * pr[:, :, :, None, :]
    kern = (jnp.einsum('dgop,tdgpi->tdgoi', c_re, lbr, precision=HIGHEST)
            - jnp.einsum('dgop,tdgpi->tdgoi', c_im, lbi, precision=HIGHEST))
    g = ar.shape[1]
    p = ar.shape[2]
    cg = b_re.shape[-1]
    s = jnp.arange(tc)[:, None]
    t = jnp.arange(tc)[None, :]
    toes, pouts, qins = [], [], []
    for d in range(2):
        lag = (t - s) if d == 0 else (s - t)
        valid = (lag >= 0)
        kd = kern[:, d][jnp.where(valid, lag, 0)]
        kd = jnp.where(valid[:, :, None, None, None], kd, 0.0)
        toes.append(jnp.transpose(kd, (2, 0, 4, 1, 3)).reshape(g, tc * cg, tc * cg))
        e_out = (tc - 1 - jnp.arange(tc)) if d == 0 else jnp.arange(tc)
        por = jnp.transpose(lbr[:, d][e_out], (1, 0, 3, 2)).reshape(g, tc * cg, p)
        poi = jnp.transpose(lbi[:, d][e_out], (1, 0, 3, 2)).reshape(g, tc * cg, p)
        pouts.append(jnp.concatenate([por, poi], axis=-1))
        e_in = (jnp.arange(tc) + 1) if d == 0 else (tc - jnp.arange(tc))
        qr = jnp.transpose(clr[:, d][e_in], (1, 3, 0, 2)).reshape(g, p, tc * cg)
        qi = jnp.transpose(cli[:, d][e_in], (1, 3, 0, 2)).reshape(g, p, tc * cg)
        qins.append(jnp.concatenate([qr, -qi], axis=1))
    la = jnp.concatenate([pr[tc], pr[tc]], axis=-1)
    lb = jnp.concatenate([-pi[tc], pi[tc]], axis=-1)
    return jnp.stack(toes), jnp.stack(pouts), jnp.stack(qins), la, lb


def _s5_local_kernel(u_ref, p_ref, e_ref):
    e_ref[...] = _dot(u_ref[...], p_ref[...])


def _s5_local(u8, pout8, nb):
    noct, nch, _ = u8.shape
    kw, n = pout8.shape[2:]
    return pl.pallas_call(
        _s5_local_kernel,
        grid=(noct, nb, 2),
        in_specs=[
            pl.BlockSpec((None, nch, kw), lambda o, b, d: (o, 0, b)),
            pl.BlockSpec((None, None, kw, n), lambda o, b, d: (d, o, 0, 0)),
        ],
        out_specs=pl.BlockSpec((None, nch, n), lambda o, b, d: (d, 0, o * nb + b)),
        out_shape=jax.ShapeDtypeStruct((2, nch, noct * nb * n), F32),
        compiler_params=_cparams("parallel", "parallel", "parallel"),
        name="s5_local",
    )(u8, pout8)


S5_STATE_ROWS = 8
S5_STATE_LANES = 256


def _s5_state_kernel(nctc, ntot, e_ref, la_ref, lb_ref, x_ref, es_ref):
    d = pl.program_id(0)
    la = la_ref[...]
    lb = lb_ref[...]
    nr, wl = la.shape

    def swap(t):
        lane = lax.broadcasted_iota(jnp.int32, t.shape, 1)
        first_half = (lane & (2 * S5_STATE - 1)) < S5_STATE
        return jnp.where(first_half, pltpu.roll(t, wl - S5_STATE, 1), pltpu.roll(t, S5_STATE, 1))

    es_ref[...] = swap(e_ref[...].reshape(ntot * nr, wl)).reshape(ntot, nr, wl)
    lbs = swap(lb)

    def body(j, carry):
        x, xs = carry
        rev_idx = jnp.where(j < nctc, nctc - 1 - j, ntot - 1 + nctc - j)
        c = jnp.where(d == 0, j, rev_idx)
        x_ref[c] = x
        return la * x + lb * xs + e_ref[c], la * xs + lbs * x + es_ref[c]

    zero = jnp.zeros(la.shape, F32)
    lax.fori_loop(0, ntot, body, (zero, zero))


def _s5_state(e, la, lb, nctc, ntot):
    _, nch, nr, lanes = e.shape
    wl = S5_STATE_LANES
    blk = pl.BlockSpec((None, nch, nr, wl), lambda d, i: (d, 0, 0, i))
    cf = pl.BlockSpec((None, nr, wl), lambda d, i: (d, 0, i))
    return pl.pallas_call(
        functools.partial(_s5_state_kernel, nctc, ntot),
        grid=(2, lanes // wl),
        in_specs=[blk, cf, cf],
        out_specs=blk,
        out_shape=jax.ShapeDtypeStruct(e.shape, F32),
        scratch_shapes=[pltpu.VMEM((nch, nr, wl), F32)],
        compiler_params=_cparams("parallel", "parallel"),
        name="s5_state",
    )(e, la, lb)


def _s5_out_kernel(u_ref, t_ref, x_ref, q_ref, y_ref):
    d = pl.program_id(2)
    y = _dot(u_ref[...], t_ref[...]) + _dot(x_ref[...].astype(BF16), q_ref[...])

    @pl.when(d == 0)
    def _():
        y_ref[...] = y

    @pl.when(d != 0)
    def _():
        y_ref[...] += y


def _s5_out(u8, toe8, xin, qin8, nb):
    noct, nch, _ = u8.shape
    kw = toe8.shape[2]
    n2 = qin8.shape[2]
    return pl.pallas_call(
        _s5_out_kernel,
        grid=(noct, nb, 2),
        in_specs=[
            pl.BlockSpec((None, nch, kw), lambda o, b, d: (o, 0, b)),
            pl.BlockSpec((None, None, kw, kw), lambda o, b, d: (d, o, 0, 0)),
            pl.BlockSpec((None, nch, n2), lambda o, b, d: (d, 0, o * nb + b)),
            pl.BlockSpec((None, None, n2, kw), lambda o, b, d: (d, o, 0, 0)),
        ],
        out_specs=pl.BlockSpec((None, nch, kw), lambda o, b, d: (o, 0, b)),
        out_shape=jax.ShapeDtypeStruct((noct, nch, nb * kw), F32),
        compiler_params=_cparams("parallel", "parallel", "arbitrary"),
        name="s5_out",
    )(u8, toe8, xin, qin8)


def _s5_octets(toe, pout, qin):
    _, g, kc, n = pout.shape
    og = S5_OCT // S5_GROUP
    noct = g // og
    tc, cg = S5_CHUNK, S5_GROUP
    eye = jnp.eye(og, dtype=BF16)
    t7 = toe.astype(BF16).reshape(2, noct, og, tc, cg, tc, cg)
    t7 = jnp.transpose(t7, (0, 1, 3, 2, 4, 5, 6))
    toe8 = t7[:, :, :, :, :, :, None, :] * eye[None, None, None, :, None, None, :, None]
    toe8 = toe8.reshape(2, noct, tc * og * cg, tc * og * cg)
    p6 = pout.astype(BF16).reshape(2, noct, og, tc, cg, n)
    p6 = jnp.transpose(p6, (0, 1, 3, 2, 4, 5))
    pout8 = p6[:, :, :, :, :, None, :] * eye[None, None, None, :, None, :, None]
    pout8 = pout8.reshape(2, noct, tc * og * cg, og * n)
    q6 = qin.astype(BF16).reshape(2, noct, og, n, tc, cg)
    qin8 = q6[:, :, :, :, :, None, :] * eye[None, None, :, None, None, :, None]
    qin8 = qin8.reshape(2, noct, og * n, tc * og * cg)
    return toe8, pout8, qin8


def _s5_mix(u8, weights, nb, nctc16):
    toe, pout, qin, la, lb = weights
    noct, nch, _ = u8.shape
    toe8, pout8, qin8 = _s5_octets(toe, pout, qin)
    e = _s5_local(u8, pout8, nb)
    assert noct * nb == S5_STATE_ROWS
    fl = e.shape[2] // S5_STATE_ROWS
    coef = lambda t: jnp.repeat(t.reshape(2, noct, fl), nb, axis=1)
    xin = _s5_state(e.reshape(2, nch, S5_STATE_ROWS, fl), coef(la), coef(lb), nctc16, nch)
    return _s5_out(u8, toe8, xin.reshape(e.shape), qin8, nb)


def _mixout_kernel(x_ref, yf_ref, yr_ref, g_ref, bo_ref, y8_ref, u_ref, lnw_ref, lnb_ref, bd_ref,
                   dsk_ref, gluw_ref, glub_ref, wout_ref, gate_ref, o_ref, ysn_ref):
    cpt = TM // S5_CHUNK
    for o8 in range(y8_ref.shape[0]):
        for s in range(S5_CHUNK):
            ysn_ref[o8, pl.ds(s, cpt, stride=S5_CHUNK), :] = y8_ref[o8, :, s * S5_OCT:(s + 1) * S5_OCT]
    ys = jnp.concatenate([ysn_ref[o8] for o8 in range(y8_ref.shape[0])], axis=1)
    bd = bd_ref[...]
    inv = 1.0 / HEAD
    rw = None
    for d, y_ref in enumerate((yf_ref, yr_ref)):
        y = y_ref[...]
        mean = _dot_ones(y, bd) * inv
        yc = y - mean
        var = _dot_ones(yc * yc, bd) * inv
        yn = yc * lax.rsqrt(var + GN_EPS) * lnw_ref[...] + lnb_ref[...]
        o = (yn + bo_ref[d]) * g_ref[d]
        rw = o if rw is None else rw + o
    u = u_ref[...]
    ss = ys + dsk_ref[...] * u
    ss = jax.nn.gelu(ss)
    ss = ss * jax.nn.sigmoid(_dot(ss.astype(BF16), gluw_ref[...]) + glub_ref[...])
    w = rw.shape[1]
    mix = _dot(rw.astype(BF16), wout_ref[0:w, :]) + _dot(ss.astype(BF16), wout_ref[w:, :])
    o_ref[...] = x_ref[...] + gate_ref[...] * mix


def _mixout(xcat, yf, yr, g, bo, y8, p, ln_w, ln_b, bd, d_skip, glu_w, glu_b, w_out, gate, nct, t0):
    b, tt, d = xcat.shape
    w = ln_w.shape[1]
    sw = d_skip.shape[1]
    noct = y8.shape[0]
    cpt = TM // S5_CHUNK
    kw = S5_CHUNK * S5_OCT
    ublk = (p.shape[2] - sw) // sw
    nt = tt // TM - t0
    full = lambda *s: pl.BlockSpec(s, lambda bi, i: (0,) * len(s))
    tok = pl.BlockSpec((None, TM, w), lambda bi, i: (bi, i + t0, 0))
    tok2 = pl.BlockSpec((2, None, TM, w), lambda bi, i: (0, bi, i + t0, 0))
    return pl.pallas_call(
        _mixout_kernel,
        grid=(b, nt),
        in_specs=[
            pl.BlockSpec((None, TM, d), lambda bi, i: (bi, i + t0, 0)),
            tok, tok, tok2, tok2,
            pl.BlockSpec((noct, cpt, kw), lambda bi, i: (0, i + t0, bi)),
            pl.BlockSpec((None, TM, sw), lambda bi, i: (bi, i + t0, ublk)),
            full(1, w), full(1, w), full(w, w), full(1, sw), full(sw, sw), full(1, sw),
            full(w + sw, d),
            pl.BlockSpec((None, None, 1, d), lambda bi, i: (bi, jnp.where(i + t0 < nct, 0, 1), 0, 0)),
        ],
        out_specs=pl.BlockSpec((None, TM, d), lambda bi, i: (bi, i, 0)),
        out_shape=jax.ShapeDtypeStruct((b, nt * TM, d), F32),
        scratch_shapes=[pltpu.VMEM((noct, TM, S5_OCT), F32)],
        compiler_params=_cparams("parallel", "parallel"),
        name="mix_out",
    )(xcat, yf, yr, g, bo, y8, p, ln_w, ln_b, bd, d_skip, glu_w, glu_b, w_out, gate)


def _ffn_kernel(x_ref, g_ref, sh_ref, sc_ref, gate_ref, wg_ref, wu_ref, wd_ref, o_ref):
    x = x_ref[...]
    h = _norm_mod(x, g_ref[...], sh_ref[...], sc_ref[...]).astype(BF16)
    a = _dot(h, wg_ref[...])
    a = a * jax.nn.sigmoid(a) * _dot(h, wu_ref[...])
    o_ref[...] = x + gate_ref[...] * _dot(a.astype(BF16), wd_ref[...])


def _ffn(xcat, g, shift, scale, gate, wg, wu, wd, nct):
    b, tt, d = xcat.shape
    ff = wg.shape[1]
    kind = lambda bi, i: (bi, jnp.where(i < nct, 0, 1), 0, 0)
    mod = pl.BlockSpec((None, None, 1, d), kind)
    return pl.pallas_call(
        _ffn_kernel,
        grid=(b, tt // TM),
        in_specs=[
            pl.BlockSpec((None, TM, d), lambda bi, i: (bi, i, 0)),
            pl.BlockSpec((1, d), lambda bi, i: (0, 0)),
            mod, mod, mod,
            pl.BlockSpec((d, ff), lambda bi, i: (0, 0)),
            pl.BlockSpec((d, ff), lambda bi, i: (0, 0)),
            pl.BlockSpec((ff, d), lambda bi, i: (0, 0)),
        ],
        out_specs=pl.BlockSpec((None, TM, d), lambda bi, i: (bi, i, 0)),
        out_shape=jax.ShapeDtypeStruct((b, tt, d), F32),
        compiler_params=_cparams("parallel", "parallel"),
        name="ffn",
    )(xcat, g, shift, scale, gate, wg, wu, wd)


MOE_TR = 1024
MOE_TM = 2048
MOE_TF = 512
MOE_BLK = 256
MOE_SUB = 256


def _route_kernel(ne, x_ref, g_ref, sh_ref, sc_ref, rt_ref, h_o, cmb_o, cnt_o):
    h = _norm_mod(x_ref[...], g_ref[...], sh_ref[...], sc_ref[...])
    h_o[...] = h.astype(BF16)
    logits = lax.dot_general(rt_ref[...], h, (((1,), (1,)), ((), ())), precision=HIGHEST,
                             preferred_element_type=F32)
    sub = lax.broadcasted_iota(jnp.int32, logits.shape, 0).astype(F32)
    none = float(logits.shape[0])
    logits = jnp.where(sub < ne, logits, -jnp.inf)
    m1 = jnp.max(logits, axis=0, keepdims=True)
    i1 = jnp.min(jnp.where(logits == m1, sub, none), axis=0, keepdims=True)
    rest = jnp.where(sub == i1, -jnp.inf, logits)
    m2 = jnp.max(rest, axis=0, keepdims=True)
    i2 = jnp.min(jnp.where(rest == m2, sub, none), axis=0, keepdims=True)
    e2 = jnp.exp(m2 - m1)
    p1 = 1.0 / (1.0 + e2)
    p2 = e2 / (1.0 + e2)
    cmb = jnp.where(sub == i1, p1, 0.0) + jnp.where(sub == i2, p2, 0.0)
    cmb_o[...] = cmb
    cnt = jnp.sum((cmb > 0.0).astype(F32), axis=1, keepdims=True)
    cnt_o[...] = jnp.broadcast_to(cnt, cnt_o.shape).astype(jnp.int32)


def _route(x, g, shift, scale, router_t, ne):
    b, l, d = x.shape
    nr = router_t.shape[0]
    nt = l // MOE_TR
    mod = pl.BlockSpec((None, 1, d), lambda bi, i: (bi, 0, 0))
    return pl.pallas_call(
        functools.partial(_route_kernel, ne),
        grid=(b, nt),
        in_specs=[
            pl.BlockSpec((None, MOE_TR, d), lambda bi, i: (bi, i, 0)),
            pl.BlockSpec((1, d), lambda bi, i: (0, 0)),
            mod, mod,
            pl.BlockSpec((nr, d), lambda bi, i: (0, 0)),
        ],
        out_specs=[
            pl.BlockSpec((MOE_TR, d), lambda bi, i: (bi * nt + i, 0)),
            pl.BlockSpec((nr, MOE_TR), lambda bi, i: (0, bi * nt + i)),
            pl.BlockSpec((None, nr, 128), lambda bi, i: (bi * nt + i, 0, 0)),
        ],
        out_shape=[
            jax.ShapeDtypeStruct((b * l, d), BF16),
            jax.ShapeDtypeStruct((nr, b * l), F32),
            jax.ShapeDtypeStruct((b * nt, nr, 128), jnp.int32),
        ],
        compiler_params=_cparams("parallel", "parallel"),
        name="moe_route",
    )(x, g, shift, scale, router_t)


def _moe_kernel(cnt_ref, h_ref, cmb_ref, tri_ref, wg_ref, wu_ref, wd_ref, o_ref,
                pos_ref, hg_ref, ya_ref):
    t = pl.program_id(0)
    e = pl.program_id(1)
    j = pl.program_id(2)
    tm = h_ref.shape[0]
    nblk = jnp.right_shift(cnt_ref[t, e] + (MOE_BLK - 1), int(math.log2(MOE_BLK)))

    @pl.when((e == 0) & (j == 0))
    def _():
        o_ref[...] = jnp.zeros_like(o_ref)
        asg = (cmb_ref[...] > 0.0).astype(BF16)
        off = jnp.zeros((asg.shape[0], 1), F32)
        for k in range(tm // MOE_SUB):
            blk = asg[:, k * MOE_SUB:(k + 1) * MOE_SUB]
            pos_ref[:, k * MOE_SUB:(k + 1) * MOE_SUB] = _dot(blk, tri_ref[...]) + off
            off = off + jnp.sum(blk.astype(F32), axis=1, keepdims=True)

    sel = lax.broadcasted_iota(jnp.int32, pos_ref.shape, 0) == e
    posrow = jnp.sum(jnp.where(sel, pos_ref[...], 0.0), axis=0, keepdims=True)
    cwrow = jnp.sum(jnp.where(sel, cmb_ref[...], 0.0), axis=0, keepdims=True)
    rowi = lax.broadcasted_iota(jnp.int32, (MOE_BLK, tm), 0).astype(F32)

    def onehot(b):
        slot = rowi + (b * MOE_BLK).astype(F32)
        return (posrow == slot) & (cwrow > 0.0)

    @pl.when(j == 0)
    def _():
        def gather(b, carry):
            sel_b = jnp.where(onehot(b), 1.0, 0.0).astype(BF16)
            hg_ref[b] = _dot(sel_b, h_ref[...]).astype(BF16)
            ya_ref[b] = jnp.zeros(ya_ref.shape[1:], F32)
            return carry
        lax.fori_loop(0, nblk, gather, 0)

    def ffn(b, carry):
        hb = hg_ref[b]
        a = _dot(hb, wg_ref[...])
        a = a * jax.nn.sigmoid(a) * _dot(hb, wu_ref[...])
        ya_ref[b] += _dot(a.astype(BF16), wd_ref[...])
        return carry
    lax.fori_loop(0, nblk, ffn, 0)

    @pl.when(j == pl.num_programs(2) - 1)
    def _():
        def scatter(b, carry):
            wsel = jnp.where(onehot(b), cwrow, 0.0).astype(BF16)
            o_ref[...] += _dot_tn(wsel, ya_ref[b].astype(BF16))
            return carry
        lax.fori_loop(0, nblk, scatter, 0)


def _moe(h, cmb, cnt, tri, wg, wu, wd):
    n, d = h.shape
    nr = cmb.shape[0]
    ne, _, ff = wg.shape
    nbmax = MOE_TM // MOE_BLK
    grid_spec = pltpu.PrefetchScalarGridSpec(
        num_scalar_prefetch=1,
        grid=(n // MOE_TM, ne, ff // MOE_TF),
        in_specs=[
            pl.BlockSpec((MOE_TM, d), lambda t, e, j, c: (t, 0)),
            pl.BlockSpec((nr, MOE_TM), lambda t, e, j, c: (0, t)),
            pl.BlockSpec((MOE_SUB, MOE_SUB), lambda t, e, j, c: (0, 0)),
            pl.BlockSpec((None, d, MOE_TF), lambda t, e, j, c: (e, 0, j)),
            pl.BlockSpec((None, d, MOE_TF), lambda t, e, j, c: (e, 0, j)),
            pl.BlockSpec((None, MOE_TF, d), lambda t, e, j, c: (e, j, 0)),
        ],
        out_specs=pl.BlockSpec((MOE_TM, d), lambda t, e, j, c: (t, 0)),
        scratch_shapes=[
            pltpu.VMEM((nr, MOE_TM), F32),
            pltpu.VMEM((nbmax, MOE_BLK, d), BF16),
            pltpu.VMEM((nbmax, MOE_BLK, d), F32),
        ],
    )
    return pl.pallas_call(
        _moe_kernel,
        grid_spec=grid_spec,
        out_shape=jax.ShapeDtypeStruct((n, d), F32),
        compiler_params=_cparams("parallel", "arbitrary", "arbitrary"),
        name="moe",
    )(cnt, h, cmb, tri, wg, wu, wd)


def _final_kernel(x_ref, m_ref, gate_ref, fg_ref, o_ref):
    y = x_ref[...] + gate_ref[...] * m_ref[...]
    ms = jnp.mean(y * y, axis=-1, keepdims=True)
    o_ref[...] = y * lax.rsqrt(ms + NORM_EPS) * fg_ref[...]


def _final(x, m, gate, final_g):
    b, l, d = x.shape
    nt = l // MOE_TR
    return pl.pallas_call(
        _final_kernel,
        grid=(b, nt),
        in_specs=[
            pl.BlockSpec((None, MOE_TR, d), lambda bi, i: (bi, i, 0)),
            pl.BlockSpec((MOE_TR, d), lambda bi, i: (bi * nt + i, 0)),
            pl.BlockSpec((None, 1, d), lambda bi, i: (bi, 0, 0)),
            pl.BlockSpec((1, d), lambda bi, i: (0, 0)),
        ],
        out_specs=pl.BlockSpec((None, MOE_TR, d), lambda bi, i: (bi, i, 0)),
        out_shape=jax.ShapeDtypeStruct((b, l, d), F32),
        compiler_params=_cparams("parallel", "parallel"),
        name="moe_final",
    )(x, m, gate, final_g)


def _shift_masks(mu, ctx_len, seq_len):
    slab = mu.shape[0]
    nct = ctx_len // TM
    tt = ctx_len + seq_len
    t = jnp.arange(tt)
    is_ctx = t < ctx_len
    tl = t - ctx_len
    col = tl % GRID_W
    rows = seq_len // GRID_W
    grow = tl // GRID_W
    left = jnp.where(is_ctx, t != 0, col != 0)
    right = jnp.where(is_ctx, t != ctx_len - 1, col != GRID_W - 1)
    upv = jnp.where(is_ctx, False, grow != 0)
    dnv = jnp.where(is_ctx, False, grow != rows - 1)
    zero = jnp.zeros_like(left)
    rowmask = jnp.stack([left, right, upv, dnv, zero, zero, zero, zero], axis=-1).astype(F32)
    rowmask = rowmask.reshape(tt // TM, TM, 8)
    c = jnp.arange(slab)
    z = jnp.zeros_like(mu)
    lat = jnp.stack([mu * (c % 4 == 0), mu * (c % 4 == 1), mu * (c % 4 == 2), mu * (c % 4 == 3),
                     1.0 - mu, z, z, z])
    ctx = jnp.stack([mu * (c % 2 == 0), mu * (c % 2 == 1), z, z, 1.0 - mu, z, z, z])
    return rowmask, jnp.stack([ctx, lat]).astype(F32)


def _pad_rows(wt):
    z = jnp.zeros_like(wt[0])
    wp = jnp.stack([jnp.concatenate([wt[0], z], axis=0), jnp.concatenate([z, wt[1]], axis=0)])
    hi = wp.astype(BF16)
    lo = (wp - hi.astype(F32)).astype(BF16)
    return jnp.stack([hi, lo], axis=1)


def kernel(x, c, ctx, c_ctx, ada_w, ada_b, norm1_g, norm2_g, w_in, w_out, shift_mu, rwkv_w0, rwkv_w_up, rwkv_a0, rwkv_a_up, rwkv_g_up, rwkv_k_k, rwkv_k_a, rwkv_r_k, rwkv_ln_w, rwkv_ln_b, s5_lam_re, s5_lam_im, s5_log_dt, s5_b_re, s5_b_im, s5_c_re, s5_c_im, s5_d, s5_glu_w, s5_glu_b, ffn_w_gate, ffn_w_up, ffn_w_down, moe_router, moe_w_gate, moe_w_up, moe_w_down, final_g):
    b, l, d = x.shape
    ctx_len = ctx.shape[1]
    depth = ada_w.shape[0]
    slab_w = shift_mu.shape[1]
    rw_w = rwkv_k_k.shape[1]
    assert ctx_len == TM and l % TM == 0 and b + 1 <= 8
    assert rw_w % (WKV_HEADS * HEAD) == 0 and depth == 2
    nct = ctx_len // TM
    nctc = ctx_len // WKV_CHUNK
    nctc16 = ctx_len // S5_CHUNK

    act = jnp.zeros((8, d), F32).at[:b].set(c).at[b].set(c_ctx)
    mods = _ada_mod(act, ada_w, ada_b).reshape(depth, 8, 6, d)

    def mod(i, k):
        cm = jnp.broadcast_to(mods[i, b, k][None, :], (b, d))
        return jnp.stack([cm, mods[i, :b, k]], axis=1)[:, :, None, :]

    hi = lax.broadcasted_iota(jnp.int32, (rw_w, rw_w), 0) // HEAD
    hj = lax.broadcasted_iota(jnp.int32, (rw_w, rw_w), 1) // HEAD
    bd = (hi == hj).astype(BF16)

    ti = lax.broadcasted_iota(jnp.int32, (TM, TM), 0)
    si = lax.broadcasted_iota(jnp.int32, (TM, TM), 1)
    same_chunk = (ti // WKV_CHUNK) == (si // WKV_CHUNK)
    tri = jnp.stack([same_chunk & (si <= ti), same_chunk & (si >= ti)]).astype(BF16)

    xcat = jnp.concatenate([ctx, x], axis=1)
    out = None
    for i in range(depth):
        last = i == depth - 1
        p, u8 = _inproj(xcat, norm1_g[i][None], mod(i, 0), mod(i, 1), w_in[i].astype(BF16), nct,
                        s5_d.shape[1])
        rowmask, lanec = _shift_masks(shift_mu[i], ctx_len, l)
        v, at, rt, bg, kg, ee, g, bo = _rwkv_prep(
            p, rowmask, lanec, rwkv_k_k[i][None], rwkv_k_a[i][None], rwkv_r_k[i].reshape(1, -1),
            rwkv_w0[i], rwkv_a0[i], _pad_rows(rwkv_w_up[i]), _pad_rows(rwkv_a_up[i]),
            _pad_rows(rwkv_g_up[i]), bd, tri, nct, slab_w)
        yf, yr = _wkv_scan(v, at, rt, bg, kg, ee, nctc)
        s5w = _s5_weights(s5_lam_re[i], s5_lam_im[i], s5_log_dt[i], s5_b_re[i], s5_b_im[i],
                          s5_c_re[i], s5_c_im[i])
        ys = _s5_mix(u8, s5w, b, nctc16)
        t0 = nct if last else 0
        xm = _mixout(xcat, yf, yr, g, bo, ys, p, rwkv_ln_w[i].reshape(1, -1), rwkv_ln_b[i].reshape(1, -1),
                     bd, s5_d[i][None], s5_glu_w[i].astype(BF16), s5_glu_b[i][None],
                     w_out[i].astype(BF16), mod(i, 2), nct, t0)
        if not last:
            j = i // 2
            xcat = _ffn(xm, norm2_g[i][None], mod(i, 3), mod(i, 4), mod(i, 5),
                        ffn_w_gate[j].astype(BF16), ffn_w_up[j].astype(BF16),
                        ffn_w_down[j].astype(BF16), nct)
        else:
            j = i // 2
            ne = moe_router.shape[2]
            nr = -(-ne // 8) * 8
            router_t = jnp.zeros((nr, d), F32).at[:ne].set(moe_router[j].T)
            lat = lambda k: mods[i, :b, k][:, None, :]
            h, cmb, cnt = _route(xm, norm2_g[i][None], lat(3), lat(4), router_t, ne)
            cnt = cnt[:, :ne, 0].reshape(-1, MOE_TM // MOE_TR, ne).sum(axis=1)
            ui = lax.broadcasted_iota(jnp.int32, (MOE_SUB, MOE_SUB), 0)
            uj = lax.broadcasted_iota(jnp.int32, (MOE_SUB, MOE_SUB), 1)
            moe = _moe(h, cmb, cnt, (ui < uj).astype(BF16), moe_w_gate[j].astype(BF16),
                       moe_w_up[j].astype(BF16), moe_w_down[j].astype(BF16))
            out = _final(xm, moe, lat(5), final_g[None])
    return out
```

```python
import functools
import math

import jax
import jax.numpy as jnp
from jax import lax
from jax.experimental import pallas as pl
from jax.experimental.pallas import tpu as pltpu

F32 = jnp.float32
BF16 = jnp.bfloat16
HIGHEST = lax.Precision.HIGHEST

GRID_W = 64
HEAD = 64
DECAY_RANK = 64
ICL_RANK = 64
GATE_RANK = 128
S5_GROUP = 16
S5_STATE = 64
NORM_EPS = 1e-6
GN_EPS = 64e-5
L2_EPS = 1e-12
LAM_RE_MAX = -1e-4
TOP_K = 2

TM = 256
WKV_CHUNK = 64
WKV_HEADS = 4
S5_CHUNK = 16
S5_OCT = 128
VMEM_LIMIT = 56 * 1024 * 1024


def _cparams(*sem):
    return pltpu.CompilerParams(dimension_semantics=sem, vmem_limit_bytes=VMEM_LIMIT)


def _dot(a, b):
    return jnp.dot(a, b, preferred_element_type=F32)


def _dot32(a, b):
    return jnp.dot(a, b, precision=HIGHEST, preferred_element_type=F32)


def _split2(x):
    hi = x.astype(BF16)
    return hi, (x - hi.astype(F32)).astype(BF16)


def _dot_ones(x, ones_bf):
    hi, lo = _split2(x)
    return _dot(hi, ones_bf) + _dot(lo, ones_bf)


def _dot_w2(x, w2_ref):
    hi, lo = _split2(x)
    return _dot(hi, w2_ref[0]) + _dot(lo, w2_ref[0]) + _dot(hi, w2_ref[1])


def _dot_nt(a, b):
    return lax.dot_general(a, b, (((1,), (1,)), ((), ())), preferred_element_type=F32)


def _dot_tn(a, b):
    return lax.dot_general(a, b, (((0,), (0,)), ((), ())), preferred_element_type=F32)


def _ada_kernel(act_ref, w_ref, b_ref, o_ref):
    a = act_ref[...]
    a = a * jax.nn.sigmoid(a)
    o_ref[...] = _dot32(a, w_ref[...]) + b_ref[...]


def _ada_mod(act, ada_w, ada_b):
    depth, d, n = ada_w.shape
    tn = 1536
    return pl.pallas_call(
        _ada_kernel,
        grid=(depth, n // tn),
        in_specs=[
            pl.BlockSpec((8, d), lambda i, j: (0, 0)),
            pl.BlockSpec((None, d, tn), lambda i, j: (i, 0, j)),
            pl.BlockSpec((None, 1, tn), lambda i, j: (i, 0, j)),
        ],
        out_specs=pl.BlockSpec((None, 8, tn), lambda i, j: (i, 0, j)),
        out_shape=jax.ShapeDtypeStruct((depth, 8, n), F32),
        compiler_params=_cparams("arbitrary", "arbitrary"),
        name="ada_mod",
    )(act, ada_w, ada_b.reshape(depth, 1, n))


def _norm_mod(x, g, shift, scale):
    ms = jnp.mean(x * x, axis=-1, keepdims=True)
    y = x * lax.rsqrt(ms + NORM_EPS) * g
    return y * (1.0 + scale) + shift


def _inproj_kernel(sw, x_ref, g_ref, sh_ref, sc_ref, w_ref, o_ref, u8_ref, us_ref):
    h = _norm_mod(x_ref[...], g_ref[...], sh_ref[...], sc_ref[...])
    p = _dot(h.astype(BF16), w_ref[...])
    o_ref[...] = p
    base = p.shape[1] - sw
    cpt = TM // S5_CHUNK
    for o8 in range(sw // S5_OCT):
        us_ref[o8] = p[:, base + o8 * S5_OCT:base + (o8 + 1) * S5_OCT]
    for o8 in range(sw // S5_OCT):
        for s in range(S5_CHUNK):
            u8_ref[o8, :, s * S5_OCT:(s + 1) * S5_OCT] = (
                us_ref[o8, pl.ds(s, cpt, stride=S5_CHUNK), :].astype(BF16))


def _inproj(xcat, g, shift, scale, w_bf, nct, sw):
    b, tt, d = xcat.shape
    n = w_bf.shape[1]
    noct = sw // S5_OCT
    cpt = TM // S5_CHUNK
    kw = S5_CHUNK * S5_OCT
    kind = lambda bi, i: (bi, jnp.where(i < nct, 0, 1), 0, 0)
    return pl.pallas_call(
        functools.partial(_inproj_kernel, sw),
        grid=(b, tt // TM),
        in_specs=[
            pl.BlockSpec((None, TM, d), lambda bi, i: (bi, i, 0)),
            pl.BlockSpec((1, d), lambda bi, i: (0, 0)),
            pl.BlockSpec((None, None, 1, d), kind),
            pl.BlockSpec((None, None, 1, d), kind),
            pl.BlockSpec((d, n), lambda bi, i: (0, 0)),
        ],
        out_specs=[pl.BlockSpec((None, TM, n), lambda bi, i: (bi, i, 0)),
                   pl.BlockSpec((noct, cpt, kw), lambda bi, i: (0, i, bi))],
        out_shape=[jax.ShapeDtypeStruct((b, tt, n), F32),
                   jax.ShapeDtypeStruct((noct, tt // S5_CHUNK, b * kw), BF16)],
        scratch_shapes=[pltpu.VMEM((noct, TM, S5_OCT), F32)],
        compiler_params=_cparams("parallel", "parallel"),
        name="inproj",
    )(xcat, g, shift, scale, w_bf)


def _split3(x):
    hi = x.astype(BF16)
    r1 = x - hi.astype(F32)
    mid = r1.astype(BF16)
    lo = (r1 - mid.astype(F32)).astype(BF16)
    return hi, mid, lo


def _prep_kernel(p_ref, up_ref, dn_ref, rm_ref, lc_ref, kk_ref, ka_ref, rk_ref, w0_ref, a0_ref,
                 wup_ref, aup_ref, gup_ref, bd_ref, tri_ref,
                 v_o, at_o, rt_o, bg_o, kg_o, ee_o, g_o, bo_o):
    x = p_ref[...]
    rm = rm_ref[...]
    lc = lc_ref[...]
    prev = pltpu.roll(x, 1, 0)
    nxt = pltpu.roll(x, TM - 1, 0)
    up = jnp.concatenate([up_ref[...], x[: TM - GRID_W]], axis=0)
    dn = jnp.concatenate([x[GRID_W:], dn_ref[...]], axis=0)
    slab = (x * lc[4:5]
            + rm[:, 0:1] * (prev * lc[0:1])
            + rm[:, 1:2] * (nxt * lc[1:2])
            + rm[:, 2:3] * (up * lc[2:3])
            + rm[:, 3:4] * (dn * lc[3:4]))
    w = kk_ref.shape[1]
    r = slab[:, 0:w]
    k = slab[:, w:2 * w]
    v = slab[:, 2 * w:3 * w]
    o = 3 * w
    wd = slab[:, o:o + 2 * DECAY_RANK]
    ad = slab[:, o + 2 * DECAY_RANK:o + 2 * DECAY_RANK + 2 * ICL_RANK]
    gd = slab[:, o + 2 * DECAY_RANK + 2 * ICL_RANK:]
    bd = bd_ref[...]
    kk = k * kk_ref[...]
    nrm = jnp.sqrt(_dot_ones(kk * kk, bd))
    kk = kk / jnp.maximum(nrm, L2_EPS)
    v_o[...] = v.astype(BF16)
    twd = jnp.tanh(wd)
    sgd = jax.nn.sigmoid(gd)
    c = WKV_CHUNK
    for d in range(2):
        z = w0_ref[d:d + 1, :] + _dot_w2(twd, wup_ref.at[d])
        w_log = -jax.nn.softplus(-z) - 0.5
        lw = -jnp.exp(w_log)
        a = jax.nn.sigmoid(a0_ref[d:d + 1, :] + _dot_w2(ad, aup_ref.at[d]))
        kt = k * (1.0 + (a - 1.0) * ka_ref[...])
        g_o[d] = _dot_w2(sgd, gup_ref.at[d])
        bo_o[d] = _dot_ones(r * kt * rk_ref[...], bd) * v
        tri = tri_ref[d]
        hi, mid, lo = _split3(lw)
        lg_in = _dot(tri, hi) + _dot(tri, mid) + _dot(tri, lo)
        e_neg = jnp.exp(-lg_in)
        at_o[d] = (-kk * jnp.exp(lg_in - lw)).astype(BF16)
        rt_o[d] = (r * jnp.exp(lg_in)).astype(BF16)
        bg_o[d] = (kk * a * e_neg).astype(BF16)
        kg_o[d] = (kt * e_neg).astype(BF16)
        for ci in range(TM // c):
            last = ci * c + (c - 1 if d == 0 else 0)
            ee_o[d, ci] = jnp.exp(lg_in[last:last + 1, :])


def _rwkv_prep(p, rowmask, lanec, k_k, k_a, r_k, w0, a0, wup, aup, gup, bd, tri, nct, slab_w):
    b, tt, _ = p.shape
    w = k_k.shape[1]
    nt = tt // TM
    cpt = TM // WKV_CHUNK
    hb = TM // GRID_W
    nhb = tt // GRID_W
    full = lambda *s: pl.BlockSpec(s, lambda bi, i: (0,) * len(s))
    tok = pl.BlockSpec((None, TM, w), lambda bi, i: (bi, i, 0))
    tok2 = pl.BlockSpec((2, None, TM, w), lambda bi, i: (0, bi, i, 0))
    bf1 = jax.ShapeDtypeStruct((b, tt, w), BF16)
    bf2 = jax.ShapeDtypeStruct((2, b, tt, w), BF16)
    sh2 = jax.ShapeDtypeStruct((2, b, tt, w), F32)
    return pl.pallas_call(
        _prep_kernel,
        grid=(b, nt),
        in_specs=[
            pl.BlockSpec((None, TM, slab_w), lambda bi, i: (bi, i, 0)),
            pl.BlockSpec((None, GRID_W, slab_w), lambda bi, i: (bi, jnp.maximum(i * hb - 1, 0), 0)),
            pl.BlockSpec((None, GRID_W, slab_w),
                         lambda bi, i: (bi, jnp.minimum(i * hb + hb, nhb - 1), 0)),
            pl.BlockSpec((None, TM, 8), lambda bi, i: (i, 0, 0)),
            pl.BlockSpec((None, 8, slab_w), lambda bi, i: (jnp.where(i < nct, 0, 1), 0, 0)),
            full(1, w), full(1, w), full(1, w), full(2, w), full(2, w),
            full(2, 2, 2 * DECAY_RANK, w), full(2, 2, 2 * ICL_RANK, w), full(2, 2, 2 * GATE_RANK, w),
            full(w, w), full(2, TM, TM),
        ],
        out_specs=[tok, tok2, tok2, tok2, tok2,
                   pl.BlockSpec((2, None, cpt, 1, w), lambda bi, i: (0, bi, i, 0, 0)),
                   tok2, tok2],
        out_shape=[bf1, bf2, bf2, bf2, bf2,
                   jax.ShapeDtypeStruct((2, b, tt // WKV_CHUNK, 1, w), F32), sh2, sh2],
        compiler_params=_cparams("parallel", "parallel"),
        name="rwkv_prep",
    )(p, p, p, rowmask, lanec, k_k, k_a, r_k, w0, a0, wup, aup, gup, bd, tri)


def _wkv_kernel(nb, ngrp, v_f, v_r, at_f, at_r, rt_f, rt_r, bg_f, bg_r, kg_f, kg_r, ee_f, ee_r,
                y_f, y_r, ht_ref):
    j = pl.program_id(0)
    c = WKV_CHUNK
    gw = WKV_HEADS * HEAD
    gn = WKV_HEADS * c

    @pl.when(j == 0)
    def _():
        ht_ref[...] = jnp.zeros_like(ht_ref)

    sh = int(math.log2(c))
    row = lax.broadcasted_iota(jnp.int32, (gn, gw), 0)
    col = lax.broadcasted_iota(jnp.int32, (gn, gw), 1)
    same = (row >> sh) == (col >> sh)
    tf = lax.broadcasted_iota(jnp.int32, (c, gn), 0)
    sf = lax.broadcasted_iota(jnp.int32, (c, gn), 1) & (c - 1)
    eye = (tf == sf).astype(F32)

    def stack(x):
        xb = jnp.concatenate([x.astype(BF16)] * WKV_HEADS, axis=0)
        return jnp.where(same, xb, jnp.zeros_like(xb))

    dirs = ((v_f, at_f, rt_f, bg_f, kg_f, ee_f, y_f, sf < tf, sf <= tf),
            (v_r, at_r, rt_r, bg_r, kg_r, ee_r, y_r, sf > tf, sf >= tf))
    chains = [(d, bi, q) for d in range(2) for bi in range(nb) for q in range(ngrp)]
    sl = lambda q: slice(q * gw, (q + 1) * gw)
    rd = lambda k: [dirs[d][k][bi, :, sl(q)] for d, bi, q in chains]
    cat0 = lambda xs: jnp.concatenate(xs, axis=0)
    v, at, rt, bg, kg, ee = rd(0), rd(1), rd(2), rd(3), rd(4), rd(5)
    before = [dirs[d][7] for d, _, _ in chains]
    incl = [dirs[d][8] for d, _, _ in chains]
    n_ch = range(len(chains))

    v_bd = [stack(x) for x in v]
    at_bd = [stack(x) for x in at]
    bk_bd = [cat0([stack(bg[i]), stack(kg[i])]) for i in n_ch]
    a = [_dot_nt(cat0([at[i], rt[i]]), bk_bd[i]) for i in n_ch]
    n = [jnp.where(before[i], a[i][0:c, 0:gn], 0.0) for i in n_ch]
    a_kk = [cat0([jnp.where(before[i], a[i][0:c, gn:], 0.0),
                  jnp.where(incl[i], a[i][c:, gn:], 0.0)]).astype(BF16) for i in n_ch]
    a_rb = [jnp.where(incl[i], a[i][c:, 0:gn], 0.0).astype(BF16) for i in n_ch]
    tm = [eye + x for x in n]
    pw = [_dot(x.astype(BF16), stack(x)) for x in n]
    for lvl in range(1, sh):
        pw_bd = [stack(x) for x in pw]
        if lvl < sh - 1:
            tp = [_dot(cat0([tm[i].astype(BF16), pw[i].astype(BF16)]), pw_bd[i]) for i in n_ch]
            tm = [tm[i] + tp[i][0:c] for i in n_ch]
            pw = [tp[i][c:] for i in n_ch]
        else:
            tm = [tm[i] + _dot(tm[i].astype(BF16), pw_bd[i]) for i in n_ch]
    tm_b = [x.astype(BF16) for x in tm]
    atp = [_dot(tm_b[i], at_bd[i]) for i in n_ch]
    av = [_dot(a_kk[i], v_bd[i]) for i in n_ch]
    wv = [_dot(tm_b[i], stack(av[i][0:c])) for i in n_ch]
    wv_bd = [stack(x) for x in wv]
    atp_bd = [stack(x) for x in atp]
    ar = [_dot(a_rb[i], jnp.concatenate([wv_bd[i], atp_bd[i]], axis=1)) for i in n_ch]
    y0 = [ar[i][:, 0:gw] + av[i][c:] for i in n_ch]
    rtp = [(ar[i][:, gw:] + rt[i].astype(F32)).astype(BF16) for i in n_ch]
    bge_bd = [stack(bg[i].astype(F32) * ee[i]) for i in n_ch]
    kge_bd = [stack(kg[i].astype(F32) * ee[i]) for i in n_ch]
    g = [_dot_tn(bge_bd[i], atp_bd[i]).astype(BF16) for i in n_ch]
    hloc_t = [_dot_tn(cat0([wv_bd[i], v_bd[i]]), cat0([bge_bd[i], kge_bd[i]])) for i in n_ch]
    for i, (d, bi, q) in enumerate(chains):
        ht = ht_ref[d, bi, q]
        ht_b = ht.astype(BF16)
        dirs[d][6][bi, :, sl(q)] = y0[i] + _dot_nt(rtp[i], ht_b)
        ht_ref[d, bi, q] = ht * ee[i] + _dot_nt(ht_b, g[i]) + hloc_t[i]


def _wkv_scan(v, at, rt, bg, kg, ee, nctc):
    b, tt, w = v.shape
    ntot = tt // WKV_CHUNK
    ngrp = w // (WKV_HEADS * HEAD)
    fwd = lambda j: j
    rev = lambda j: jnp.where(j < nctc, nctc - 1 - j, ntot - 1 + nctc - j)
    tok = lambda cm: pl.BlockSpec((b, WKV_CHUNK, w), lambda j: (0, cm(j), 0))
    tok2 = lambda d, cm: pl.BlockSpec((None, b, WKV_CHUNK, w), lambda j: (d, 0, cm(j), 0))
    eesp = lambda d, cm: pl.BlockSpec((None, b, None, 1, w), lambda j: (d, 0, cm(j), 0, 0))
    pair = lambda f: [f(0, fwd), f(1, rev)]
    ysh = jax.ShapeDtypeStruct((b, tt, w), F32)
    return pl.pallas_call(
        functools.partial(_wkv_kernel, b, ngrp),
        grid=(ntot,),
        in_specs=[tok(fwd), tok(rev)] + pair(tok2) + pair(tok2) + pair(tok2) + pair(tok2) + pair(eesp),
        out_specs=[tok(fwd), tok(rev)],
        out_shape=[ysh, ysh],
        scratch_shapes=[pltpu.VMEM((2, b, ngrp, WKV_HEADS * HEAD, WKV_HEADS * HEAD), F32)],
        compiler_params=_cparams("arbitrary"),
        name="wkv_scan",
    )(v, v, at, at, rt, rt, bg, bg, kg, kg, ee, ee)


def _s5_weights(lam_re, lam_im, log_dt, b_re, b_im, c_re, c_im):
    tc = S5_CHUNK
    lr = jnp.minimum(lam_re.astype(F32), LAM_RE_MAX)
    li = lam_im.astype(F32)
    dt = jnp.exp(log_dt.astype(F32))[..., None]
    mag = jnp.exp(lr * dt)
    ar = mag * jnp.cos(li * dt)
    ai = mag * jnp.sin(li * dt)
    den = lr * lr + li * li
    xr = ar - 1.0
    cr = (xr * lr + ai * li) / den
    ci = (ai * lr - xr * li) / den
    br = cr[..., None] * b_re - ci[..., None] * b_im
    bi = cr[..., None] * b_im + ci[..., None] * b_re
    pr, pi = [jnp.ones_like(ar)], [jnp.zeros_like(ar)]
    for _ in range(tc):
        pr_n = pr[-1] * ar - pi[-1] * ai
        pi_n = pr[-1] * ai + pi[-1] * ar
        pr.append(pr_n)
        pi.append(pi_n)
    pr = jnp.stack(pr)
    pi = jnp.stack(pi)
    lbr = pr[..., None] * br - pi[..., None] * bi
    lbi = pr[..., None] * bi + pi[..., None] * br
    clr = c_re * pr[:, :, :, None, :] - c_im * pi[:, :, :, None, :]
    cli = c_re * pi[:, :, :, None, :] + c_im * pr[:, :, :, None, :]
    lbr_t = jnp.swapaxes(lbr, -1, -2)
    lbi_t = jnp.swapaxes(lbi, -1, -2)
    kern_t = jnp.sum(lbr_t[..., :, None, :] * c_re[None, :, :, None, :, :]
                     - lbi_t[..., :, None, :] * c_im[None, :, :, None, :, :], axis=-1)
    g = ar.shape[1]
    og = S5_OCT // S5_GROUP
    noct = g // og
    eye = jnp.eye(og, dtype=F32)

    def bdiag(x):
        nt, _, _, a, n = x.shape
        x = x.reshape(nt, 2, noct, og, a, n)
        y = x[:, :, :, :, :, None, :] * eye[None, None, None, :, None, :, None]
        return jnp.transpose(y, (1, 2, 0, 3, 4, 5, 6)).reshape(2, noct, nt, og * a, og * n).astype(BF16)

    kbd = bdiag(kern_t)
    pbd = bdiag(jnp.concatenate([lbr_t, lbi_t], axis=-1))
    qbd = bdiag(jnp.concatenate([clr, -cli], axis=-1))
    s = jnp.arange(tc)[:, None]
    t = jnp.arange(tc)[None, :]
    kw = tc * S5_OCT
    toes, pouts, qins = [], [], []
    for d in range(2):
        lag = (t - s) if d == 0 else (s - t)
        valid = lag >= 0
        blk = kbd[d][:, jnp.where(valid, lag, 0)]
        blk = jnp.where(valid[None, :, :, None, None], blk, jnp.zeros_like(blk))
        toes.append(jnp.transpose(blk, (0, 1, 3, 2, 4)).reshape(noct, kw, kw))
        e_out = (tc - 1 - jnp.arange(tc)) if d == 0 else jnp.arange(tc)
        pouts.append(pbd[d][:, e_out].reshape(noct, kw, -1))
        e_in = (jnp.arange(tc) + 1) if d == 0 else (tc - jnp.arange(tc))
        qins.append(qbd[d][:, e_in].reshape(noct, kw, -1))
    la = jnp.concatenate([pr[tc], pr[tc]], axis=-1)
    lb = jnp.concatenate([-pi[tc], pi[tc]], axis=-1)
    return jnp.stack(toes), jnp.stack(pouts), jnp.stack(qins), la, lb


def _s5_local_kernel(u_ref, p_ref, e_ref):
    e_ref[...] = _dot(u_ref[...], p_ref[...])


def _s5_local(u8, pout8, nb):
    noct, nch, _ = u8.shape
    kw, n = pout8.shape[2:]
    return pl.pallas_call(
        _s5_local_kernel,
        grid=(noct, nb, 2),
        in_specs=[
            pl.BlockSpec((None, nch, kw), lambda o, b, d: (o, 0, b)),
            pl.BlockSpec((None, None, kw, n), lambda o, b, d: (d, o, 0, 0)),
        ],
        out_specs=pl.BlockSpec((None, nch, n), lambda o, b, d: (d, 0, o * nb + b)),
        out_shape=jax.ShapeDtypeStruct((2, nch, noct * nb * n), F32),
        compiler_params=_cparams("parallel", "parallel", "parallel"),
        name="s5_local",
    )(u8, pout8)


S5_STATE_ROWS = 8
S5_STATE_LANES = 256


def _s5_state_kernel(nctc, ntot, e_ref, la_ref, lb_ref, x_ref, es_ref):
    d = pl.program_id(0)
    la = la_ref[...]
    lb = lb_ref[...]
    nr, wl = la.shape

    def swap(t):
        lane = lax.broadcasted_iota(jnp.int32, t.shape, 1)
        first_half = (lane & (2 * S5_STATE - 1)) < S5_STATE
        return jnp.where(first_half, pltpu.roll(t, wl - S5_STATE, 1), pltpu.roll(t, S5_STATE, 1))

    es_ref[...] = swap(e_ref[...].reshape(ntot * nr, wl)).reshape(ntot, nr, wl)
    lbs = swap(lb)

    def body(j, carry):
        x, xs = carry
        rev_idx = jnp.where(j < nctc, nctc - 1 - j, ntot - 1 + nctc - j)
        c = jnp.where(d == 0, j, rev_idx)
        x_ref[c] = x
        return la * x + lb * xs + e_ref[c], la * xs + lbs * x + es_ref[c]

    zero = jnp.zeros(la.shape, F32)
    lax.fori_loop(0, ntot, body, (zero, zero))


def _s5_state(e, la, lb, nctc, ntot):
    _, nch, nr, lanes = e.shape
    wl = S5_STATE_LANES
    blk = pl.BlockSpec((None, nch, nr, wl), lambda d, i: (d, 0, 0, i))
    cf = pl.BlockSpec((None, nr, wl), lambda d, i: (d, 0, i))
    return pl.pallas_call(
        functools.partial(_s5_state_kernel, nctc, ntot),
        grid=(2, lanes // wl),
        in_specs=[blk, cf, cf],
        out_specs=blk,
        out_shape=jax.ShapeDtypeStruct(e.shape, F32),
        scratch_shapes=[pltpu.VMEM((nch, nr, wl), F32)],
        compiler_params=_cparams("parallel", "parallel"),
        name="s5_state",
    )(e, la, lb)


def _s5_out_kernel(u_ref, t_ref, x_ref, q_ref, y_ref):
    d = pl.program_id(2)
    y = _dot(u_ref[...], t_ref[...]) + _dot_nt(x_ref[...].astype(BF16), q_ref[...])

    @pl.when(d == 0)
    def _():
        y_ref[...] = y

    @pl.when(d != 0)
    def _():
        y_ref[...] += y


def _s5_out(u8, toe8, xin, qin8, nb):
    noct, nch, _ = u8.shape
    kw = toe8.shape[2]
    n2 = qin8.shape[3]
    return pl.pallas_call(
        _s5_out_kernel,
        grid=(noct, nb, 2),
        in_specs=[
            pl.BlockSpec((None, nch, kw), lambda o, b, d: (o, 0, b)),
            pl.BlockSpec((None, None, kw, kw), lambda o, b, d: (d, o, 0, 0)),
            pl.BlockSpec((None, nch, n2), lambda o, b, d: (d, 0, o * nb + b)),
            pl.BlockSpec((None, None, kw, n2), lambda o, b, d: (d, o, 0, 0)),
        ],
        out_specs=pl.BlockSpec((None, nch, kw), lambda o, b, d: (o, 0, b)),
        out_shape=jax.ShapeDtypeStruct((noct, nch, nb * kw), F32),
        compiler_params=_cparams("parallel", "parallel", "arbitrary"),
        name="s5_out",
    )(u8, toe8, xin, qin8)


def _s5_mix(u8, weights, nb, nctc16):
    toe8, pout8, qin8, la, lb = weights
    noct, nch, _ = u8.shape
    e = _s5_local(u8, pout8, nb)
    assert noct * nb == S5_STATE_ROWS
    fl = e.shape[2] // S5_STATE_ROWS
    coef = lambda t: jnp.repeat(t.reshape(2, noct, fl), nb, axis=1)
    xin = _s5_state(e.reshape(2, nch, S5_STATE_ROWS, fl), coef(la), coef(lb), nctc16, nch)
    return _s5_out(u8, toe8, xin.reshape(e.shape), qin8, nb)


def _mixout_kernel(x_ref, yf_ref, yr_ref, g_ref, bo_ref, y8_ref, u_ref, lnw_ref, lnb_ref, bd_ref,
                   dsk_ref, gluw_ref, glub_ref, wout_ref, gate_ref, o_ref, ysn_ref):
    cpt = TM // S5_CHUNK
    for o8 in range(y8_ref.shape[0]):
        for s in range(S5_CHUNK):
            ysn_ref[o8, pl.ds(s, cpt, stride=S5_CHUNK), :] = y8_ref[o8, :, s * S5_OCT:(s + 1) * S5_OCT]
    ys = jnp.concatenate([ysn_ref[o8] for o8 in range(y8_ref.shape[0])], axis=1)
    bd = bd_ref[...]
    inv = 1.0 / HEAD
    rw = None
    for d, y_ref in enumerate((yf_ref, yr_ref)):
        y = y_ref[...]
        mean = _dot_ones(y, bd) * inv
        yc = y - mean
        var = _dot_ones(yc * yc, bd) * inv
        yn = yc * lax.rsqrt(var + GN_EPS) * lnw_ref[...] + lnb_ref[...]
        o = (yn + bo_ref[d]) * g_ref[d]
        rw = o if rw is None else rw + o
    u = u_ref[...]
    ss = ys + dsk_ref[...] * u
    ss = jax.nn.gelu(ss)
    ss = ss * jax.nn.sigmoid(_dot(ss.astype(BF16), gluw_ref[...]) + glub_ref[...])
    w = rw.shape[1]
    mix = _dot(rw.astype(BF16), wout_ref[0:w, :]) + _dot(ss.astype(BF16), wout_ref[w:, :])
    o_ref[...] = x_ref[...] + gate_ref[...] * mix


def _mixout(xcat, yf, yr, g, bo, y8, p, ln_w, ln_b, bd, d_skip, glu_w, glu_b, w_out, gate, nct, t0):
    b, tt, d = xcat.shape
    w = ln_w.shape[1]
    sw = d_skip.shape[1]
    noct = y8.shape[0]
    cpt = TM // S5_CHUNK
    kw = S5_CHUNK * S5_OCT
    ublk = (p.shape[2] - sw) // sw
    nt = tt // TM - t0
    full = lambda *s: pl.BlockSpec(s, lambda bi, i: (0,) * len(s))
    tok = pl.BlockSpec((None, TM, w), lambda bi, i: (bi, i + t0, 0))
    tok2 = pl.BlockSpec((2, None, TM, w), lambda bi, i: (0, bi, i + t0, 0))
    return pl.pallas_call(
        _mixout_kernel,
        grid=(b, nt),
        in_specs=[
            pl.BlockSpec((None, TM, d), lambda bi, i: (bi, i + t0, 0)),
            tok, tok, tok2, tok2,
            pl.BlockSpec((noct, cpt, kw), lambda bi, i: (0, i + t0, bi)),
            pl.BlockSpec((None, TM, sw), lambda bi, i: (bi, i + t0, ublk)),
            full(1, w), full(1, w), full(w, w), full(1, sw), full(sw, sw), full(1, sw),
            full(w + sw, d),
            pl.BlockSpec((None, None, 1, d), lambda bi, i: (bi, jnp.where(i + t0 < nct, 0, 1), 0, 0)),
        ],
        out_specs=pl.BlockSpec((None, TM, d), lambda bi, i: (bi, i, 0)),
        out_shape=jax.ShapeDtypeStruct((b, nt * TM, d), F32),
        scratch_shapes=[pltpu.VMEM((noct, TM, S5_OCT), F32)],
        compiler_params=_cparams("parallel", "parallel"),
        name="mix_out",
    )(xcat, yf, yr, g, bo, y8, p, ln_w, ln_b, bd, d_skip, glu_w, glu_b, w_out, gate)


def _ffn_kernel(x_ref, g_ref, sh_ref, sc_ref, gate_ref, wg_ref, wu_ref, wd_ref, o_ref):
    x = x_ref[...]
    h = _norm_mod(x, g_ref[...], sh_ref[...], sc_ref[...]).astype(BF16)
    a = _dot(h, wg_ref[...])
    a = a * jax.nn.sigmoid(a) * _dot(h, wu_ref[...])
    o_ref[...] = x + gate_ref[...] * _dot(a.astype(BF16), wd_ref[...])


def _ffn(xcat, g, shift, scale, gate, wg, wu, wd, nct):
    b, tt, d = xcat.shape
    ff = wg.shape[1]
    kind = lambda bi, i: (bi, jnp.where(i < nct, 0, 1), 0, 0)
    mod = pl.BlockSpec((None, None, 1, d), kind)
    return pl.pallas_call(
        _ffn_kernel,
        grid=(b, tt // TM),
        in_specs=[
            pl.BlockSpec((None, TM, d), lambda bi, i: (bi, i, 0)),
            pl.BlockSpec((1, d), lambda bi, i: (0, 0)),
            mod, mod, mod,
            pl.BlockSpec((d, ff), lambda bi, i: (0, 0)),
            pl.BlockSpec((d, ff), lambda bi, i: (0, 0)),
            pl.BlockSpec((ff, d), lambda bi, i: (0, 0)),
        ],
        out_specs=pl.BlockSpec((None, TM, d), lambda bi, i: (bi, i, 0)),
        out_shape=jax.ShapeDtypeStruct((b, tt, d), F32),
        compiler_params=_cparams("parallel", "parallel"),
        name="ffn",
    )(xcat, g, shift, scale, gate, wg, wu, wd)


MOE_TR = 1024
MOE_TM = 2048
MOE_TF = 512
MOE_BLK = 256
MOE_SUB = 256


def _route_kernel(ne, x_ref, g_ref, sh_ref, sc_ref, rt_ref, h_o, cmb_o, cnt_o):
    h = _norm_mod(x_ref[...], g_ref[...], sh_ref[...], sc_ref[...])
    h_o[...] = h.astype(BF16)
    logits = lax.dot_general(rt_ref[...], h, (((1,), (1,)), ((), ())), precision=HIGHEST,
                             preferred_element_type=F32)
    sub = lax.broadcasted_iota(jnp.int32, logits.shape, 0).astype(F32)
    none = float(logits.shape[0])
    logits = jnp.where(sub < ne, logits, -jnp.inf)
    m1 = jnp.max(logits, axis=0, keepdims=True)
    i1 = jnp.min(jnp.where(logits == m1, sub, none), axis=0, keepdims=True)
    rest = jnp.where(sub == i1, -jnp.inf, logits)
    m2 = jnp.max(rest, axis=0, keepdims=True)
    i2 = jnp.min(jnp.where(rest == m2, sub, none), axis=0, keepdims=True)
    e2 = jnp.exp(m2 - m1)
    p1 = 1.0 / (1.0 + e2)
    p2 = e2 / (1.0 + e2)
    cmb = jnp.where(sub == i1, p1, 0.0) + jnp.where(sub == i2, p2, 0.0)
    cmb_o[...] = cmb
    cnt = jnp.sum((cmb > 0.0).astype(F32), axis=1, keepdims=True)
    cnt_o[...] = jnp.broadcast_to(cnt, cnt_o.shape).astype(jnp.int32)


def _route(x, g, shift, scale, router_t, ne):
    b, l, d = x.shape
    nr = router_t.shape[0]
    nt = l // MOE_TR
    mod = pl.BlockSpec((None, 1, d), lambda bi, i: (bi, 0, 0))
    return pl.pallas_call(
        functools.partial(_route_kernel, ne),
        grid=(b, nt),
        in_specs=[
            pl.BlockSpec((None, MOE_TR, d), lambda bi, i: (bi, i, 0)),
            pl.BlockSpec((1, d), lambda bi, i: (0, 0)),
            mod, mod,
            pl.BlockSpec((nr, d), lambda bi, i: (0, 0)),
        ],
        out_specs=[
            pl.BlockSpec((MOE_TR, d), lambda bi, i: (bi * nt + i, 0)),
            pl.BlockSpec((nr, MOE_TR), lambda bi, i: (0, bi * nt + i)),
            pl.BlockSpec((None, nr, 128), lambda bi, i: (bi * nt + i, 0, 0)),
        ],
        out_shape=[
            jax.ShapeDtypeStruct((b * l, d), BF16),
            jax.ShapeDtypeStruct((nr, b * l), F32),
            jax.ShapeDtypeStruct((b * nt, nr, 128), jnp.int32),
        ],
        compiler_params=_cparams("parallel", "parallel"),
        name="moe_route",
    )(x, g, shift, scale, router_t)


def _moe_kernel(cnt_ref, h_ref, cmb_ref, tri_ref, wg_ref, wu_ref, wd_ref, o_ref,
                pos_ref, hg_ref, ya_ref):
    t = pl.program_id(0)
    e = pl.program_id(1)
    j = pl.program_id(2)
    tm = h_ref.shape[0]
    nblk = jnp.right_shift(cnt_ref[t, e] + (MOE_BLK - 1), int(math.log2(MOE_BLK)))

    @pl.when((e == 0) & (j == 0))
    def _():
        o_ref[...] = jnp.zeros_like(o_ref)
        asg = (cmb_ref[...] > 0.0).astype(BF16)
        off = jnp.zeros((asg.shape[0], 1), F32)
        for k in range(tm // MOE_SUB):
            blk = asg[:, k * MOE_SUB:(k + 1) * MOE_SUB]
            pos_ref[:, k * MOE_SUB:(k + 1) * MOE_SUB] = _dot(blk, tri_ref[...]) + off
            off = off + jnp.sum(blk.astype(F32), axis=1, keepdims=True)

    sel = lax.broadcasted_iota(jnp.int32, pos_ref.shape, 0) == e
    posrow = jnp.sum(jnp.where(sel, pos_ref[...], 0.0), axis=0, keepdims=True)
    cwrow = jnp.sum(jnp.where(sel, cmb_ref[...], 0.0), axis=0, keepdims=True)
    rowi = lax.broadcasted_iota(jnp.int32, (MOE_BLK, tm), 0).astype(F32)

    def onehot(b):
        slot = rowi + (b * MOE_BLK).astype(F32)
        return (posrow == slot) & (cwrow > 0.0)

    @pl.when(j == 0)
    def _():
        def gather(b, carry):
            sel_b = jnp.where(onehot(b), 1.0, 0.0).astype(BF16)
            hg_ref[b] = _dot(sel_b, h_ref[...]).astype(BF16)
            ya_ref[b] = jnp.zeros(ya_ref.shape[1:], F32)
            return carry
        lax.fori_loop(0, nblk, gather, 0)

    def ffn(b, carry):
        hb = hg_ref[b]
        a = _dot(hb, wg_ref[...])
        a = a * jax.nn.sigmoid(a) * _dot(hb, wu_ref[...])
        ya_ref[b] += _dot(a.astype(BF16), wd_ref[...])
        return carry
    lax.fori_loop(0, nblk, ffn, 0)

    @pl.when(j == pl.num_programs(2) - 1)
    def _():
        def scatter(b, carry):
            wsel = jnp.where(onehot(b), cwrow, 0.0).astype(BF16)
            o_ref[...] += _dot_tn(wsel, ya_ref[b].astype(BF16))
            return carry
        lax.fori_loop(0, nblk, scatter, 0)


def _moe(h, cmb, cnt, tri, wg, wu, wd):
    n, d = h.shape
    nr = cmb.shape[0]
    ne, _, ff = wg.shape
    nbmax = MOE_TM // MOE_BLK
    grid_spec = pltpu.PrefetchScalarGridSpec(
        num_scalar_prefetch=1,
        grid=(n // MOE_TM, ne, ff // MOE_TF),
        in_specs=[
            pl.BlockSpec((MOE_TM, d), lambda t, e, j, c: (t, 0)),
            pl.BlockSpec((nr, MOE_TM), lambda t, e, j, c: (0, t)),
            pl.BlockSpec((MOE_SUB, MOE_SUB), lambda t, e, j, c: (0, 0)),
            pl.BlockSpec((None, d, MOE_TF), lambda t, e, j, c: (e, 0, j)),
            pl.BlockSpec((None, d, MOE_TF), lambda t, e, j, c: (e, 0, j)),
            pl.BlockSpec((None, MOE_TF, d), lambda t, e, j, c: (e, j, 0)),
        ],
        out_specs=pl.BlockSpec((MOE_TM, d), lambda t, e, j, c: (t, 0)),
        scratch_shapes=[
            pltpu.VMEM((nr, MOE_TM), F32),
            pltpu.VMEM((nbmax, MOE_BLK, d), BF16),
            pltpu.VMEM((nbmax, MOE_BLK, d), F32),
        ],
    )
    return pl.pallas_call(
        _moe_kernel,
        grid_spec=grid_spec,
        out_shape=jax.ShapeDtypeStruct((n, d), F32),
        compiler_params=_cparams("parallel", "arbitrary", "arbitrary"),
        name="moe",
    )(cnt, h, cmb, tri, wg, wu, wd)


def _final_kernel(x_ref, m_ref, gate_ref, fg_ref, o_ref):
    y = x_ref[...] + gate_ref[...] * m_ref[...]
    ms = jnp.mean(y * y, axis=-1, keepdims=True)
    o_ref[...] = y * lax.rsqrt(ms + NORM_EPS) * fg_ref[...]


def _final(x, m, gate, final_g):
    b, l, d = x.shape
    nt = l // MOE_TR
    return pl.pallas_call(
        _final_kernel,
        grid=(b, nt),
        in_specs=[
            pl.BlockSpec((None, MOE_TR, d), lambda bi, i: (bi, i, 0)),
            pl.BlockSpec((MOE_TR, d), lambda bi, i: (bi * nt + i, 0)),
            pl.BlockSpec((None, 1, d), lambda bi, i: (bi, 0, 0)),
            pl.BlockSpec((1, d), lambda bi, i: (0, 0)),
        ],
        out_specs=pl.BlockSpec((None, MOE_TR, d), lambda bi, i: (bi, i, 0)),
        out_shape=jax.ShapeDtypeStruct((b, l, d), F32),
        compiler_params=_cparams("parallel", "parallel"),
        name="moe_final",
    )(x, m, gate, final_g)


def _shift_masks(mu, ctx_len, seq_len):
    slab = mu.shape[0]
    nct = ctx_len // TM
    tt = ctx_len + seq_len
    t = jnp.arange(tt)
    is_ctx = t < ctx_len
    tl = t - ctx_len
    col = tl % GRID_W
    rows = seq_len // GRID_W
    grow = tl // GRID_W
    left = jnp.where(is_ctx, t != 0, col != 0)
    right = jnp.where(is_ctx, t != ctx_len - 1, col != GRID_W - 1)
    upv = jnp.where(is_ctx, False, grow != 0)
    dnv = jnp.where(is_ctx, False, grow != rows - 1)
    zero = jnp.zeros_like(left)
    rowmask = jnp.stack([left, right, upv, dnv, zero, zero, zero, zero], axis=-1).astype(F32)
    rowmask = rowmask.reshape(tt // TM, TM, 8)
    c = jnp.arange(slab)
    z = jnp.zeros_like(mu)
    lat = jnp.stack([mu * (c % 4 == 0), mu * (c % 4 == 1), mu * (c % 4 == 2), mu * (c % 4 == 3),
                     1.0 - mu, z, z, z])
    ctx = jnp.stack([mu * (c % 2 == 0), mu * (c % 2 == 1), z, z, 1.0 - mu, z, z, z])
    return rowmask, jnp.stack([ctx, lat]).astype(F32)


def _pad_rows(wt):
    z = jnp.zeros_like(wt[0])
    wp = jnp.stack([jnp.concatenate([wt[0], z], axis=0), jnp.concatenate([z, wt[1]], axis=0)])
    hi = wp.astype(BF16)
    lo = (wp - hi.astype(F32)).astype(BF16)
    return jnp.stack([hi, lo], axis=1)


def kernel(x, c, ctx, c_ctx, ada_w, ada_b, norm1_g, norm2_g, w_in, w_out, shift_mu, rwkv_w0, rwkv_w_up, rwkv_a0, rwkv_a_up, rwkv_g_up, rwkv_k_k, rwkv_k_a, rwkv_r_k, rwkv_ln_w, rwkv_ln_b, s5_lam_re, s5_lam_im, s5_log_dt, s5_b_re, s5_b_im, s5_c_re, s5_c_im, s5_d, s5_glu_w, s5_glu_b, ffn_w_gate, ffn_w_up, ffn_w_down, moe_router, moe_w_gate, moe_w_up, moe_w_down, final_g):
    b, l, d = x.shape
    ctx_len = ctx.shape[1]
    depth = ada_w.shape[0]
    slab_w = shift_mu.shape[1]
    rw_w = rwkv_k_k.shape[1]
    assert ctx_len == TM and l % TM == 0 and b + 1 <= 8
    assert rw_w % (WKV_HEADS * HEAD) == 0 and depth == 2
    nct = ctx_len // TM
    nctc = ctx_len // WKV_CHUNK
    nctc16 = ctx_len // S5_CHUNK

    act = jnp.zeros((8, d), F32).at[:b].set(c).at[b].set(c_ctx)
    mods = _ada_mod(act, ada_w, ada_b).reshape(depth, 8, 6, d)

    def mod(i, k):
        cm = jnp.broadcast_to(mods[i, b, k][None, :], (b, d))
        return jnp.stack([cm, mods[i, :b, k]], axis=1)[:, :, None, :]

    hi = lax.broadcasted_iota(jnp.int32, (rw_w, rw_w), 0) // HEAD
    hj = lax.broadcasted_iota(jnp.int32, (rw_w, rw_w), 1) // HEAD
    bd = (hi == hj).astype(BF16)

    ti = lax.broadcasted_iota(jnp.int32, (TM, TM), 0)
    si = lax.broadcasted_iota(jnp.int32, (TM, TM), 1)
    same_chunk = (ti // WKV_CHUNK) == (si // WKV_CHUNK)
    tri = jnp.stack([same_chunk & (si <= ti), same_chunk & (si >= ti)]).astype(BF16)

    xcat = jnp.concatenate([ctx, x], axis=1)
    out = None
    for i in range(depth):
        last = i == depth - 1
        p, u8 = _inproj(xcat, norm1_g[i][None], mod(i, 0), mod(i, 1), w_in[i].astype(BF16), nct,
                        s5_d.shape[1])
        rowmask, lanec = _shift_masks(shift_mu[i], ctx_len, l)
        v, at, rt, bg, kg, ee, g, bo = _rwkv_prep(
            p, rowmask, lanec, rwkv_k_k[i][None], rwkv_k_a[i][None], rwkv_r_k[i].reshape(1, -1),
            rwkv_w0[i], rwkv_a0[i], _pad_rows(rwkv_w_up[i]), _pad_rows(rwkv_a_up[i]),
            _pad_rows(rwkv_g_up[i]), bd, tri, nct, slab_w)
        yf, yr = _wkv_scan(v, at, rt, bg, kg, ee, nctc)
        s5w = _s5_weights(s5_lam_re[i], s5_lam_im[i], s5_log_dt[i], s5_b_re[i], s5_b_im[i],
                          s5_c_re[i], s5_c_im[i])
        ys = _s5_mix(u8, s5w, b, nctc16)
        t0 = nct if last else 0
        xm = _mixout(xcat, yf, yr, g, bo, ys, p, rwkv_ln_w[i].reshape(1, -1), rwkv_ln_b[i].reshape(1, -1),
                     bd, s5_d[i][None], s5_glu_w[i].astype(BF16), s5_glu_b[i][None],
                     w_out[i].astype(BF16), mod(i, 2), nct, t0)
        if not last:
            j = i // 2
            xcat = _ffn(xm, norm2_g[i][None], mod(i, 3), mod(i, 4), mod(i, 5),
                        ffn_w_gate[j].astype(BF16), ffn_w_up[j].astype(BF16),
                        ffn_w_down[j].astype(BF16), nct)
        else:
            j = i // 2
            ne = moe_router.shape[2]
            nr = -(-ne // 8) * 8
            router_t = jnp.zeros((nr, d), F32).at[:ne].set(moe_router[j].T)
            lat = lambda k: mods[i, :b, k][:, None, :]
            h, cmb, cnt = _route(xm, norm2_g[i][None], lat(3), lat(4), router_t, ne)
            cnt = cnt[:, :ne, 0].reshape(-1, MOE_TM // MOE_TR, ne).sum(axis=1)
            ui = lax.broadcasted_iota(jnp.int32, (MOE_SUB, MOE_SUB), 0)
            uj = lax.broadcasted_iota(jnp.int32, (MOE_SUB, MOE_SUB), 1)
            moe = _moe(h, cmb, cnt, (ui < uj).astype(BF16), moe_w_gate[j].astype(BF16),
                       moe_w_up[j].astype(BF16), moe_w_down[j].astype(BF16))
            out = _final(xm, moe, lat(5), final_g[None])
    return out
```

```python
import functools
import math

import jax
import jax.numpy as jnp
from jax import lax
from jax.experimental import pallas as pl
from jax.experimental.pallas import tpu as pltpu

F32 = jnp.float32
BF16 = jnp.bfloat16
HIGHEST = lax.Precision.HIGHEST

GRID_W = 64
HEAD = 64
DECAY_RANK = 64
ICL_RANK = 64
GATE_RANK = 128
S5_GROUP = 16
S5_STATE = 64
NORM_EPS = 1e-6
GN_EPS = 64e-5
L2_EPS = 1e-12
LAM_RE_MAX = -1e-4
TOP_K = 2

TM = 256
WKV_CHUNK = 64
WKV_HEADS = 4
S5_CHUNK = 16
S5_OCT = 128
VMEM_LIMIT = 56 * 1024 * 1024


def _cparams(*sem):
    return pltpu.CompilerParams(dimension_semantics=sem, vmem_limit_bytes=VMEM_LIMIT)


def _dot(a, b):
    return jnp.dot(a, b, preferred_element_type=F32)


def _dot32(a, b):
    return jnp.dot(a, b, precision=HIGHEST, preferred_element_type=F32)


def _split2(x):
    hi = x.astype(BF16)
    return hi, (x - hi.astype(F32)).astype(BF16)


def _dot_ones(x, ones_bf):
    hi, lo = _split2(x)
    return _dot(hi, ones_bf) + _dot(lo, ones_bf)


def _dot_w2(x, w2_ref):
    hi, lo = _split2(x)
    return _dot(hi, w2_ref[0]) + _dot(lo, w2_ref[0]) + _dot(hi, w2_ref[1])


def _dot_nt(a, b):
    return lax.dot_general(a, b, (((1,), (1,)), ((), ())), preferred_element_type=F32)


def _dot_tn(a, b):
    return lax.dot_general(a, b, (((0,), (0,)), ((), ())), preferred_element_type=F32)


def _ada_kernel(act_ref, w_ref, b_ref, o_ref):
    a = act_ref[...]
    a = a * jax.nn.sigmoid(a)
    o_ref[...] = _dot32(a, w_ref[...]) + b_ref[...]


def _ada_mod(act, ada_w, ada_b):
    depth, d, n = ada_w.shape
    tn = 1536
    return pl.pallas_call(
        _ada_kernel,
        grid=(depth, n // tn),
        in_specs=[
            pl.BlockSpec((8, d), lambda i, j: (0, 0)),
            pl.BlockSpec((None, d, tn), lambda i, j: (i, 0, j)),
            pl.BlockSpec((None, 1, tn), lambda i, j: (i, 0, j)),
        ],
        out_specs=pl.BlockSpec((None, 8, tn), lambda i, j: (i, 0, j)),
        out_shape=jax.ShapeDtypeStruct((depth, 8, n), F32),
        compiler_params=_cparams("arbitrary", "arbitrary"),
        name="ada_mod",
    )(act, ada_w, ada_b.reshape(depth, 1, n))


def _norm_mod(x, g, shift, scale):
    ms = jnp.mean(x * x, axis=-1, keepdims=True)
    y = x * lax.rsqrt(ms + NORM_EPS) * g
    return y * (1.0 + scale) + shift


def _inproj_kernel(sw, x_ref, g_ref, sh_ref, sc_ref, w_ref, o_ref, u8_ref, us_ref):
    h = _norm_mod(x_ref[...], g_ref[...], sh_ref[...], sc_ref[...])
    p = _dot(h.astype(BF16), w_ref[...])
    o_ref[...] = p
    base = p.shape[1] - sw
    cpt = TM // S5_CHUNK
    for o8 in range(sw // S5_OCT):
        us_ref[o8] = p[:, base + o8 * S5_OCT:base + (o8 + 1) * S5_OCT]
    for o8 in range(sw // S5_OCT):
        for s in range(S5_CHUNK):
            u8_ref[o8, :, s * S5_OCT:(s + 1) * S5_OCT] = (
                us_ref[o8, pl.ds(s, cpt, stride=S5_CHUNK), :].astype(BF16))


def _inproj(xcat, g, shift, scale, w_bf, nct, sw):
    b, tt, d = xcat.shape
    n = w_bf.shape[1]
    noct = sw // S5_OCT
    cpt = TM // S5_CHUNK
    kw = S5_CHUNK * S5_OCT
    kind = lambda bi, i: (bi, jnp.where(i < nct, 0, 1), 0, 0)
    return pl.pallas_call(
        functools.partial(_inproj_kernel, sw),
        grid=(b, tt // TM),
        in_specs=[
            pl.BlockSpec((None, TM, d), lambda bi, i: (bi, i, 0)),
            pl.BlockSpec((1, d), lambda bi, i: (0, 0)),
            pl.BlockSpec((None, None, 1, d), kind),
            pl.BlockSpec((None, None, 1, d), kind),
            pl.BlockSpec((d, n), lambda bi, i: (0, 0)),
        ],
        out_specs=[pl.BlockSpec((None, TM, n), lambda bi, i: (bi, i, 0)),
                   pl.BlockSpec((noct, cpt, kw), lambda bi, i: (0, i, bi))],
        out_shape=[jax.ShapeDtypeStruct((b, tt, n), F32),
                   jax.ShapeDtypeStruct((noct, tt // S5_CHUNK, b * kw), BF16)],
        scratch_shapes=[pltpu.VMEM((noct, TM, S5_OCT), F32)],
        compiler_params=_cparams("parallel", "parallel"),
        name="inproj",
    )(xcat, g, shift, scale, w_bf)


def _split3(x):
    hi = x.astype(BF16)
    r1 = x - hi.astype(F32)
    mid = r1.astype(BF16)
    lo = (r1 - mid.astype(F32)).astype(BF16)
    return hi, mid, lo


def _prep_kernel(p_ref, up_ref, dn_ref, rm_ref, lc_ref, kk_ref, ka_ref, rk_ref, w0_ref, a0_ref,
                 wup_ref, aup_ref, gup_ref, bd_ref, tri_ref,
                 v_o, at_o, rt_o, bg_o, kg_o, ee_o, g_o, bo_o):
    x = p_ref[...]
    rm = rm_ref[...]
    lc = lc_ref[...]
    prev = pltpu.roll(x, 1, 0)
    nxt = pltpu.roll(x, TM - 1, 0)
    up = jnp.concatenate([up_ref[...], x[: TM - GRID_W]], axis=0)
    dn = jnp.concatenate([x[GRID_W:], dn_ref[...]], axis=0)
    slab = (x * lc[4:5]
            + rm[:, 0:1] * (prev * lc[0:1])
            + rm[:, 1:2] * (nxt * lc[1:2])
            + rm[:, 2:3] * (up * lc[2:3])
            + rm[:, 3:4] * (dn * lc[3:4]))
    w = kk_ref.shape[1]
    r = slab[:, 0:w]
    k = slab[:, w:2 * w]
    v = slab[:, 2 * w:3 * w]
    o = 3 * w
    wd = slab[:, o:o + 2 * DECAY_RANK]
    ad = slab[:, o + 2 * DECAY_RANK:o + 2 * DECAY_RANK + 2 * ICL_RANK]
    gd = slab[:, o + 2 * DECAY_RANK + 2 * ICL_RANK:]
    bd = bd_ref[...]
    kk = k * kk_ref[...]
    nrm = jnp.sqrt(_dot_ones(kk * kk, bd))
    kk = kk / jnp.maximum(nrm, L2_EPS)
    v_o[...] = v.astype(BF16)
    twd = jnp.tanh(wd)
    sgd = jax.nn.sigmoid(gd)
    c = WKV_CHUNK
    for d in range(2):
        z = w0_ref[d:d + 1, :] + _dot_w2(twd, wup_ref.at[d])
        w_log = -jax.nn.softplus(-z) - 0.5
        lw = -jnp.exp(w_log)
        a = jax.nn.sigmoid(a0_ref[d:d + 1, :] + _dot_w2(ad, aup_ref.at[d]))
        kt = k * (1.0 + (a - 1.0) * ka_ref[...])
        g_o[d] = _dot_w2(sgd, gup_ref.at[d])
        bo_o[d] = _dot_ones(r * kt * rk_ref[...], bd) * v
        tri = tri_ref[d]
        hi, mid, lo = _split3(lw)
        lg_in = _dot(tri, hi) + _dot(tri, mid) + _dot(tri, lo)
        e_neg = jnp.exp(-lg_in)
        at_o[d] = (-kk * jnp.exp(lg_in - lw)).astype(BF16)
        rt_o[d] = (r * jnp.exp(lg_in)).astype(BF16)
        bg_o[d] = (kk * a * e_neg).astype(BF16)
        kg_o[d] = (kt * e_neg).astype(BF16)
        for ci in range(TM // c):
            last = ci * c + (c - 1 if d == 0 else 0)
            ee_o[d, ci] = jnp.exp(lg_in[last:last + 1, :])


def _rwkv_prep(p, rowmask, lanec, k_k, k_a, r_k, w0, a0, wup, aup, gup, bd, tri, nct, slab_w):
    b, tt, _ = p.shape
    w = k_k.shape[1]
    nt = tt // TM
    cpt = TM // WKV_CHUNK
    hb = TM // GRID_W
    nhb = tt // GRID_W
    full = lambda *s: pl.BlockSpec(s, lambda bi, i: (0,) * len(s))
    tok = pl.BlockSpec((None, TM, w), lambda bi, i: (bi, i, 0))
    tok2 = pl.BlockSpec((2, None, TM, w), lambda bi, i: (0, bi, i, 0))
    bf1 = jax.ShapeDtypeStruct((b, tt, w), BF16)
    bf2 = jax.ShapeDtypeStruct((2, b, tt, w), BF16)
    sh2 = jax.ShapeDtypeStruct((2, b, tt, w), F32)
    return pl.pallas_call(
        _prep_kernel,
        grid=(b, nt),
        in_specs=[
            pl.BlockSpec((None, TM, slab_w), lambda bi, i: (bi, i, 0)),
            pl.BlockSpec((None, GRID_W, slab_w), lambda bi, i: (bi, jnp.maximum(i * hb - 1, 0), 0)),
            pl.BlockSpec((None, GRID_W, slab_w),
                         lambda bi, i: (bi, jnp.minimum(i * hb + hb, nhb - 1), 0)),
            pl.BlockSpec((None, TM, 8), lambda bi, i: (i, 0, 0)),
            pl.BlockSpec((None, 8, slab_w), lambda bi, i: (jnp.where(i < nct, 0, 1), 0, 0)),
            full(1, w), full(1, w), full(1, w), full(2, w), full(2, w),
            full(2, 2, 2 * DECAY_RANK, w), full(2, 2, 2 * ICL_RANK, w), full(2, 2, 2 * GATE_RANK, w),
            full(w, w), full(2, TM, TM),
        ],
        out_specs=[tok, tok2, tok2, tok2, tok2,
                   pl.BlockSpec((2, None, cpt, 1, w), lambda bi, i: (0, bi, i, 0, 0)),
                   tok2, tok2],
        out_shape=[bf1, bf2, bf2, bf2, bf2,
                   jax.ShapeDtypeStruct((2, b, tt // WKV_CHUNK, 1, w), F32), sh2, sh2],
        compiler_params=_cparams("parallel", "parallel"),
        name="rwkv_prep",
    )(p, p, p, rowmask, lanec, k_k, k_a, r_k, w0, a0, wup, aup, gup, bd, tri)


def _wkv_kernel(nb, ngrp, v_f, v_r, at_f, at_r, rt_f, rt_r, bg_f, bg_r, kg_f, kg_r, ee_f, ee_r,
                y_f, y_r, ht_ref):
    j = pl.program_id(0)
    c = WKV_CHUNK
    gw = WKV_HEADS * HEAD
    gn = WKV_HEADS * c

    @pl.when(j == 0)
    def _():
        ht_ref[...] = jnp.zeros_like(ht_ref)

    sh = int(math.log2(c))
    row = lax.broadcasted_iota(jnp.int32, (gn, gw), 0)
    col = lax.broadcasted_iota(jnp.int32, (gn, gw), 1)
    same = (row >> sh) == (col >> sh)
    tf = lax.broadcasted_iota(jnp.int32, (c, gn), 0)
    sf = lax.broadcasted_iota(jnp.int32, (c, gn), 1) & (c - 1)
    eye = (tf == sf).astype(F32)

    def stack(x):
        xb = jnp.concatenate([x.astype(BF16)] * WKV_HEADS, axis=0)
        return jnp.where(same, xb, jnp.zeros_like(xb))

    dirs = ((v_f, at_f, rt_f, bg_f, kg_f, ee_f, y_f, sf < tf, sf <= tf),
            (v_r, at_r, rt_r, bg_r, kg_r, ee_r, y_r, sf > tf, sf >= tf))
    chains = [(d, bi, q) for d in range(2) for bi in range(nb) for q in range(ngrp)]
    sl = lambda q: slice(q * gw, (q + 1) * gw)
    rd = lambda k: [dirs[d][k][bi, :, sl(q)] for d, bi, q in chains]
    cat0 = lambda xs: jnp.concatenate(xs, axis=0)
    v, at, rt, bg, kg, ee = rd(0), rd(1), rd(2), rd(3), rd(4), rd(5)
    before = [dirs[d][7] for d, _, _ in chains]
    incl = [dirs[d][8] for d, _, _ in chains]
    n_ch = range(len(chains))

    v_bd = [stack(x) for x in v]
    at_bd = [stack(x) for x in at]
    bk_bd = [cat0([stack(bg[i]), stack(kg[i])]) for i in n_ch]
    a = [_dot_nt(cat0([at[i], rt[i]]), bk_bd[i]) for i in n_ch]
    n = [jnp.where(before[i], a[i][0:c, 0:gn], 0.0) for i in n_ch]
    a_kk = [cat0([jnp.where(before[i], a[i][0:c, gn:], 0.0),
                  jnp.where(incl[i], a[i][c:, gn:], 0.0)]).astype(BF16) for i in n_ch]
    a_rb = [jnp.where(incl[i], a[i][c:, 0:gn], 0.0).astype(BF16) for i in n_ch]
    tm = [eye + x for x in n]
    pw = [_dot(x.astype(BF16), stack(x)) for x in n]
    for lvl in range(1, sh):
        pw_bd = [stack(x) for x in pw]
        if lvl < sh - 1:
            tp = [_dot(cat0([tm[i].astype(BF16), pw[i].astype(BF16)]), pw_bd[i]) for i in n_ch]
            tm = [tm[i] + tp[i][0:c] for i in n_ch]
            pw = [tp[i][c:] for i in n_ch]
        else:
            tm = [tm[i] + _dot(tm[i].astype(BF16), pw_bd[i]) for i in n_ch]
    tm_b = [x.astype(BF16) for x in tm]
    atp = [_dot(tm_b[i], at_bd[i]) for i in n_ch]
    av = [_dot(a_kk[i], v_bd[i]) for i in n_ch]
    wv = [_dot(tm_b[i], stack(av[i][0:c])) for i in n_ch]
    wv_bd = [stack(x) for x in wv]
    atp_bd = [stack(x) for x in atp]
    ar = [_dot(a_rb[i], jnp.concatenate([wv_bd[i], atp_bd[i]], axis=1)) for i in n_ch]
    y0 = [ar[i][:, 0:gw] + av[i][c:] for i in n_ch]
    rtp = [(ar[i][:, gw:] + rt[i].astype(F32)).astype(BF16) for i in n_ch]
    bge_bd = [stack(bg[i].astype(F32) * ee[i]) for i in n_ch]
    kge_bd = [stack(kg[i].astype(F32) * ee[i]) for i in n_ch]
    g = [_dot_tn(bge_bd[i], atp_bd[i]).astype(BF16) for i in n_ch]
    hloc_t = [_dot_tn(cat0([wv_bd[i], v_bd[i]]), cat0([bge_bd[i], kge_bd[i]])) for i in n_ch]
    for i, (d, bi, q) in enumerate(chains):
        ht = ht_ref[d, bi, q]
        ht_b = ht.astype(BF16)
        dirs[d][6][bi, :, sl(q)] = y0[i] + _dot_nt(rtp[i], ht_b)
        ht_ref[d, bi, q] = ht * ee[i] + _dot_nt(ht_b, g[i]) + hloc_t[i]


def _wkv_scan(v, at, rt, bg, kg, ee, nctc):
    b, tt, w = v.shape
    ntot = tt // WKV_CHUNK
    ngrp = w // (WKV_HEADS * HEAD)
    fwd = lambda j: j
    rev = lambda j: jnp.where(j < nctc, nctc - 1 - j, ntot - 1 + nctc - j)
    tok = lambda cm: pl.BlockSpec((b, WKV_CHUNK, w), lambda j: (0, cm(j), 0))
    tok2 = lambda d, cm: pl.BlockSpec((None, b, WKV_CHUNK, w), lambda j: (d, 0, cm(j), 0))
    eesp = lambda d, cm: pl.BlockSpec((None, b, None, 1, w), lambda j: (d, 0, cm(j), 0, 0))
    pair = lambda f: [f(0, fwd), f(1, rev)]
    ysh = jax.ShapeDtypeStruct((b, tt, w), F32)
    return pl.pallas_call(
        functools.partial(_wkv_kernel, b, ngrp),
        grid=(ntot,),
        in_specs=[tok(fwd), tok(rev)] + pair(tok2) + pair(tok2) + pair(tok2) + pair(tok2) + pair(eesp),
        out_specs=[tok(fwd), tok(rev)],
        out_shape=[ysh, ysh],
        scratch_shapes=[pltpu.VMEM((2, b, ngrp, WKV_HEADS * HEAD, WKV_HEADS * HEAD), F32)],
        compiler_params=_cparams("arbitrary"),
        name="wkv_scan",
    )(v, v, at, at, rt, rt, bg, bg, kg, kg, ee, ee)


def _s5_weights(lam_re, lam_im, log_dt, b_re, b_im, c_re, c_im):
    tc = S5_CHUNK
    lr = jnp.minimum(lam_re.astype(F32), LAM_RE_MAX)
    li = lam_im.astype(F32)
    dt = jnp.exp(log_dt.astype(F32))[..., None]
    mag = jnp.exp(lr * dt)
    ar = mag * jnp.cos(li * dt)
    ai = mag * jnp.sin(li * dt)
    den = lr * lr + li * li
    xr = ar - 1.0
    cr = (xr * lr + ai * li) / den
    ci = (ai * lr - xr * li) / den
    br = cr[..., None] * b_re - ci[..., None] * b_im
    bi = cr[..., None] * b_im + ci[..., None] * b_re
    pr, pi = [jnp.ones_like(ar)], [jnp.zeros_like(ar)]
    for _ in range(tc):
        pr_n = pr[-1] * ar - pi[-1] * ai
        pi_n = pr[-1] * ai + pi[-1] * ar
        pr.append(pr_n)
        pi.append(pi_n)
    pr = jnp.stack(pr)
    pi = jnp.stack(pi)
    lbr = pr[..., None] * br - pi[..., None] * bi
    lbi = pr[..., None] * bi + pi[..., None] * br
    clr = c_re * pr[:, :, :, None, :] - c_im * pi[:, :, :, None, :]
    cli = c_re * pi[:, :, :, None, :] + c_im * pr[:, :, :, None, :]
    lbr_t = jnp.swapaxes(lbr, -1, -2)
    lbi_t = jnp.swapaxes(lbi, -1, -2)
    kern_t = jnp.sum(lbr_t[..., :, None, :] * c_re[None, :, :, None, :, :]
                     - lbi_t[..., :, None, :] * c_im[None, :, :, None, :, :], axis=-1)
    g = ar.shape[1]
    og = S5_OCT // S5_GROUP
    noct = g // og
    eye = jnp.eye(og, dtype=F32)

    def bdiag(x):
        nt, _, _, a, n = x.shape
        x = x.reshape(nt, 2, noct, og, a, n)
        y = x[:, :, :, :, :, None, :] * eye[None, None, None, :, None, :, None]
        return jnp.transpose(y, (1, 2, 0, 3, 4, 5, 6)).reshape(2, noct, nt, og * a, og * n).astype(BF16)

    kbd = bdiag(kern_t)

    def pair_block(d, lp):
        kd = kbd[d]
        zero = jnp.zeros_like(kd[:, 0])
        k = lambda tau: kd[:, tau] if tau >= 0 else zero
        if d == 0:
            rows = [[k(2 * lp), k(2 * lp + 1)], [k(2 * lp - 1), k(2 * lp)]]
        else:
            rows = [[k(2 * lp), k(2 * lp - 1)], [k(2 * lp + 1), k(2 * lp)]]
        return jnp.concatenate([jnp.concatenate(r, axis=-1) for r in rows], axis=-2)

    wpair = jnp.stack([jnp.stack([pair_block(d, lp) for lp in range(tc // 2)], axis=1)
                       for d in range(2)])
    lbc = jnp.concatenate([lbr_t, lbi_t], axis=-1)
    clc = jnp.concatenate([clr, -cli], axis=-1)
    pout8, qin8 = _s5_expand(lbc, clc)
    la = jnp.concatenate([pr[tc], pr[tc]], axis=-1)
    lb = jnp.concatenate([-pi[tc], pi[tc]], axis=-1)
    return wpair, pout8, qin8, la, lb


def _s5_expand_kernel(p_ref, q_ref, po_ref, qo_ref):
    for x_ref, o_ref in ((p_ref, po_ref), (q_ref, qo_ref)):
        o_ref[...] = jnp.zeros_like(o_ref)
        a, n = x_ref.shape[1:]
        for gi in range(x_ref.shape[0]):
            o_ref[gi * a:(gi + 1) * a, gi * n:(gi + 1) * n] = x_ref[gi].astype(BF16)


def _s5_expand(lbc, clc):
    nt, _, g, a, n = lbc.shape
    tc = nt - 1
    og = S5_OCT // S5_GROUP
    noct = g // og
    lag_out = lambda d, o, s: (jnp.where(d == 0, tc - 1 - s, s), d, o, 0, 0)
    lag_in = lambda d, o, s: (jnp.where(d == 0, s + 1, tc - s), d, o, 0, 0)
    osp = pl.BlockSpec((None, None, None, og * a, og * n), lambda d, o, s: (d, o, s, 0, 0))
    osh = jax.ShapeDtypeStruct((2, noct, tc, og * a, og * n), BF16)
    po, qo = pl.pallas_call(
        _s5_expand_kernel,
        grid=(2, noct, tc),
        in_specs=[pl.BlockSpec((None, None, og, a, n), lag_out),
                  pl.BlockSpec((None, None, og, a, n), lag_in)],
        out_specs=[osp, osp],
        out_shape=[osh, osh],
        compiler_params=_cparams("parallel", "parallel", "parallel"),
        name="s5_expand",
    )(lbc, clc)
    return po.reshape(2, noct, tc * og * a, og * n), qo.reshape(2, noct, tc * og * a, og * n)


def _s5_local_kernel(u_ref, p_ref, e_ref):
    e_ref[...] = _dot(u_ref[...], p_ref[...])


def _s5_local(u8, pout8, nb):
    noct, nch, _ = u8.shape
    kw, n = pout8.shape[2:]
    return pl.pallas_call(
        _s5_local_kernel,
        grid=(noct, nb, 2),
        in_specs=[
            pl.BlockSpec((None, nch, kw), lambda o, b, d: (o, 0, b)),
            pl.BlockSpec((None, None, kw, n), lambda o, b, d: (d, o, 0, 0)),
        ],
        out_specs=pl.BlockSpec((None, nch, n), lambda o, b, d: (d, 0, o * nb + b)),
        out_shape=jax.ShapeDtypeStruct((2, nch, noct * nb * n), F32),
        compiler_params=_cparams("parallel", "parallel", "parallel"),
        name="s5_local",
    )(u8, pout8)


S5_STATE_ROWS = 8
S5_STATE_LANES = 256


def _s5_state_kernel(nctc, ntot, e_ref, la_ref, lb_ref, x_ref, es_ref):
    d = pl.program_id(0)
    la = la_ref[...]
    lb = lb_ref[...]
    nr, wl = la.shape

    def swap(t):
        lane = lax.broadcasted_iota(jnp.int32, t.shape, 1)
        first_half = (lane & (2 * S5_STATE - 1)) < S5_STATE
        return jnp.where(first_half, pltpu.roll(t, wl - S5_STATE, 1), pltpu.roll(t, S5_STATE, 1))

    es_ref[...] = swap(e_ref[...].reshape(ntot * nr, wl)).reshape(ntot, nr, wl)
    lbs = swap(lb)

    def body(j, carry):
        x, xs = carry
        rev_idx = jnp.where(j < nctc, nctc - 1 - j, ntot - 1 + nctc - j)
        c = jnp.where(d == 0, j, rev_idx)
        x_ref[c] = x
        return la * x + lb * xs + e_ref[c], la * xs + lbs * x + es_ref[c]

    zero = jnp.zeros(la.shape, F32)
    lax.fori_loop(0, ntot, body, (zero, zero))


def _s5_state(e, la, lb, nctc, ntot):
    _, nch, nr, lanes = e.shape
    wl = S5_STATE_LANES
    blk = pl.BlockSpec((None, nch, nr, wl), lambda d, i: (d, 0, 0, i))
    cf = pl.BlockSpec((None, nr, wl), lambda d, i: (d, 0, i))
    return pl.pallas_call(
        functools.partial(_s5_state_kernel, nctc, ntot),
        grid=(2, lanes // wl),
        in_specs=[blk, cf, cf],
        out_specs=blk,
        out_shape=jax.ShapeDtypeStruct(e.shape, F32),
        scratch_shapes=[pltpu.VMEM((nch, nr, wl), F32)],
        compiler_params=_cparams("parallel", "parallel"),
        name="s5_state",
    )(e, la, lb)


def _s5_out_kernel(rev, u_ref, w_ref, x_ref, q_ref, y_ref):
    pw = w_ref.shape[1]
    npair = w_ref.shape[0]
    x = x_ref[...].astype(BF16)
    for tp in range(npair):
        acc = _dot_nt(x, q_ref[tp * pw:(tp + 1) * pw, :])
        for lp in range(npair - tp if rev else tp + 1):
            sp = tp + lp if rev else tp - lp
            acc = acc + _dot(u_ref[:, sp * pw:(sp + 1) * pw], w_ref[lp])
        y_ref[:, tp * pw:(tp + 1) * pw] = acc


def _s5_out(u8, wpair, xin, qin8, nb, d):
    noct, nch, _ = u8.shape
    npair, pw = wpair.shape[2:4]
    kw = npair * pw
    n2 = qin8.shape[3]
    return pl.pallas_call(
        functools.partial(_s5_out_kernel, d == 1),
        grid=(noct, nb),
        in_specs=[
            pl.BlockSpec((None, nch, kw), lambda o, b: (o, 0, b)),
            pl.BlockSpec((None, None, npair, pw, pw), lambda o, b: (d, o, 0, 0, 0)),
            pl.BlockSpec((None, nch, n2), lambda o, b: (d, 0, o * nb + b)),
            pl.BlockSpec((None, None, kw, n2), lambda o, b: (d, o, 0, 0)),
        ],
        out_specs=pl.BlockSpec((None, nch, kw), lambda o, b: (o, 0, b)),
        out_shape=jax.ShapeDtypeStruct((noct, nch, nb * kw), F32),
        compiler_params=_cparams("parallel", "parallel"),
        name="s5_out",
    )(u8, wpair, xin, qin8)


def _s5_mix(u8, weights, nb, nctc16):
    wpair, pout8, qin8, la, lb = weights
    noct, nch, _ = u8.shape
    e = _s5_local(u8, pout8, nb)
    assert noct * nb == S5_STATE_ROWS
    fl = e.shape[2] // S5_STATE_ROWS
    coef = lambda t: jnp.repeat(t.reshape(2, noct, fl), nb, axis=1)
    xin = _s5_state(e.reshape(2, nch, S5_STATE_ROWS, fl), coef(la), coef(lb), nctc16, nch)
    xin = xin.reshape(e.shape)
    return _s5_out(u8, wpair, xin, qin8, nb, 0), _s5_out(u8, wpair, xin, qin8, nb, 1)


def _mixout_kernel(x_ref, yf_ref, yr_ref, g_ref, bo_ref, y8f_ref, y8r_ref, u_ref, lnw_ref, lnb_ref,
                   bd_ref, dsk_ref, gluw_ref, glub_ref, wout_ref, gate_ref, o_ref, ysn_ref):
    cpt = TM // S5_CHUNK
    for o8 in range(y8f_ref.shape[0]):
        for s in range(S5_CHUNK):
            lanes = slice(s * S5_OCT, (s + 1) * S5_OCT)
            ysn_ref[o8, pl.ds(s, cpt, stride=S5_CHUNK), :] = y8f_ref[o8, :, lanes] + y8r_ref[o8, :, lanes]
    ys = jnp.concatenate([ysn_ref[o8] for o8 in range(y8f_ref.shape[0])], axis=1)
    bd = bd_ref[...]
    inv = 1.0 / HEAD
    rw = None
    for d, y_ref in enumerate((yf_ref, yr_ref)):
        y = y_ref[...]
        mean = _dot_ones(y, bd) * inv
        yc = y - mean
        var = _dot_ones(yc * yc, bd) * inv
        yn = yc * lax.rsqrt(var + GN_EPS) * lnw_ref[...] + lnb_ref[...]
        o = (yn + bo_ref[d]) * g_ref[d]
        rw = o if rw is None else rw + o
    u = u_ref[...]
    ss = ys + dsk_ref[...] * u
    ss = jax.nn.gelu(ss)
    ss = ss * jax.nn.sigmoid(_dot(ss.astype(BF16), gluw_ref[...]) + glub_ref[...])
    w = rw.shape[1]
    mix = _dot(rw.astype(BF16), wout_ref[0:w, :]) + _dot(ss.astype(BF16), wout_ref[w:, :])
    o_ref[...] = x_ref[...] + gate_ref[...] * mix


def _mixout(xcat, yf, yr, g, bo, y8, p, ln_w, ln_b, bd, d_skip, glu_w, glu_b, w_out, gate, nct, t0):
    b, tt, d = xcat.shape
    w = ln_w.shape[1]
    sw = d_skip.shape[1]
    y8f, y8r = y8
    noct = y8f.shape[0]
    cpt = TM // S5_CHUNK
    kw = S5_CHUNK * S5_OCT
    ublk = (p.shape[2] - sw) // sw
    nt = tt // TM - t0
    full = lambda *s: pl.BlockSpec(s, lambda bi, i: (0,) * len(s))
    tok = pl.BlockSpec((None, TM, w), lambda bi, i: (bi, i + t0, 0))
    tok2 = pl.BlockSpec((2, None, TM, w), lambda bi, i: (0, bi, i + t0, 0))
    return pl.pallas_call(
        _mixout_kernel,
        grid=(b, nt),
        in_specs=[
            pl.BlockSpec((None, TM, d), lambda bi, i: (bi, i + t0, 0)),
            tok, tok, tok2, tok2,
            pl.BlockSpec((noct, cpt, kw), lambda bi, i: (0, i + t0, bi)),
            pl.BlockSpec((noct, cpt, kw), lambda bi, i: (0, i + t0, bi)),
            pl.BlockSpec((None, TM, sw), lambda bi, i: (bi, i + t0, ublk)),
            full(1, w), full(1, w), full(w, w), full(1, sw), full(sw, sw), full(1, sw),
            full(w + sw, d),
            pl.BlockSpec((None, None, 1, d), lambda bi, i: (bi, jnp.where(i + t0 < nct, 0, 1), 0, 0)),
        ],
        out_specs=pl.BlockSpec((None, TM, d), lambda bi, i: (bi, i, 0)),
        out_shape=jax.ShapeDtypeStruct((b, nt * TM, d), F32),
        scratch_shapes=[pltpu.VMEM((noct, TM, S5_OCT), F32)],
        compiler_params=_cparams("parallel", "parallel"),
        name="mix_out",
    )(xcat, yf, yr, g, bo, y8f, y8r, p, ln_w, ln_b, bd, d_skip, glu_w, glu_b, w_out, gate)


def _ffn_kernel(x_ref, g_ref, sh_ref, sc_ref, gate_ref, wg_ref, wu_ref, wd_ref, o_ref):
    x = x_ref[...]
    h = _norm_mod(x, g_ref[...], sh_ref[...], sc_ref[...]).astype(BF16)
    a = _dot(h, wg_ref[...])
    a = a * jax.nn.sigmoid(a) * _dot(h, wu_ref[...])
    o_ref[...] = x + gate_ref[...] * _dot(a.astype(BF16), wd_ref[...])


def _ffn(xcat, g, shift, scale, gate, wg, wu, wd, nct):
    b, tt, d = xcat.shape
    ff = wg.shape[1]
    kind = lambda bi, i: (bi, jnp.where(i < nct, 0, 1), 0, 0)
    mod = pl.BlockSpec((None, None, 1, d), kind)
    return pl.pallas_call(
        _ffn_kernel,
        grid=(b, tt // TM),
        in_specs=[
            pl.BlockSpec((None, TM, d), lambda bi, i: (bi, i, 0)),
            pl.BlockSpec((1, d), lambda bi, i: (0, 0)),
            mod, mod, mod,
            pl.BlockSpec((d, ff), lambda bi, i: (0, 0)),
            pl.BlockSpec((d, ff), lambda bi, i: (0, 0)),
            pl.BlockSpec((ff, d), lambda bi, i: (0, 0)),
        ],
        out_specs=pl.BlockSpec((None, TM, d), lambda bi, i: (bi, i, 0)),
        out_shape=jax.ShapeDtypeStruct((b, tt, d), F32),
        compiler_params=_cparams("parallel", "parallel"),
        name="ffn",
    )(xcat, g, shift, scale, gate, wg, wu, wd)


MOE_TR = 1024
MOE_TM = 2048
MOE_TF = 512
MOE_BLK = 256
MOE_SUB = 256


def _route_kernel(ne, x_ref, g_ref, sh_ref, sc_ref, rt_ref, h_o, cmb_o, cnt_o):
    h = _norm_mod(x_ref[...], g_ref[...], sh_ref[...], sc_ref[...])
    h_o[...] = h.astype(BF16)
    logits = lax.dot_general(rt_ref[...], h, (((1,), (1,)), ((), ())), precision=HIGHEST,
                             preferred_element_type=F32)
    sub = lax.broadcasted_iota(jnp.int32, logits.shape, 0).astype(F32)
    none = float(logits.shape[0])
    logits = jnp.where(sub < ne, logits, -jnp.inf)
    m1 = jnp.max(logits, axis=0, keepdims=True)
    i1 = jnp.min(jnp.where(logits == m1, sub, none), axis=0, keepdims=True)
    rest = jnp.where(sub == i1, -jnp.inf, logits)
    m2 = jnp.max(rest, axis=0, keepdims=True)
    i2 = jnp.min(jnp.where(rest == m2, sub, none), axis=0, keepdims=True)
    e2 = jnp.exp(m2 - m1)
    p1 = 1.0 / (1.0 + e2)
    p2 = e2 / (1.0 + e2)
    cmb = jnp.where(sub == i1, p1, 0.0) + jnp.where(sub == i2, p2, 0.0)
    cmb_o[...] = cmb
    cnt = jnp.sum((cmb > 0.0).astype(F32), axis=1, keepdims=True)
    cnt_o[...] = jnp.broadcast_to(cnt, cnt_o.shape).astype(jnp.int32)


def _route(x, g, shift, scale, router_t, ne):
    b, l, d = x.shape
    nr = router_t.shape[0]
    nt = l // MOE_TR
    mod = pl.BlockSpec((None, 1, d), lambda bi, i: (bi, 0, 0))
    return pl.pallas_call(
        functools.partial(_route_kernel, ne),
        grid=(b, nt),
        in_specs=[
            pl.BlockSpec((None, MOE_TR, d), lambda bi, i: (bi, i, 0)),
            pl.BlockSpec((1, d), lambda bi, i: (0, 0)),
            mod, mod,
            pl.BlockSpec((nr, d), lambda bi, i: (0, 0)),
        ],
        out_specs=[
            pl.BlockSpec((MOE_TR, d), lambda bi, i: (bi * nt + i, 0)),
            pl.BlockSpec((nr, MOE_TR), lambda bi, i: (0, bi * nt + i)),
            pl.BlockSpec((None, nr, 128), lambda bi, i: (bi * nt + i, 0, 0)),
        ],
        out_shape=[
            jax.ShapeDtypeStruct((b * l, d), BF16),
            jax.ShapeDtypeStruct((nr, b * l), F32),
            jax.ShapeDtypeStruct((b * nt, nr, 128), jnp.int32),
        ],
        compiler_params=_cparams("parallel", "parallel"),
        name="moe_route",
    )(x, g, shift, scale, router_t)


def _moe_kernel(cnt_ref, h_ref, cmb_ref, tri_ref, wg_ref, wu_ref, wd_ref, o_ref,
                pos_ref, hg_ref, ya_ref):
    t = pl.program_id(0)
    e = pl.program_id(1)
    j = pl.program_id(2)
    tm = h_ref.shape[0]
    nblk = jnp.right_shift(cnt_ref[t, e] + (MOE_BLK - 1), int(math.log2(MOE_BLK)))

    @pl.when((e == 0) & (j == 0))
    def _():
        o_ref[...] = jnp.zeros_like(o_ref)
        asg = (cmb_ref[...] > 0.0).astype(BF16)
        off = jnp.zeros((asg.shape[0], 1), F32)
        for k in range(tm // MOE_SUB):
            blk = asg[:, k * MOE_SUB:(k + 1) * MOE_SUB]
            pos_ref[:, k * MOE_SUB:(k + 1) * MOE_SUB] = _dot(blk, tri_ref[...]) + off
            off = off + jnp.sum(blk.astype(F32), axis=1, keepdims=True)

    sel = lax.broadcasted_iota(jnp.int32, pos_ref.shape, 0) == e
    posrow = jnp.sum(jnp.where(sel, pos_ref[...], 0.0), axis=0, keepdims=True)
    cwrow = jnp.sum(jnp.where(sel, cmb_ref[...], 0.0), axis=0, keepdims=True)
    rowi = lax.broadcasted_iota(jnp.int32, (MOE_BLK, tm), 0).astype(F32)

    def onehot(b):
        slot = rowi + (b * MOE_BLK).astype(F32)
        return (posrow == slot) & (cwrow > 0.0)

    @pl.when(j == 0)
    def _():
        def gather(b, carry):
            sel_b = jnp.where(onehot(b), 1.0, 0.0).astype(BF16)
            hg_ref[b] = _dot(sel_b, h_ref[...]).astype(BF16)
            ya_ref[b] = jnp.zeros(ya_ref.shape[1:], F32)
            return carry
        lax.fori_loop(0, nblk, gather, 0)

    def ffn(b, carry):
        hb = hg_ref[b]
        a = _dot(hb, wg_ref[...])
        a = a * jax.nn.sigmoid(a) * _dot(hb, wu_ref[...])
        ya_ref[b] += _dot(a.astype(BF16), wd_ref[...])
        return carry
    lax.fori_loop(0, nblk, ffn, 0)

    @pl.when(j == pl.num_programs(2) - 1)
    def _():
        def scatter(b, carry):
            wsel = jnp.where(onehot(b), cwrow, 0.0).astype(BF16)
            o_ref[...] += _dot_tn(wsel, ya_ref[b].astype(BF16))
            return carry
        lax.fori_loop(0, nblk, scatter, 0)


def _moe(h, cmb, cnt, tri, wg, wu, wd):
    n, d = h.shape
    nr = cmb.shape[0]
    ne, _, ff = wg.shape
    nbmax = MOE_TM // MOE_BLK
    grid_spec = pltpu.PrefetchScalarGridSpec(
        num_scalar_prefetch=1,
        grid=(n // MOE_TM, ne, ff // MOE_TF),
        in_specs=[
            pl.BlockSpec((MOE_TM, d), lambda t, e, j, c: (t, 0)),
            pl.BlockSpec((nr, MOE_TM), lambda t, e, j, c: (0, t)),
            pl.BlockSpec((MOE_SUB, MOE_SUB), lambda t, e, j, c: (0, 0)),
            pl.BlockSpec((None, d, MOE_TF), lambda t, e, j, c: (e, 0, j)),
            pl.BlockSpec((None, d, MOE_TF), lambda t, e, j, c: (e, 0, j)),
            pl.BlockSpec((None, MOE_TF, d), lambda t, e, j, c: (e, j, 0)),
        ],
        out_specs=pl.BlockSpec((MOE_TM, d), lambda t, e, j, c: (t, 0)),
        scratch_shapes=[
            pltpu.VMEM((nr, MOE_TM), F32),
            pltpu.VMEM((nbmax, MOE_BLK, d), BF16),
            pltpu.VMEM((nbmax, MOE_BLK, d), F32),
        ],
    )
    return pl.pallas_call(
        _moe_kernel,
        grid_spec=grid_spec,
        out_shape=jax.ShapeDtypeStruct((n, d), F32),
        compiler_params=_cparams("parallel", "arbitrary", "arbitrary"),
        name="moe",
    )(cnt, h, cmb, tri, wg, wu, wd)


def _final_kernel(x_ref, m_ref, gate_ref, fg_ref, o_ref):
    y = x_ref[...] + gate_ref[...] * m_ref[...]
    ms = jnp.mean(y * y, axis=-1, keepdims=True)
    o_ref[...] = y * lax.rsqrt(ms + NORM_EPS) * fg_ref[...]


def _final(x, m, gate, final_g):
    b, l, d = x.shape
    nt = l // MOE_TR
    return pl.pallas_call(
        _final_kernel,
        grid=(b, nt),
        in_specs=[
            pl.BlockSpec((None, MOE_TR, d), lambda bi, i: (bi, i, 0)),
            pl.BlockSpec((MOE_TR, d), lambda bi, i: (bi * nt + i, 0)),
            pl.BlockSpec((None, 1, d), lambda bi, i: (bi, 0, 0)),
            pl.BlockSpec((1, d), lambda bi, i: (0, 0)),
        ],
        out_specs=pl.BlockSpec((None, MOE_TR, d), lambda bi, i: (bi, i, 0)),
        out_shape=jax.ShapeDtypeStruct((b, l, d), F32),
        compiler_params=_cparams("parallel", "parallel"),
        name="moe_final",
    )(x, m, gate, final_g)


def _shift_masks(mu, ctx_len, seq_len):
    slab = mu.shape[0]
    nct = ctx_len // TM
    tt = ctx_len + seq_len
    t = jnp.arange(tt)
    is_ctx = t < ctx_len
    tl = t - ctx_len
    col = tl % GRID_W
    rows = seq_len // GRID_W
    grow = tl // GRID_W
    left = jnp.where(is_ctx, t != 0, col != 0)
    right = jnp.where(is_ctx, t != ctx_len - 1, col != GRID_W - 1)
    upv = jnp.where(is_ctx, False, grow != 0)
    dnv = jnp.where(is_ctx, False, grow != rows - 1)
    zero = jnp.zeros_like(left)
    rowmask = jnp.stack([left, right, upv, dnv, zero, zero, zero, zero], axis=-1).astype(F32)
    rowmask = rowmask.reshape(tt // TM, TM, 8)
    c = jnp.arange(slab)
    z = jnp.zeros_like(mu)
    lat = jnp.stack([mu * (c % 4 == 0), mu * (c % 4 == 1), mu * (c % 4 == 2), mu * (c % 4 == 3),
                     1.0 - mu, z, z, z])
    ctx = jnp.stack([mu * (c % 2 == 0), mu * (c % 2 == 1), z, z, 1.0 - mu, z, z, z])
    return rowmask, jnp.stack([ctx, lat]).astype(F32)


def _pad_rows(wt):
    z = jnp.zeros_like(wt[0])
    wp = jnp.stack([jnp.concatenate([wt[0], z], axis=0), jnp.concatenate([z, wt[1]], axis=0)])
    hi = wp.astype(BF16)
    lo = (wp - hi.astype(F32)).astype(BF16)
    return jnp.stack([hi, lo], axis=1)


def kernel(x, c, ctx, c_ctx, ada_w, ada_b, norm1_g, norm2_g, w_in, w_out, shift_mu, rwkv_w0, rwkv_w_up, rwkv_a0, rwkv_a_up, rwkv_g_up, rwkv_k_k, rwkv_k_a, rwkv_r_k, rwkv_ln_w, rwkv_ln_b, s5_lam_re, s5_lam_im, s5_log_dt, s5_b_re, s5_b_im, s5_c_re, s5_c_im, s5_d, s5_glu_w, s5_glu_b, ffn_w_gate, ffn_w_up, ffn_w_down, moe_router, moe_w_gate, moe_w_up, moe_w_down, final_g):
    b, l, d = x.shape
    ctx_len = ctx.shape[1]
    depth = ada_w.shape[0]
    slab_w = shift_mu.shape[1]
    rw_w = rwkv_k_k.shape[1]
    assert ctx_len == TM and l % TM == 0 and b + 1 <= 8
    assert rw_w % (WKV_HEADS * HEAD) == 0 and depth == 2
    nct = ctx_len // TM
    nctc = ctx_len // WKV_CHUNK
    nctc16 = ctx_len // S5_CHUNK

    act = jnp.zeros((8, d), F32).at[:b].set(c).at[b].set(c_ctx)
    mods = _ada_mod(act, ada_w, ada_b).reshape(depth, 8, 6, d)

    def mod(i, k):
        cm = jnp.broadcast_to(mods[i, b, k][None, :], (b, d))
        return jnp.stack([cm, mods[i, :b, k]], axis=1)[:, :, None, :]

    hi = lax.broadcasted_iota(jnp.int32, (rw_w, rw_w), 0) // HEAD
    hj = lax.broadcasted_iota(jnp.int32, (rw_w, rw_w), 1) // HEAD
    bd = (hi == hj).astype(BF16)

    ti = lax.broadcasted_iota(jnp.int32, (TM, TM), 0)
    si = lax.broadcasted_iota(jnp.int32, (TM, TM), 1)
    same_chunk = (ti // WKV_CHUNK) == (si // WKV_CHUNK)
    tri = jnp.stack([same_chunk & (si <= ti), same_chunk & (si >= ti)]).astype(BF16)

    xcat = jnp.concatenate([ctx, x], axis=1)
    out = None
    for i in range(depth):
        last = i == depth - 1
        p, u8 = _inproj(xcat, norm1_g[i][None], mod(i, 0), mod(i, 1), w_in[i].astype(BF16), nct,
                        s5_d.shape[1])
        rowmask, lanec = _shift_masks(shift_mu[i], ctx_len, l)
        v, at, rt, bg, kg, ee, g, bo = _rwkv_prep(
            p, rowmask, lanec, rwkv_k_k[i][None], rwkv_k_a[i][None], rwkv_r_k[i].reshape(1, -1),
            rwkv_w0[i], rwkv_a0[i], _pad_rows(rwkv_w_up[i]), _pad_rows(rwkv_a_up[i]),
            _pad_rows(rwkv_g_up[i]), bd, tri, nct, slab_w)
        yf, yr = _wkv_scan(v, at, rt, bg, kg, ee, nctc)
        s5w = _s5_weights(s5_lam_re[i], s5_lam_im[i], s5_log_dt[i], s5_b_re[i], s5_b_im[i],
                          s5_c_re[i], s5_c_im[i])
        ys = _s5_mix(u8, s5w, b, nctc16)
        t0 = nct if last else 0
        xm = _mixout(xcat, yf, yr, g, bo, ys, p, rwkv_ln_w[i].reshape(1, -1), rwkv_ln_b[i].reshape(1, -1),
                     bd, s5_d[i][None], s5_glu_w[i].astype(BF16), s5_glu_b[i][None],
                     w_out[i].astype(BF16), mod(i, 2), nct, t0)
        if not last:
            j = i // 2
            xcat = _ffn(xm, norm2_g[i][None], mod(i, 3), mod(i, 4), mod(i, 5),
                        ffn_w_gate[j].astype(BF16), ffn_w_up[j].astype(BF16),
                        ffn_w_down[j].astype(BF16), nct)
        else:
            j = i // 2
            ne = moe_router.shape[2]
            nr = -(-ne // 8) * 8
            router_t = jnp.zeros((nr, d), F32).at[:ne].set(moe_router[j].T)
            lat = lambda k: mods[i, :b, k][:, None, :]
            h, cmb, cnt = _route(xm, norm2_g[i][None], lat(3), lat(4), router_t, ne)
            cnt = cnt[:, :ne, 0].reshape(-1, MOE_TM // MOE_TR, ne).sum(axis=1)
            ui = lax.broadcasted_iota(jnp.int32, (MOE_SUB, MOE_SUB), 0)
            uj = lax.broadcasted_iota(jnp.int32, (MOE_SUB, MOE_SUB), 1)
            moe = _moe(h, cmb, cnt, (ui < uj).astype(BF16), moe_w_gate[j].astype(BF16),
                       moe_w_up[j].astype(BF16), moe_w_down[j].astype(BF16))
            out = _final(xm, moe, lat(5), final_g[None])
    return out
```

```python
import functools
import math

import jax
import jax.numpy as jnp
from jax import lax
from jax.experimental import pallas as pl
from jax.experimental.pallas import tpu as pltpu

F32 = jnp.float32
BF16 = jnp.bfloat16
HIGHEST = lax.Precision.HIGHEST

GRID_W = 64
HEAD = 64
DECAY_RANK = 64
ICL_RANK = 64
GATE_RANK = 128
S5_GROUP = 16
S5_STATE = 64
NORM_EPS = 1e-6
GN_EPS = 64e-5
L2_EPS = 1e-12
LAM_RE_MAX = -1e-4
TOP_K = 2

TM = 256
WKV_CHUNK = 64
WKV_HEADS = 4
S5_CHUNK = 16
S5_OCT = 128
VMEM_LIMIT = 56 * 1024 * 1024


def _cparams(*sem):
    return pltpu.CompilerParams(dimension_semantics=sem, vmem_limit_bytes=VMEM_LIMIT)


def _dot(a, b):
    return jnp.dot(a, b, preferred_element_type=F32)


def _dot32(a, b):
    return jnp.dot(a, b, precision=HIGHEST, preferred_element_type=F32)


def _split2(x):
    hi = x.astype(BF16)
    return hi, (x - hi.astype(F32)).astype(BF16)


def _dot_ones(x, ones_bf):
    hi, lo = _split2(x)
    return _dot(hi, ones_bf) + _dot(lo, ones_bf)


def _dot_w2(x, w2_ref):
    hi, lo = _split2(x)
    return _dot(hi, w2_ref[0]) + _dot(lo, w2_ref[0]) + _dot(hi, w2_ref[1])


def _dot_nt(a, b):
    return lax.dot_general(a, b, (((1,), (1,)), ((), ())), preferred_element_type=F32)


def _dot_tn(a, b):
    return lax.dot_general(a, b, (((0,), (0,)), ((), ())), preferred_element_type=F32)


def _ada_kernel(act_ref, w_ref, b_ref, o_ref):
    a = act_ref[...]
    a = a * jax.nn.sigmoid(a)
    o_ref[...] = _dot32(a, w_ref[...]) + b_ref[...]


def _ada_mod(act, ada_w, ada_b):
    depth, d, n = ada_w.shape
    tn = 1536
    return pl.pallas_call(
        _ada_kernel,
        grid=(depth, n // tn),
        in_specs=[
            pl.BlockSpec((8, d), lambda i, j: (0, 0)),
            pl.BlockSpec((None, d, tn), lambda i, j: (i, 0, j)),
            pl.BlockSpec((None, 1, tn), lambda i, j: (i, 0, j)),
        ],
        out_specs=pl.BlockSpec((None, 8, tn), lambda i, j: (i, 0, j)),
        out_shape=jax.ShapeDtypeStruct((depth, 8, n), F32),
        compiler_params=_cparams("arbitrary", "arbitrary"),
        name="ada_mod",
    )(act, ada_w, ada_b.reshape(depth, 1, n))


def _norm_mod(x, g, shift, scale):
    ms = jnp.mean(x * x, axis=-1, keepdims=True)
    y = x * lax.rsqrt(ms + NORM_EPS) * g
    return y * (1.0 + scale) + shift


def _inproj_kernel(sw, x_ref, g_ref, sh_ref, sc_ref, w_ref, o_ref, u8_ref, us_ref):
    h = _norm_mod(x_ref[...], g_ref[...], sh_ref[...], sc_ref[...])
    p = _dot(h.astype(BF16), w_ref[...])
    o_ref[...] = p
    base = p.shape[1] - sw
    cpt = TM // S5_CHUNK
    for o8 in range(sw // S5_OCT):
        us_ref[o8] = p[:, base + o8 * S5_OCT:base + (o8 + 1) * S5_OCT]
    for o8 in range(sw // S5_OCT):
        for s in range(S5_CHUNK):
            u8_ref[o8, :, s * S5_OCT:(s + 1) * S5_OCT] = (
                us_ref[o8, pl.ds(s, cpt, stride=S5_CHUNK), :].astype(BF16))


def _inproj(xcat, g, shift, scale, w_bf, nct, sw):
    b, tt, d = xcat.shape
    n = w_bf.shape[1]
    noct = sw // S5_OCT
    cpt = TM // S5_CHUNK
    kw = S5_CHUNK * S5_OCT
    kind = lambda bi, i: (bi, jnp.where(i < nct, 0, 1), 0, 0)
    return pl.pallas_call(
        functools.partial(_inproj_kernel, sw),
        grid=(b, tt // TM),
        in_specs=[
            pl.BlockSpec((None, TM, d), lambda bi, i: (bi, i, 0)),
            pl.BlockSpec((1, d), lambda bi, i: (0, 0)),
            pl.BlockSpec((None, None, 1, d), kind),
            pl.BlockSpec((None, None, 1, d), kind),
            pl.BlockSpec((d, n), lambda bi, i: (0, 0)),
        ],
        out_specs=[pl.BlockSpec((None, TM, n), lambda bi, i: (bi, i, 0)),
                   pl.BlockSpec((noct, cpt, kw), lambda bi, i: (0, i, bi))],
        out_shape=[jax.ShapeDtypeStruct((b, tt, n), F32),
                   jax.ShapeDtypeStruct((noct, tt // S5_CHUNK, b * kw), BF16)],
        scratch_shapes=[pltpu.VMEM((noct, TM, S5_OCT), F32)],
        compiler_params=_cparams("parallel", "parallel"),
        name="inproj",
    )(xcat, g, shift, scale, w_bf)


def _split3(x):
    hi = x.astype(BF16)
    r1 = x - hi.astype(F32)
    mid = r1.astype(BF16)
    lo = (r1 - mid.astype(F32)).astype(BF16)
    return hi, mid, lo


def _prep_kernel(p_ref, up_ref, dn_ref, rm_ref, lc_ref, kk_ref, ka_ref, rk_ref, w0_ref, a0_ref,
                 wup_ref, aup_ref, gup_ref, bd_ref, tri_ref,
                 v_o, at_o, rt_o, bg_o, kg_o, ee_o, g_o, bo_o):
    x = p_ref[...]
    rm = rm_ref[...]
    lc = lc_ref[...]
    prev = pltpu.roll(x, 1, 0)
    nxt = pltpu.roll(x, TM - 1, 0)
    up = jnp.concatenate([up_ref[...], x[: TM - GRID_W]], axis=0)
    dn = jnp.concatenate([x[GRID_W:], dn_ref[...]], axis=0)
    slab = (x * lc[4:5]
            + rm[:, 0:1] * (prev * lc[0:1])
            + rm[:, 1:2] * (nxt * lc[1:2])
            + rm[:, 2:3] * (up * lc[2:3])
            + rm[:, 3:4] * (dn * lc[3:4]))
    w = kk_ref.shape[1]
    r = slab[:, 0:w]
    k = slab[:, w:2 * w]
    v = slab[:, 2 * w:3 * w]
    o = 3 * w
    wd = slab[:, o:o + 2 * DECAY_RANK]
    ad = slab[:, o + 2 * DECAY_RANK:o + 2 * DECAY_RANK + 2 * ICL_RANK]
    gd = slab[:, o + 2 * DECAY_RANK + 2 * ICL_RANK:]
    bd = bd_ref[...]
    kk = k * kk_ref[...]
    nrm = jnp.sqrt(_dot_ones(kk * kk, bd))
    kk = kk / jnp.maximum(nrm, L2_EPS)
    v_o[...] = v.astype(BF16)
    twd = jnp.tanh(wd)
    sgd = jax.nn.sigmoid(gd)
    c = WKV_CHUNK
    for d in range(2):
        z = w0_ref[d:d + 1, :] + _dot_w2(twd, wup_ref.at[d])
        w_log = -jax.nn.softplus(-z) - 0.5
        lw = -jnp.exp(w_log)
        a = jax.nn.sigmoid(a0_ref[d:d + 1, :] + _dot_w2(ad, aup_ref.at[d]))
        kt = k * (1.0 + (a - 1.0) * ka_ref[...])
        g_o[d] = _dot_w2(sgd, gup_ref.at[d])
        bo_o[d] = _dot_ones(r * kt * rk_ref[...], bd) * v
        tri = tri_ref[d]
        hi, mid, lo = _split3(lw)
        lg_in = _dot(tri, hi) + _dot(tri, mid) + _dot(tri, lo)
        e_neg = jnp.exp(-lg_in)
        at_o[d] = (-kk * jnp.exp(lg_in - lw)).astype(BF16)
        rt_o[d] = (r * jnp.exp(lg_in)).astype(BF16)
        bg_o[d] = (kk * a * e_neg).astype(BF16)
        kg_o[d] = (kt * e_neg).astype(BF16)
        for ci in range(TM // c):
            last = ci * c + (c - 1 if d == 0 else 0)
            ee_o[d, ci] = jnp.exp(lg_in[last:last + 1, :])


def _rwkv_prep(p, rowmask, lanec, k_k, k_a, r_k, w0, a0, wup, aup, gup, bd, tri, nct, slab_w):
    b, tt, _ = p.shape
    w = k_k.shape[1]
    nt = tt // TM
    cpt = TM // WKV_CHUNK
    hb = TM // GRID_W
    nhb = tt // GRID_W
    full = lambda *s: pl.BlockSpec(s, lambda bi, i: (0,) * len(s))
    tok = pl.BlockSpec((None, TM, w), lambda bi, i: (bi, i, 0))
    tok2 = pl.BlockSpec((2, None, TM, w), lambda bi, i: (0, bi, i, 0))
    bf1 = jax.ShapeDtypeStruct((b, tt, w), BF16)
    bf2 = jax.ShapeDtypeStruct((2, b, tt, w), BF16)
    sh2 = jax.ShapeDtypeStruct((2, b, tt, w), F32)
    return pl.pallas_call(
        _prep_kernel,
        grid=(b, nt),
        in_specs=[
            pl.BlockSpec((None, TM, slab_w), lambda bi, i: (bi, i, 0)),
            pl.BlockSpec((None, GRID_W, slab_w), lambda bi, i: (bi, jnp.maximum(i * hb - 1, 0), 0)),
            pl.BlockSpec((None, GRID_W, slab_w),
                         lambda bi, i: (bi, jnp.minimum(i * hb + hb, nhb - 1), 0)),
            pl.BlockSpec((None, TM, 8), lambda bi, i: (i, 0, 0)),
            pl.BlockSpec((None, 8, slab_w), lambda bi, i: (jnp.where(i < nct, 0, 1), 0, 0)),
            full(1, w), full(1, w), full(1, w), full(2, w), full(2, w),
            full(2, 2, 2 * DECAY_RANK, w), full(2, 2, 2 * ICL_RANK, w), full(2, 2, 2 * GATE_RANK, w),
            full(w, w), full(2, TM, TM),
        ],
        out_specs=[tok, tok2, tok2, tok2, tok2,
                   pl.BlockSpec((2, None, cpt, 1, w), lambda bi, i: (0, bi, i, 0, 0)),
                   tok2, tok2],
        out_shape=[bf1, bf2, bf2, bf2, bf2,
                   jax.ShapeDtypeStruct((2, b, tt // WKV_CHUNK, 1, w), F32), sh2, sh2],
        compiler_params=_cparams("parallel", "parallel"),
        name="rwkv_prep",
    )(p, p, p, rowmask, lanec, k_k, k_a, r_k, w0, a0, wup, aup, gup, bd, tri)


def _wkv_kernel(nb, ngrp, v_f, v_r, at_f, at_r, rt_f, rt_r, bg_f, bg_r, kg_f, kg_r, ee_f, ee_r,
                y_f, y_r, ht_ref):
    j = pl.program_id(0)
    c = WKV_CHUNK
    gw = WKV_HEADS * HEAD
    gn = WKV_HEADS * c

    @pl.when(j == 0)
    def _():
        ht_ref[...] = jnp.zeros_like(ht_ref)

    sh = int(math.log2(c))
    row = lax.broadcasted_iota(jnp.int32, (gn, gw), 0)
    col = lax.broadcasted_iota(jnp.int32, (gn, gw), 1)
    same = (row >> sh) == (col >> sh)
    tf = lax.broadcasted_iota(jnp.int32, (c, gn), 0)
    sf = lax.broadcasted_iota(jnp.int32, (c, gn), 1) & (c - 1)
    eye = (tf == sf).astype(F32)

    def stack(x):
        xb = jnp.concatenate([x.astype(BF16)] * WKV_HEADS, axis=0)
        return jnp.where(same, xb, jnp.zeros_like(xb))

    dirs = ((v_f, at_f, rt_f, bg_f, kg_f, ee_f, y_f, sf < tf, sf <= tf),
            (v_r, at_r, rt_r, bg_r, kg_r, ee_r, y_r, sf > tf, sf >= tf))
    chains = [(d, bi, q) for d in range(2) for bi in range(nb) for q in range(ngrp)]
    sl = lambda q: slice(q * gw, (q + 1) * gw)
    rd = lambda k: [dirs[d][k][bi, :, sl(q)] for d, bi, q in chains]
    cat0 = lambda xs: jnp.concatenate(xs, axis=0)
    v, at, rt, bg, kg, ee = rd(0), rd(1), rd(2), rd(3), rd(4), rd(5)
    before = [dirs[d][7] for d, _, _ in chains]
    incl = [dirs[d][8] for d, _, _ in chains]
    n_ch = range(len(chains))

    v_bd = [stack(x) for x in v]
    at_bd = [stack(x) for x in at]
    bk_bd = [cat0([stack(bg[i]), stack(kg[i])]) for i in n_ch]
    a = [_dot_nt(cat0([at[i], rt[i]]), bk_bd[i]) for i in n_ch]
    n = [jnp.where(before[i], a[i][0:c, 0:gn], 0.0) for i in n_ch]
    a_kk = [cat0([jnp.where(before[i], a[i][0:c, gn:], 0.0),
                  jnp.where(incl[i], a[i][c:, gn:], 0.0)]).astype(BF16) for i in n_ch]
    a_rb = [jnp.where(incl[i], a[i][c:, 0:gn], 0.0).astype(BF16) for i in n_ch]
    tm = [eye + x for x in n]
    pw = [_dot(x.astype(BF16), stack(x)) for x in n]
    for lvl in range(1, sh):
        pw_bd = [stack(x) for x in pw]
        if lvl < sh - 1:
            tp = [_dot(cat0([tm[i].astype(BF16), pw[i].astype(BF16)]), pw_bd[i]) for i in n_ch]
            tm = [tm[i] + tp[i][0:c] for i in n_ch]
            pw = [tp[i][c:] for i in n_ch]
        else:
            tm = [tm[i] + _dot(tm[i].astype(BF16), pw_bd[i]) for i in n_ch]
    tm_b = [x.astype(BF16) for x in tm]
    atp = [_dot(tm_b[i], at_bd[i]) for i in n_ch]
    av = [_dot(a_kk[i], v_bd[i]) for i in n_ch]
    wv = [_dot(tm_b[i], stack(av[i][0:c])) for i in n_ch]
    wv_bd = [stack(x) for x in wv]
    atp_bd = [stack(x) for x in atp]
    ar = [_dot(a_rb[i], jnp.concatenate([wv_bd[i], atp_bd[i]], axis=1)) for i in n_ch]
    y0 = [ar[i][:, 0:gw] + av[i][c:] for i in n_ch]
    rtp = [(ar[i][:, gw:] + rt[i].astype(F32)).astype(BF16) for i in n_ch]
    bge_bd = [stack(bg[i].astype(F32) * ee[i]) for i in n_ch]
    kge_bd = [stack(kg[i].astype(F32) * ee[i]) for i in n_ch]
    g = [_dot_tn(bge_bd[i], atp_bd[i]).astype(BF16) for i in n_ch]
    hloc_t = [_dot_tn(cat0([wv_bd[i], v_bd[i]]), cat0([bge_bd[i], kge_bd[i]])) for i in n_ch]
    for i, (d, bi, q) in enumerate(chains):
        ht = ht_ref[d, bi, q]
        ht_b = ht.astype(BF16)
        dirs[d][6][bi, :, sl(q)] = y0[i] + _dot_nt(rtp[i], ht_b)
        ht_ref[d, bi, q] = ht * ee[i] + _dot_nt(ht_b, g[i]) + hloc_t[i]


def _wkv_scan(v, at, rt, bg, kg, ee, nctc):
    b, tt, w = v.shape
    ntot = tt // WKV_CHUNK
    ngrp = w // (WKV_HEADS * HEAD)
    fwd = lambda j: j
    rev = lambda j: jnp.where(j < nctc, nctc - 1 - j, ntot - 1 + nctc - j)
    tok = lambda cm: pl.BlockSpec((b, WKV_CHUNK, w), lambda j: (0, cm(j), 0))
    tok2 = lambda d, cm: pl.BlockSpec((None, b, WKV_CHUNK, w), lambda j: (d, 0, cm(j), 0))
    eesp = lambda d, cm: pl.BlockSpec((None, b, None, 1, w), lambda j: (d, 0, cm(j), 0, 0))
    pair = lambda f: [f(0, fwd), f(1, rev)]
    ysh = jax.ShapeDtypeStruct((b, tt, w), F32)
    return pl.pallas_call(
        functools.partial(_wkv_kernel, b, ngrp),
        grid=(ntot,),
        in_specs=[tok(fwd), tok(rev)] + pair(tok2) + pair(tok2) + pair(tok2) + pair(tok2) + pair(eesp),
        out_specs=[tok(fwd), tok(rev)],
        out_shape=[ysh, ysh],
        scratch_shapes=[pltpu.VMEM((2, b, ngrp, WKV_HEADS * HEAD, WKV_HEADS * HEAD), F32)],
        compiler_params=_cparams("arbitrary"),
        name="wkv_scan",
    )(v, v, at, at, rt, rt, bg, bg, kg, kg, ee, ee)


def _s5_weights(lam_re, lam_im, log_dt, b_re, b_im, c_re, c_im):
    tc = S5_CHUNK
    lr = jnp.minimum(lam_re.astype(F32), LAM_RE_MAX)
    li = lam_im.astype(F32)
    dt = jnp.exp(log_dt.astype(F32))[..., None]
    mag = jnp.exp(lr * dt)
    ar = mag * jnp.cos(li * dt)
    ai = mag * jnp.sin(li * dt)
    den = lr * lr + li * li
    xr = ar - 1.0
    cr = (xr * lr + ai * li) / den
    ci = (ai * lr - xr * li) / den
    br = cr[..., None] * b_re - ci[..., None] * b_im
    bi = cr[..., None] * b_im + ci[..., None] * b_re
    pr, pi = [jnp.ones_like(ar)], [jnp.zeros_like(ar)]
    for _ in range(tc):
        pr_n = pr[-1] * ar - pi[-1] * ai
        pi_n = pr[-1] * ai + pi[-1] * ar
        pr.append(pr_n)
        pi.append(pi_n)
    pr = jnp.stack(pr)
    pi = jnp.stack(pi)
    lbr = pr[..., None] * br - pi[..., None] * bi
    lbi = pr[..., None] * bi + pi[..., None] * br
    clr = c_re * pr[:, :, :, None, :] - c_im * pi[:, :, :, None, :]
    cli = c_re * pi[:, :, :, None, :] + c_im * pr[:, :, :, None, :]
    lbr_t = jnp.swapaxes(lbr, -1, -2)
    lbi_t = jnp.swapaxes(lbi, -1, -2)
    kern_t = jnp.sum(lbr_t[..., :, None, :] * c_re[None, :, :, None, :, :]
                     - lbi_t[..., :, None, :] * c_im[None, :, :, None, :, :], axis=-1)
    g = ar.shape[1]
    og = S5_OCT // S5_GROUP
    noct = g // og
    eye = jnp.eye(og, dtype=F32)

    def bdiag(x):
        nt, _, _, a, n = x.shape
        x = x.reshape(nt, 2, noct, og, a, n)
        y = x[:, :, :, :, :, None, :] * eye[None, None, None, :, None, :, None]
        return jnp.transpose(y, (1, 2, 0, 3, 4, 5, 6)).reshape(2, noct, nt, og * a, og * n).astype(BF16)

    kbd = bdiag(kern_t)

    def pair_block(d, lp):
        kd = kbd[d]
        zero = jnp.zeros_like(kd[:, 0])
        k = lambda tau: kd[:, tau] if tau >= 0 else zero
        if d == 0:
            rows = [[k(2 * lp), k(2 * lp + 1)], [k(2 * lp - 1), k(2 * lp)]]
        else:
            rows = [[k(2 * lp), k(2 * lp - 1)], [k(2 * lp + 1), k(2 * lp)]]
        return jnp.concatenate([jnp.concatenate(r, axis=-1) for r in rows], axis=-2)

    wpair = jnp.stack([jnp.stack([pair_block(d, lp) for lp in range(tc // 2)], axis=1)
                       for d in range(2)])
    lbc = jnp.concatenate([lbr_t, lbi_t], axis=-1)
    clc = jnp.concatenate([clr, -cli], axis=-1)
    pout8, qin8 = _s5_expand(lbc, clc)
    la = jnp.concatenate([pr[tc], pr[tc]], axis=-1)
    lb = jnp.concatenate([-pi[tc], pi[tc]], axis=-1)
    return wpair, pout8, qin8, la, lb


def _s5_expand_kernel(p_ref, q_ref, po_ref, qo_ref):
    nt, _, og, a, n = p_ref.shape
    tc = nt - 1
    po_ref[...] = jnp.zeros_like(po_ref)
    qo_ref[...] = jnp.zeros_like(qo_ref)
    for d in range(2):
        for s in range(tc):
            lag_out = tc - 1 - s if d == 0 else s
            lag_in = s + 1 if d == 0 else tc - s
            for gi in range(og):
                rows, lanes = slice(gi * a, (gi + 1) * a), slice(gi * n, (gi + 1) * n)
                po_ref[d, s, rows, lanes] = p_ref[lag_out, d, gi].astype(BF16)
                qo_ref[d, s, rows, lanes] = q_ref[lag_in, d, gi].astype(BF16)


def _s5_expand(lbc, clc):
    nt, _, g, a, n = lbc.shape
    tc = nt - 1
    og = S5_OCT // S5_GROUP
    noct = g // og
    isp = pl.BlockSpec((nt, 2, og, a, n), lambda o: (0, 0, o, 0, 0))
    osp = pl.BlockSpec((2, None, tc, og * a, og * n), lambda o: (0, o, 0, 0, 0))
    osh = jax.ShapeDtypeStruct((2, noct, tc, og * a, og * n), BF16)
    po, qo = pl.pallas_call(
        _s5_expand_kernel,
        grid=(noct,),
        in_specs=[isp, isp],
        out_specs=[osp, osp],
        out_shape=[osh, osh],
        compiler_params=_cparams("parallel"),
        name="s5_expand",
    )(lbc, clc)
    return po.reshape(2, noct, tc * og * a, og * n), qo.reshape(2, noct, tc * og * a, og * n)


def _s5_local_kernel(u_ref, p_ref, e_ref):
    e_ref[...] = _dot(u_ref[...], p_ref[...])


def _s5_local(u8, pout8, nb):
    noct, nch, _ = u8.shape
    kw, n = pout8.shape[2:]
    return pl.pallas_call(
        _s5_local_kernel,
        grid=(noct, nb, 2),
        in_specs=[
            pl.BlockSpec((None, nch, kw), lambda o, b, d: (o, 0, b)),
            pl.BlockSpec((None, None, kw, n), lambda o, b, d: (d, o, 0, 0)),
        ],
        out_specs=pl.BlockSpec((None, nch, n), lambda o, b, d: (d, 0, o * nb + b)),
        out_shape=jax.ShapeDtypeStruct((2, nch, noct * nb * n), F32),
        compiler_params=_cparams("parallel", "parallel", "parallel"),
        name="s5_local",
    )(u8, pout8)


S5_STATE_ROWS = 8
S5_STATE_LANES = 256


def _s5_state_kernel(nctc, ntot, e_ref, la_ref, lb_ref, x_ref, es_ref):
    d = pl.program_id(0)
    la = la_ref[...]
    lb = lb_ref[...]
    nr, wl = la.shape

    def swap(t):
        lane = lax.broadcasted_iota(jnp.int32, t.shape, 1)
        first_half = (lane & (2 * S5_STATE - 1)) < S5_STATE
        return jnp.where(first_half, pltpu.roll(t, wl - S5_STATE, 1), pltpu.roll(t, S5_STATE, 1))

    es_ref[...] = swap(e_ref[...].reshape(ntot * nr, wl)).reshape(ntot, nr, wl)
    lbs = swap(lb)

    def body(j, carry):
        x, xs = carry
        rev_idx = jnp.where(j < nctc, nctc - 1 - j, ntot - 1 + nctc - j)
        c = jnp.where(d == 0, j, rev_idx)
        x_ref[c] = x
        return la * x + lb * xs + e_ref[c], la * xs + lbs * x + es_ref[c]

    zero = jnp.zeros(la.shape, F32)
    lax.fori_loop(0, ntot, body, (zero, zero))


def _s5_state(e, la, lb, nctc, ntot):
    _, nch, nr, lanes = e.shape
    wl = S5_STATE_LANES
    blk = pl.BlockSpec((None, nch, nr, wl), lambda d, i: (d, 0, 0, i))
    cf = pl.BlockSpec((None, nr, wl), lambda d, i: (d, 0, i))
    return pl.pallas_call(
        functools.partial(_s5_state_kernel, nctc, ntot),
        grid=(2, lanes // wl),
        in_specs=[blk, cf, cf],
        out_specs=blk,
        out_shape=jax.ShapeDtypeStruct(e.shape, F32),
        scratch_shapes=[pltpu.VMEM((nch, nr, wl), F32)],
        compiler_params=_cparams("parallel", "parallel"),
        name="s5_state",
    )(e, la, lb)


def _s5_out_kernel(rev, u_ref, w_ref, x_ref, q_ref, y_ref):
    pw = w_ref.shape[1]
    npair = w_ref.shape[0]
    x = x_ref[...].astype(BF16)
    for tp in range(npair):
        acc = _dot_nt(x, q_ref[tp * pw:(tp + 1) * pw, :])
        for lp in range(npair - tp if rev else tp + 1):
            sp = tp + lp if rev else tp - lp
            acc = acc + _dot(u_ref[:, sp * pw:(sp + 1) * pw], w_ref[lp])
        y_ref[:, tp * pw:(tp + 1) * pw] = acc


def _s5_out(u8, wpair, xin, qin8, nb, d):
    noct, nch, _ = u8.shape
    npair, pw = wpair.shape[2:4]
    kw = npair * pw
    n2 = qin8.shape[3]
    return pl.pallas_call(
        functools.partial(_s5_out_kernel, d == 1),
        grid=(noct, nb),
        in_specs=[
            pl.BlockSpec((None, nch, kw), lambda o, b: (o, 0, b)),
            pl.BlockSpec((None, None, npair, pw, pw), lambda o, b: (d, o, 0, 0, 0)),
            pl.BlockSpec((None, nch, n2), lambda o, b: (d, 0, o * nb + b)),
            pl.BlockSpec((None, None, kw, n2), lambda o, b: (d, o, 0, 0)),
        ],
        out_specs=pl.BlockSpec((None, nch, kw), lambda o, b: (o, 0, b)),
        out_shape=jax.ShapeDtypeStruct((noct, nch, nb * kw), F32),
        compiler_params=_cparams("parallel", "parallel"),
        name="s5_out",
    )(u8, wpair, xin, qin8)


def _s5_mix(u8, weights, nb, nctc16):
    wpair, pout8, qin8, la, lb = weights
    noct, nch, _ = u8.shape
    e = _s5_local(u8, pout8, nb)
    assert noct * nb == S5_STATE_ROWS
    fl = e.shape[2] // S5_STATE_ROWS
    coef = lambda t: jnp.repeat(t.reshape(2, noct, fl), nb, axis=1)
    xin = _s5_state(e.reshape(2, nch, S5_STATE_ROWS, fl), coef(la), coef(lb), nctc16, nch)
    xin = xin.reshape(e.shape)
    return _s5_out(u8, wpair, xin, qin8, nb, 0), _s5_out(u8, wpair, xin, qin8, nb, 1)


def _mixout_kernel(x_ref, yf_ref, yr_ref, g_ref, bo_ref, y8f_ref, y8r_ref, u_ref, lnw_ref, lnb_ref,
                   bd_ref, dsk_ref, gluw_ref, glub_ref, wout_ref, gate_ref, o_ref, ysn_ref):
    cpt = TM // S5_CHUNK
    for o8 in range(y8f_ref.shape[0]):
        for s in range(S5_CHUNK):
            lanes = slice(s * S5_OCT, (s + 1) * S5_OCT)
            ysn_ref[o8, pl.ds(s, cpt, stride=S5_CHUNK), :] = y8f_ref[o8, :, lanes] + y8r_ref[o8, :, lanes]
    ys = jnp.concatenate([ysn_ref[o8] for o8 in range(y8f_ref.shape[0])], axis=1)
    bd = bd_ref[...]
    inv = 1.0 / HEAD
    rw = None
    for d, y_ref in enumerate((yf_ref, yr_ref)):
        y = y_ref[...]
        mean = _dot_ones(y, bd) * inv
        yc = y - mean
        var = _dot_ones(yc * yc, bd) * inv
        yn = yc * lax.rsqrt(var + GN_EPS) * lnw_ref[...] + lnb_ref[...]
        o = (yn + bo_ref[d]) * g_ref[d]
        rw = o if rw is None else rw + o
    u = u_ref[...]
    ss = ys + dsk_ref[...] * u
    ss = jax.nn.gelu(ss)
    ss = ss * jax.nn.sigmoid(_dot(ss.astype(BF16), gluw_ref[...]) + glub_ref[...])
    w = rw.shape[1]
    mix = _dot(rw.astype(BF16), wout_ref[0:w, :]) + _dot(ss.astype(BF16), wout_ref[w:, :])
    o_ref[...] = x_ref[...] + gate_ref[...] * mix


def _mixout(xcat, yf, yr, g, bo, y8, p, ln_w, ln_b, bd, d_skip, glu_w, glu_b, w_out, gate, nct, t0):
    b, tt, d = xcat.shape
    w = ln_w.shape[1]
    sw = d_skip.shape[1]
    y8f, y8r = y8
    noct = y8f.shape[0]
    cpt = TM // S5_CHUNK
    kw = S5_CHUNK * S5_OCT
    ublk = (p.shape[2] - sw) // sw
    nt = tt // TM - t0
    full = lambda *s: pl.BlockSpec(s, lambda bi, i: (0,) * len(s))
    tok = pl.BlockSpec((None, TM, w), lambda bi, i: (bi, i + t0, 0))
    tok2 = pl.BlockSpec((2, None, TM, w), lambda bi, i: (0, bi, i + t0, 0))
    return pl.pallas_call(
        _mixout_kernel,
        grid=(b, nt),
        in_specs=[
            pl.BlockSpec((None, TM, d), lambda bi, i: (bi, i + t0, 0)),
            tok, tok, tok2, tok2,
            pl.BlockSpec((noct, cpt, kw), lambda bi, i: (0, i + t0, bi)),
            pl.BlockSpec((noct, cpt, kw), lambda bi, i: (0, i + t0, bi)),
            pl.BlockSpec((None, TM, sw), lambda bi, i: (bi, i + t0, ublk)),
            full(1, w), full(1, w), full(w, w), full(1, sw), full(sw, sw), full(1, sw),
            full(w + sw, d),
            pl.BlockSpec((None, None, 1, d), lambda bi, i: (bi, jnp.where(i + t0 < nct, 0, 1), 0, 0)),
        ],
        out_specs=pl.BlockSpec((None, TM, d), lambda bi, i: (bi, i, 0)),
        out_shape=jax.ShapeDtypeStruct((b, nt * TM, d), F32),
        scratch_shapes=[pltpu.VMEM((noct, TM, S5_OCT), F32)],
        compiler_params=_cparams("parallel", "parallel"),
        name="mix_out",
    )(xcat, yf, yr, g, bo, y8f, y8r, p, ln_w, ln_b, bd, d_skip, glu_w, glu_b, w_out, gate)


def _ffn_kernel(x_ref, g_ref, sh_ref, sc_ref, gate_ref, wg_ref, wu_ref, wd_ref, o_ref):
    x = x_ref[...]
    h = _norm_mod(x, g_ref[...], sh_ref[...], sc_ref[...]).astype(BF16)
    a = _dot(h, wg_ref[...])
    a = a * jax.nn.sigmoid(a) * _dot(h, wu_ref[...])
    o_ref[...] = x + gate_ref[...] * _dot(a.astype(BF16), wd_ref[...])


def _ffn(xcat, g, shift, scale, gate, wg, wu, wd, nct):
    b, tt, d = xcat.shape
    ff = wg.shape[1]
    kind = lambda bi, i: (bi, jnp.where(i < nct, 0, 1), 0, 0)
    mod = pl.BlockSpec((None, None, 1, d), kind)
    return pl.pallas_call(
        _ffn_kernel,
        grid=(b, tt // TM),
        in_specs=[
            pl.BlockSpec((None, TM, d), lambda bi, i: (bi, i, 0)),
            pl.BlockSpec((1, d), lambda bi, i: (0, 0)),
            mod, mod, mod,
            pl.BlockSpec((d, ff), lambda bi, i: (0, 0)),
            pl.BlockSpec((d, ff), lambda bi, i: (0, 0)),
            pl.BlockSpec((ff, d), lambda bi, i: (0, 0)),
        ],
        out_specs=pl.BlockSpec((None, TM, d), lambda bi, i: (bi, i, 0)),
        out_shape=jax.ShapeDtypeStruct((b, tt, d), F32),
        compiler_params=_cparams("parallel", "parallel"),
        name="ffn",
    )(xcat, g, shift, scale, gate, wg, wu, wd)


MOE_TR = 1024
MOE_TM = 2048
MOE_TF = 896
MOE_BLK = 256
MOE_SUB = 256


def _route_kernel(ne, x_ref, g_ref, sh_ref, sc_ref, rt_ref, h_o, cmb_o, cnt_o):
    h = _norm_mod(x_ref[...], g_ref[...], sh_ref[...], sc_ref[...])
    h_o[...] = h.astype(BF16)
    logits = lax.dot_general(rt_ref[...], h, (((1,), (1,)), ((), ())), precision=HIGHEST,
                             preferred_element_type=F32)
    sub = lax.broadcasted_iota(jnp.int32, logits.shape, 0).astype(F32)
    none = float(logits.shape[0])
    logits = jnp.where(sub < ne, logits, -jnp.inf)
    m1 = jnp.max(logits, axis=0, keepdims=True)
    i1 = jnp.min(jnp.where(logits == m1, sub, none), axis=0, keepdims=True)
    rest = jnp.where(sub == i1, -jnp.inf, logits)
    m2 = jnp.max(rest, axis=0, keepdims=True)
    i2 = jnp.min(jnp.where(rest == m2, sub, none), axis=0, keepdims=True)
    e2 = jnp.exp(m2 - m1)
    p1 = 1.0 / (1.0 + e2)
    p2 = e2 / (1.0 + e2)
    cmb = jnp.where(sub == i1, p1, 0.0) + jnp.where(sub == i2, p2, 0.0)
    cmb_o[...] = cmb
    cnt = jnp.sum((cmb > 0.0).astype(F32), axis=1, keepdims=True)
    cnt_o[...] = jnp.broadcast_to(cnt, cnt_o.shape).astype(jnp.int32)


def _route(x, g, shift, scale, router_t, ne):
    b, l, d = x.shape
    nr = router_t.shape[0]
    nt = l // MOE_TR
    mod = pl.BlockSpec((None, 1, d), lambda bi, i: (bi, 0, 0))
    return pl.pallas_call(
        functools.partial(_route_kernel, ne),
        grid=(b, nt),
        in_specs=[
            pl.BlockSpec((None, MOE_TR, d), lambda bi, i: (bi, i, 0)),
            pl.BlockSpec((1, d), lambda bi, i: (0, 0)),
            mod, mod,
            pl.BlockSpec((nr, d), lambda bi, i: (0, 0)),
        ],
        out_specs=[
            pl.BlockSpec((MOE_TR, d), lambda bi, i: (bi * nt + i, 0)),
            pl.BlockSpec((nr, MOE_TR), lambda bi, i: (0, bi * nt + i)),
            pl.BlockSpec((None, nr, 128), lambda bi, i: (bi * nt + i, 0, 0)),
        ],
        out_shape=[
            jax.ShapeDtypeStruct((b * l, d), BF16),
            jax.ShapeDtypeStruct((nr, b * l), F32),
            jax.ShapeDtypeStruct((b * nt, nr, 128), jnp.int32),
        ],
        compiler_params=_cparams("parallel", "parallel"),
        name="moe_route",
    )(x, g, shift, scale, router_t)


def _moe_kernel(cnt_ref, h_ref, cmb_ref, tri_ref, wg_ref, wu_ref, wd_ref, o_ref,
                pos_ref, hg_ref, ya_ref):
    t = pl.program_id(0)
    e = pl.program_id(1)
    j = pl.program_id(2)
    tm = h_ref.shape[0]
    nblk = jnp.right_shift(cnt_ref[t, e] + (MOE_BLK - 1), int(math.log2(MOE_BLK)))

    @pl.when((e == 0) & (j == 0))
    def _():
        o_ref[...] = jnp.zeros_like(o_ref)
        asg = (cmb_ref[...] > 0.0).astype(BF16)
        off = jnp.zeros((asg.shape[0], 1), F32)
        for k in range(tm // MOE_SUB):
            blk = asg[:, k * MOE_SUB:(k + 1) * MOE_SUB]
            pos_ref[:, k * MOE_SUB:(k + 1) * MOE_SUB] = _dot(blk, tri_ref[...]) + off
            off = off + jnp.sum(blk.astype(F32), axis=1, keepdims=True)

    sel = lax.broadcasted_iota(jnp.int32, pos_ref.shape, 0) == e
    posrow = jnp.sum(jnp.where(sel, pos_ref[...], 0.0), axis=0, keepdims=True)
    cwrow = jnp.sum(jnp.where(sel, cmb_ref[...], 0.0), axis=0, keepdims=True)
    rowi = lax.broadcasted_iota(jnp.int32, (MOE_BLK, tm), 0).astype(F32)

    def onehot(b):
        slot = rowi + (b * MOE_BLK).astype(F32)
        return (posrow == slot) & (cwrow > 0.0)

    @pl.when(j == 0)
    def _():
        def gather(b, carry):
            sel_b = jnp.where(onehot(b), 1.0, 0.0).astype(BF16)
            hg_ref[b] = _dot(sel_b, h_ref[...]).astype(BF16)
            ya_ref[b] = jnp.zeros(ya_ref.shape[1:], F32)
            return carry
        lax.fori_loop(0, nblk, gather, 0)

    d_model = hg_ref.shape[2]

    def swiglu(hb):
        a = _dot(hb, wg_ref[...])
        a = a * jax.nn.sigmoid(a) * _dot(hb, wu_ref[...])
        return _dot(a.astype(BF16), wd_ref[...])

    def ffn_pair(i, carry):
        rows = pl.ds(2 * i, 2)
        y = swiglu(hg_ref[rows].reshape(2 * MOE_BLK, d_model))
        ya_ref[rows] += y.reshape(2, MOE_BLK, d_model)
        return carry
    npair = jnp.right_shift(nblk, 1)
    lax.fori_loop(0, npair, ffn_pair, 0)

    @pl.when(nblk > 2 * npair)
    def _():
        ya_ref[nblk - 1] += swiglu(hg_ref[nblk - 1])

    @pl.when(j == pl.num_programs(2) - 1)
    def _():
        def scatter(b, carry):
            wsel = jnp.where(onehot(b), cwrow, 0.0).astype(BF16)
            o_ref[...] += _dot_tn(wsel, ya_ref[b].astype(BF16))
            return carry
        lax.fori_loop(0, nblk, scatter, 0)


def _moe(h, cmb, cnt, tri, wg, wu, wd):
    n, d = h.shape
    nr = cmb.shape[0]
    ne, _, ff = wg.shape
    nbmax = MOE_TM // MOE_BLK
    grid_spec = pltpu.PrefetchScalarGridSpec(
        num_scalar_prefetch=1,
        grid=(n // MOE_TM, ne, ff // MOE_TF),
        in_specs=[
            pl.BlockSpec((MOE_TM, d), lambda t, e, j, c: (t, 0)),
            pl.BlockSpec((nr, MOE_TM), lambda t, e, j, c: (0, t)),
            pl.BlockSpec((MOE_SUB, MOE_SUB), lambda t, e, j, c: (0, 0)),
            pl.BlockSpec((None, d, MOE_TF), lambda t, e, j, c: (e, 0, j)),
            pl.BlockSpec((None, d, MOE_TF), lambda t, e, j, c: (e, 0, j)),
            pl.BlockSpec((None, MOE_TF, d), lambda t, e, j, c: (e, j, 0)),
        ],
        out_specs=pl.BlockSpec((MOE_TM, d), lambda t, e, j, c: (t, 0)),
        scratch_shapes=[
            pltpu.VMEM((nr, MOE_TM), F32),
            pltpu.VMEM((nbmax, MOE_BLK, d), BF16),
            pltpu.VMEM((nbmax, MOE_BLK, d), F32),
        ],
    )
    return pl.pallas_call(
        _moe_kernel,
        grid_spec=grid_spec,
        out_shape=jax.ShapeDtypeStruct((n, d), F32),
        compiler_params=_cparams("parallel", "arbitrary", "arbitrary"),
        name="moe",
    )(cnt, h, cmb, tri, wg, wu, wd)


def _final_kernel(x_ref, m_ref, gate_ref, fg_ref, o_ref):
    y = x_ref[...] + gate_ref[...] * m_ref[...]
    ms = jnp.mean(y * y, axis=-1, keepdims=True)
    o_ref[...] = y * lax.rsqrt(ms + NORM_EPS) * fg_ref[...]


def _final(x, m, gate, final_g):
    b, l, d = x.shape
    nt = l // MOE_TR
    return pl.pallas_call(
        _final_kernel,
        grid=(b, nt),
        in_specs=[
            pl.BlockSpec((None, MOE_TR, d), lambda bi, i: (bi, i, 0)),
            pl.BlockSpec((MOE_TR, d), lambda bi, i: (bi * nt + i, 0)),
            pl.BlockSpec((None, 1, d), lambda bi, i: (bi, 0, 0)),
            pl.BlockSpec((1, d), lambda bi, i: (0, 0)),
        ],
        out_specs=pl.BlockSpec((None, MOE_TR, d), lambda bi, i: (bi, i, 0)),
        out_shape=jax.ShapeDtypeStruct((b, l, d), F32),
        compiler_params=_cparams("parallel", "parallel"),
        name="moe_final",
    )(x, m, gate, final_g)


def _shift_masks(mu, ctx_len, seq_len):
    slab = mu.shape[0]
    nct = ctx_len // TM
    tt = ctx_len + seq_len
    t = jnp.arange(tt)
    is_ctx = t < ctx_len
    tl = t - ctx_len
    col = tl % GRID_W
    rows = seq_len // GRID_W
    grow = tl // GRID_W
    left = jnp.where(is_ctx, t != 0, col != 0)
    right = jnp.where(is_ctx, t != ctx_len - 1, col != GRID_W - 1)
    upv = jnp.where(is_ctx, False, grow != 0)
    dnv = jnp.where(is_ctx, False, grow != rows - 1)
    zero = jnp.zeros_like(left)
    rowmask = jnp.stack([left, right, upv, dnv, zero, zero, zero, zero], axis=-1).astype(F32)
    rowmask = rowmask.reshape(tt // TM, TM, 8)
    c = jnp.arange(slab)
    z = jnp.zeros_like(mu)
    lat = jnp.stack([mu * (c % 4 == 0), mu * (c % 4 == 1), mu * (c % 4 == 2), mu * (c % 4 == 3),
                     1.0 - mu, z, z, z])
    ctx = jnp.stack([mu * (c % 2 == 0), mu * (c % 2 == 1), z, z, 1.0 - mu, z, z, z])
    return rowmask, jnp.stack([ctx, lat]).astype(F32)


def _pad_rows(wt):
    z = jnp.zeros_like(wt[0])
    wp = jnp.stack([jnp.concatenate([wt[0], z], axis=0), jnp.concatenate([z, wt[1]], axis=0)])
    hi = wp.astype(BF16)
    lo = (wp - hi.astype(F32)).astype(BF16)
    return jnp.stack([hi, lo], axis=1)


def kernel(x, c, ctx, c_ctx, ada_w, ada_b, norm1_g, norm2_g, w_in, w_out, shift_mu, rwkv_w0, rwkv_w_up, rwkv_a0, rwkv_a_up, rwkv_g_up, rwkv_k_k, rwkv_k_a, rwkv_r_k, rwkv_ln_w, rwkv_ln_b, s5_lam_re, s5_lam_im, s5_log_dt, s5_b_re, s5_b_im, s5_c_re, s5_c_im, s5_d, s5_glu_w, s5_glu_b, ffn_w_gate, ffn_w_up, ffn_w_down, moe_router, moe_w_gate, moe_w_up, moe_w_down, final_g):
    b, l, d = x.shape
    ctx_len = ctx.shape[1]
    depth = ada_w.shape[0]
    slab_w = shift_mu.shape[1]
    rw_w = rwkv_k_k.shape[1]
    assert ctx_len == TM and l % TM == 0 and b + 1 <= 8
    assert rw_w % (WKV_HEADS * HEAD) == 0 and depth == 2
    nct = ctx_len // TM
    nctc = ctx_len // WKV_CHUNK
    nctc16 = ctx_len // S5_CHUNK

    act = jnp.zeros((8, d), F32).at[:b].set(c).at[b].set(c_ctx)
    mods = _ada_mod(act, ada_w, ada_b).reshape(depth, 8, 6, d)

    def mod(i, k):
        cm = jnp.broadcast_to(mods[i, b, k][None, :], (b, d))
        return jnp.stack([cm, mods[i, :b, k]], axis=1)[:, :, None, :]

    hi = lax.broadcasted_iota(jnp.int32, (rw_w, rw_w), 0) // HEAD
    hj = lax.broadcasted_iota(jnp.int32, (rw_w, rw_w), 1) // HEAD
    bd = (hi == hj).astype(BF16)

    ti = lax.broadcasted_iota(jnp.int32, (TM, TM), 0)
    si = lax.broadcasted_iota(jnp.int32, (TM, TM), 1)
    same_chunk = (ti // WKV_CHUNK) == (si // WKV_CHUNK)
    tri = jnp.stack([same_chunk & (si <= ti), same_chunk & (si >= ti)]).astype(BF16)

    xcat = jnp.concatenate([ctx, x], axis=1)
    out = None
    for i in range(depth):
        last = i == depth - 1
        p, u8 = _inproj(xcat, norm1_g[i][None], mod(i, 0), mod(i, 1), w_in[i].astype(BF16), nct,
                        s5_d.shape[1])
        rowmask, lanec = _shift_masks(shift_mu[i], ctx_len, l)
        v, at, rt, bg, kg, ee, g, bo = _rwkv_prep(
            p, rowmask, lanec, rwkv_k_k[i][None], rwkv_k_a[i][None], rwkv_r_k[i].reshape(1, -1),
            rwkv_w0[i], rwkv_a0[i], _pad_rows(rwkv_w_up[i]), _pad_rows(rwkv_a_up[i]),
            _pad_rows(rwkv_g_up[i]), bd, tri, nct, slab_w)
        yf, yr = _wkv_scan(v, at, rt, bg, kg, ee, nctc)
        s5w = _s5_weights(s5_lam_re[i], s5_lam_im[i], s5_log_dt[i], s5_b_re[i], s5_b_im[i],
                          s5_c_re[i], s5_c_im[i])
        ys = _s5_mix(u8, s5w, b, nctc16)
        t0 = nct if last else 0
        xm = _mixout(xcat, yf, yr, g, bo, ys, p, rwkv_ln_w[i].reshape(1, -1), rwkv_ln_b[i].reshape(1, -1),
                     bd, s5_d[i][None], s5_glu_w[i].astype(BF16), s5_glu_b[i][None],
                     w_out[i].astype(BF16), mod(i, 2), nct, t0)
        if not last:
            j = i // 2
            xcat = _ffn(xm, norm2_g[i][None], mod(i, 3), mod(i, 4), mod(i, 5),
                        ffn_w_gate[j].astype(BF16), ffn_w_up[j].astype(BF16),
                        ffn_w_down[j].astype(BF16), nct)
        else:
            j = i // 2
            ne = moe_router.shape[2]
            nr = -(-ne // 8) * 8
            router_t = jnp.zeros((nr, d), F32).at[:ne].set(moe_router[j].T)
            lat = lambda k: mods[i, :b, k][:, None, :]
            h, cmb, cnt = _route(xm, norm2_g[i][None], lat(3), lat(4), router_t, ne)
            cnt = cnt[:, :ne, 0].reshape(-1, MOE_TM // MOE_TR, ne).sum(axis=1)
            ui = lax.broadcasted_iota(jnp.int32, (MOE_SUB, MOE_SUB), 0)
            uj = lax.broadcasted_iota(jnp.int32, (MOE_SUB, MOE_SUB), 1)
            moe = _moe(h, cmb, cnt, (ui < uj).astype(BF16), moe_w_gate[j].astype(BF16),
                       moe_w_up[j].astype(BF16), moe_w_down[j].astype(BF16))
            out = _final(xm, moe, lat(5), final_g[None])
    return out
```

```python
import functools
import math

import jax
import jax.numpy as jnp
from jax import lax
from jax.experimental import pallas as pl
from jax.experimental.pallas import tpu as pltpu

F32 = jnp.float32
BF16 = jnp.bfloat16
HIGHEST = lax.Precision.HIGHEST

GRID_W = 64
HEAD = 64
DECAY_RANK = 64
ICL_RANK = 64
GATE_RANK = 128
S5_GROUP = 16
S5_STATE = 64
NORM_EPS = 1e-6
GN_EPS = 64e-5
L2_EPS = 1e-12
LAM_RE_MAX = -1e-4
TOP_K = 2

TM = 256
WKV_CHUNK = 64
WKV_HEADS = 4
S5_CHUNK = 16
S5_OCT = 128
VMEM_LIMIT = 56 * 1024 * 1024


def _cparams(*sem):
    return pltpu.CompilerParams(dimension_semantics=sem, vmem_limit_bytes=VMEM_LIMIT)


def _dot(a, b):
    return jnp.dot(a, b, preferred_element_type=F32)


def _dot32(a, b):
    return jnp.dot(a, b, precision=HIGHEST, preferred_element_type=F32)


def _split2(x):
    hi = x.astype(BF16)
    return hi, (x - hi.astype(F32)).astype(BF16)


def _dot_ones(x, ones_bf):
    hi, lo = _split2(x)
    return _dot(hi, ones_bf) + _dot(lo, ones_bf)


def _dot_w2(x, w2_ref):
    hi, lo = _split2(x)
    return _dot(hi, w2_ref[0]) + _dot(lo, w2_ref[0]) + _dot(hi, w2_ref[1])


def _dot_nt(a, b):
    return lax.dot_general(a, b, (((1,), (1,)), ((), ())), preferred_element_type=F32)


def _dot_tn(a, b):
    return lax.dot_general(a, b, (((0,), (0,)), ((), ())), preferred_element_type=F32)


def _ada_kernel(act_ref, w_ref, b_ref, o_ref):
    a = act_ref[...]
    a = a * jax.nn.sigmoid(a)
    o_ref[...] = _dot32(a, w_ref[...]) + b_ref[...]


def _ada_mod(act, ada_w, ada_b):
    depth, d, n = ada_w.shape
    tn = 1536
    return pl.pallas_call(
        _ada_kernel,
        grid=(depth, n // tn),
        in_specs=[
            pl.BlockSpec((8, d), lambda i, j: (0, 0)),
            pl.BlockSpec((None, d, tn), lambda i, j: (i, 0, j)),
            pl.BlockSpec((None, 1, tn), lambda i, j: (i, 0, j)),
        ],
        out_specs=pl.BlockSpec((None, 8, tn), lambda i, j: (i, 0, j)),
        out_shape=jax.ShapeDtypeStruct((depth, 8, n), F32),
        compiler_params=_cparams("arbitrary", "arbitrary"),
        name="ada_mod",
    )(act, ada_w, ada_b.reshape(depth, 1, n))


def _norm_mod(x, g, shift, scale):
    ms = jnp.mean(x * x, axis=-1, keepdims=True)
    y = x * lax.rsqrt(ms + NORM_EPS) * g
    return y * (1.0 + scale) + shift


def _inproj_kernel(sw, x_ref, g_ref, sh_ref, sc_ref, w_ref, o_ref, u8_ref, us_ref):
    h = _norm_mod(x_ref[...], g_ref[...], sh_ref[...], sc_ref[...])
    p = _dot(h.astype(BF16), w_ref[...])
    o_ref[...] = p
    base = p.shape[1] - sw
    cpt = TM // S5_CHUNK
    for o8 in range(sw // S5_OCT):
        us_ref[o8] = p[:, base + o8 * S5_OCT:base + (o8 + 1) * S5_OCT]
    for o8 in range(sw // S5_OCT):
        for s in range(S5_CHUNK):
            u8_ref[o8, :, s * S5_OCT:(s + 1) * S5_OCT] = (
                us_ref[o8, pl.ds(s, cpt, stride=S5_CHUNK), :].astype(BF16))


def _inproj(xcat, g, shift, scale, w_bf, nct, sw):
    b, tt, d = xcat.shape
    n = w_bf.shape[1]
    noct = sw // S5_OCT
    cpt = TM // S5_CHUNK
    kw = S5_CHUNK * S5_OCT
    kind = lambda bi, i: (bi, jnp.where(i < nct, 0, 1), 0, 0)
    return pl.pallas_call(
        functools.partial(_inproj_kernel, sw),
        grid=(b, tt // TM),
        in_specs=[
            pl.BlockSpec((None, TM, d), lambda bi, i: (bi, i, 0)),
            pl.BlockSpec((1, d), lambda bi, i: (0, 0)),
            pl.BlockSpec((None, None, 1, d), kind),
            pl.BlockSpec((None, None, 1, d), kind),
            pl.BlockSpec((d, n), lambda bi, i: (0, 0)),
        ],
        out_specs=[pl.BlockSpec((None, TM, n), lambda bi, i: (bi, i, 0)),
                   pl.BlockSpec((noct, cpt, kw), lambda bi, i: (0, i, bi))],
        out_shape=[jax.ShapeDtypeStruct((b, tt, n), F32),
                   jax.ShapeDtypeStruct((noct, tt // S5_CHUNK, b * kw), BF16)],
        scratch_shapes=[pltpu.VMEM((noct, TM, S5_OCT), F32)],
        compiler_params=_cparams("parallel", "parallel"),
        name="inproj",
    )(xcat, g, shift, scale, w_bf)


def _split3(x):
    hi = x.astype(BF16)
    r1 = x - hi.astype(F32)
    mid = r1.astype(BF16)
    lo = (r1 - mid.astype(F32)).astype(BF16)
    return hi, mid, lo


def _prep_kernel(p_ref, up_ref, dn_ref, rm_ref, lc_ref, kk_ref, ka_ref, rk_ref, w0_ref, a0_ref,
                 wup_ref, aup_ref, gup_ref, bd_ref, tri_ref,
                 v_o, at_o, rt_o, bg_o, kg_o, ee_o, g_o, bo_o):
    x = p_ref[...]
    rm = rm_ref[...]
    lc = lc_ref[...]
    prev = pltpu.roll(x, 1, 0)
    nxt = pltpu.roll(x, TM - 1, 0)
    up = jnp.concatenate([up_ref[...], x[: TM - GRID_W]], axis=0)
    dn = jnp.concatenate([x[GRID_W:], dn_ref[...]], axis=0)
    slab = (x * lc[4:5]
            + rm[:, 0:1] * (prev * lc[0:1])
            + rm[:, 1:2] * (nxt * lc[1:2])
            + rm[:, 2:3] * (up * lc[2:3])
            + rm[:, 3:4] * (dn * lc[3:4]))
    w = kk_ref.shape[1]
    r = slab[:, 0:w]
    k = slab[:, w:2 * w]
    v = slab[:, 2 * w:3 * w]
    o = 3 * w
    wd = slab[:, o:o + 2 * DECAY_RANK]
    ad = slab[:, o + 2 * DECAY_RANK:o + 2 * DECAY_RANK + 2 * ICL_RANK]
    gd = slab[:, o + 2 * DECAY_RANK + 2 * ICL_RANK:]
    bd = bd_ref[...]
    kk = k * kk_ref[...]
    nrm = jnp.sqrt(_dot_ones(kk * kk, bd))
    kk = kk / jnp.maximum(nrm, L2_EPS)
    v_o[...] = v.astype(BF16)
    twd = jnp.tanh(wd)
    sgd = jax.nn.sigmoid(gd)
    c = WKV_CHUNK
    for d in range(2):
        z = w0_ref[d:d + 1, :] + _dot_w2(twd, wup_ref.at[d])
        w_log = -jax.nn.softplus(-z) - 0.5
        lw = -jnp.exp(w_log)
        a = jax.nn.sigmoid(a0_ref[d:d + 1, :] + _dot_w2(ad, aup_ref.at[d]))
        kt = k * (1.0 + (a - 1.0) * ka_ref[...])
        g_o[d] = _dot_w2(sgd, gup_ref.at[d])
        bo_o[d] = _dot_ones(r * kt * rk_ref[...], bd) * v
        tri = tri_ref[d]
        hi, mid, lo = _split3(lw)
        lg_in = _dot(tri, hi) + _dot(tri, mid) + _dot(tri, lo)
        e_neg = jnp.exp(-lg_in)
        at_o[d] = (-kk * jnp.exp(lg_in - lw)).astype(BF16)
        rt_o[d] = (r * jnp.exp(lg_in)).astype(BF16)
        bg_o[d] = (kk * a * e_neg).astype(BF16)
        kg_o[d] = (kt * e_neg).astype(BF16)
        for ci in range(TM // c):
            last = ci * c + (c - 1 if d == 0 else 0)
            ee_o[d, ci] = jnp.exp(lg_in[last:last + 1, :])


def _rwkv_prep(p, rowmask, lanec, k_k, k_a, r_k, w0, a0, wup, aup, gup, bd, tri, nct, slab_w):
    b, tt, _ = p.shape
    w = k_k.shape[1]
    nt = tt // TM
    cpt = TM // WKV_CHUNK
    hb = TM // GRID_W
    nhb = tt // GRID_W
    full = lambda *s: pl.BlockSpec(s, lambda bi, i: (0,) * len(s))
    tok = pl.BlockSpec((None, TM, w), lambda bi, i: (bi, i, 0))
    tok2 = pl.BlockSpec((2, None, TM, w), lambda bi, i: (0, bi, i, 0))
    bf1 = jax.ShapeDtypeStruct((b, tt, w), BF16)
    bf2 = jax.ShapeDtypeStruct((2, b, tt, w), BF16)
    sh2 = jax.ShapeDtypeStruct((2, b, tt, w), F32)
    return pl.pallas_call(
        _prep_kernel,
        grid=(b, nt),
        in_specs=[
            pl.BlockSpec((None, TM, slab_w), lambda bi, i: (bi, i, 0)),
            pl.BlockSpec((None, GRID_W, slab_w), lambda bi, i: (bi, jnp.maximum(i * hb - 1, 0), 0)),
            pl.BlockSpec((None, GRID_W, slab_w),
                         lambda bi, i: (bi, jnp.minimum(i * hb + hb, nhb - 1), 0)),
            pl.BlockSpec((None, TM, 8), lambda bi, i: (i, 0, 0)),
            pl.BlockSpec((None, 8, slab_w), lambda bi, i: (jnp.where(i < nct, 0, 1), 0, 0)),
            full(1, w), full(1, w), full(1, w), full(2, w), full(2, w),
            full(2, 2, 2 * DECAY_RANK, w), full(2, 2, 2 * ICL_RANK, w), full(2, 2, 2 * GATE_RANK, w),
            full(w, w), full(2, TM, TM),
        ],
        out_specs=[tok, tok2, tok2, tok2, tok2,
                   pl.BlockSpec((2, None, cpt, 1, w), lambda bi, i: (0, bi, i, 0, 0)),
                   tok2, tok2],
        out_shape=[bf1, bf2, bf2, bf2, bf2,
                   jax.ShapeDtypeStruct((2, b, tt // WKV_CHUNK, 1, w), F32), sh2, sh2],
        compiler_params=_cparams("parallel", "parallel"),
        name="rwkv_prep",
    )(p, p, p, rowmask, lanec, k_k, k_a, r_k, w0, a0, wup, aup, gup, bd, tri)


def _wkv_kernel(nb, ngrp, v_f, v_r, at_f, at_r, rt_f, rt_r, bg_f, bg_r, kg_f, kg_r, ee_f, ee_r,
                y_f, y_r, ht_ref):
    j = pl.program_id(0)
    c = WKV_CHUNK
    gw = WKV_HEADS * HEAD
    gn = WKV_HEADS * c

    @pl.when(j == 0)
    def _():
        ht_ref[...] = jnp.zeros_like(ht_ref)

    sh = int(math.log2(c))
    row = lax.broadcasted_iota(jnp.int32, (gn, gw), 0)
    col = lax.broadcasted_iota(jnp.int32, (gn, gw), 1)
    same = (row >> sh) == (col >> sh)
    tf = lax.broadcasted_iota(jnp.int32, (c, gn), 0)
    sf = lax.broadcasted_iota(jnp.int32, (c, gn), 1) & (c - 1)
    eye = (tf == sf).astype(F32)

    def stack(x):
        xb = jnp.concatenate([x.astype(BF16)] * WKV_HEADS, axis=0)
        return jnp.where(same, xb, jnp.zeros_like(xb))

    dirs = ((v_f, at_f, rt_f, bg_f, kg_f, ee_f, y_f, sf < tf, sf <= tf),
            (v_r, at_r, rt_r, bg_r, kg_r, ee_r, y_r, sf > tf, sf >= tf))
    chains = [(d, bi, q) for d in range(2) for bi in range(nb) for q in range(ngrp)]
    sl = lambda q: slice(q * gw, (q + 1) * gw)
    rd = lambda k: [dirs[d][k][bi, :, sl(q)] for d, bi, q in chains]
    cat0 = lambda xs: jnp.concatenate(xs, axis=0)
    v, at, rt, bg, kg, ee = rd(0), rd(1), rd(2), rd(3), rd(4), rd(5)
    before = [dirs[d][7] for d, _, _ in chains]
    incl = [dirs[d][8] for d, _, _ in chains]
    n_ch = range(len(chains))

    v_bd = [stack(x) for x in v]
    at_bd = [stack(x) for x in at]
    bk_bd = [cat0([stack(bg[i]), stack(kg[i])]) for i in n_ch]
    a = [_dot_nt(cat0([at[i], rt[i]]), bk_bd[i]) for i in n_ch]
    n = [jnp.where(before[i], a[i][0:c, 0:gn], 0.0) for i in n_ch]
    a_kk = [cat0([jnp.where(before[i], a[i][0:c, gn:], 0.0),
                  jnp.where(incl[i], a[i][c:, gn:], 0.0)]).astype(BF16) for i in n_ch]
    a_rb = [jnp.where(incl[i], a[i][c:, 0:gn], 0.0).astype(BF16) for i in n_ch]
    tm = [eye + x for x in n]
    pw = [_dot(x.astype(BF16), stack(x)) for x in n]
    for lvl in range(1, sh):
        pw_bd = [stack(x) for x in pw]
        if lvl < sh - 1:
            tp = [_dot(cat0([tm[i].astype(BF16), pw[i].astype(BF16)]), pw_bd[i]) for i in n_ch]
            tm = [tm[i] + tp[i][0:c] for i in n_ch]
            pw = [tp[i][c:] for i in n_ch]
        else:
            tm = [tm[i] + _dot(tm[i].astype(BF16), pw_bd[i]) for i in n_ch]
    tm_b = [x.astype(BF16) for x in tm]
    atp = [_dot(tm_b[i], at_bd[i]) for i in n_ch]
    av = [_dot(a_kk[i], v_bd[i]) for i in n_ch]
    wv = [_dot(tm_b[i], stack(av[i][0:c])) for i in n_ch]
    wv_bd = [stack(x) for x in wv]
    atp_bd = [stack(x) for x in atp]
    ar = [_dot(a_rb[i], jnp.concatenate([wv_bd[i], atp_bd[i]], axis=1)) for i in n_ch]
    y0 = [ar[i][:, 0:gw] + av[i][c:] for i in n_ch]
    rtp = [(ar[i][:, gw:] + rt[i].astype(F32)).astype(BF16) for i in n_ch]
    bge_bd = [stack(bg[i].astype(F32) * ee[i]) for i in n_ch]
    kge_bd = [stack(kg[i].astype(F32) * ee[i]) for i in n_ch]
    g = [_dot_tn(bge_bd[i], atp_bd[i]).astype(BF16) for i in n_ch]
    hloc_t = [_dot_tn(cat0([wv_bd[i], v_bd[i]]), cat0([bge_bd[i], kge_bd[i]])) for i in n_ch]
    for i, (d, bi, q) in enumerate(chains):
        ht = ht_ref[d, bi, q]
        ht_b = ht.astype(BF16)
        dirs[d][6][bi, :, sl(q)] = y0[i] + _dot_nt(rtp[i], ht_b)
        ht_ref[d, bi, q] = ht * ee[i] + _dot_nt(ht_b, g[i]) + hloc_t[i]


def _wkv_scan(v, at, rt, bg, kg, ee, nctc):
    b, tt, w = v.shape
    ntot = tt // WKV_CHUNK
    ngrp = w // (WKV_HEADS * HEAD)
    fwd = lambda j: j
    rev = lambda j: jnp.where(j < nctc, nctc - 1 - j, ntot - 1 + nctc - j)
    tok = lambda cm: pl.BlockSpec((b, WKV_CHUNK, w), lambda j: (0, cm(j), 0))
    tok2 = lambda d, cm: pl.BlockSpec((None, b, WKV_CHUNK, w), lambda j: (d, 0, cm(j), 0))
    eesp = lambda d, cm: pl.BlockSpec((None, b, None, 1, w), lambda j: (d, 0, cm(j), 0, 0))
    pair = lambda f: [f(0, fwd), f(1, rev)]
    ysh = jax.ShapeDtypeStruct((b, tt, w), F32)
    return pl.pallas_call(
        functools.partial(_wkv_kernel, b, ngrp),
        grid=(ntot,),
        in_specs=[tok(fwd), tok(rev)] + pair(tok2) + pair(tok2) + pair(tok2) + pair(tok2) + pair(eesp),
        out_specs=[tok(fwd), tok(rev)],
        out_shape=[ysh, ysh],
        scratch_shapes=[pltpu.VMEM((2, b, ngrp, WKV_HEADS * HEAD, WKV_HEADS * HEAD), F32)],
        compiler_params=_cparams("arbitrary"),
        name="wkv_scan",
    )(v, v, at, at, rt, rt, bg, bg, kg, kg, ee, ee)


def _s5_weights(lam_re, lam_im, log_dt, b_re, b_im, c_re, c_im):
    tc = S5_CHUNK
    lr = jnp.minimum(lam_re.astype(F32), LAM_RE_MAX)
    li = lam_im.astype(F32)
    dt = jnp.exp(log_dt.astype(F32))[..., None]
    mag = jnp.exp(lr * dt)
    ar = mag * jnp.cos(li * dt)
    ai = mag * jnp.sin(li * dt)
    den = lr * lr + li * li
    xr = ar - 1.0
    cr = (xr * lr + ai * li) / den
    ci = (ai * lr - xr * li) / den
    br = cr[..., None] * b_re - ci[..., None] * b_im
    bi = cr[..., None] * b_im + ci[..., None] * b_re
    pr, pi = [jnp.ones_like(ar)], [jnp.zeros_like(ar)]
    for _ in range(tc):
        pr_n = pr[-1] * ar - pi[-1] * ai
        pi_n = pr[-1] * ai + pi[-1] * ar
        pr.append(pr_n)
        pi.append(pi_n)
    pr = jnp.stack(pr)
    pi = jnp.stack(pi)
    lbr = pr[..., None] * br - pi[..., None] * bi
    lbi = pr[..., None] * bi + pi[..., None] * br
    clr = c_re * pr[:, :, :, None, :] - c_im * pi[:, :, :, None, :]
    cli = c_re * pi[:, :, :, None, :] + c_im * pr[:, :, :, None, :]
    lbr_t = jnp.swapaxes(lbr, -1, -2)
    lbi_t = jnp.swapaxes(lbi, -1, -2)
    kern_t = jnp.sum(lbr_t[..., :, None, :] * c_re[None, :, :, None, :, :]
                     - lbi_t[..., :, None, :] * c_im[None, :, :, None, :, :], axis=-1)
    g = ar.shape[1]
    og = S5_OCT // S5_GROUP
    noct = g // og
    eye = jnp.eye(og, dtype=F32)

    def bdiag(x):
        nt, _, _, a, n = x.shape
        x = x.reshape(nt, 2, noct, og, a, n)
        y = x[:, :, :, :, :, None, :] * eye[None, None, None, :, None, :, None]
        return jnp.transpose(y, (1, 2, 0, 3, 4, 5, 6)).reshape(2, noct, nt, og * a, og * n).astype(BF16)

    kbd = bdiag(kern_t)

    def pair_block(d, lp):
        kd = kbd[d]
        zero = jnp.zeros_like(kd[:, 0])
        k = lambda tau: kd[:, tau] if tau >= 0 else zero
        if d == 0:
            rows = [[k(2 * lp), k(2 * lp + 1)], [k(2 * lp - 1), k(2 * lp)]]
        else:
            rows = [[k(2 * lp), k(2 * lp - 1)], [k(2 * lp + 1), k(2 * lp)]]
        return jnp.concatenate([jnp.concatenate(r, axis=-1) for r in rows], axis=-2)

    wpair = jnp.stack([jnp.stack([pair_block(d, lp) for lp in range(tc // 2)], axis=1)
                       for d in range(2)])
    lbc = jnp.concatenate([lbr_t, lbi_t], axis=-1)
    clc = jnp.concatenate([clr, -cli], axis=-1)
    pout8, qin8 = _s5_expand(lbc, clc)
    la = jnp.concatenate([pr[tc], pr[tc]], axis=-1)
    lb = jnp.concatenate([-pi[tc], pi[tc]], axis=-1)
    return wpair, pout8, qin8, la, lb


def _s5_expand_kernel(p_ref, q_ref, po_ref, qo_ref):
    nt, _, og, a, n = p_ref.shape
    tc = nt - 1
    po_ref[...] = jnp.zeros_like(po_ref)
    qo_ref[...] = jnp.zeros_like(qo_ref)
    for d in range(2):
        for s in range(tc):
            lag_out = tc - 1 - s if d == 0 else s
            lag_in = s + 1 if d == 0 else tc - s
            for gi in range(og):
                rows, lanes = slice(gi * a, (gi + 1) * a), slice(gi * n, (gi + 1) * n)
                po_ref[d, s, rows, lanes] = p_ref[lag_out, d, gi].astype(BF16)
                qo_ref[d, s, rows, lanes] = q_ref[lag_in, d, gi].astype(BF16)


def _s5_expand(lbc, clc):
    nt, _, g, a, n = lbc.shape
    tc = nt - 1
    og = S5_OCT // S5_GROUP
    noct = g // og
    isp = pl.BlockSpec((nt, 2, og, a, n), lambda o: (0, 0, o, 0, 0))
    osp = pl.BlockSpec((2, None, tc, og * a, og * n), lambda o: (0, o, 0, 0, 0))
    osh = jax.ShapeDtypeStruct((2, noct, tc, og * a, og * n), BF16)
    po, qo = pl.pallas_call(
        _s5_expand_kernel,
        grid=(noct,),
        in_specs=[isp, isp],
        out_specs=[osp, osp],
        out_shape=[osh, osh],
        compiler_params=_cparams("parallel"),
        name="s5_expand",
    )(lbc, clc)
    return po.reshape(2, noct, tc * og * a, og * n), qo.reshape(2, noct, tc * og * a, og * n)


def _s5_local_kernel(u_ref, p_ref, e_ref):
    e_ref[...] = _dot(u_ref[...], p_ref[...])


def _s5_local(u8, pout8, nb):
    noct, nch, _ = u8.shape
    kw, n = pout8.shape[2:]
    return pl.pallas_call(
        _s5_local_kernel,
        grid=(noct, nb, 2),
        in_specs=[
            pl.BlockSpec((None, nch, kw), lambda o, b, d: (o, 0, b)),
            pl.BlockSpec((None, None, kw, n), lambda o, b, d: (d, o, 0, 0)),
        ],
        out_specs=pl.BlockSpec((None, nch, n), lambda o, b, d: (d, 0, o * nb + b)),
        out_shape=jax.ShapeDtypeStruct((2, nch, noct * nb * n), F32),
        compiler_params=_cparams("parallel", "parallel", "parallel"),
        name="s5_local",
    )(u8, pout8)


S5_STATE_ROWS = 8
S5_STATE_LANES = 256


def _s5_state_kernel(nctc, ntot, e_ref, la_ref, lb_ref, x_ref, es_ref):
    d = pl.program_id(0)
    la = la_ref[...]
    lb = lb_ref[...]
    nr, wl = la.shape

    def swap(t):
        lane = lax.broadcasted_iota(jnp.int32, t.shape, 1)
        first_half = (lane & (2 * S5_STATE - 1)) < S5_STATE
        return jnp.where(first_half, pltpu.roll(t, wl - S5_STATE, 1), pltpu.roll(t, S5_STATE, 1))

    es_ref[...] = swap(e_ref[...].reshape(ntot * nr, wl)).reshape(ntot, nr, wl)
    lbs = swap(lb)

    def body(j, carry):
        x, xs = carry
        rev_idx = jnp.where(j < nctc, nctc - 1 - j, ntot - 1 + nctc - j)
        c = jnp.where(d == 0, j, rev_idx)
        x_ref[c] = x
        return la * x + lb * xs + e_ref[c], la * xs + lbs * x + es_ref[c]

    zero = jnp.zeros(la.shape, F32)
    lax.fori_loop(0, ntot, body, (zero, zero))


def _s5_state(e, la, lb, nctc, ntot):
    _, nch, nr, lanes = e.shape
    wl = S5_STATE_LANES
    blk = pl.BlockSpec((None, nch, nr, wl), lambda d, i: (d, 0, 0, i))
    cf = pl.BlockSpec((None, nr, wl), lambda d, i: (d, 0, i))
    return pl.pallas_call(
        functools.partial(_s5_state_kernel, nctc, ntot),
        grid=(2, lanes // wl),
        in_specs=[blk, cf, cf],
        out_specs=blk,
        out_shape=jax.ShapeDtypeStruct(e.shape, F32),
        scratch_shapes=[pltpu.VMEM((nch, nr, wl), F32)],
        compiler_params=_cparams("parallel", "parallel"),
        name="s5_state",
    )(e, la, lb)


def _s5_out_kernel(rev, u_ref, w_ref, x_ref, q_ref, y_ref):
    pw = w_ref.shape[1]
    npair = w_ref.shape[0]
    x = x_ref[...].astype(BF16)
    for tp in range(npair):
        acc = _dot_nt(x, q_ref[tp * pw:(tp + 1) * pw, :])
        for lp in range(npair - tp if rev else tp + 1):
            sp = tp + lp if rev else tp - lp
            acc = acc + _dot(u_ref[:, sp * pw:(sp + 1) * pw], w_ref[lp])
        y_ref[:, tp * pw:(tp + 1) * pw] = acc


def _s5_out(u8, wpair, xin, qin8, nb, d):
    noct, nch, _ = u8.shape
    npair, pw = wpair.shape[2:4]
    kw = npair * pw
    n2 = qin8.shape[3]
    return pl.pallas_call(
        functools.partial(_s5_out_kernel, d == 1),
        grid=(noct, nb),
        in_specs=[
            pl.BlockSpec((None, nch, kw), lambda o, b: (o, 0, b)),
            pl.BlockSpec((None, None, npair, pw, pw), lambda o, b: (d, o, 0, 0, 0)),
            pl.BlockSpec((None, nch, n2), lambda o, b: (d, 0, o * nb + b)),
            pl.BlockSpec((None, None, kw, n2), lambda o, b: (d, o, 0, 0)),
        ],
        out_specs=pl.BlockSpec((None, nch, kw), lambda o, b: (o, 0, b)),
        out_shape=jax.ShapeDtypeStruct((noct, nch, nb * kw), F32),
        compiler_params=_cparams("parallel", "parallel"),
        name="s5_out",
    )(u8, wpair, xin, qin8)


def _s5_mix(u8, weights, nb, nctc16):
    wpair, pout8, qin8, la, lb = weights
    noct, nch, _ = u8.shape
    e = _s5_local(u8, pout8, nb)
    assert noct * nb == S5_STATE_ROWS
    fl = e.shape[2] // S5_STATE_ROWS
    coef = lambda t: jnp.repeat(t.reshape(2, noct, fl), nb, axis=1)
    xin = _s5_state(e.reshape(2, nch, S5_STATE_ROWS, fl), coef(la), coef(lb), nctc16, nch)
    xin = xin.reshape(e.shape)
    return _s5_out(u8, wpair, xin, qin8, nb, 0), _s5_out(u8, wpair, xin, qin8, nb, 1)


def _mixout_kernel(x_ref, yf_ref, yr_ref, g_ref, bo_ref, y8f_ref, y8r_ref, u_ref, lnw_ref, lnb_ref,
                   bd_ref, dsk_ref, gluw_ref, glub_ref, wout_ref, gate_ref, o_ref, ysn_ref):
    cpt = TM // S5_CHUNK
    for o8 in range(y8f_ref.shape[0]):
        for s in range(S5_CHUNK):
            lanes = slice(s * S5_OCT, (s + 1) * S5_OCT)
            ysn_ref[o8, pl.ds(s, cpt, stride=S5_CHUNK), :] = y8f_ref[o8, :, lanes] + y8r_ref[o8, :, lanes]
    ys = jnp.concatenate([ysn_ref[o8] for o8 in range(y8f_ref.shape[0])], axis=1)
    bd = bd_ref[...]
    inv = 1.0 / HEAD
    rw = None
    for d, y_ref in enumerate((yf_ref, yr_ref)):
        y = y_ref[...]
        mean = _dot_ones(y, bd) * inv
        yc = y - mean
        var = _dot_ones(yc * yc, bd) * inv
        yn = yc * lax.rsqrt(var + GN_EPS) * lnw_ref[...] + lnb_ref[...]
        o = (yn + bo_ref[d]) * g_ref[d]
        rw = o if rw is None else rw + o
    u = u_ref[...]
    ss = ys + dsk_ref[...] * u
    ss = jax.nn.gelu(ss)
    ss = ss * jax.nn.sigmoid(_dot(ss.astype(BF16), gluw_ref[...]) + glub_ref[...])
    w = rw.shape[1]
    mix = _dot(rw.astype(BF16), wout_ref[0:w, :]) + _dot(ss.astype(BF16), wout_ref[w:, :])
    o_ref[...] = x_ref[...] + gate_ref[...] * mix


def _mixout(xcat, yf, yr, g, bo, y8, p, ln_w, ln_b, bd, d_skip, glu_w, glu_b, w_out, gate, nct, t0):
    b, tt, d = xcat.shape
    w = ln_w.shape[1]
    sw = d_skip.shape[1]
    y8f, y8r = y8
    noct = y8f.shape[0]
    cpt = TM // S5_CHUNK
    kw = S5_CHUNK * S5_OCT
    ublk = (p.shape[2] - sw) // sw
    nt = tt // TM - t0
    full = lambda *s: pl.BlockSpec(s, lambda bi, i: (0,) * len(s))
    tok = pl.BlockSpec((None, TM, w), lambda bi, i: (bi, i + t0, 0))
    tok2 = pl.BlockSpec((2, None, TM, w), lambda bi, i: (0, bi, i + t0, 0))
    return pl.pallas_call(
        _mixout_kernel,
        grid=(b, nt),
        in_specs=[
            pl.BlockSpec((None, TM, d), lambda bi, i: (bi, i + t0, 0)),
            tok, tok, tok2, tok2,
            pl.BlockSpec((noct, cpt, kw), lambda bi, i: (0, i + t0, bi)),
            pl.BlockSpec((noct, cpt, kw), lambda bi, i: (0, i + t0, bi)),
            pl.BlockSpec((None, TM, sw), lambda bi, i: (bi, i + t0, ublk)),
            full(1, w), full(1, w), full(w, w), full(1, sw), full(sw, sw), full(1, sw),
            full(w + sw, d),
            pl.BlockSpec((None, None, 1, d), lambda bi, i: (bi, jnp.where(i + t0 < nct, 0, 1), 0, 0)),
        ],
        out_specs=pl.BlockSpec((None, TM, d), lambda bi, i: (bi, i, 0)),
        out_shape=jax.ShapeDtypeStruct((b, nt * TM, d), F32),
        scratch_shapes=[pltpu.VMEM((noct, TM, S5_OCT), F32)],
        compiler_params=_cparams("parallel", "parallel"),
        name="mix_out",
    )(xcat, yf, yr, g, bo, y8f, y8r, p, ln_w, ln_b, bd, d_skip, glu_w, glu_b, w_out, gate)


def _ffn_kernel(x_ref, g_ref, sh_ref, sc_ref, gate_ref, wg_ref, wu_ref, wd_ref, o_ref):
    x = x_ref[...]
    h = _norm_mod(x, g_ref[...], sh_ref[...], sc_ref[...]).astype(BF16)
    a = _dot(h, wg_ref[...])
    a = a * jax.nn.sigmoid(a) * _dot(h, wu_ref[...])
    o_ref[...] = x + gate_ref[...] * _dot(a.astype(BF16), wd_ref[...])


def _ffn(xcat, g, shift, scale, gate, wg, wu, wd, nct):
    b, tt, d = xcat.shape
    ff = wg.shape[1]
    kind = lambda bi, i: (bi, jnp.where(i < nct, 0, 1), 0, 0)
    mod = pl.BlockSpec((None, None, 1, d), kind)
    return pl.pallas_call(
        _ffn_kernel,
        grid=(b, tt // TM),
        in_specs=[
            pl.BlockSpec((None, TM, d), lambda bi, i: (bi, i, 0)),
            pl.BlockSpec((1, d), lambda bi, i: (0, 0)),
            mod, mod, mod,
            pl.BlockSpec((d, ff), lambda bi, i: (0, 0)),
            pl.BlockSpec((d, ff), lambda bi, i: (0, 0)),
            pl.BlockSpec((ff, d), lambda bi, i: (0, 0)),
        ],
        out_specs=pl.BlockSpec((None, TM, d), lambda bi, i: (bi, i, 0)),
        out_shape=jax.ShapeDtypeStruct((b, tt, d), F32),
        compiler_params=_cparams("parallel", "parallel"),
        name="ffn",
    )(xcat, g, shift, scale, gate, wg, wu, wd)


MOE_TR = 1024
MOE_TM = 2048
MOE_TF = 896
MOE_BLK = 256
MOE_SUB = 256


def _route_kernel(ne, x_ref, g_ref, sh_ref, sc_ref, rt_ref, h_o, cmb_o, cnt_o):
    h = _norm_mod(x_ref[...], g_ref[...], sh_ref[...], sc_ref[...])
    h_o[...] = h.astype(BF16)
    logits = lax.dot_general(rt_ref[...], h, (((1,), (1,)), ((), ())), precision=HIGHEST,
                             preferred_element_type=F32)
    sub = lax.broadcasted_iota(jnp.int32, logits.shape, 0).astype(F32)
    none = float(logits.shape[0])
    logits = jnp.where(sub < ne, logits, -jnp.inf)
    m1 = jnp.max(logits, axis=0, keepdims=True)
    i1 = jnp.min(jnp.where(logits == m1, sub, none), axis=0, keepdims=True)
    rest = jnp.where(sub == i1, -jnp.inf, logits)
    m2 = jnp.max(rest, axis=0, keepdims=True)
    i2 = jnp.min(jnp.where(rest == m2, sub, none), axis=0, keepdims=True)
    e2 = jnp.exp(m2 - m1)
    p1 = 1.0 / (1.0 + e2)
    p2 = e2 / (1.0 + e2)
    cmb = jnp.where(sub == i1, p1, 0.0) + jnp.where(sub == i2, p2, 0.0)
    cmb_o[...] = cmb
    asg = (cmb > 0.0).astype(F32)
    for q in range(cnt_o.shape[0]):
        cnt = jnp.sum(asg[:, q * MOE_SUB:(q + 1) * MOE_SUB], axis=1, keepdims=True)
        cnt_o[q] = jnp.broadcast_to(cnt, cnt_o.shape[1:]).astype(jnp.int32)


def _route(x, g, shift, scale, router_t, ne):
    b, l, d = x.shape
    nr = router_t.shape[0]
    nt = l // MOE_TR
    mod = pl.BlockSpec((None, 1, d), lambda bi, i: (bi, 0, 0))
    return pl.pallas_call(
        functools.partial(_route_kernel, ne),
        grid=(b, nt),
        in_specs=[
            pl.BlockSpec((None, MOE_TR, d), lambda bi, i: (bi, i, 0)),
            pl.BlockSpec((1, d), lambda bi, i: (0, 0)),
            mod, mod,
            pl.BlockSpec((nr, d), lambda bi, i: (0, 0)),
        ],
        out_specs=[
            pl.BlockSpec((MOE_TR, d), lambda bi, i: (bi * nt + i, 0)),
            pl.BlockSpec((nr, MOE_TR), lambda bi, i: (0, bi * nt + i)),
            pl.BlockSpec((None, MOE_TR // MOE_SUB, nr, 128), lambda bi, i: (bi * nt + i, 0, 0, 0)),
        ],
        out_shape=[
            jax.ShapeDtypeStruct((b * l, d), BF16),
            jax.ShapeDtypeStruct((nr, b * l), F32),
            jax.ShapeDtypeStruct((b * nt, MOE_TR // MOE_SUB, nr, 128), jnp.int32),
        ],
        compiler_params=_cparams("parallel", "parallel"),
        name="moe_route",
    )(x, g, shift, scale, router_t)


def _moe_kernel(cnt_ref, h_ref, cmb_ref, tri_ref, wg_ref, wu_ref, wd_ref, o_ref,
                pos_ref, hg_ref, ya_ref):
    t = pl.program_id(0)
    e = pl.program_id(1)
    j = pl.program_id(2)
    tm = h_ref.shape[0]
    nsub = tm // MOE_SUB
    offs = [0]
    for k in range(nsub):
        offs.append(offs[-1] + cnt_ref[t * nsub + k, e])
    nblk = jnp.right_shift(offs[-1] + (MOE_BLK - 1), int(math.log2(MOE_BLK)))

    @pl.when((e == 0) & (j == 0))
    def _():
        o_ref[...] = jnp.zeros_like(o_ref)
        asg = (cmb_ref[...] > 0.0).astype(BF16)
        off = jnp.zeros((asg.shape[0], 1), F32)
        for k in range(tm // MOE_SUB):
            blk = asg[:, k * MOE_SUB:(k + 1) * MOE_SUB]
            pos_ref[:, k * MOE_SUB:(k + 1) * MOE_SUB] = _dot(blk, tri_ref[...]) + off
            off = off + jnp.sum(blk.astype(F32), axis=1, keepdims=True)

    sel = lax.broadcasted_iota(jnp.int32, pos_ref.shape, 0) == e
    posrow = jnp.sum(jnp.where(sel, pos_ref[...], 0.0), axis=0, keepdims=True)
    cwrow = jnp.sum(jnp.where(sel, cmb_ref[...], 0.0), axis=0, keepdims=True)
    rowi = lax.broadcasted_iota(jnp.int32, (MOE_BLK, MOE_SUB), 0).astype(F32)
    subs = [slice(k * MOE_SUB, (k + 1) * MOE_SUB) for k in range(nsub)]

    def onehot(b, k):
        slot = rowi + (b * MOE_BLK).astype(F32)
        return (posrow[:, subs[k]] == slot) & (cwrow[:, subs[k]] > 0.0)

    def hits(b, k):
        return (offs[k] < (b + 1) * MOE_BLK) & (offs[k + 1] > b * MOE_BLK)

    @pl.when(j == 0)
    def _():
        def gather(b, carry):
            ya_ref[b] = jnp.zeros(ya_ref.shape[1:], F32)
            for k in range(nsub):
                @pl.when(hits(b, k))
                def _():
                    sel_b = jnp.where(onehot(b, k), 1.0, 0.0).astype(BF16)
                    ya_ref[b] += _dot(sel_b, h_ref[subs[k], :])
            hg_ref[b] = ya_ref[b].astype(BF16)
            ya_ref[b] = jnp.zeros(ya_ref.shape[1:], F32)
            return carry
        lax.fori_loop(0, nblk, gather, 0)

    d_model = hg_ref.shape[2]

    def swiglu(hb):
        a = _dot(hb, wg_ref[...])
        a = a * jax.nn.sigmoid(a) * _dot(hb, wu_ref[...])
        return _dot(a.astype(BF16), wd_ref[...])

    def ffn_pair(i, carry):
        rows = pl.ds(2 * i, 2)
        y = swiglu(hg_ref[rows].reshape(2 * MOE_BLK, d_model))
        ya_ref[rows] += y.reshape(2, MOE_BLK, d_model)
        return carry
    npair = jnp.right_shift(nblk, 1)
    lax.fori_loop(0, npair, ffn_pair, 0)

    @pl.when(nblk > 2 * npair)
    def _():
        ya_ref[nblk - 1] += swiglu(hg_ref[nblk - 1])

    @pl.when(j == pl.num_programs(2) - 1)
    def _():
        def scatter(b, carry):
            yb = ya_ref[b].astype(BF16)
            for k in range(nsub):
                @pl.when(hits(b, k))
                def _():
                    wsel = jnp.where(onehot(b, k), cwrow[:, subs[k]], 0.0).astype(BF16)
                    o_ref[subs[k], :] += _dot_tn(wsel, yb)
            return carry
        lax.fori_loop(0, nblk, scatter, 0)


def _moe(h, cmb, cnt, tri, wg, wu, wd):
    n, d = h.shape
    nr = cmb.shape[0]
    ne, _, ff = wg.shape
    nbmax = MOE_TM // MOE_BLK
    grid_spec = pltpu.PrefetchScalarGridSpec(
        num_scalar_prefetch=1,
        grid=(n // MOE_TM, ne, ff // MOE_TF),
        in_specs=[
            pl.BlockSpec((MOE_TM, d), lambda t, e, j, c: (t, 0)),
            pl.BlockSpec((nr, MOE_TM), lambda t, e, j, c: (0, t)),
            pl.BlockSpec((MOE_SUB, MOE_SUB), lambda t, e, j, c: (0, 0)),
            pl.BlockSpec((None, d, MOE_TF), lambda t, e, j, c: (e, 0, j)),
            pl.BlockSpec((None, d, MOE_TF), lambda t, e, j, c: (e, 0, j)),
            pl.BlockSpec((None, MOE_TF, d), lambda t, e, j, c: (e, j, 0)),
        ],
        out_specs=pl.BlockSpec((MOE_TM, d), lambda t, e, j, c: (t, 0)),
        scratch_shapes=[
            pltpu.VMEM((nr, MOE_TM), F32),
            pltpu.VMEM((nbmax, MOE_BLK, d), BF16),
            pltpu.VMEM((nbmax, MOE_BLK, d), F32),
        ],
    )
    return pl.pallas_call(
        _moe_kernel,
        grid_spec=grid_spec,
        out_shape=jax.ShapeDtypeStruct((n, d), F32),
        compiler_params=_cparams("parallel", "arbitrary", "arbitrary"),
        name="moe",
    )(cnt, h, cmb, tri, wg, wu, wd)


def _final_kernel(x_ref, m_ref, gate_ref, fg_ref, o_ref):
    y = x_ref[...] + gate_ref[...] * m_ref[...]
    ms = jnp.mean(y * y, axis=-1, keepdims=True)
    o_ref[...] = y * lax.rsqrt(ms + NORM_EPS) * fg_ref[...]


def _final(x, m, gate, final_g):
    b, l, d = x.shape
    nt = l // MOE_TR
    return pl.pallas_call(
        _final_kernel,
        grid=(b, nt),
        in_specs=[
            pl.BlockSpec((None, MOE_TR, d), lambda bi, i: (bi, i, 0)),
            pl.BlockSpec((MOE_TR, d), lambda bi, i: (bi * nt + i, 0)),
            pl.BlockSpec((None, 1, d), lambda bi, i: (bi, 0, 0)),
            pl.BlockSpec((1, d), lambda bi, i: (0, 0)),
        ],
        out_specs=pl.BlockSpec((None, MOE_TR, d), lambda bi, i: (bi, i, 0)),
        out_shape=jax.ShapeDtypeStruct((b, l, d), F32),
        compiler_params=_cparams("parallel", "parallel"),
        name="moe_final",
    )(x, m, gate, final_g)


def _shift_masks(mu, ctx_len, seq_len):
    slab = mu.shape[0]
    nct = ctx_len // TM
    tt = ctx_len + seq_len
    t = jnp.arange(tt)
    is_ctx = t < ctx_len
    tl = t - ctx_len
    col = tl % GRID_W
    rows = seq_len // GRID_W
    grow = tl // GRID_W
    left = jnp.where(is_ctx, t != 0, col != 0)
    right = jnp.where(is_ctx, t != ctx_len - 1, col != GRID_W - 1)
    upv = jnp.where(is_ctx, False, grow != 0)
    dnv = jnp.where(is_ctx, False, grow != rows - 1)
    zero = jnp.zeros_like(left)
    rowmask = jnp.stack([left, right, upv, dnv, zero, zero, zero, zero], axis=-1).astype(F32)
    rowmask = rowmask.reshape(tt // TM, TM, 8)
    c = jnp.arange(slab)
    z = jnp.zeros_like(mu)
    lat = jnp.stack([mu * (c % 4 == 0), mu * (c % 4 == 1), mu * (c % 4 == 2), mu * (c % 4 == 3),
                     1.0 - mu, z, z, z])
    ctx = jnp.stack([mu * (c % 2 == 0), mu * (c % 2 == 1), z, z, 1.0 - mu, z, z, z])
    return rowmask, jnp.stack([ctx, lat]).astype(F32)


def _pad_rows(wt):
    z = jnp.zeros_like(wt[0])
    wp = jnp.stack([jnp.concatenate([wt[0], z], axis=0), jnp.concatenate([z, wt[1]], axis=0)])
    hi = wp.astype(BF16)
    lo = (wp - hi.astype(F32)).astype(BF16)
    return jnp.stack([hi, lo], axis=1)


def kernel(x, c, ctx, c_ctx, ada_w, ada_b, norm1_g, norm2_g, w_in, w_out, shift_mu, rwkv_w0, rwkv_w_up, rwkv_a0, rwkv_a_up, rwkv_g_up, rwkv_k_k, rwkv_k_a, rwkv_r_k, rwkv_ln_w, rwkv_ln_b, s5_lam_re, s5_lam_im, s5_log_dt, s5_b_re, s5_b_im, s5_c_re, s5_c_im, s5_d, s5_glu_w, s5_glu_b, ffn_w_gate, ffn_w_up, ffn_w_down, moe_router, moe_w_gate, moe_w_up, moe_w_down, final_g):
    b, l, d = x.shape
    ctx_len = ctx.shape[1]
    depth = ada_w.shape[0]
    slab_w = shift_mu.shape[1]
    rw_w = rwkv_k_k.shape[1]
    assert ctx_len == TM and l % TM == 0 and b + 1 <= 8
    assert rw_w % (WKV_HEADS * HEAD) == 0 and depth == 2
    nct = ctx_len // TM
    nctc = ctx_len // WKV_CHUNK
    nctc16 = ctx_len // S5_CHUNK

    act = jnp.zeros((8, d), F32).at[:b].set(c).at[b].set(c_ctx)
    mods = _ada_mod(act, ada_w, ada_b).reshape(depth, 8, 6, d)

    def mod(i, k):
        cm = jnp.broadcast_to(mods[i, b, k][None, :], (b, d))
        return jnp.stack([cm, mods[i, :b, k]], axis=1)[:, :, None, :]

    hi = lax.broadcasted_iota(jnp.int32, (rw_w, rw_w), 0) // HEAD
    hj = lax.broadcasted_iota(jnp.int32, (rw_w, rw_w), 1) // HEAD
    bd = (hi == hj).astype(BF16)

    ti = lax.broadcasted_iota(jnp.int32, (TM, TM), 0)
    si = lax.broadcasted_iota(jnp.int32, (TM, TM), 1)
    same_chunk = (ti // WKV_CHUNK) == (si // WKV_CHUNK)
    tri = jnp.stack([same_chunk & (si <= ti), same_chunk & (si >= ti)]).astype(BF16)

    xcat = jnp.concatenate([ctx, x], axis=1)
    out = None
    for i in range(depth):
        last = i == depth - 1
        p, u8 = _inproj(xcat, norm1_g[i][None], mod(i, 0), mod(i, 1), w_in[i].astype(BF16), nct,
                        s5_d.shape[1])
        rowmask, lanec = _shift_masks(shift_mu[i], ctx_len, l)
        v, at, rt, bg, kg, ee, g, bo = _rwkv_prep(
            p, rowmask, lanec, rwkv_k_k[i][None], rwkv_k_a[i][None], rwkv_r_k[i].reshape(1, -1),
            rwkv_w0[i], rwkv_a0[i], _pad_rows(rwkv_w_up[i]), _pad_rows(rwkv_a_up[i]),
            _pad_rows(rwkv_g_up[i]), bd, tri, nct, slab_w)
        yf, yr = _wkv_scan(v, at, rt, bg, kg, ee, nctc)
        s5w = _s5_weights(s5_lam_re[i], s5_lam_im[i], s5_log_dt[i], s5_b_re[i], s5_b_im[i],
                          s5_c_re[i], s5_c_im[i])
        ys = _s5_mix(u8, s5w, b, nctc16)
        t0 = nct if last else 0
        xm = _mixout(xcat, yf, yr, g, bo, ys, p, rwkv_ln_w[i].reshape(1, -1), rwkv_ln_b[i].reshape(1, -1),
                     bd, s5_d[i][None], s5_glu_w[i].astype(BF16), s5_glu_b[i][None],
                     w_out[i].astype(BF16), mod(i, 2), nct, t0)
        if not last:
            j = i // 2
            xcat = _ffn(xm, norm2_g[i][None], mod(i, 3), mod(i, 4), mod(i, 5),
                        ffn_w_gate[j].astype(BF16), ffn_w_up[j].astype(BF16),
                        ffn_w_down[j].astype(BF16), nct)
        else:
            j = i // 2
            ne = moe_router.shape[2]
            nr = -(-ne // 8) * 8
            router_t = jnp.zeros((nr, d), F32).at[:ne].set(moe_router[j].T)
            lat = lambda k: mods[i, :b, k][:, None, :]
            h, cmb, cnt = _route(xm, norm2_g[i][None], lat(3), lat(4), router_t, ne)
            cnt = cnt[:, :, :ne, 0].reshape(-1, ne)
            ui = lax.broadcasted_iota(jnp.int32, (MOE_SUB, MOE_SUB), 0)
            uj = lax.broadcasted_iota(jnp.int32, (MOE_SUB, MOE_SUB), 1)
            moe = _moe(h, cmb, cnt, (ui < uj).astype(BF16), moe_w_gate[j].astype(BF16),
                       moe_w_up[j].astype(BF16), moe_w_down[j].astype(BF16))
            out = _final(xm, moe, lat(5), final_g[None])
    return out
```

```python
import functools
import math

import jax
import jax.numpy as jnp
from jax import lax
from jax.experimental import pallas as pl
from jax.experimental.pallas import tpu as pltpu

F32 = jnp.float32
BF16 = jnp.bfloat16
HIGHEST = lax.Precision.HIGHEST

GRID_W = 64
HEAD = 64
DECAY_RANK = 64
ICL_RANK = 64
GATE_RANK = 128
S5_GROUP = 16
S5_STATE = 64
NORM_EPS = 1e-6
GN_EPS = 64e-5
L2_EPS = 1e-12
LAM_RE_MAX = -1e-4
TOP_K = 2

TM = 256
WKV_CHUNK = 64
WKV_HEADS = 4
S5_CHUNK = 16
S5_OCT = 128
VMEM_LIMIT = 56 * 1024 * 1024


def _cparams(*sem):
    return pltpu.CompilerParams(dimension_semantics=sem, vmem_limit_bytes=VMEM_LIMIT)


def _dot(a, b):
    return jnp.dot(a, b, preferred_element_type=F32)


def _dot32(a, b):
    return jnp.dot(a, b, precision=HIGHEST, preferred_element_type=F32)


def _split2(x):
    hi = x.astype(BF16)
    return hi, (x - hi.astype(F32)).astype(BF16)


def _dot_ones(x, ones_bf):
    hi, lo = _split2(x)
    return _dot(hi, ones_bf) + _dot(lo, ones_bf)


def _dot_w2(x, w2_ref):
    hi, lo = _split2(x)
    return _dot(hi, w2_ref[0]) + _dot(lo, w2_ref[0]) + _dot(hi, w2_ref[1])


def _dot_nt(a, b):
    return lax.dot_general(a, b, (((1,), (1,)), ((), ())), preferred_element_type=F32)


def _dot_tn(a, b):
    return lax.dot_general(a, b, (((0,), (0,)), ((), ())), preferred_element_type=F32)


def _ada_kernel(act_ref, w_ref, b_ref, o_ref):
    a = act_ref[...]
    a = a * jax.nn.sigmoid(a)
    o_ref[...] = _dot32(a, w_ref[...]) + b_ref[...]


def _ada_mod(act, ada_w, ada_b):
    depth, d, n = ada_w.shape
    tn = 1536
    return pl.pallas_call(
        _ada_kernel,
        grid=(depth, n // tn),
        in_specs=[
            pl.BlockSpec((8, d), lambda i, j: (0, 0)),
            pl.BlockSpec((None, d, tn), lambda i, j: (i, 0, j)),
            pl.BlockSpec((None, 1, tn), lambda i, j: (i, 0, j)),
        ],
        out_specs=pl.BlockSpec((None, 8, tn), lambda i, j: (i, 0, j)),
        out_shape=jax.ShapeDtypeStruct((depth, 8, n), F32),
        compiler_params=_cparams("arbitrary", "arbitrary"),
        name="ada_mod",
    )(act, ada_w, ada_b.reshape(depth, 1, n))


def _norm_mod(x, g, shift, scale):
    ms = jnp.mean(x * x, axis=-1, keepdims=True)
    y = x * lax.rsqrt(ms + NORM_EPS) * g
    return y * (1.0 + scale) + shift


def _inproj_kernel(sw, x_ref, g_ref, sh_ref, sc_ref, w_ref, o_ref, u8_ref, us_ref):
    h = _norm_mod(x_ref[...], g_ref[...], sh_ref[...], sc_ref[...])
    p = _dot(h.astype(BF16), w_ref[...])
    o_ref[...] = p.astype(BF16)
    base = p.shape[1] - sw
    cpt = TM // S5_CHUNK
    for o8 in range(sw // S5_OCT):
        us_ref[o8] = p[:, base + o8 * S5_OCT:base + (o8 + 1) * S5_OCT]
    for o8 in range(sw // S5_OCT):
        for s in range(S5_CHUNK):
            u8_ref[o8, :, s * S5_OCT:(s + 1) * S5_OCT] = (
                us_ref[o8, pl.ds(s, cpt, stride=S5_CHUNK), :].astype(BF16))


def _inproj(xcat, g, shift, scale, w_bf, nct, sw):
    b, tt, d = xcat.shape
    n = w_bf.shape[1]
    noct = sw // S5_OCT
    cpt = TM // S5_CHUNK
    kw = S5_CHUNK * S5_OCT
    kind = lambda bi, i: (bi, jnp.where(i < nct, 0, 1), 0, 0)
    return pl.pallas_call(
        functools.partial(_inproj_kernel, sw),
        grid=(b, tt // TM),
        in_specs=[
            pl.BlockSpec((None, TM, d), lambda bi, i: (bi, i, 0)),
            pl.BlockSpec((1, d), lambda bi, i: (0, 0)),
            pl.BlockSpec((None, None, 1, d), kind),
            pl.BlockSpec((None, None, 1, d), kind),
            pl.BlockSpec((d, n), lambda bi, i: (0, 0)),
        ],
        out_specs=[pl.BlockSpec((None, TM, n), lambda bi, i: (bi, i, 0)),
                   pl.BlockSpec((noct, cpt, kw), lambda bi, i: (0, i, bi))],
        out_shape=[jax.ShapeDtypeStruct((b, tt, n), BF16),
                   jax.ShapeDtypeStruct((noct, tt // S5_CHUNK, b * kw), BF16)],
        scratch_shapes=[pltpu.VMEM((noct, TM, S5_OCT), F32)],
        compiler_params=_cparams("parallel", "parallel"),
        name="inproj",
    )(xcat, g, shift, scale, w_bf)


def _split3(x):
    hi = x.astype(BF16)
    r1 = x - hi.astype(F32)
    mid = r1.astype(BF16)
    lo = (r1 - mid.astype(F32)).astype(BF16)
    return hi, mid, lo


def _prep_kernel(p_ref, up_ref, dn_ref, rm_ref, lc_ref, kk_ref, ka_ref, rk_ref, w0_ref, a0_ref,
                 wup_ref, aup_ref, gup_ref, bd_ref, tri_ref,
                 v_o, at_o, rt_o, bg_o, kg_o, ee_o, g_o, bo_o):
    x = p_ref[...].astype(F32)
    rm = rm_ref[...]
    lc = lc_ref[...]
    prev = pltpu.roll(x, 1, 0)
    nxt = pltpu.roll(x, TM - 1, 0)
    up = jnp.concatenate([up_ref[...].astype(F32), x[: TM - GRID_W]], axis=0)
    dn = jnp.concatenate([x[GRID_W:], dn_ref[...].astype(F32)], axis=0)
    slab = (x * lc[4:5]
            + rm[:, 0:1] * (prev * lc[0:1])
            + rm[:, 1:2] * (nxt * lc[1:2])
            + rm[:, 2:3] * (up * lc[2:3])
            + rm[:, 3:4] * (dn * lc[3:4]))
    w = kk_ref.shape[1]
    r = slab[:, 0:w]
    k = slab[:, w:2 * w]
    v = slab[:, 2 * w:3 * w]
    o = 3 * w
    wd = slab[:, o:o + 2 * DECAY_RANK]
    ad = slab[:, o + 2 * DECAY_RANK:o + 2 * DECAY_RANK + 2 * ICL_RANK]
    gd = slab[:, o + 2 * DECAY_RANK + 2 * ICL_RANK:]
    bd = bd_ref[...]
    kk = k * kk_ref[...]
    nrm = jnp.sqrt(_dot_ones(kk * kk, bd))
    kk = kk / jnp.maximum(nrm, L2_EPS)
    v_o[...] = v.astype(BF16)
    twd = jnp.tanh(wd)
    sgd = jax.nn.sigmoid(gd)
    c = WKV_CHUNK
    for d in range(2):
        z = w0_ref[d:d + 1, :] + _dot_w2(twd, wup_ref.at[d])
        w_log = -jax.nn.softplus(-z) - 0.5
        lw = -jnp.exp(w_log)
        a = jax.nn.sigmoid(a0_ref[d:d + 1, :] + _dot_w2(ad, aup_ref.at[d]))
        kt = k * (1.0 + (a - 1.0) * ka_ref[...])
        g_o[d] = _dot_w2(sgd, gup_ref.at[d]).astype(BF16)
        bo_o[d] = (_dot_ones(r * kt * rk_ref[...], bd) * v).astype(BF16)
        tri = tri_ref[d]
        hi, mid, lo = _split3(lw)
        lg_in = _dot(tri, hi) + _dot(tri, mid) + _dot(tri, lo)
        e_neg = jnp.exp(-lg_in)
        at_o[d] = (-kk * jnp.exp(lg_in - lw)).astype(BF16)
        rt_o[d] = (r * jnp.exp(lg_in)).astype(BF16)
        bg_o[d] = (kk * a * e_neg).astype(BF16)
        kg_o[d] = (kt * e_neg).astype(BF16)
        for ci in range(TM // c):
            last = ci * c + (c - 1 if d == 0 else 0)
            ee_o[d, ci] = jnp.exp(lg_in[last:last + 1, :])


def _rwkv_prep(p, rowmask, lanec, k_k, k_a, r_k, w0, a0, wup, aup, gup, bd, tri, nct, slab_w):
    b, tt, _ = p.shape
    w = k_k.shape[1]
    nt = tt // TM
    cpt = TM // WKV_CHUNK
    hb = TM // GRID_W
    nhb = tt // GRID_W
    full = lambda *s: pl.BlockSpec(s, lambda bi, i: (0,) * len(s))
    tok = pl.BlockSpec((None, TM, w), lambda bi, i: (bi, i, 0))
    tok2 = pl.BlockSpec((2, None, TM, w), lambda bi, i: (0, bi, i, 0))
    bf1 = jax.ShapeDtypeStruct((b, tt, w), BF16)
    bf2 = jax.ShapeDtypeStruct((2, b, tt, w), BF16)
    sh2 = jax.ShapeDtypeStruct((2, b, tt, w), F32)
    return pl.pallas_call(
        _prep_kernel,
        grid=(b, nt),
        in_specs=[
            pl.BlockSpec((None, TM, slab_w), lambda bi, i: (bi, i, 0)),
            pl.BlockSpec((None, GRID_W, slab_w), lambda bi, i: (bi, jnp.maximum(i * hb - 1, 0), 0)),
            pl.BlockSpec((None, GRID_W, slab_w),
                         lambda bi, i: (bi, jnp.minimum(i * hb + hb, nhb - 1), 0)),
            pl.BlockSpec((None, TM, 8), lambda bi, i: (i, 0, 0)),
            pl.BlockSpec((None, 8, slab_w), lambda bi, i: (jnp.where(i < nct, 0, 1), 0, 0)),
            full(1, w), full(1, w), full(1, w), full(2, w), full(2, w),
            full(2, 2, 2 * DECAY_RANK, w), full(2, 2, 2 * ICL_RANK, w), full(2, 2, 2 * GATE_RANK, w),
            full(w, w), full(2, TM, TM),
        ],
        out_specs=[tok, tok2, tok2, tok2, tok2,
                   pl.BlockSpec((2, None, cpt, 1, w), lambda bi, i: (0, bi, i, 0, 0)),
                   tok2, tok2],
        out_shape=[bf1, bf2, bf2, bf2, bf2,
                   jax.ShapeDtypeStruct((2, b, tt // WKV_CHUNK, 1, w), F32), bf2, bf2],
        compiler_params=_cparams("parallel", "parallel"),
        name="rwkv_prep",
    )(p, p, p, rowmask, lanec, k_k, k_a, r_k, w0, a0, wup, aup, gup, bd, tri)


def _wkv_kernel(nb, ngrp, v_f, v_r, at_f, at_r, rt_f, rt_r, bg_f, bg_r, kg_f, kg_r, ee_f, ee_r,
                y_f, y_r, ht_ref):
    j = pl.program_id(0)
    c = WKV_CHUNK
    gw = WKV_HEADS * HEAD
    gn = WKV_HEADS * c

    @pl.when(j == 0)
    def _():
        ht_ref[...] = jnp.zeros_like(ht_ref)

    sh = int(math.log2(c))
    row = lax.broadcasted_iota(jnp.int32, (gn, gw), 0)
    col = lax.broadcasted_iota(jnp.int32, (gn, gw), 1)
    same = (row >> sh) == (col >> sh)
    tf = lax.broadcasted_iota(jnp.int32, (c, gn), 0)
    sf = lax.broadcasted_iota(jnp.int32, (c, gn), 1) & (c - 1)
    eye = (tf == sf).astype(F32)

    def stack(x):
        xb = jnp.concatenate([x.astype(BF16)] * WKV_HEADS, axis=0)
        return jnp.where(same, xb, jnp.zeros_like(xb))

    dirs = ((v_f, at_f, rt_f, bg_f, kg_f, ee_f, y_f, sf < tf, sf <= tf),
            (v_r, at_r, rt_r, bg_r, kg_r, ee_r, y_r, sf > tf, sf >= tf))
    chains = [(d, bi, q) for d in range(2) for bi in range(nb) for q in range(ngrp)]
    sl = lambda q: slice(q * gw, (q + 1) * gw)
    rd = lambda k: [dirs[d][k][bi, :, sl(q)] for d, bi, q in chains]
    cat0 = lambda xs: jnp.concatenate(xs, axis=0)
    v, at, rt, bg, kg, ee = rd(0), rd(1), rd(2), rd(3), rd(4), rd(5)
    before = [dirs[d][7] for d, _, _ in chains]
    incl = [dirs[d][8] for d, _, _ in chains]
    n_ch = range(len(chains))

    v_bd = [stack(x) for x in v]
    at_bd = [stack(x) for x in at]
    bk_bd = [cat0([stack(bg[i]), stack(kg[i])]) for i in n_ch]
    a = [_dot_nt(cat0([at[i], rt[i]]), bk_bd[i]) for i in n_ch]
    n = [jnp.where(before[i], a[i][0:c, 0:gn], 0.0) for i in n_ch]
    a_kk = [cat0([jnp.where(before[i], a[i][0:c, gn:], 0.0),
                  jnp.where(incl[i], a[i][c:, gn:], 0.0)]).astype(BF16) for i in n_ch]
    a_rb = [jnp.where(incl[i], a[i][c:, 0:gn], 0.0).astype(BF16) for i in n_ch]
    tm = [eye + x for x in n]
    pw = [_dot(x.astype(BF16), stack(x)) for x in n]
    for lvl in range(1, sh):
        pw_bd = [stack(x) for x in pw]
        if lvl < sh - 1:
            tp = [_dot(cat0([tm[i].astype(BF16), pw[i].astype(BF16)]), pw_bd[i]) for i in n_ch]
            tm = [tm[i] + tp[i][0:c] for i in n_ch]
            pw = [tp[i][c:] for i in n_ch]
        else:
            tm = [tm[i] + _dot(tm[i].astype(BF16), pw_bd[i]) for i in n_ch]
    tm_b = [x.astype(BF16) for x in tm]
    atp = [_dot(tm_b[i], at_bd[i]) for i in n_ch]
    av = [_dot(a_kk[i], v_bd[i]) for i in n_ch]
    wv = [_dot(tm_b[i], stack(av[i][0:c])) for i in n_ch]
    wv_bd = [stack(x) for x in wv]
    atp_bd = [stack(x) for x in atp]
    ar = [_dot(a_rb[i], jnp.concatenate([wv_bd[i], atp_bd[i]], axis=1)) for i in n_ch]
    y0 = [ar[i][:, 0:gw] + av[i][c:] for i in n_ch]
    rtp = [(ar[i][:, gw:] + rt[i].astype(F32)).astype(BF16) for i in n_ch]
    bge_bd = [stack(bg[i].astype(F32) * ee[i]) for i in n_ch]
    kge_bd = [stack(kg[i].astype(F32) * ee[i]) for i in n_ch]
    g = [_dot_tn(bge_bd[i], atp_bd[i]).astype(BF16) for i in n_ch]
    hloc_t = [_dot_tn(cat0([wv_bd[i], v_bd[i]]), cat0([bge_bd[i], kge_bd[i]])) for i in n_ch]
    for i, (d, bi, q) in enumerate(chains):
        ht = ht_ref[d, bi, q]
        ht_b = ht.astype(BF16)
        dirs[d][6][bi, :, sl(q)] = y0[i] + _dot_nt(rtp[i], ht_b)
        ht_ref[d, bi, q] = ht * ee[i] + _dot_nt(ht_b, g[i]) + hloc_t[i]


def _wkv_scan(v, at, rt, bg, kg, ee, nctc):
    b, tt, w = v.shape
    ntot = tt // WKV_CHUNK
    ngrp = w // (WKV_HEADS * HEAD)
    fwd = lambda j: j
    rev = lambda j: jnp.where(j < nctc, nctc - 1 - j, ntot - 1 + nctc - j)
    tok = lambda cm: pl.BlockSpec((b, WKV_CHUNK, w), lambda j: (0, cm(j), 0))
    tok2 = lambda d, cm: pl.BlockSpec((None, b, WKV_CHUNK, w), lambda j: (d, 0, cm(j), 0))
    eesp = lambda d, cm: pl.BlockSpec((None, b, None, 1, w), lambda j: (d, 0, cm(j), 0, 0))
    pair = lambda f: [f(0, fwd), f(1, rev)]
    ysh = jax.ShapeDtypeStruct((b, tt, w), F32)
    return pl.pallas_call(
        functools.partial(_wkv_kernel, b, ngrp),
        grid=(ntot,),
        in_specs=[tok(fwd), tok(rev)] + pair(tok2) + pair(tok2) + pair(tok2) + pair(tok2) + pair(eesp),
        out_specs=[tok(fwd), tok(rev)],
        out_shape=[ysh, ysh],
        scratch_shapes=[pltpu.VMEM((2, b, ngrp, WKV_HEADS * HEAD, WKV_HEADS * HEAD), F32)],
        compiler_params=_cparams("arbitrary"),
        name="wkv_scan",
    )(v, v, at, at, rt, rt, bg, bg, kg, kg, ee, ee)


def _s5_weights(lam_re, lam_im, log_dt, b_re, b_im, c_re, c_im):
    tc = S5_CHUNK
    lr = jnp.minimum(lam_re.astype(F32), LAM_RE_MAX)
    li = lam_im.astype(F32)
    dt = jnp.exp(log_dt.astype(F32))[..., None]
    mag = jnp.exp(lr * dt)
    ar = mag * jnp.cos(li * dt)
    ai = mag * jnp.sin(li * dt)
    den = lr * lr + li * li
    xr = ar - 1.0
    cr = (xr * lr + ai * li) / den
    ci = (ai * lr - xr * li) / den
    br = cr[..., None] * b_re - ci[..., None] * b_im
    bi = cr[..., None] * b_im + ci[..., None] * b_re
    pr, pi = [jnp.ones_like(ar)], [jnp.zeros_like(ar)]
    for _ in range(tc):
        pr_n = pr[-1] * ar - pi[-1] * ai
        pi_n = pr[-1] * ai + pi[-1] * ar
        pr.append(pr_n)
        pi.append(pi_n)
    pr = jnp.stack(pr)
    pi = jnp.stack(pi)
    lbr = pr[..., None] * br - pi[..., None] * bi
    lbi = pr[..., None] * bi + pi[..., None] * br
    clr = c_re * pr[:, :, :, None, :] - c_im * pi[:, :, :, None, :]
    cli = c_re * pi[:, :, :, None, :] + c_im * pr[:, :, :, None, :]
    lbr_t = jnp.swapaxes(lbr, -1, -2)
    lbi_t = jnp.swapaxes(lbi, -1, -2)
    kern_t = jnp.sum(lbr_t[..., :, None, :] * c_re[None, :, :, None, :, :]
                     - lbi_t[..., :, None, :] * c_im[None, :, :, None, :, :], axis=-1)
    g = ar.shape[1]
    og = S5_OCT // S5_GROUP
    noct = g // og
    eye = jnp.eye(og, dtype=F32)

    def bdiag(x):
        nt, _, _, a, n = x.shape
        x = x.reshape(nt, 2, noct, og, a, n)
        y = x[:, :, :, :, :, None, :] * eye[None, None, None, :, None, :, None]
        return jnp.transpose(y, (1, 2, 0, 3, 4, 5, 6)).reshape(2, noct, nt, og * a, og * n).astype(BF16)

    kbd = bdiag(kern_t)

    def pair_block(d, lp):
        kd = kbd[d]
        zero = jnp.zeros_like(kd[:, 0])
        k = lambda tau: kd[:, tau] if tau >= 0 else zero
        if d == 0:
            rows = [[k(2 * lp), k(2 * lp + 1)], [k(2 * lp - 1), k(2 * lp)]]
        else:
            rows = [[k(2 * lp), k(2 * lp - 1)], [k(2 * lp + 1), k(2 * lp)]]
        return jnp.concatenate([jnp.concatenate(r, axis=-1) for r in rows], axis=-2)

    wpair = jnp.stack([jnp.stack([pair_block(d, lp) for lp in range(tc // 2)], axis=1)
                       for d in range(2)])
    lbc = jnp.concatenate([lbr_t, lbi_t], axis=-1)
    clc = jnp.concatenate([clr, -cli], axis=-1)
    pout8, qin8 = _s5_expand(lbc, clc)
    la = jnp.concatenate([pr[tc], pr[tc]], axis=-1)
    lb = jnp.concatenate([-pi[tc], pi[tc]], axis=-1)
    return wpair, pout8, qin8, la, lb


def _s5_expand_kernel(p_ref, q_ref, po_ref, qo_ref):
    nt, _, og, a, n = p_ref.shape
    tc = nt - 1
    po_ref[...] = jnp.zeros_like(po_ref)
    qo_ref[...] = jnp.zeros_like(qo_ref)
    for d in range(2):
        for s in range(tc):
            lag_out = tc - 1 - s if d == 0 else s
            lag_in = s + 1 if d == 0 else tc - s
            for gi in range(og):
                rows, lanes = slice(gi * a, (gi + 1) * a), slice(gi * n, (gi + 1) * n)
                po_ref[d, s, rows, lanes] = p_ref[lag_out, d, gi].astype(BF16)
                qo_ref[d, s, rows, lanes] = q_ref[lag_in, d, gi].astype(BF16)


def _s5_expand(lbc, clc):
    nt, _, g, a, n = lbc.shape
    tc = nt - 1
    og = S5_OCT // S5_GROUP
    noct = g // og
    isp = pl.BlockSpec((nt, 2, og, a, n), lambda o: (0, 0, o, 0, 0))
    osp = pl.BlockSpec((2, None, tc, og * a, og * n), lambda o: (0, o, 0, 0, 0))
    osh = jax.ShapeDtypeStruct((2, noct, tc, og * a, og * n), BF16)
    po, qo = pl.pallas_call(
        _s5_expand_kernel,
        grid=(noct,),
        in_specs=[isp, isp],
        out_specs=[osp, osp],
        out_shape=[osh, osh],
        compiler_params=_cparams("parallel"),
        name="s5_expand",
    )(lbc, clc)
    return po.reshape(2, noct, tc * og * a, og * n), qo.reshape(2, noct, tc * og * a, og * n)


def _s5_local_kernel(u_ref, p_ref, e_ref):
    e_ref[...] = _dot(u_ref[...], p_ref[...])


def _s5_local(u8, pout8, nb):
    noct, nch, _ = u8.shape
    kw, n = pout8.shape[2:]
    return pl.pallas_call(
        _s5_local_kernel,
        grid=(noct, nb, 2),
        in_specs=[
            pl.BlockSpec((None, nch, kw), lambda o, b, d: (o, 0, b)),
            pl.BlockSpec((None, None, kw, n), lambda o, b, d: (d, o, 0, 0)),
        ],
        out_specs=pl.BlockSpec((None, nch, n), lambda o, b, d: (d, 0, o * nb + b)),
        out_shape=jax.ShapeDtypeStruct((2, nch, noct * nb * n), F32),
        compiler_params=_cparams("parallel", "parallel", "parallel"),
        name="s5_local",
    )(u8, pout8)


S5_STATE_ROWS = 8
S5_STATE_LANES = 256


def _s5_state_kernel(nctc, ntot, e_ref, la_ref, lb_ref, x_ref, es_ref):
    d = pl.program_id(0)
    la = la_ref[...]
    lb = lb_ref[...]
    nr, wl = la.shape

    def swap(t):
        lane = lax.broadcasted_iota(jnp.int32, t.shape, 1)
        first_half = (lane & (2 * S5_STATE - 1)) < S5_STATE
        return jnp.where(first_half, pltpu.roll(t, wl - S5_STATE, 1), pltpu.roll(t, S5_STATE, 1))

    es_ref[...] = swap(e_ref[...].reshape(ntot * nr, wl)).reshape(ntot, nr, wl)
    lbs = swap(lb)

    def body(j, carry):
        x, xs = carry
        rev_idx = jnp.where(j < nctc, nctc - 1 - j, ntot - 1 + nctc - j)
        c = jnp.where(d == 0, j, rev_idx)
        x_ref[c] = x
        return la * x + lb * xs + e_ref[c], la * xs + lbs * x + es_ref[c]

    zero = jnp.zeros(la.shape, F32)
    lax.fori_loop(0, ntot, body, (zero, zero))


def _s5_state(e, la, lb, nctc, ntot):
    _, nch, nr, lanes = e.shape
    wl = S5_STATE_LANES
    blk = pl.BlockSpec((None, nch, nr, wl), lambda d, i: (d, 0, 0, i))
    cf = pl.BlockSpec((None, nr, wl), lambda d, i: (d, 0, i))
    return pl.pallas_call(
        functools.partial(_s5_state_kernel, nctc, ntot),
        grid=(2, lanes // wl),
        in_specs=[blk, cf, cf],
        out_specs=blk,
        out_shape=jax.ShapeDtypeStruct(e.shape, F32),
        scratch_shapes=[pltpu.VMEM((nch, nr, wl), F32)],
        compiler_params=_cparams("parallel", "parallel"),
        name="s5_state",
    )(e, la, lb)


def _s5_out_kernel(rev, u_ref, w_ref, x_ref, q_ref, y_ref):
    pw = w_ref.shape[1]
    npair = w_ref.shape[0]
    x = x_ref[...].astype(BF16)
    for tp in range(npair):
        acc = _dot_nt(x, q_ref[tp * pw:(tp + 1) * pw, :])
        for lp in range(npair - tp if rev else tp + 1):
            sp = tp + lp if rev else tp - lp
            acc = acc + _dot(u_ref[:, sp * pw:(sp + 1) * pw], w_ref[lp])
        y_ref[:, tp * pw:(tp + 1) * pw] = acc


def _s5_out(u8, wpair, xin, qin8, nb, d):
    noct, nch, _ = u8.shape
    npair, pw = wpair.shape[2:4]
    kw = npair * pw
    n2 = qin8.shape[3]
    return pl.pallas_call(
        functools.partial(_s5_out_kernel, d == 1),
        grid=(noct, nb),
        in_specs=[
            pl.BlockSpec((None, nch, kw), lambda o, b: (o, 0, b)),
            pl.BlockSpec((None, None, npair, pw, pw), lambda o, b: (d, o, 0, 0, 0)),
            pl.BlockSpec((None, nch, n2), lambda o, b: (d, 0, o * nb + b)),
            pl.BlockSpec((None, None, kw, n2), lambda o, b: (d, o, 0, 0)),
        ],
        out_specs=pl.BlockSpec((None, nch, kw), lambda o, b: (o, 0, b)),
        out_shape=jax.ShapeDtypeStruct((noct, nch, nb * kw), F32),
        compiler_params=_cparams("parallel", "parallel"),
        name="s5_out",
    )(u8, wpair, xin, qin8)


def _s5_mix(u8, weights, nb, nctc16):
    wpair, pout8, qin8, la, lb = weights
    noct, nch, _ = u8.shape
    e = _s5_local(u8, pout8, nb)
    assert noct * nb == S5_STATE_ROWS
    fl = e.shape[2] // S5_STATE_ROWS
    coef = lambda t: jnp.repeat(t.reshape(2, noct, fl), nb, axis=1)
    xin = _s5_state(e.reshape(2, nch, S5_STATE_ROWS, fl), coef(la), coef(lb), nctc16, nch)
    xin = xin.reshape(e.shape)
    return _s5_out(u8, wpair, xin, qin8, nb, 0), _s5_out(u8, wpair, xin, qin8, nb, 1)


def _mixout_kernel(x_ref, yf_ref, yr_ref, g_ref, bo_ref, y8f_ref, y8r_ref, u_ref, lnw_ref, lnb_ref,
                   bd_ref, dsk_ref, gluw_ref, glub_ref, wout_ref, gate_ref, o_ref, ysn_ref):
    cpt = TM // S5_CHUNK
    for o8 in range(y8f_ref.shape[0]):
        for s in range(S5_CHUNK):
            lanes = slice(s * S5_OCT, (s + 1) * S5_OCT)
            ysn_ref[o8, pl.ds(s, cpt, stride=S5_CHUNK), :] = y8f_ref[o8, :, lanes] + y8r_ref[o8, :, lanes]
    ys = jnp.concatenate([ysn_ref[o8] for o8 in range(y8f_ref.shape[0])], axis=1)
    bd = bd_ref[...]
    inv = 1.0 / HEAD
    rw = None
    for d, y_ref in enumerate((yf_ref, yr_ref)):
        y = y_ref[...]
        mean = _dot_ones(y, bd) * inv
        yc = y - mean
        var = _dot_ones(yc * yc, bd) * inv
        yn = yc * lax.rsqrt(var + GN_EPS) * lnw_ref[...] + lnb_ref[...]
        o = (yn + bo_ref[d].astype(F32)) * g_ref[d].astype(F32)
        rw = o if rw is None else rw + o
    u = u_ref[...].astype(F32)
    ss = ys + dsk_ref[...] * u
    ss = jax.nn.gelu(ss)
    ss = ss * jax.nn.sigmoid(_dot(ss.astype(BF16), gluw_ref[...]) + glub_ref[...])
    w = rw.shape[1]
    mix = _dot(rw.astype(BF16), wout_ref[0:w, :]) + _dot(ss.astype(BF16), wout_ref[w:, :])
    o_ref[...] = x_ref[...] + gate_ref[...] * mix


def _mixout(xcat, yf, yr, g, bo, y8, p, ln_w, ln_b, bd, d_skip, glu_w, glu_b, w_out, gate, nct, t0):
    b, tt, d = xcat.shape
    w = ln_w.shape[1]
    sw = d_skip.shape[1]
    y8f, y8r = y8
    noct = y8f.shape[0]
    cpt = TM // S5_CHUNK
    kw = S5_CHUNK * S5_OCT
    ublk = (p.shape[2] - sw) // sw
    nt = tt // TM - t0
    full = lambda *s: pl.BlockSpec(s, lambda bi, i: (0,) * len(s))
    tok = pl.BlockSpec((None, TM, w), lambda bi, i: (bi, i + t0, 0))
    tok2 = pl.BlockSpec((2, None, TM, w), lambda bi, i: (0, bi, i + t0, 0))
    return pl.pallas_call(
        _mixout_kernel,
        grid=(b, nt),
        in_specs=[
            pl.BlockSpec((None, TM, d), lambda bi, i: (bi, i + t0, 0)),
            tok, tok, tok2, tok2,
            pl.BlockSpec((noct, cpt, kw), lambda bi, i: (0, i + t0, bi)),
            pl.BlockSpec((noct, cpt, kw), lambda bi, i: (0, i + t0, bi)),
            pl.BlockSpec((None, TM, sw), lambda bi, i: (bi, i + t0, ublk)),
            full(1, w), full(1, w), full(w, w), full(1, sw), full(sw, sw), full(1, sw),
            full(w + sw, d),
            pl.BlockSpec((None, None, 1, d), lambda bi, i: (bi, jnp.where(i + t0 < nct, 0, 1), 0, 0)),
        ],
        out_specs=pl.BlockSpec((None, TM, d), lambda bi, i: (bi, i, 0)),
        out_shape=jax.ShapeDtypeStruct((b, nt * TM, d), F32),
        scratch_shapes=[pltpu.VMEM((noct, TM, S5_OCT), F32)],
        compiler_params=_cparams("parallel", "parallel"),
        name="mix_out",
    )(xcat, yf, yr, g, bo, y8f, y8r, p, ln_w, ln_b, bd, d_skip, glu_w, glu_b, w_out, gate)


def _ffn_kernel(x_ref, g_ref, sh_ref, sc_ref, gate_ref, wg_ref, wu_ref, wd_ref, o_ref):
    x = x_ref[...]
    h = _norm_mod(x, g_ref[...], sh_ref[...], sc_ref[...]).astype(BF16)
    a = _dot(h, wg_ref[...])
    a = a * jax.nn.sigmoid(a) * _dot(h, wu_ref[...])
    o_ref[...] = x + gate_ref[...] * _dot(a.astype(BF16), wd_ref[...])


def _ffn(xcat, g, shift, scale, gate, wg, wu, wd, nct):
    b, tt, d = xcat.shape
    ff = wg.shape[1]
    kind = lambda bi, i: (bi, jnp.where(i < nct, 0, 1), 0, 0)
    mod = pl.BlockSpec((None, None, 1, d), kind)
    return pl.pallas_call(
        _ffn_kernel,
        grid=(b, tt // TM),
        in_specs=[
            pl.BlockSpec((None, TM, d), lambda bi, i: (bi, i, 0)),
            pl.BlockSpec((1, d), lambda bi, i: (0, 0)),
            mod, mod, mod,
            pl.BlockSpec((d, ff), lambda bi, i: (0, 0)),
            pl.BlockSpec((d, ff), lambda bi, i: (0, 0)),
            pl.BlockSpec((ff, d), lambda bi, i: (0, 0)),
        ],
        out_specs=pl.BlockSpec((None, TM, d), lambda bi, i: (bi, i, 0)),
        out_shape=jax.ShapeDtypeStruct((b, tt, d), F32),
        compiler_params=_cparams("parallel", "parallel"),
        name="ffn",
    )(xcat, g, shift, scale, gate, wg, wu, wd)


MOE_TR = 1024
MOE_TM = 2048
MOE_TF = 896
MOE_BLK = 256
MOE_SUB = 256


def _route_kernel(ne, x_ref, g_ref, sh_ref, sc_ref, rt_ref, h_o, cmb_o, cnt_o):
    h = _norm_mod(x_ref[...], g_ref[...], sh_ref[...], sc_ref[...])
    h_o[...] = h.astype(BF16)
    logits = lax.dot_general(rt_ref[...], h, (((1,), (1,)), ((), ())), precision=HIGHEST,
                             preferred_element_type=F32)
    sub = lax.broadcasted_iota(jnp.int32, logits.shape, 0).astype(F32)
    none = float(logits.shape[0])
    logits = jnp.where(sub < ne, logits, -jnp.inf)
    m1 = jnp.max(logits, axis=0, keepdims=True)
    i1 = jnp.min(jnp.where(logits == m1, sub, none), axis=0, keepdims=True)
    rest = jnp.where(sub == i1, -jnp.inf, logits)
    m2 = jnp.max(rest, axis=0, keepdims=True)
    i2 = jnp.min(jnp.where(rest == m2, sub, none), axis=0, keepdims=True)
    e2 = jnp.exp(m2 - m1)
    p1 = 1.0 / (1.0 + e2)
    p2 = e2 / (1.0 + e2)
    cmb = jnp.where(sub == i1, p1, 0.0) + jnp.where(sub == i2, p2, 0.0)
    cmb_o[...] = cmb
    cnt = jnp.sum((cmb > 0.0).astype(F32), axis=1, keepdims=True)
    cnt_o[...] = jnp.broadcast_to(cnt, cnt_o.shape).astype(jnp.int32)


def _route(x, g, shift, scale, router_t, ne):
    b, l, d = x.shape
    nr = router_t.shape[0]
    nt = l // MOE_TR
    mod = pl.BlockSpec((None, 1, d), lambda bi, i: (bi, 0, 0))
    return pl.pallas_call(
        functools.partial(_route_kernel, ne),
        grid=(b, nt),
        in_specs=[
            pl.BlockSpec((None, MOE_TR, d), lambda bi, i: (bi, i, 0)),
            pl.BlockSpec((1, d), lambda bi, i: (0, 0)),
            mod, mod,
            pl.BlockSpec((nr, d), lambda bi, i: (0, 0)),
        ],
        out_specs=[
            pl.BlockSpec((MOE_TR, d), lambda bi, i: (bi * nt + i, 0)),
            pl.BlockSpec((nr, MOE_TR), lambda bi, i: (0, bi * nt + i)),
            pl.BlockSpec((None, nr, 128), lambda bi, i: (bi * nt + i, 0, 0)),
        ],
        out_shape=[
            jax.ShapeDtypeStruct((b * l, d), BF16),
            jax.ShapeDtypeStruct((nr, b * l), F32),
            jax.ShapeDtypeStruct((b * nt, nr, 128), jnp.int32),
        ],
        compiler_params=_cparams("parallel", "parallel"),
        name="moe_route",
    )(x, g, shift, scale, router_t)


def _moe_kernel(cnt_ref, h_ref, cmb_ref, tri_ref, wg_ref, wu_ref, wd_ref, o_ref,
                pos_ref, hg_ref, ya_ref):
    t = pl.program_id(0)
    e = pl.program_id(1)
    j = pl.program_id(2)
    tm = h_ref.shape[0]
    nblk = jnp.right_shift(cnt_ref[t, e] + (MOE_BLK - 1), int(math.log2(MOE_BLK)))

    @pl.when((e == 0) & (j == 0))
    def _():
        o_ref[...] = jnp.zeros_like(o_ref)
        asg = (cmb_ref[...] > 0.0).astype(BF16)
        off = jnp.zeros((asg.shape[0], 1), F32)
        for k in range(tm // MOE_SUB):
            blk = asg[:, k * MOE_SUB:(k + 1) * MOE_SUB]
            pos_ref[:, k * MOE_SUB:(k + 1) * MOE_SUB] = _dot(blk, tri_ref[...]) + off
            off = off + jnp.sum(blk.astype(F32), axis=1, keepdims=True)

    sel = lax.broadcasted_iota(jnp.int32, pos_ref.shape, 0) == e
    posrow = jnp.sum(jnp.where(sel, pos_ref[...], 0.0), axis=0, keepdims=True)
    cwrow = jnp.sum(jnp.where(sel, cmb_ref[...], 0.0), axis=0, keepdims=True)
    rowi = lax.broadcasted_iota(jnp.int32, (MOE_BLK, tm), 0).astype(F32)

    def onehot(b):
        slot = rowi + (b * MOE_BLK).astype(F32)
        return (posrow == slot) & (cwrow > 0.0)

    @pl.when(j == 0)
    def _():
        def gather(b, carry):
            sel_b = jnp.where(onehot(b), 1.0, 0.0).astype(BF16)
            hg_ref[b] = _dot(sel_b, h_ref[...]).astype(BF16)
            ya_ref[b] = jnp.zeros(ya_ref.shape[1:], F32)
            return carry
        lax.fori_loop(0, nblk, gather, 0)

    d_model = hg_ref.shape[2]

    def swiglu(hb):
        a = _dot(hb, wg_ref[...])
        a = a * jax.nn.sigmoid(a) * _dot(hb, wu_ref[...])
        return _dot(a.astype(BF16), wd_ref[...])

    def ffn_pair(i, carry):
        rows = pl.ds(2 * i, 2)
        y = swiglu(hg_ref[rows].reshape(2 * MOE_BLK, d_model))
        ya_ref[rows] += y.reshape(2, MOE_BLK, d_model)
        return carry
    npair = jnp.right_shift(nblk, 1)
    lax.fori_loop(0, npair, ffn_pair, 0)

    @pl.when(nblk > 2 * npair)
    def _():
        ya_ref[nblk - 1] += swiglu(hg_ref[nblk - 1])

    @pl.when(j == pl.num_programs(2) - 1)
    def _():
        def scatter(b, carry):
            wsel = jnp.where(onehot(b), cwrow, 0.0).astype(BF16)
            o_ref[...] += _dot_tn(wsel, ya_ref[b].astype(BF16))
            return carry
        lax.fori_loop(0, nblk, scatter, 0)


def _moe(h, cmb, cnt, tri, wg, wu, wd):
    n, d = h.shape
    nr = cmb.shape[0]
    ne, _, ff = wg.shape
    nbmax = MOE_TM // MOE_BLK
    grid_spec = pltpu.PrefetchScalarGridSpec(
        num_scalar_prefetch=1,
        grid=(n // MOE_TM, ne, ff // MOE_TF),
        in_specs=[
            pl.BlockSpec((MOE_TM, d), lambda t, e, j, c: (t, 0)),
            pl.BlockSpec((nr, MOE_TM), lambda t, e, j, c: (0, t)),
            pl.BlockSpec((MOE_SUB, MOE_SUB), lambda t, e, j, c: (0, 0)),
            pl.BlockSpec((None, d, MOE_TF), lambda t, e, j, c: (e, 0, j)),
            pl.BlockSpec((None, d, MOE_TF), lambda t, e, j, c: (e, 0, j)),
            pl.BlockSpec((None, MOE_TF, d), lambda t, e, j, c: (e, j, 0)),
        ],
        out_specs=pl.BlockSpec((MOE_TM, d), lambda t, e, j, c: (t, 0)),
        scratch_shapes=[
            pltpu.VMEM((nr, MOE_TM), F32),
            pltpu.VMEM((nbmax, MOE_BLK, d), BF16),
            pltpu.VMEM((nbmax, MOE_BLK, d), F32),
        ],
    )
    return pl.pallas_call(
        _moe_kernel,
        grid_spec=grid_spec,
        out_shape=jax.ShapeDtypeStruct((n, d), F32),
        compiler_params=_cparams("parallel", "arbitrary", "arbitrary"),
        name="moe",
    )(cnt, h, cmb, tri, wg, wu, wd)


def _final_kernel(x_ref, m_ref, gate_ref, fg_ref, o_ref):
    y = x_ref[...] + gate_ref[...] * m_ref[...]
    ms = jnp.mean(y * y, axis=-1, keepdims=True)
    o_ref[...] = y * lax.rsqrt(ms + NORM_EPS) * fg_ref[...]


def _final(x, m, gate, final_g):
    b, l, d = x.shape
    nt = l // MOE_TR
    return pl.pallas_call(
        _final_kernel,
        grid=(b, nt),
        in_specs=[
            pl.BlockSpec((None, MOE_TR, d), lambda bi, i: (bi, i, 0)),
            pl.BlockSpec((MOE_TR, d), lambda bi, i: (bi * nt + i, 0)),
            pl.BlockSpec((None, 1, d), lambda bi, i: (bi, 0, 0)),
            pl.BlockSpec((1, d), lambda bi, i: (0, 0)),
        ],
        out_specs=pl.BlockSpec((None, MOE_TR, d), lambda bi, i: (bi, i, 0)),
        out_shape=jax.ShapeDtypeStruct((b, l, d), F32),
        compiler_params=_cparams("parallel", "parallel"),
        name="moe_final",
    )(x, m, gate, final_g)


def _shift_masks(mu, ctx_len, seq_len):
    slab = mu.shape[0]
    nct = ctx_len // TM
    tt = ctx_len + seq_len
    t = jnp.arange(tt)
    is_ctx = t < ctx_len
    tl = t - ctx_len
    col = tl % GRID_W
    rows = seq_len // GRID_W
    grow = tl // GRID_W
    left = jnp.where(is_ctx, t != 0, col != 0)
    right = jnp.where(is_ctx, t != ctx_len - 1, col != GRID_W - 1)
    upv = jnp.where(is_ctx, False, grow != 0)
    dnv = jnp.where(is_ctx, False, grow != rows - 1)
    zero = jnp.zeros_like(left)
    rowmask = jnp.stack([left, right, upv, dnv, zero, zero, zero, zero], axis=-1).astype(F32)
    rowmask = rowmask.reshape(tt // TM, TM, 8)
    c = jnp.arange(slab)
    z = jnp.zeros_like(mu)
    lat = jnp.stack([mu * (c % 4 == 0), mu * (c % 4 == 1), mu * (c % 4 == 2), mu * (c % 4 == 3),
                     1.0 - mu, z, z, z])
    ctx = jnp.stack([mu * (c % 2 == 0), mu * (c % 2 == 1), z, z, 1.0 - mu, z, z, z])
    return rowmask, jnp.stack([ctx, lat]).astype(F32)


def _pad_rows(wt):
    z = jnp.zeros_like(wt[0])
    wp = jnp.stack([jnp.concatenate([wt[0], z], axis=0), jnp.concatenate([z, wt[1]], axis=0)])
    hi = wp.astype(BF16)
    lo = (wp - hi.astype(F32)).astype(BF16)
    return jnp.stack([hi, lo], axis=1)


def kernel(x, c, ctx, c_ctx, ada_w, ada_b, norm1_g, norm2_g, w_in, w_out, shift_mu, rwkv_w0, rwkv_w_up, rwkv_a0, rwkv_a_up, rwkv_g_up, rwkv_k_k, rwkv_k_a, rwkv_r_k, rwkv_ln_w, rwkv_ln_b, s5_lam_re, s5_lam_im, s5_log_dt, s5_b_re, s5_b_im, s5_c_re, s5_c_im, s5_d, s5_glu_w, s5_glu_b, ffn_w_gate, ffn_w_up, ffn_w_down, moe_router, moe_w_gate, moe_w_up, moe_w_down, final_g):
    b, l, d = x.shape
    ctx_len = ctx.shape[1]
    depth = ada_w.shape[0]
    slab_w = shift_mu.shape[1]
    rw_w = rwkv_k_k.shape[1]
    assert ctx_len == TM and l % TM == 0 and b + 1 <= 8
    assert rw_w % (WKV_HEADS * HEAD) == 0 and depth == 2
    nct = ctx_len // TM
    nctc = ctx_len // WKV_CHUNK
    nctc16 = ctx_len // S5_CHUNK

    act = jnp.zeros((8, d), F32).at[:b].set(c).at[b].set(c_ctx)
    mods = _ada_mod(act, ada_w, ada_b).reshape(depth, 8, 6, d)

    def mod(i, k):
        cm = jnp.broadcast_to(mods[i, b, k][None, :], (b, d))
        return jnp.stack([cm, mods[i, :b, k]], axis=1)[:, :, None, :]

    hi = lax.broadcasted_iota(jnp.int32, (rw_w, rw_w), 0) // HEAD
    hj = lax.broadcasted_iota(jnp.int32, (rw_w, rw_w), 1) // HEAD
    bd = (hi == hj).astype(BF16)

    ti = lax.broadcasted_iota(jnp.int32, (TM, TM), 0)
    si = lax.broadcasted_iota(jnp.int32, (TM, TM), 1)
    same_chunk = (ti // WKV_CHUNK) == (si // WKV_CHUNK)
    tri = jnp.stack([same_chunk & (si <= ti), same_chunk & (si >= ti)]).astype(BF16)

    xcat = jnp.concatenate([ctx, x], axis=1)
    out = None
    for i in range(depth):
        last = i == depth - 1
        p, u8 = _inproj(xcat, norm1_g[i][None], mod(i, 0), mod(i, 1), w_in[i].astype(BF16), nct,
                        s5_d.shape[1])
        rowmask, lanec = _shift_masks(shift_mu[i], ctx_len, l)
        v, at, rt, bg, kg, ee, g, bo = _rwkv_prep(
            p, rowmask, lanec, rwkv_k_k[i][None], rwkv_k_a[i][None], rwkv_r_k[i].reshape(1, -1),
            rwkv_w0[i], rwkv_a0[i], _pad_rows(rwkv_w_up[i]), _pad_rows(rwkv_a_up[i]),
            _pad_rows(rwkv_g_up[i]), bd, tri, nct, slab_w)
        yf, yr = _wkv_scan(v, at, rt, bg, kg, ee, nctc)
        s5w = _s5_weights(s5_lam_re[i], s5_lam_im[i], s5_log_dt[i], s5_b_re[i], s5_b_im[i],
                          s5_c_re[i], s5_c_im[i])
        ys = _s5_mix(u8, s5w, b, nctc16)
        t0 = nct if last else 0
        xm = _mixout(xcat, yf, yr, g, bo, ys, p, rwkv_ln_w[i].reshape(1, -1), rwkv_ln_b[i].reshape(1, -1),
                     bd, s5_d[i][None], s5_glu_w[i].astype(BF16), s5_glu_b[i][None],
                     w_out[i].astype(BF16), mod(i, 2), nct, t0)
        if not last:
            j = i // 2
            xcat = _ffn(xm, norm2_g[i][None], mod(i, 3), mod(i, 4), mod(i, 5),
                        ffn_w_gate[j].astype(BF16), ffn_w_up[j].astype(BF16),
                        ffn_w_down[j].astype(BF16), nct)
        else:
            j = i // 2
            ne = moe_router.shape[2]
            nr = -(-ne // 8) * 8
            router_t = jnp.zeros((nr, d), F32).at[:ne].set(moe_router[j].T)
            lat = lambda k: mods[i, :b, k][:, None, :]
            h, cmb, cnt = _route(xm, norm2_g[i][None], lat(3), lat(4), router_t, ne)
            cnt = cnt[:, :ne, 0].reshape(-1, MOE_TM // MOE_TR, ne).sum(axis=1)
            ui = lax.broadcasted_iota(jnp.int32, (MOE_SUB, MOE_SUB), 0)
            uj = lax.broadcasted_iota(jnp.int32, (MOE_SUB, MOE_SUB), 1)
            moe = _moe(h, cmb, cnt, (ui < uj).astype(BF16), moe_w_gate[j].astype(BF16),
                       moe_w_up[j].astype(BF16), moe_w_down[j].astype(BF16))
            out = _final(xm, moe, lat(5), final_g[None])
    return out
```

```python
import functools
import math

import jax
import jax.numpy as jnp
from jax import lax
from jax.experimental import pallas as pl
from jax.experimental.pallas import tpu as pltpu

F32 = jnp.float32
BF16 = jnp.bfloat16
HIGHEST = lax.Precision.HIGHEST

GRID_W = 64
HEAD = 64
DECAY_RANK = 64
ICL_RANK = 64
GATE_RANK = 128
S5_GROUP = 16
S5_STATE = 64
NORM_EPS = 1e-6
GN_EPS = 64e-5
L2_EPS = 1e-12
LAM_RE_MAX = -1e-4
TOP_K = 2

TM = 256
WKV_CHUNK = 64
WKV_HEADS = 4
WKV_STEP = 2
S5_CHUNK = 16
S5_OCT = 128
VMEM_LIMIT = 56 * 1024 * 1024


def _cparams(*sem):
    return pltpu.CompilerParams(dimension_semantics=sem, vmem_limit_bytes=VMEM_LIMIT)


def _dot(a, b):
    return jnp.dot(a, b, preferred_element_type=F32)


def _dot32(a, b):
    return jnp.dot(a, b, precision=HIGHEST, preferred_element_type=F32)


def _split2(x):
    hi = x.astype(BF16)
    return hi, (x - hi.astype(F32)).astype(BF16)


def _dot_ones(x, ones_bf):
    hi, lo = _split2(x)
    return _dot(hi, ones_bf) + _dot(lo, ones_bf)


def _dot_w2(x, w2_ref):
    hi, lo = _split2(x)
    return _dot(hi, w2_ref[0]) + _dot(lo, w2_ref[0]) + _dot(hi, w2_ref[1])


def _dot_nt(a, b):
    return lax.dot_general(a, b, (((1,), (1,)), ((), ())), preferred_element_type=F32)


def _dot_tn(a, b):
    return lax.dot_general(a, b, (((0,), (0,)), ((), ())), preferred_element_type=F32)


def _ada_kernel(act_ref, w_ref, b_ref, o_ref):
    a = act_ref[...]
    a = a * jax.nn.sigmoid(a)
    o_ref[...] = _dot32(a, w_ref[...]) + b_ref[...]


def _ada_mod(act, ada_w, ada_b):
    depth, d, n = ada_w.shape
    tn = 1536
    return pl.pallas_call(
        _ada_kernel,
        grid=(depth, n // tn),
        in_specs=[
            pl.BlockSpec((8, d), lambda i, j: (0, 0)),
            pl.BlockSpec((None, d, tn), lambda i, j: (i, 0, j)),
            pl.BlockSpec((None, 1, tn), lambda i, j: (i, 0, j)),
        ],
        out_specs=pl.BlockSpec((None, 8, tn), lambda i, j: (i, 0, j)),
        out_shape=jax.ShapeDtypeStruct((depth, 8, n), F32),
        compiler_params=_cparams("arbitrary", "arbitrary"),
        name="ada_mod",
    )(act, ada_w, ada_b.reshape(depth, 1, n))


def _norm_mod(x, g, shift, scale):
    ms = jnp.mean(x * x, axis=-1, keepdims=True)
    y = x * lax.rsqrt(ms + NORM_EPS) * g
    return y * (1.0 + scale) + shift


def _inproj_kernel(sw, x_ref, g_ref, sh_ref, sc_ref, w_ref, o_ref, u8_ref, us_ref):
    h = _norm_mod(x_ref[...], g_ref[...], sh_ref[...], sc_ref[...])
    p = _dot(h.astype(BF16), w_ref[...])
    o_ref[...] = p.astype(BF16)
    base = p.shape[1] - sw
    cpt = TM // S5_CHUNK
    for o8 in range(sw // S5_OCT):
        us_ref[o8] = p[:, base + o8 * S5_OCT:base + (o8 + 1) * S5_OCT]
    for o8 in range(sw // S5_OCT):
        for s in range(S5_CHUNK):
            u8_ref[o8, :, s * S5_OCT:(s + 1) * S5_OCT] = (
                us_ref[o8, pl.ds(s, cpt, stride=S5_CHUNK), :].astype(BF16))


def _inproj(xcat, g, shift, scale, w_bf, nct, sw):
    b, tt, d = xcat.shape
    n = w_bf.shape[1]
    noct = sw // S5_OCT
    cpt = TM // S5_CHUNK
    kw = S5_CHUNK * S5_OCT
    kind = lambda bi, i: (bi, jnp.where(i < nct, 0, 1), 0, 0)
    return pl.pallas_call(
        functools.partial(_inproj_kernel, sw),
        grid=(b, tt // TM),
        in_specs=[
            pl.BlockSpec((None, TM, d), lambda bi, i: (bi, i, 0)),
            pl.BlockSpec((1, d), lambda bi, i: (0, 0)),
            pl.BlockSpec((None, None, 1, d), kind),
            pl.BlockSpec((None, None, 1, d), kind),
            pl.BlockSpec((d, n), lambda bi, i: (0, 0)),
        ],
        out_specs=[pl.BlockSpec((None, TM, n), lambda bi, i: (bi, i, 0)),
                   pl.BlockSpec((noct, cpt, kw), lambda bi, i: (0, i, bi))],
        out_shape=[jax.ShapeDtypeStruct((b, tt, n), BF16),
                   jax.ShapeDtypeStruct((noct, tt // S5_CHUNK, b * kw), BF16)],
        scratch_shapes=[pltpu.VMEM((noct, TM, S5_OCT), F32)],
        compiler_params=_cparams("parallel", "parallel"),
        name="inproj",
    )(xcat, g, shift, scale, w_bf)


def _split3(x):
    hi = x.astype(BF16)
    r1 = x - hi.astype(F32)
    mid = r1.astype(BF16)
    lo = (r1 - mid.astype(F32)).astype(BF16)
    return hi, mid, lo


def _prep_kernel(p_ref, up_ref, dn_ref, rm_ref, lc_ref, kk_ref, ka_ref, rk_ref, w0_ref, a0_ref,
                 wup_ref, aup_ref, gup_ref, bd_ref, tri_ref,
                 v_o, at_o, rt_o, bg_o, kg_o, ee_o, g_o, bo_o):
    x = p_ref[...].astype(F32)
    rm = rm_ref[...]
    lc = lc_ref[...]
    prev = pltpu.roll(x, 1, 0)
    nxt = pltpu.roll(x, TM - 1, 0)
    up = jnp.concatenate([up_ref[...].astype(F32), x[: TM - GRID_W]], axis=0)
    dn = jnp.concatenate([x[GRID_W:], dn_ref[...].astype(F32)], axis=0)
    slab = (x * lc[4:5]
            + rm[:, 0:1] * (prev * lc[0:1])
            + rm[:, 1:2] * (nxt * lc[1:2])
            + rm[:, 2:3] * (up * lc[2:3])
            + rm[:, 3:4] * (dn * lc[3:4]))
    w = kk_ref.shape[1]
    r = slab[:, 0:w]
    k = slab[:, w:2 * w]
    v = slab[:, 2 * w:3 * w]
    o = 3 * w
    wd = slab[:, o:o + 2 * DECAY_RANK]
    ad = slab[:, o + 2 * DECAY_RANK:o + 2 * DECAY_RANK + 2 * ICL_RANK]
    gd = slab[:, o + 2 * DECAY_RANK + 2 * ICL_RANK:]
    bd = bd_ref[...]
    kk = k * kk_ref[...]
    nrm = jnp.sqrt(_dot_ones(kk * kk, bd))
    kk = kk / jnp.maximum(nrm, L2_EPS)
    v_o[...] = v.astype(BF16)
    twd = jnp.tanh(wd)
    sgd = jax.nn.sigmoid(gd)
    c = WKV_CHUNK
    for d in range(2):
        z = w0_ref[d:d + 1, :] + _dot_w2(twd, wup_ref.at[d])
        w_log = -jax.nn.softplus(-z) - 0.5
        lw = -jnp.exp(w_log)
        a = jax.nn.sigmoid(a0_ref[d:d + 1, :] + _dot_w2(ad, aup_ref.at[d]))
        kt = k * (1.0 + (a - 1.0) * ka_ref[...])
        g_o[d] = _dot_w2(sgd, gup_ref.at[d]).astype(BF16)
        bo_o[d] = (_dot_ones(r * kt * rk_ref[...], bd) * v).astype(BF16)
        tri = tri_ref[d]
        hi, mid, lo = _split3(lw)
        lg_in = _dot(tri, hi) + _dot(tri, mid) + _dot(tri, lo)
        e_neg = jnp.exp(-lg_in)
        at_o[d] = (-kk * jnp.exp(lg_in - lw)).astype(BF16)
        rt_o[d] = (r * jnp.exp(lg_in)).astype(BF16)
        bg_o[d] = (kk * a * e_neg).astype(BF16)
        kg_o[d] = (kt * e_neg).astype(BF16)
        for ci in range(TM // c):
            last = ci * c + (c - 1 if d == 0 else 0)
            ee_o[d, ci] = jnp.exp(lg_in[last:last + 1, :])


def _rwkv_prep(p, rowmask, lanec, k_k, k_a, r_k, w0, a0, wup, aup, gup, bd, tri, nct, slab_w):
    b, tt, _ = p.shape
    w = k_k.shape[1]
    nt = tt // TM
    cpt = TM // WKV_CHUNK
    hb = TM // GRID_W
    nhb = tt // GRID_W
    full = lambda *s: pl.BlockSpec(s, lambda bi, i: (0,) * len(s))
    tok = pl.BlockSpec((None, TM, w), lambda bi, i: (bi, i, 0))
    tok2 = pl.BlockSpec((2, None, TM, w), lambda bi, i: (0, bi, i, 0))
    bf1 = jax.ShapeDtypeStruct((b, tt, w), BF16)
    bf2 = jax.ShapeDtypeStruct((2, b, tt, w), BF16)
    sh2 = jax.ShapeDtypeStruct((2, b, tt, w), F32)
    return pl.pallas_call(
        _prep_kernel,
        grid=(b, nt),
        in_specs=[
            pl.BlockSpec((None, TM, slab_w), lambda bi, i: (bi, i, 0)),
            pl.BlockSpec((None, GRID_W, slab_w), lambda bi, i: (bi, jnp.maximum(i * hb - 1, 0), 0)),
            pl.BlockSpec((None, GRID_W, slab_w),
                         lambda bi, i: (bi, jnp.minimum(i * hb + hb, nhb - 1), 0)),
            pl.BlockSpec((None, TM, 8), lambda bi, i: (i, 0, 0)),
            pl.BlockSpec((None, 8, slab_w), lambda bi, i: (jnp.where(i < nct, 0, 1), 0, 0)),
            full(1, w), full(1, w), full(1, w), full(2, w), full(2, w),
            full(2, 2, 2 * DECAY_RANK, w), full(2, 2, 2 * ICL_RANK, w), full(2, 2, 2 * GATE_RANK, w),
            full(w, w), full(2, TM, TM),
        ],
        out_specs=[tok, tok2, tok2, tok2, tok2,
                   pl.BlockSpec((2, None, cpt, 1, w), lambda bi, i: (0, bi, i, 0, 0)),
                   tok2, tok2],
        out_shape=[bf1, bf2, bf2, bf2, bf2,
                   jax.ShapeDtypeStruct((2, b, tt // WKV_CHUNK, 1, w), F32), bf2, bf2],
        compiler_params=_cparams("parallel", "parallel"),
        name="rwkv_prep",
    )(p, p, p, rowmask, lanec, k_k, k_a, r_k, w0, a0, wup, aup, gup, bd, tri)


def _wkv_kernel(nb, ngrp, v_f, v_r, at_f, at_r, rt_f, rt_r, bg_f, bg_r, kg_f, kg_r, ee_f, ee_r,
                y_f, y_r, ht_ref):
    j = pl.program_id(0)
    c = WKV_CHUNK
    gw = WKV_HEADS * HEAD
    gn = WKV_HEADS * c

    @pl.when(j == 0)
    def _():
        ht_ref[...] = jnp.zeros_like(ht_ref)

    sh = int(math.log2(c))
    row = lax.broadcasted_iota(jnp.int32, (gn, gw), 0)
    col = lax.broadcasted_iota(jnp.int32, (gn, gw), 1)
    same = (row >> sh) == (col >> sh)
    tf = lax.broadcasted_iota(jnp.int32, (c, gn), 0)
    sf = lax.broadcasted_iota(jnp.int32, (c, gn), 1) & (c - 1)
    eye = (tf == sf).astype(F32)

    def stack(x):
        xb = jnp.concatenate([x.astype(BF16)] * WKV_HEADS, axis=0)
        return jnp.where(same, xb, jnp.zeros_like(xb))

    dirs = ((v_f, at_f, rt_f, bg_f, kg_f, ee_f, y_f, sf < tf, sf <= tf),
            (v_r, at_r, rt_r, bg_r, kg_r, ee_r, y_r, sf > tf, sf >= tf))
    nsc = v_f.shape[1] // c
    chains = [(d, bi, q, sc) for d in range(2) for bi in range(nb) for q in range(ngrp)
              for sc in range(nsc)]
    sl = lambda q: slice(q * gw, (q + 1) * gw)
    rows = lambda sc: slice(sc * c, (sc + 1) * c)
    rd = lambda k: [dirs[d][k][bi, rows(sc), sl(q)] for d, bi, q, sc in chains]
    cat0 = lambda xs: jnp.concatenate(xs, axis=0)
    v, at, rt, bg, kg = rd(0), rd(1), rd(2), rd(3), rd(4)
    ee = [dirs[d][5][bi, sc, :, sl(q)] for d, bi, q, sc in chains]
    before = [dirs[ch[0]][7] for ch in chains]
    incl = [dirs[ch[0]][8] for ch in chains]
    n_ch = range(len(chains))

    v_bd = [stack(x) for x in v]
    at_bd = [stack(x) for x in at]
    bk_bd = [cat0([stack(bg[i]), stack(kg[i])]) for i in n_ch]
    a = [_dot_nt(cat0([at[i], rt[i]]), bk_bd[i]) for i in n_ch]
    n = [jnp.where(before[i], a[i][0:c, 0:gn], 0.0) for i in n_ch]
    a_kk = [cat0([jnp.where(before[i], a[i][0:c, gn:], 0.0),
                  jnp.where(incl[i], a[i][c:, gn:], 0.0)]).astype(BF16) for i in n_ch]
    a_rb = [jnp.where(incl[i], a[i][c:, 0:gn], 0.0).astype(BF16) for i in n_ch]
    tm = [eye + x for x in n]
    pw = [_dot(x.astype(BF16), stack(x)) for x in n]
    for lvl in range(1, sh):
        pw_bd = [stack(x) for x in pw]
        if lvl < sh - 1:
            tp = [_dot(cat0([tm[i].astype(BF16), pw[i].astype(BF16)]), pw_bd[i]) for i in n_ch]
            tm = [tm[i] + tp[i][0:c] for i in n_ch]
            pw = [tp[i][c:] for i in n_ch]
        else:
            tm = [tm[i] + _dot(tm[i].astype(BF16), pw_bd[i]) for i in n_ch]
    tm_b = [x.astype(BF16) for x in tm]
    atp = [_dot(tm_b[i], at_bd[i]) for i in n_ch]
    av = [_dot(a_kk[i], v_bd[i]) for i in n_ch]
    wv = [_dot(tm_b[i], stack(av[i][0:c])) for i in n_ch]
    wv_bd = [stack(x) for x in wv]
    atp_bd = [stack(x) for x in atp]
    ar = [_dot(a_rb[i], jnp.concatenate([wv_bd[i], atp_bd[i]], axis=1)) for i in n_ch]
    y0 = [ar[i][:, 0:gw] + av[i][c:] for i in n_ch]
    rtp = [(ar[i][:, gw:] + rt[i].astype(F32)).astype(BF16) for i in n_ch]
    bge_bd = [stack(bg[i].astype(F32) * ee[i]) for i in n_ch]
    kge_bd = [stack(kg[i].astype(F32) * ee[i]) for i in n_ch]
    g = [_dot_tn(bge_bd[i], atp_bd[i]).astype(BF16) for i in n_ch]
    hloc_t = [_dot_tn(cat0([wv_bd[i], v_bd[i]]), cat0([bge_bd[i], kge_bd[i]])) for i in n_ch]
    index = {ch: i for i, ch in enumerate(chains)}
    for d in range(2):
        order = range(nsc) if d == 0 else range(nsc - 1, -1, -1)
        for bi in range(nb):
            for q in range(ngrp):
                ht = ht_ref[d, bi, q]
                for sc in order:
                    i = index[(d, bi, q, sc)]
                    ht_b = ht.astype(BF16)
                    dirs[d][6][bi, rows(sc), sl(q)] = y0[i] + _dot_nt(rtp[i], ht_b)
                    ht = ht * ee[i] + _dot_nt(ht_b, g[i]) + hloc_t[i]
                ht_ref[d, bi, q] = ht


def _wkv_scan(v, at, rt, bg, kg, ee, nctc):
    b, tt, w = v.shape
    rows = WKV_STEP * WKV_CHUNK
    assert nctc % WKV_STEP == 0 and (tt // WKV_CHUNK) % WKV_STEP == 0
    ntot = tt // rows
    nctb = nctc // WKV_STEP
    ngrp = w // (WKV_HEADS * HEAD)
    fwd = lambda j: j
    rev = lambda j: jnp.where(j < nctb, nctb - 1 - j, ntot - 1 + nctb - j)
    tok = lambda cm: pl.BlockSpec((b, rows, w), lambda j: (0, cm(j), 0))
    tok2 = lambda d, cm: pl.BlockSpec((None, b, rows, w), lambda j: (d, 0, cm(j), 0))
    eesp = lambda d, cm: pl.BlockSpec((None, b, WKV_STEP, 1, w), lambda j: (d, 0, cm(j), 0, 0))
    pair = lambda f: [f(0, fwd), f(1, rev)]
    ysh = jax.ShapeDtypeStruct((b, tt, w), F32)
    return pl.pallas_call(
        functools.partial(_wkv_kernel, b, ngrp),
        grid=(ntot,),
        in_specs=[tok(fwd), tok(rev)] + pair(tok2) + pair(tok2) + pair(tok2) + pair(tok2) + pair(eesp),
        out_specs=[tok(fwd), tok(rev)],
        out_shape=[ysh, ysh],
        scratch_shapes=[pltpu.VMEM((2, b, ngrp, WKV_HEADS * HEAD, WKV_HEADS * HEAD), F32)],
        compiler_params=_cparams("arbitrary"),
        name="wkv_scan",
    )(v, v, at, at, rt, rt, bg, bg, kg, kg, ee, ee)


def _s5_weights(lam_re, lam_im, log_dt, b_re, b_im, c_re, c_im):
    tc = S5_CHUNK
    lr = jnp.minimum(lam_re.astype(F32), LAM_RE_MAX)
    li = lam_im.astype(F32)
    dt = jnp.exp(log_dt.astype(F32))[..., None]
    mag = jnp.exp(lr * dt)
    ar = mag * jnp.cos(li * dt)
    ai = mag * jnp.sin(li * dt)
    den = lr * lr + li * li
    xr = ar - 1.0
    cr = (xr * lr + ai * li) / den
    ci = (ai * lr - xr * li) / den
    br = cr[..., None] * b_re - ci[..., None] * b_im
    bi = cr[..., None] * b_im + ci[..., None] * b_re
    pr, pi = [jnp.ones_like(ar)], [jnp.zeros_like(ar)]
    for _ in range(tc):
        pr_n = pr[-1] * ar - pi[-1] * ai
        pi_n = pr[-1] * ai + pi[-1] * ar
        pr.append(pr_n)
        pi.append(pi_n)
    pr = jnp.stack(pr)
    pi = jnp.stack(pi)
    lbr = pr[..., None] * br - pi[..., None] * bi
    lbi = pr[..., None] * bi + pi[..., None] * br
    clr = c_re * pr[:, :, :, None, :] - c_im * pi[:, :, :, None, :]
    cli = c_re * pi[:, :, :, None, :] + c_im * pr[:, :, :, None, :]
    lbr_t = jnp.swapaxes(lbr, -1, -2)
    lbi_t = jnp.swapaxes(lbi, -1, -2)
    kern_t = jnp.sum(lbr_t[..., :, None, :] * c_re[None, :, :, None, :, :]
                     - lbi_t[..., :, None, :] * c_im[None, :, :, None, :, :], axis=-1)
    g = ar.shape[1]
    og = S5_OCT // S5_GROUP
    noct = g // og
    eye = jnp.eye(og, dtype=F32)

    def bdiag(x):
        nt, _, _, a, n = x.shape
        x = x.reshape(nt, 2, noct, og, a, n)
        y = x[:, :, :, :, :, None, :] * eye[None, None, None, :, None, :, None]
        return jnp.transpose(y, (1, 2, 0, 3, 4, 5, 6)).reshape(2, noct, nt, og * a, og * n).astype(BF16)

    kbd = bdiag(kern_t)

    def pair_block(d, lp):
        kd = kbd[d]
        zero = jnp.zeros_like(kd[:, 0])
        k = lambda tau: kd[:, tau] if tau >= 0 else zero
        if d == 0:
            rows = [[k(2 * lp), k(2 * lp + 1)], [k(2 * lp - 1), k(2 * lp)]]
        else:
            rows = [[k(2 * lp), k(2 * lp - 1)], [k(2 * lp + 1), k(2 * lp)]]
        return jnp.concatenate([jnp.concatenate(r, axis=-1) for r in rows], axis=-2)

    wpair = jnp.stack([jnp.stack([pair_block(d, lp) for lp in range(tc // 2)], axis=1)
                       for d in range(2)])
    lbc = jnp.concatenate([lbr_t, lbi_t], axis=-1)
    clc = jnp.concatenate([clr, -cli], axis=-1)
    pout8, qin8 = _s5_expand(lbc, clc)
    la = jnp.concatenate([pr[tc], pr[tc]], axis=-1)
    lb = jnp.concatenate([-pi[tc], pi[tc]], axis=-1)
    return wpair, pout8, qin8, la, lb


def _s5_expand_kernel(p_ref, q_ref, po_ref, qo_ref):
    nt, _, og, a, n = p_ref.shape
    tc = nt - 1
    po_ref[...] = jnp.zeros_like(po_ref)
    qo_ref[...] = jnp.zeros_like(qo_ref)
    for d in range(2):
        for s in range(tc):
            lag_out = tc - 1 - s if d == 0 else s
            lag_in = s + 1 if d == 0 else tc - s
            for gi in range(og):
                rows, lanes = slice(gi * a, (gi + 1) * a), slice(gi * n, (gi + 1) * n)
                po_ref[d, s, rows, lanes] = p_ref[lag_out, d, gi].astype(BF16)
                qo_ref[d, s, rows, lanes] = q_ref[lag_in, d, gi].astype(BF16)


def _s5_expand(lbc, clc):
    nt, _, g, a, n = lbc.shape
    tc = nt - 1
    og = S5_OCT // S5_GROUP
    noct = g // og
    isp = pl.BlockSpec((nt, 2, og, a, n), lambda o: (0, 0, o, 0, 0))
    osp = pl.BlockSpec((2, None, tc, og * a, og * n), lambda o: (0, o, 0, 0, 0))
    osh = jax.ShapeDtypeStruct((2, noct, tc, og * a, og * n), BF16)
    po, qo = pl.pallas_call(
        _s5_expand_kernel,
        grid=(noct,),
        in_specs=[isp, isp],
        out_specs=[osp, osp],
        out_shape=[osh, osh],
        compiler_params=_cparams("parallel"),
        name="s5_expand",
    )(lbc, clc)
    return po.reshape(2, noct, tc * og * a, og * n), qo.reshape(2, noct, tc * og * a, og * n)


def _s5_local_kernel(u_ref, p_ref, e_ref):
    e_ref[...] = _dot(u_ref[...], p_ref[...])


def _s5_local(u8, pout8, nb):
    noct, nch, _ = u8.shape
    kw, n = pout8.shape[2:]
    return pl.pallas_call(
        _s5_local_kernel,
        grid=(noct, nb, 2),
        in_specs=[
            pl.BlockSpec((None, nch, kw), lambda o, b, d: (o, 0, b)),
            pl.BlockSpec((None, None, kw, n), lambda o, b, d: (d, o, 0, 0)),
        ],
        out_specs=pl.BlockSpec((None, nch, n), lambda o, b, d: (d, 0, o * nb + b)),
        out_shape=jax.ShapeDtypeStruct((2, nch, noct * nb * n), F32),
        compiler_params=_cparams("parallel", "parallel", "parallel"),
        name="s5_local",
    )(u8, pout8)


S5_STATE_ROWS = 8
S5_STATE_LANES = 256


def _s5_state_kernel(nctc, ntot, e_ref, la_ref, lb_ref, x_ref, es_ref):
    d = pl.program_id(0)
    la = la_ref[...]
    lb = lb_ref[...]
    nr, wl = la.shape

    def swap(t):
        lane = lax.broadcasted_iota(jnp.int32, t.shape, 1)
        first_half = (lane & (2 * S5_STATE - 1)) < S5_STATE
        return jnp.where(first_half, pltpu.roll(t, wl - S5_STATE, 1), pltpu.roll(t, S5_STATE, 1))

    es_ref[...] = swap(e_ref[...].reshape(ntot * nr, wl)).reshape(ntot, nr, wl)
    lbs = swap(lb)

    def body(j, carry):
        x, xs = carry
        rev_idx = jnp.where(j < nctc, nctc - 1 - j, ntot - 1 + nctc - j)
        c = jnp.where(d == 0, j, rev_idx)
        x_ref[c] = x
        return la * x + lb * xs + e_ref[c], la * xs + lbs * x + es_ref[c]

    zero = jnp.zeros(la.shape, F32)
    lax.fori_loop(0, ntot, body, (zero, zero))


def _s5_state(e, la, lb, nctc, ntot):
    _, nch, nr, lanes = e.shape
    wl = S5_STATE_LANES
    blk = pl.BlockSpec((None, nch, nr, wl), lambda d, i: (d, 0, 0, i))
    cf = pl.BlockSpec((None, nr, wl), lambda d, i: (d, 0, i))
    return pl.pallas_call(
        functools.partial(_s5_state_kernel, nctc, ntot),
        grid=(2, lanes // wl),
        in_specs=[blk, cf, cf],
        out_specs=blk,
        out_shape=jax.ShapeDtypeStruct(e.shape, F32),
        scratch_shapes=[pltpu.VMEM((nch, nr, wl), F32)],
        compiler_params=_cparams("parallel", "parallel"),
        name="s5_state",
    )(e, la, lb)


def _s5_out_kernel(rev, u_ref, w_ref, x_ref, q_ref, y_ref):
    pw = w_ref.shape[1]
    npair = w_ref.shape[0]
    x = x_ref[...].astype(BF16)
    for tp in range(npair):
        acc = _dot_nt(x, q_ref[tp * pw:(tp + 1) * pw, :])
        for lp in range(npair - tp if rev else tp + 1):
            sp = tp + lp if rev else tp - lp
            acc = acc + _dot(u_ref[:, sp * pw:(sp + 1) * pw], w_ref[lp])
        y_ref[:, tp * pw:(tp + 1) * pw] = acc


def _s5_out(u8, wpair, xin, qin8, nb, d):
    noct, nch, _ = u8.shape
    npair, pw = wpair.shape[2:4]
    kw = npair * pw
    n2 = qin8.shape[3]
    return pl.pallas_call(
        functools.partial(_s5_out_kernel, d == 1),
        grid=(noct, nb),
        in_specs=[
            pl.BlockSpec((None, nch, kw), lambda o, b: (o, 0, b)),
            pl.BlockSpec((None, None, npair, pw, pw), lambda o, b: (d, o, 0, 0, 0)),
            pl.BlockSpec((None, nch, n2), lambda o, b: (d, 0, o * nb + b)),
            pl.BlockSpec((None, None, kw, n2), lambda o, b: (d, o, 0, 0)),
        ],
        out_specs=pl.BlockSpec((None, nch, kw), lambda o, b: (o, 0, b)),
        out_shape=jax.ShapeDtypeStruct((noct, nch, nb * kw), F32),
        compiler_params=_cparams("parallel", "parallel"),
        name="s5_out",
    )(u8, wpair, xin, qin8)


def _s5_mix(u8, weights, nb, nctc16):
    wpair, pout8, qin8, la, lb = weights
    noct, nch, _ = u8.shape
    e = _s5_local(u8, pout8, nb)
    assert noct * nb == S5_STATE_ROWS
    fl = e.shape[2] // S5_STATE_ROWS
    coef = lambda t: jnp.repeat(t.reshape(2, noct, fl), nb, axis=1)
    xin = _s5_state(e.reshape(2, nch, S5_STATE_ROWS, fl), coef(la), coef(lb), nctc16, nch)
    xin = xin.reshape(e.shape)
    return _s5_out(u8, wpair, xin, qin8, nb, 0), _s5_out(u8, wpair, xin, qin8, nb, 1)


def _mixout_kernel(x_ref, yf_ref, yr_ref, g_ref, bo_ref, y8f_ref, y8r_ref, u_ref, lnw_ref, lnb_ref,
                   bd_ref, dsk_ref, gluw_ref, glub_ref, wout_ref, gate_ref, o_ref, ysn_ref):
    cpt = TM // S5_CHUNK
    for o8 in range(y8f_ref.shape[0]):
        for s in range(S5_CHUNK):
            lanes = slice(s * S5_OCT, (s + 1) * S5_OCT)
            ysn_ref[o8, pl.ds(s, cpt, stride=S5_CHUNK), :] = y8f_ref[o8, :, lanes] + y8r_ref[o8, :, lanes]
    ys = jnp.concatenate([ysn_ref[o8] for o8 in range(y8f_ref.shape[0])], axis=1)
    bd = bd_ref[...]
    inv = 1.0 / HEAD
    rw = None
    for d, y_ref in enumerate((yf_ref, yr_ref)):
        y = y_ref[...]
        mean = _dot_ones(y, bd) * inv
        yc = y - mean
        var = _dot_ones(yc * yc, bd) * inv
        yn = yc * lax.rsqrt(var + GN_EPS) * lnw_ref[...] + lnb_ref[...]
        o = (yn + bo_ref[d].astype(F32)) * g_ref[d].astype(F32)
        rw = o if rw is None else rw + o
    u = u_ref[...].astype(F32)
    ss = ys + dsk_ref[...] * u
    ss = jax.nn.gelu(ss)
    ss = ss * jax.nn.sigmoid(_dot(ss.astype(BF16), gluw_ref[...]) + glub_ref[...])
    w = rw.shape[1]
    mix = _dot(rw.astype(BF16), wout_ref[0:w, :]) + _dot(ss.astype(BF16), wout_ref[w:, :])
    o_ref[...] = x_ref[...] + gate_ref[...] * mix


def _mixout(xcat, yf, yr, g, bo, y8, p, ln_w, ln_b, bd, d_skip, glu_w, glu_b, w_out, gate, nct, t0):
    b, tt, d = xcat.shape
    w = ln_w.shape[1]
    sw = d_skip.shape[1]
    y8f, y8r = y8
    noct = y8f.shape[0]
    cpt = TM // S5_CHUNK
    kw = S5_CHUNK * S5_OCT
    ublk = (p.shape[2] - sw) // sw
    nt = tt // TM - t0
    full = lambda *s: pl.BlockSpec(s, lambda bi, i: (0,) * len(s))
    tok = pl.BlockSpec((None, TM, w), lambda bi, i: (bi, i + t0, 0))
    tok2 = pl.BlockSpec((2, None, TM, w), lambda bi, i: (0, bi, i + t0, 0))
    return pl.pallas_call(
        _mixout_kernel,
        grid=(b, nt),
        in_specs=[
            pl.BlockSpec((None, TM, d), lambda bi, i: (bi, i + t0, 0)),
            tok, tok, tok2, tok2,
            pl.BlockSpec((noct, cpt, kw), lambda bi, i: (0, i + t0, bi)),
            pl.BlockSpec((noct, cpt, kw), lambda bi, i: (0, i + t0, bi)),
            pl.BlockSpec((None, TM, sw), lambda bi, i: (bi, i + t0, ublk)),
            full(1, w), full(1, w), full(w, w), full(1, sw), full(sw, sw), full(1, sw),
            full(w + sw, d),
            pl.BlockSpec((None, None, 1, d), lambda bi, i: (bi, jnp.where(i + t0 < nct, 0, 1), 0, 0)),
        ],
        out_specs=pl.BlockSpec((None, TM, d), lambda bi, i: (bi, i, 0)),
        out_shape=jax.ShapeDtypeStruct((b, nt * TM, d), F32),
        scratch_shapes=[pltpu.VMEM((noct, TM, S5_OCT), F32)],
        compiler_params=_cparams("parallel", "parallel"),
        name="mix_out",
    )(xcat, yf, yr, g, bo, y8f, y8r, p, ln_w, ln_b, bd, d_skip, glu_w, glu_b, w_out, gate)


def _ffn_kernel(x_ref, g_ref, sh_ref, sc_ref, gate_ref, wg_ref, wu_ref, wd_ref, o_ref):
    x = x_ref[...]
    h = _norm_mod(x, g_ref[...], sh_ref[...], sc_ref[...]).astype(BF16)
    a = _dot(h, wg_ref[...])
    a = a * jax.nn.sigmoid(a) * _dot(h, wu_ref[...])
    o_ref[...] = x + gate_ref[...] * _dot(a.astype(BF16), wd_ref[...])


def _ffn(xcat, g, shift, scale, gate, wg, wu, wd, nct):
    b, tt, d = xcat.shape
    ff = wg.shape[1]
    kind = lambda bi, i: (bi, jnp.where(i < nct, 0, 1), 0, 0)
    mod = pl.BlockSpec((None, None, 1, d), kind)
    return pl.pallas_call(
        _ffn_kernel,
        grid=(b, tt // TM),
        in_specs=[
            pl.BlockSpec((None, TM, d), lambda bi, i: (bi, i, 0)),
            pl.BlockSpec((1, d), lambda bi, i: (0, 0)),
            mod, mod, mod,
            pl.BlockSpec((d, ff), lambda bi, i: (0, 0)),
            pl.BlockSpec((d, ff), lambda bi, i: (0, 0)),
            pl.BlockSpec((ff, d), lambda bi, i: (0, 0)),
        ],
        out_specs=pl.BlockSpec((None, TM, d), lambda bi, i: (bi, i, 0)),
        out_shape=jax.ShapeDtypeStruct((b, tt, d), F32),
        compiler_params=_cparams("parallel", "parallel"),
        name="ffn",
    )(xcat, g, shift, scale, gate, wg, wu, wd)


MOE_TR = 1024
MOE_TM = 2048
MOE_TF = 896
MOE_BLK = 256
MOE_SUB = 256


def _route_kernel(ne, x_ref, g_ref, sh_ref, sc_ref, rt_ref, h_o, cmb_o, cnt_o):
    h = _norm_mod(x_ref[...], g_ref[...], sh_ref[...], sc_ref[...])
    h_o[...] = h.astype(BF16)
    logits = lax.dot_general(rt_ref[...], h, (((1,), (1,)), ((), ())), precision=HIGHEST,
                             preferred_element_type=F32)
    sub = lax.broadcasted_iota(jnp.int32, logits.shape, 0).astype(F32)
    none = float(logits.shape[0])
    logits = jnp.where(sub < ne, logits, -jnp.inf)
    m1 = jnp.max(logits, axis=0, keepdims=True)
    i1 = jnp.min(jnp.where(logits == m1, sub, none), axis=0, keepdims=True)
    rest = jnp.where(sub == i1, -jnp.inf, logits)
    m2 = jnp.max(rest, axis=0, keepdims=True)
    i2 = jnp.min(jnp.where(rest == m2, sub, none), axis=0, keepdims=True)
    e2 = jnp.exp(m2 - m1)
    p1 = 1.0 / (1.0 + e2)
    p2 = e2 / (1.0 + e2)
    cmb = jnp.where(sub == i1, p1, 0.0) + jnp.where(sub == i2, p2, 0.0)
    cmb_o[...] = cmb
    cnt = jnp.sum((cmb > 0.0).astype(F32), axis=1, keepdims=True)
    cnt_o[...] = jnp.broadcast_to(cnt, cnt_o.shape).astype(jnp.int32)


def _route(x, g, shift, scale, router_t, ne):
    b, l, d = x.shape
    nr = router_t.shape[0]
    nt = l // MOE_TR
    mod = pl.BlockSpec((None, 1, d), lambda bi, i: (bi, 0, 0))
    return pl.pallas_call(
        functools.partial(_route_kernel, ne),
        grid=(b, nt),
        in_specs=[
            pl.BlockSpec((None, MOE_TR, d), lambda bi, i: (bi, i, 0)),
            pl.BlockSpec((1, d), lambda bi, i: (0, 0)),
            mod, mod,
            pl.BlockSpec((nr, d), lambda bi, i: (0, 0)),
        ],
        out_specs=[
            pl.BlockSpec((MOE_TR, d), lambda bi, i: (bi * nt + i, 0)),
            pl.BlockSpec((nr, MOE_TR), lambda bi, i: (0, bi * nt + i)),
            pl.BlockSpec((None, nr, 128), lambda bi, i: (bi * nt + i, 0, 0)),
        ],
        out_shape=[
            jax.ShapeDtypeStruct((b * l, d), BF16),
            jax.ShapeDtypeStruct((nr, b * l), F32),
            jax.ShapeDtypeStruct((b * nt, nr, 128), jnp.int32),
        ],
        compiler_params=_cparams("parallel", "parallel"),
        name="moe_route",
    )(x, g, shift, scale, router_t)


def _moe_kernel(cnt_ref, h_ref, cmb_ref, tri_ref, wg_ref, wu_ref, wd_ref, o_ref,
                pos_ref, hg_ref, ya_ref):
    t = pl.program_id(0)
    e = pl.program_id(1)
    j = pl.program_id(2)
    tm = h_ref.shape[0]
    nblk = jnp.right_shift(cnt_ref[t, e] + (MOE_BLK - 1), int(math.log2(MOE_BLK)))

    @pl.when((e == 0) & (j == 0))
    def _():
        o_ref[...] = jnp.zeros_like(o_ref)
        asg = (cmb_ref[...] > 0.0).astype(BF16)
        off = jnp.zeros((asg.shape[0], 1), F32)
        for k in range(tm // MOE_SUB):
            blk = asg[:, k * MOE_SUB:(k + 1) * MOE_SUB]
            pos_ref[:, k * MOE_SUB:(k + 1) * MOE_SUB] = _dot(blk, tri_ref[...]) + off
            off = off + jnp.sum(blk.astype(F32), axis=1, keepdims=True)

    sel = lax.broadcasted_iota(jnp.int32, pos_ref.shape, 0) == e
    posrow = jnp.sum(jnp.where(sel, pos_ref[...], 0.0), axis=0, keepdims=True)
    cwrow = jnp.sum(jnp.where(sel, cmb_ref[...], 0.0), axis=0, keepdims=True)
    rowi = lax.broadcasted_iota(jnp.int32, (MOE_BLK, tm), 0).astype(F32)

    def onehot(b):
        slot = rowi + (b * MOE_BLK).astype(F32)
        return (posrow == slot) & (cwrow > 0.0)

    @pl.when(j == 0)
    def _():
        def gather(b, carry):
            sel_b = jnp.where(onehot(b), 1.0, 0.0).astype(BF16)
            hg_ref[b] = _dot(sel_b, h_ref[...]).astype(BF16)
            ya_ref[b] = jnp.zeros(ya_ref.shape[1:], F32)
            return carry
        lax.fori_loop(0, nblk, gather, 0)

    d_model = hg_ref.shape[2]

    def swiglu(hb):
        a = _dot(hb, wg_ref[...])
        a = a * jax.nn.sigmoid(a) * _dot(hb, wu_ref[...])
        return _dot(a.astype(BF16), wd_ref[...])

    def ffn_pair(i, carry):
        rows = pl.ds(2 * i, 2)
        y = swiglu(hg_ref[rows].reshape(2 * MOE_BLK, d_model))
        ya_ref[rows] += y.reshape(2, MOE_BLK, d_model)
        return carry
    npair = jnp.right_shift(nblk, 1)
    lax.fori_loop(0, npair, ffn_pair, 0)

    @pl.when(nblk > 2 * npair)
    def _():
        ya_ref[nblk - 1] += swiglu(hg_ref[nblk - 1])

    @pl.when(j == pl.num_programs(2) - 1)
    def _():
        def scatter(b, carry):
            wsel = jnp.where(onehot(b), cwrow, 0.0).astype(BF16)
            o_ref[...] += _dot_tn(wsel, ya_ref[b].astype(BF16))
            return carry
        lax.fori_loop(0, nblk, scatter, 0)


def _moe(h, cmb, cnt, tri, wg, wu, wd):
    n, d = h.shape
    nr = cmb.shape[0]
    ne, _, ff = wg.shape
    nbmax = MOE_TM // MOE_BLK
    grid_spec = pltpu.PrefetchScalarGridSpec(
        num_scalar_prefetch=1,
        grid=(n // MOE_TM, ne, ff // MOE_TF),
        in_specs=[
            pl.BlockSpec((MOE_TM, d), lambda t, e, j, c: (t, 0)),
            pl.BlockSpec((nr, MOE_TM), lambda t, e, j, c: (0, t)),
            pl.BlockSpec((MOE_SUB, MOE_SUB), lambda t, e, j, c: (0, 0)),
            pl.BlockSpec((None, d, MOE_TF), lambda t, e, j, c: (e, 0, j)),
            pl.BlockSpec((None, d, MOE_TF), lambda t, e, j, c: (e, 0, j)),
            pl.BlockSpec((None, MOE_TF, d), lambda t, e, j, c: (e, j, 0)),
        ],
        out_specs=pl.BlockSpec((MOE_TM, d), lambda t, e, j, c: (t, 0)),
        scratch_shapes=[
            pltpu.VMEM((nr, MOE_TM), F32),
            pltpu.VMEM((nbmax, MOE_BLK, d), BF16),
            pltpu.VMEM((nbmax, MOE_BLK, d), F32),
        ],
    )
    return pl.pallas_call(
        _moe_kernel,
        grid_spec=grid_spec,
        out_shape=jax.ShapeDtypeStruct((n, d), F32),
        compiler_params=_cparams("parallel", "arbitrary", "arbitrary"),
        name="moe",
    )(cnt, h, cmb, tri, wg, wu, wd)


def _final_kernel(x_ref, m_ref, gate_ref, fg_ref, o_ref):
    y = x_ref[...] + gate_ref[...] * m_ref[...]
    ms = jnp.mean(y * y, axis=-1, keepdims=True)
    o_ref[...] = y * lax.rsqrt(ms + NORM_EPS) * fg_ref[...]


def _final(x, m, gate, final_g):
    b, l, d = x.shape
    nt = l // MOE_TR
    return pl.pallas_call(
        _final_kernel,
        grid=(b, nt),
        in_specs=[
            pl.BlockSpec((None, MOE_TR, d), lambda bi, i: (bi, i, 0)),
            pl.BlockSpec((MOE_TR, d), lambda bi, i: (bi * nt + i, 0)),
            pl.BlockSpec((None, 1, d), lambda bi, i: (bi, 0, 0)),
            pl.BlockSpec((1, d), lambda bi, i: (0, 0)),
        ],
        out_specs=pl.BlockSpec((None, MOE_TR, d), lambda bi, i: (bi, i, 0)),
        out_shape=jax.ShapeDtypeStruct((b, l, d), F32),
        compiler_params=_cparams("parallel", "parallel"),
        name="moe_final",
    )(x, m, gate, final_g)


def _shift_masks(mu, ctx_len, seq_len):
    slab = mu.shape[0]
    nct = ctx_len // TM
    tt = ctx_len + seq_len
    t = jnp.arange(tt)
    is_ctx = t < ctx_len
    tl = t - ctx_len
    col = tl % GRID_W
    rows = seq_len // GRID_W
    grow = tl // GRID_W
    left = jnp.where(is_ctx, t != 0, col != 0)
    right = jnp.where(is_ctx, t != ctx_len - 1, col != GRID_W - 1)
    upv = jnp.where(is_ctx, False, grow != 0)
    dnv = jnp.where(is_ctx, False, grow != rows - 1)
    zero = jnp.zeros_like(left)
    rowmask = jnp.stack([left, right, upv, dnv, zero, zero, zero, zero], axis=-1).astype(F32)
    rowmask = rowmask.reshape(tt // TM, TM, 8)
    c = jnp.arange(slab)
    z = jnp.zeros_like(mu)
    lat = jnp.stack([mu * (c % 4 == 0), mu * (c % 4 == 1), mu * (c % 4 == 2), mu * (c % 4 == 3),
                     1.0 - mu, z, z, z])
    ctx = jnp.stack([mu * (c % 2 == 0), mu * (c % 2 == 1), z, z, 1.0 - mu, z, z, z])
    return rowmask, jnp.stack([ctx, lat]).astype(F32)


def _pad_rows(wt):
    z = jnp.zeros_like(wt[0])
    wp = jnp.stack([jnp.concatenate([wt[0], z], axis=0), jnp.concatenate([z, wt[1]], axis=0)])
    hi = wp.astype(BF16)
    lo = (wp - hi.astype(F32)).astype(BF16)
    return jnp.stack([hi, lo], axis=1)


def kernel(x, c, ctx, c_ctx, ada_w, ada_b, norm1_g, norm2_g, w_in, w_out, shift_mu, rwkv_w0, rwkv_w_up, rwkv_a0, rwkv_a_up, rwkv_g_up, rwkv_k_k, rwkv_k_a, rwkv_r_k, rwkv_ln_w, rwkv_ln_b, s5_lam_re, s5_lam_im, s5_log_dt, s5_b_re, s5_b_im, s5_c_re, s5_c_im, s5_d, s5_glu_w, s5_glu_b, ffn_w_gate, ffn_w_up, ffn_w_down, moe_router, moe_w_gate, moe_w_up, moe_w_down, final_g):
    b, l, d = x.shape
    ctx_len = ctx.shape[1]
    depth = ada_w.shape[0]
    slab_w = shift_mu.shape[1]
    rw_w = rwkv_k_k.shape[1]
    assert ctx_len == TM and l % TM == 0 and b + 1 <= 8
    assert rw_w % (WKV_HEADS * HEAD) == 0 and depth == 2
    nct = ctx_len // TM
    nctc = ctx_len // WKV_CHUNK
    nctc16 = ctx_len // S5_CHUNK

    act = jnp.zeros((8, d), F32).at[:b].set(c).at[b].set(c_ctx)
    mods = _ada_mod(act, ada_w, ada_b).reshape(depth, 8, 6, d)

    def mod(i, k):
        cm = jnp.broadcast_to(mods[i, b, k][None, :], (b, d))
        return jnp.stack([cm, mods[i, :b, k]], axis=1)[:, :, None, :]

    hi = lax.broadcasted_iota(jnp.int32, (rw_w, rw_w), 0) // HEAD
    hj = lax.broadcasted_iota(jnp.int32, (rw_w, rw_w), 1) // HEAD
    bd = (hi == hj).astype(BF16)

    ti = lax.broadcasted_iota(jnp.int32, (TM, TM), 0)
    si = lax.broadcasted_iota(jnp.int32, (TM, TM), 1)
    same_chunk = (ti // WKV_CHUNK) == (si // WKV_CHUNK)
    tri = jnp.stack([same_chunk & (si <= ti), same_chunk & (si >= ti)]).astype(BF16)

    xcat = jnp.concatenate([ctx, x], axis=1)
    out = None
    for i in range(depth):
        last = i == depth - 1
        p, u8 = _inproj(xcat, norm1_g[i][None], mod(i, 0), mod(i, 1), w_in[i].astype(BF16), nct,
                        s5_d.shape[1])
        rowmask, lanec = _shift_masks(shift_mu[i], ctx_len, l)
        v, at, rt, bg, kg, ee, g, bo = _rwkv_prep(
            p, rowmask, lanec, rwkv_k_k[i][None], rwkv_k_a[i][None], rwkv_r_k[i].reshape(1, -1),
            rwkv_w0[i], rwkv_a0[i], _pad_rows(rwkv_w_up[i]), _pad_rows(rwkv_a_up[i]),
            _pad_rows(rwkv_g_up[i]), bd, tri, nct, slab_w)
        yf, yr = _wkv_scan(v, at, rt, bg, kg, ee, nctc)
        s5w = _s5_weights(s5_lam_re[i], s5_lam_im[i], s5_log_dt[i], s5_b_re[i], s5_b_im[i],
                          s5_c_re[i], s5_c_im[i])
        ys = _s5_mix(u8, s5w, b, nctc16)
        t0 = nct if last else 0
        xm = _mixout(xcat, yf, yr, g, bo, ys, p, rwkv_ln_w[i].reshape(1, -1), rwkv_ln_b[i].reshape(1, -1),
                     bd, s5_d[i][None], s5_glu_w[i].astype(BF16), s5_glu_b[i][None],
                     w_out[i].astype(BF16), mod(i, 2), nct, t0)
        if not last:
            j = i // 2
            xcat = _ffn(xm, norm2_g[i][None], mod(i, 3), mod(i, 4), mod(i, 5),
                        ffn_w_gate[j].astype(BF16), ffn_w_up[j].astype(BF16),
                        ffn_w_down[j].astype(BF16), nct)
        else:
            j = i // 2
            ne = moe_router.shape[2]
            nr = -(-ne // 8) * 8
            router_t = jnp.zeros((nr, d), F32).at[:ne].set(moe_router[j].T)
            lat = lambda k: mods[i, :b, k][:, None, :]
            h, cmb, cnt = _route(xm, norm2_g[i][None], lat(3), lat(4), router_t, ne)
            cnt = cnt[:, :ne, 0].reshape(-1, MOE_TM // MOE_TR, ne).sum(axis=1)
            ui = lax.broadcasted_iota(jnp.int32, (MOE_SUB, MOE_SUB), 0)
            uj = lax.broadcasted_iota(jnp.int32, (MOE_SUB, MOE_SUB), 1)
            moe = _moe(h, cmb, cnt, (ui < uj).astype(BF16), moe_w_gate[j].astype(BF16),
                       moe_w_up[j].astype(BF16), moe_w_down[j].astype(BF16))
            out = _final(xm, moe, lat(5), final_g[None])
    return out
```

```python
import functools
import math

import jax
import jax.numpy as jnp
from jax import lax
from jax.experimental import pallas as pl
from jax.experimental.pallas import tpu as pltpu

F32 = jnp.float32
BF16 = jnp.bfloat16
HIGHEST = lax.Precision.HIGHEST

GRID_W = 64
HEAD = 64
DECAY_RANK = 64
ICL_RANK = 64
GATE_RANK = 128
S5_GROUP = 16
S5_STATE = 64
NORM_EPS = 1e-6
GN_EPS = 64e-5
L2_EPS = 1e-12
LAM_RE_MAX = -1e-4
TOP_K = 2

TM = 256
WKV_CHUNK = 64
WKV_HEADS = 4
WKV_STEP = 4
S5_CHUNK = 16
S5_OCT = 128
VMEM_LIMIT = 56 * 1024 * 1024


def _cparams(*sem):
    return pltpu.CompilerParams(dimension_semantics=sem, vmem_limit_bytes=VMEM_LIMIT)


def _dot(a, b):
    return jnp.dot(a, b, preferred_element_type=F32)


def _dot32(a, b):
    return jnp.dot(a, b, precision=HIGHEST, preferred_element_type=F32)


def _split2(x):
    hi = x.astype(BF16)
    return hi, (x - hi.astype(F32)).astype(BF16)


def _dot_ones(x, ones_bf):
    hi, lo = _split2(x)
    return _dot(hi, ones_bf) + _dot(lo, ones_bf)


def _dot_w2(x, w2_ref):
    hi, lo = _split2(x)
    return _dot(hi, w2_ref[0]) + _dot(lo, w2_ref[0]) + _dot(hi, w2_ref[1])


def _dot_nt(a, b):
    return lax.dot_general(a, b, (((1,), (1,)), ((), ())), preferred_element_type=F32)


def _dot_tn(a, b):
    return lax.dot_general(a, b, (((0,), (0,)), ((), ())), preferred_element_type=F32)


def _ada_kernel(act_ref, w_ref, b_ref, o_ref):
    a = act_ref[...]
    a = a * jax.nn.sigmoid(a)
    o_ref[...] = _dot32(a, w_ref[...]) + b_ref[...]


def _ada_mod(act, ada_w, ada_b):
    depth, d, n = ada_w.shape
    tn = 1536
    return pl.pallas_call(
        _ada_kernel,
        grid=(depth, n // tn),
        in_specs=[
            pl.BlockSpec((8, d), lambda i, j: (0, 0)),
            pl.BlockSpec((None, d, tn), lambda i, j: (i, 0, j)),
            pl.BlockSpec((None, 1, tn), lambda i, j: (i, 0, j)),
        ],
        out_specs=pl.BlockSpec((None, 8, tn), lambda i, j: (i, 0, j)),
        out_shape=jax.ShapeDtypeStruct((depth, 8, n), F32),
        compiler_params=_cparams("arbitrary", "arbitrary"),
        name="ada_mod",
    )(act, ada_w, ada_b.reshape(depth, 1, n))


def _norm_mod(x, g, shift, scale):
    ms = jnp.mean(x * x, axis=-1, keepdims=True)
    y = x * lax.rsqrt(ms + NORM_EPS) * g
    return y * (1.0 + scale) + shift


def _inproj_kernel(sw, x_ref, g_ref, sh_ref, sc_ref, w_ref, o_ref, u8_ref, us_ref):
    h = _norm_mod(x_ref[...], g_ref[...], sh_ref[...], sc_ref[...])
    p = _dot(h.astype(BF16), w_ref[...])
    o_ref[...] = p.astype(BF16)
    base = p.shape[1] - sw
    cpt = TM // S5_CHUNK
    for o8 in range(sw // S5_OCT):
        us_ref[o8] = p[:, base + o8 * S5_OCT:base + (o8 + 1) * S5_OCT]
    for o8 in range(sw // S5_OCT):
        for s in range(S5_CHUNK):
            u8_ref[o8, :, s * S5_OCT:(s + 1) * S5_OCT] = (
                us_ref[o8, pl.ds(s, cpt, stride=S5_CHUNK), :].astype(BF16))


def _inproj(xcat, g, shift, scale, w_bf, nct, sw):
    b, tt, d = xcat.shape
    n = w_bf.shape[1]
    noct = sw // S5_OCT
    cpt = TM // S5_CHUNK
    kw = S5_CHUNK * S5_OCT
    kind = lambda bi, i: (bi, jnp.where(i < nct, 0, 1), 0, 0)
    return pl.pallas_call(
        functools.partial(_inproj_kernel, sw),
        grid=(b, tt // TM),
        in_specs=[
            pl.BlockSpec((None, TM, d), lambda bi, i: (bi, i, 0)),
            pl.BlockSpec((1, d), lambda bi, i: (0, 0)),
            pl.BlockSpec((None, None, 1, d), kind),
            pl.BlockSpec((None, None, 1, d), kind),
            pl.BlockSpec((d, n), lambda bi, i: (0, 0)),
        ],
        out_specs=[pl.BlockSpec((None, TM, n), lambda bi, i: (bi, i, 0)),
                   pl.BlockSpec((noct, cpt, kw), lambda bi, i: (0, i, bi))],
        out_shape=[jax.ShapeDtypeStruct((b, tt, n), BF16),
                   jax.ShapeDtypeStruct((noct, tt // S5_CHUNK, b * kw), BF16)],
        scratch_shapes=[pltpu.VMEM((noct, TM, S5_OCT), F32)],
        compiler_params=_cparams("parallel", "parallel"),
        name="inproj",
    )(xcat, g, shift, scale, w_bf)


def _split3(x):
    hi = x.astype(BF16)
    r1 = x - hi.astype(F32)
    mid = r1.astype(BF16)
    lo = (r1 - mid.astype(F32)).astype(BF16)
    return hi, mid, lo


def _prep_kernel(p_ref, up_ref, dn_ref, rm_ref, lc_ref, kk_ref, ka_ref, rk_ref, w0_ref, a0_ref,
                 wup_ref, aup_ref, gup_ref, bd_ref, tri_ref,
                 v_o, at_o, rt_o, bg_o, kg_o, ee_o, g_o, bo_o):
    x = p_ref[...].astype(F32)
    rm = rm_ref[...]
    lc = lc_ref[...]
    prev = pltpu.roll(x, 1, 0)
    nxt = pltpu.roll(x, TM - 1, 0)
    up = jnp.concatenate([up_ref[...].astype(F32), x[: TM - GRID_W]], axis=0)
    dn = jnp.concatenate([x[GRID_W:], dn_ref[...].astype(F32)], axis=0)
    slab = (x * lc[4:5]
            + rm[:, 0:1] * (prev * lc[0:1])
            + rm[:, 1:2] * (nxt * lc[1:2])
            + rm[:, 2:3] * (up * lc[2:3])
            + rm[:, 3:4] * (dn * lc[3:4]))
    w = kk_ref.shape[1]
    r = slab[:, 0:w]
    k = slab[:, w:2 * w]
    v = slab[:, 2 * w:3 * w]
    o = 3 * w
    wd = slab[:, o:o + 2 * DECAY_RANK]
    ad = slab[:, o + 2 * DECAY_RANK:o + 2 * DECAY_RANK + 2 * ICL_RANK]
    gd = slab[:, o + 2 * DECAY_RANK + 2 * ICL_RANK:]
    bd = bd_ref[...]
    kk = k * kk_ref[...]
    nrm = jnp.sqrt(_dot_ones(kk * kk, bd))
    kk = kk / jnp.maximum(nrm, L2_EPS)
    v_o[...] = v.astype(BF16)
    twd = jnp.tanh(wd)
    sgd = jax.nn.sigmoid(gd)
    c = WKV_CHUNK
    for d in range(2):
        z = w0_ref[d:d + 1, :] + _dot_w2(twd, wup_ref.at[d])
        w_log = -jax.nn.softplus(-z) - 0.5
        lw = -jnp.exp(w_log)
        a = jax.nn.sigmoid(a0_ref[d:d + 1, :] + _dot_w2(ad, aup_ref.at[d]))
        kt = k * (1.0 + (a - 1.0) * ka_ref[...])
        g_o[d] = _dot_w2(sgd, gup_ref.at[d]).astype(BF16)
        bo_o[d] = (_dot_ones(r * kt * rk_ref[...], bd) * v).astype(BF16)
        tri = tri_ref[d]
        hi, mid, lo = _split3(lw)
        lg_in = _dot(tri, hi) + _dot(tri, mid) + _dot(tri, lo)
        e_neg = jnp.exp(-lg_in)
        at_o[d] = (-kk * jnp.exp(lg_in - lw)).astype(BF16)
        rt_o[d] = (r * jnp.exp(lg_in)).astype(BF16)
        bg_o[d] = (kk * a * e_neg).astype(BF16)
        kg_o[d] = (kt * e_neg).astype(BF16)
        for ci in range(TM // c):
            last = ci * c + (c - 1 if d == 0 else 0)
            ee_o[d, ci] = jnp.exp(lg_in[last:last + 1, :])


def _rwkv_prep(p, rowmask, lanec, k_k, k_a, r_k, w0, a0, wup, aup, gup, bd, tri, nct, slab_w):
    b, tt, _ = p.shape
    w = k_k.shape[1]
    nt = tt // TM
    cpt = TM // WKV_CHUNK
    hb = TM // GRID_W
    nhb = tt // GRID_W
    full = lambda *s: pl.BlockSpec(s, lambda bi, i: (0,) * len(s))
    tok = pl.BlockSpec((None, TM, w), lambda bi, i: (bi, i, 0))
    tok2 = pl.BlockSpec((2, None, TM, w), lambda bi, i: (0, bi, i, 0))
    bf1 = jax.ShapeDtypeStruct((b, tt, w), BF16)
    bf2 = jax.ShapeDtypeStruct((2, b, tt, w), BF16)
    sh2 = jax.ShapeDtypeStruct((2, b, tt, w), F32)
    return pl.pallas_call(
        _prep_kernel,
        grid=(b, nt),
        in_specs=[
            pl.BlockSpec((None, TM, slab_w), lambda bi, i: (bi, i, 0)),
            pl.BlockSpec((None, GRID_W, slab_w), lambda bi, i: (bi, jnp.maximum(i * hb - 1, 0), 0)),
            pl.BlockSpec((None, GRID_W, slab_w),
                         lambda bi, i: (bi, jnp.minimum(i * hb + hb, nhb - 1), 0)),
            pl.BlockSpec((None, TM, 8), lambda bi, i: (i, 0, 0)),
            pl.BlockSpec((None, 8, slab_w), lambda bi, i: (jnp.where(i < nct, 0, 1), 0, 0)),
            full(1, w), full(1, w), full(1, w), full(2, w), full(2, w),
            full(2, 2, 2 * DECAY_RANK, w), full(2, 2, 2 * ICL_RANK, w), full(2, 2, 2 * GATE_RANK, w),
            full(w, w), full(2, TM, TM),
        ],
        out_specs=[tok, tok2, tok2, tok2, tok2,
                   pl.BlockSpec((2, None, cpt, 1, w), lambda bi, i: (0, bi, i, 0, 0)),
                   tok2, tok2],
        out_shape=[bf1, bf2, bf2, bf2, bf2,
                   jax.ShapeDtypeStruct((2, b, tt // WKV_CHUNK, 1, w), F32), bf2, bf2],
        compiler_params=_cparams("parallel", "parallel"),
        name="rwkv_prep",
    )(p, p, p, rowmask, lanec, k_k, k_a, r_k, w0, a0, wup, aup, gup, bd, tri)


def _wkv_kernel(nb, ngrp, v_f, v_r, at_f, at_r, rt_f, rt_r, bg_f, bg_r, kg_f, kg_r, ee_f, ee_r,
                y_f, y_r, ht_ref):
    j = pl.program_id(0)
    c = WKV_CHUNK
    gw = WKV_HEADS * HEAD
    gn = WKV_HEADS * c

    @pl.when(j == 0)
    def _():
        ht_ref[...] = jnp.zeros_like(ht_ref)

    sh = int(math.log2(c))
    row = lax.broadcasted_iota(jnp.int32, (gn, gw), 0)
    col = lax.broadcasted_iota(jnp.int32, (gn, gw), 1)
    same = (row >> sh) == (col >> sh)
    tf = lax.broadcasted_iota(jnp.int32, (c, gn), 0)
    sf = lax.broadcasted_iota(jnp.int32, (c, gn), 1) & (c - 1)
    eye = (tf == sf).astype(F32)

    def stack(x):
        xb = jnp.concatenate([x.astype(BF16)] * WKV_HEADS, axis=0)
        return jnp.where(same, xb, jnp.zeros_like(xb))

    dirs = ((v_f, at_f, rt_f, bg_f, kg_f, ee_f, y_f, sf < tf, sf <= tf),
            (v_r, at_r, rt_r, bg_r, kg_r, ee_r, y_r, sf > tf, sf >= tf))
    nsc = v_f.shape[1] // c
    chains = [(d, bi, q, sc) for d in range(2) for bi in range(nb) for q in range(ngrp)
              for sc in range(nsc)]
    sl = lambda q: slice(q * gw, (q + 1) * gw)
    rows = lambda sc: slice(sc * c, (sc + 1) * c)
    rd = lambda k: [dirs[d][k][bi, rows(sc), sl(q)] for d, bi, q, sc in chains]
    cat0 = lambda xs: jnp.concatenate(xs, axis=0)
    v, at, rt, bg, kg = rd(0), rd(1), rd(2), rd(3), rd(4)
    ee = [dirs[d][5][bi, sc, :, sl(q)] for d, bi, q, sc in chains]
    before = [dirs[ch[0]][7] for ch in chains]
    incl = [dirs[ch[0]][8] for ch in chains]
    n_ch = range(len(chains))

    v_bd = [stack(x) for x in v]
    at_bd = [stack(x) for x in at]
    bk_bd = [cat0([stack(bg[i]), stack(kg[i])]) for i in n_ch]
    a = [_dot_nt(cat0([at[i], rt[i]]), bk_bd[i]) for i in n_ch]
    n = [jnp.where(before[i], a[i][0:c, 0:gn], 0.0) for i in n_ch]
    a_kk = [cat0([jnp.where(before[i], a[i][0:c, gn:], 0.0),
                  jnp.where(incl[i], a[i][c:, gn:], 0.0)]).astype(BF16) for i in n_ch]
    a_rb = [jnp.where(incl[i], a[i][c:, 0:gn], 0.0).astype(BF16) for i in n_ch]
    tm = [eye + x for x in n]
    pw = [_dot(x.astype(BF16), stack(x)) for x in n]
    for lvl in range(1, sh):
        pw_bd = [stack(x) for x in pw]
        if lvl < sh - 1:
            tp = [_dot(cat0([tm[i].astype(BF16), pw[i].astype(BF16)]), pw_bd[i]) for i in n_ch]
            tm = [tm[i] + tp[i][0:c] for i in n_ch]
            pw = [tp[i][c:] for i in n_ch]
        else:
            tm = [tm[i] + _dot(tm[i].astype(BF16), pw_bd[i]) for i in n_ch]
    tm_b = [x.astype(BF16) for x in tm]
    atp = [_dot(tm_b[i], at_bd[i]) for i in n_ch]
    av = [_dot(a_kk[i], v_bd[i]) for i in n_ch]
    wv = [_dot(tm_b[i], stack(av[i][0:c])) for i in n_ch]
    wv_bd = [stack(x) for x in wv]
    atp_bd = [stack(x) for x in atp]
    ar = [_dot(a_rb[i], jnp.concatenate([wv_bd[i], atp_bd[i]], axis=1)) for i in n_ch]
    y0 = [ar[i][:, 0:gw] + av[i][c:] for i in n_ch]
    rtp = [(ar[i][:, gw:] + rt[i].astype(F32)).astype(BF16) for i in n_ch]
    bge_bd = [stack(bg[i].astype(F32) * ee[i]) for i in n_ch]
    kge_bd = [stack(kg[i].astype(F32) * ee[i]) for i in n_ch]
    g = [_dot_tn(bge_bd[i], atp_bd[i]).astype(BF16) for i in n_ch]
    hloc_t = [_dot_tn(cat0([wv_bd[i], v_bd[i]]), cat0([bge_bd[i], kge_bd[i]])) for i in n_ch]
    index = {ch: i for i, ch in enumerate(chains)}
    for d in range(2):
        order = range(nsc) if d == 0 else range(nsc - 1, -1, -1)
        for bi in range(nb):
            for q in range(ngrp):
                ht = ht_ref[d, bi, q]
                for sc in order:
                    i = index[(d, bi, q, sc)]
                    ht_b = ht.astype(BF16)
                    dirs[d][6][bi, rows(sc), sl(q)] = y0[i] + _dot_nt(rtp[i], ht_b)
                    ht = ht * ee[i] + _dot_nt(ht_b, g[i]) + hloc_t[i]
                ht_ref[d, bi, q] = ht


def _wkv_scan(v, at, rt, bg, kg, ee, nctc):
    b, tt, w = v.shape
    rows = WKV_STEP * WKV_CHUNK
    assert nctc % WKV_STEP == 0 and (tt // WKV_CHUNK) % WKV_STEP == 0
    ntot = tt // rows
    nctb = nctc // WKV_STEP
    ngrp = w // (WKV_HEADS * HEAD)
    fwd = lambda j: j
    rev = lambda j: jnp.where(j < nctb, nctb - 1 - j, ntot - 1 + nctb - j)
    tok = lambda cm: pl.BlockSpec((b, rows, w), lambda j: (0, cm(j), 0))
    tok2 = lambda d, cm: pl.BlockSpec((None, b, rows, w), lambda j: (d, 0, cm(j), 0))
    eesp = lambda d, cm: pl.BlockSpec((None, b, WKV_STEP, 1, w), lambda j: (d, 0, cm(j), 0, 0))
    pair = lambda f: [f(0, fwd), f(1, rev)]
    ysh = jax.ShapeDtypeStruct((b, tt, w), F32)
    return pl.pallas_call(
        functools.partial(_wkv_kernel, b, ngrp),
        grid=(ntot,),
        in_specs=[tok(fwd), tok(rev)] + pair(tok2) + pair(tok2) + pair(tok2) + pair(tok2) + pair(eesp),
        out_specs=[tok(fwd), tok(rev)],
        out_shape=[ysh, ysh],
        scratch_shapes=[pltpu.VMEM((2, b, ngrp, WKV_HEADS * HEAD, WKV_HEADS * HEAD), F32)],
        compiler_params=_cparams("arbitrary"),
        name="wkv_scan",
    )(v, v, at, at, rt, rt, bg, bg, kg, kg, ee, ee)


def _s5_weights(lam_re, lam_im, log_dt, b_re, b_im, c_re, c_im):
    tc = S5_CHUNK
    lr = jnp.minimum(lam_re.astype(F32), LAM_RE_MAX)
    li = lam_im.astype(F32)
    dt = jnp.exp(log_dt.astype(F32))[..., None]
    mag = jnp.exp(lr * dt)
    ar = mag * jnp.cos(li * dt)
    ai = mag * jnp.sin(li * dt)
    den = lr * lr + li * li
    xr = ar - 1.0
    cr = (xr * lr + ai * li) / den
    ci = (ai * lr - xr * li) / den
    br = cr[..., None] * b_re - ci[..., None] * b_im
    bi = cr[..., None] * b_im + ci[..., None] * b_re
    pr, pi = [jnp.ones_like(ar)], [jnp.zeros_like(ar)]
    for _ in range(tc):
        pr_n = pr[-1] * ar - pi[-1] * ai
        pi_n = pr[-1] * ai + pi[-1] * ar
        pr.append(pr_n)
        pi.append(pi_n)
    pr = jnp.stack(pr)
    pi = jnp.stack(pi)
    lbr = pr[..., None] * br - pi[..., None] * bi
    lbi = pr[..., None] * bi + pi[..., None] * br
    clr = c_re * pr[:, :, :, None, :] - c_im * pi[:, :, :, None, :]
    cli = c_re * pi[:, :, :, None, :] + c_im * pr[:, :, :, None, :]
    lbr_t = jnp.swapaxes(lbr, -1, -2)
    lbi_t = jnp.swapaxes(lbi, -1, -2)
    kern_t = jnp.sum(lbr_t[..., :, None, :] * c_re[None, :, :, None, :, :]
                     - lbi_t[..., :, None, :] * c_im[None, :, :, None, :, :], axis=-1)
    g = ar.shape[1]
    og = S5_OCT // S5_GROUP
    noct = g // og
    eye = jnp.eye(og, dtype=F32)

    def bdiag(x):
        nt, _, _, a, n = x.shape
        x = x.reshape(nt, 2, noct, og, a, n)
        y = x[:, :, :, :, :, None, :] * eye[None, None, None, :, None, :, None]
        return jnp.transpose(y, (1, 2, 0, 3, 4, 5, 6)).reshape(2, noct, nt, og * a, og * n).astype(BF16)

    kbd = bdiag(kern_t)

    def pair_block(d, lp):
        kd = kbd[d]
        zero = jnp.zeros_like(kd[:, 0])
        k = lambda tau: kd[:, tau] if tau >= 0 else zero
        if d == 0:
            rows = [[k(2 * lp), k(2 * lp + 1)], [k(2 * lp - 1), k(2 * lp)]]
        else:
            rows = [[k(2 * lp), k(2 * lp - 1)], [k(2 * lp + 1), k(2 * lp)]]
        return jnp.concatenate([jnp.concatenate(r, axis=-1) for r in rows], axis=-2)

    wpair = jnp.stack([jnp.stack([pair_block(d, lp) for lp in range(tc // 2)], axis=1)
                       for d in range(2)])
    lbc = jnp.concatenate([lbr_t, lbi_t], axis=-1)
    clc = jnp.concatenate([clr, -cli], axis=-1)
    pout8, qin8 = _s5_expand(lbc, clc)
    la = jnp.concatenate([pr[tc], pr[tc]], axis=-1)
    lb = jnp.concatenate([-pi[tc], pi[tc]], axis=-1)
    return wpair, pout8, qin8, la, lb


def _s5_expand_kernel(p_ref, q_ref, po_ref, qo_ref):
    nt, _, og, a, n = p_ref.shape
    tc = nt - 1
    po_ref[...] = jnp.zeros_like(po_ref)
    qo_ref[...] = jnp.zeros_like(qo_ref)
    for d in range(2):
        for s in range(tc):
            lag_out = tc - 1 - s if d == 0 else s
            lag_in = s + 1 if d == 0 else tc - s
            for gi in range(og):
                rows, lanes = slice(gi * a, (gi + 1) * a), slice(gi * n, (gi + 1) * n)
                po_ref[d, s, rows, lanes] = p_ref[lag_out, d, gi].astype(BF16)
                qo_ref[d, s, rows, lanes] = q_ref[lag_in, d, gi].astype(BF16)


def _s5_expand(lbc, clc):
    nt, _, g, a, n = lbc.shape
    tc = nt - 1
    og = S5_OCT // S5_GROUP
    noct = g // og
    isp = pl.BlockSpec((nt, 2, og, a, n), lambda o: (0, 0, o, 0, 0))
    osp = pl.BlockSpec((2, None, tc, og * a, og * n), lambda o: (0, o, 0, 0, 0))
    osh = jax.ShapeDtypeStruct((2, noct, tc, og * a, og * n), BF16)
    po, qo = pl.pallas_call(
        _s5_expand_kernel,
        grid=(noct,),
        in_specs=[isp, isp],
        out_specs=[osp, osp],
        out_shape=[osh, osh],
        compiler_params=_cparams("parallel"),
        name="s5_expand",
    )(lbc, clc)
    return po.reshape(2, noct, tc * og * a, og * n), qo.reshape(2, noct, tc * og * a, og * n)


def _s5_local_kernel(u_ref, p_ref, e_ref):
    e_ref[...] = _dot(u_ref[...], p_ref[...])


def _s5_local(u8, pout8, nb):
    noct, nch, _ = u8.shape
    kw, n = pout8.shape[2:]
    return pl.pallas_call(
        _s5_local_kernel,
        grid=(noct, nb, 2),
        in_specs=[
            pl.BlockSpec((None, nch, kw), lambda o, b, d: (o, 0, b)),
            pl.BlockSpec((None, None, kw, n), lambda o, b, d: (d, o, 0, 0)),
        ],
        out_specs=pl.BlockSpec((None, nch, n), lambda o, b, d: (d, 0, o * nb + b)),
        out_shape=jax.ShapeDtypeStruct((2, nch, noct * nb * n), F32),
        compiler_params=_cparams("parallel", "parallel", "parallel"),
        name="s5_local",
    )(u8, pout8)


S5_STATE_ROWS = 8
S5_STATE_LANES = 256


def _s5_state_kernel(nctc, ntot, e_ref, la_ref, lb_ref, x_ref, es_ref):
    d = pl.program_id(0)
    la = la_ref[...]
    lb = lb_ref[...]
    nr, wl = la.shape

    def swap(t):
        lane = lax.broadcasted_iota(jnp.int32, t.shape, 1)
        first_half = (lane & (2 * S5_STATE - 1)) < S5_STATE
        return jnp.where(first_half, pltpu.roll(t, wl - S5_STATE, 1), pltpu.roll(t, S5_STATE, 1))

    es_ref[...] = swap(e_ref[...].reshape(ntot * nr, wl)).reshape(ntot, nr, wl)
    lbs = swap(lb)

    def body(j, carry):
        x, xs = carry
        rev_idx = jnp.where(j < nctc, nctc - 1 - j, ntot - 1 + nctc - j)
        c = jnp.where(d == 0, j, rev_idx)
        x_ref[c] = x
        return la * x + lb * xs + e_ref[c], la * xs + lbs * x + es_ref[c]

    zero = jnp.zeros(la.shape, F32)
    lax.fori_loop(0, ntot, body, (zero, zero))


def _s5_state(e, la, lb, nctc, ntot):
    _, nch, nr, lanes = e.shape
    wl = S5_STATE_LANES
    blk = pl.BlockSpec((None, nch, nr, wl), lambda d, i: (d, 0, 0, i))
    cf = pl.BlockSpec((None, nr, wl), lambda d, i: (d, 0, i))
    return pl.pallas_call(
        functools.partial(_s5_state_kernel, nctc, ntot),
        grid=(2, lanes // wl),
        in_specs=[blk, cf, cf],
        out_specs=blk,
        out_shape=jax.ShapeDtypeStruct(e.shape, F32),
        scratch_shapes=[pltpu.VMEM((nch, nr, wl), F32)],
        compiler_params=_cparams("parallel", "parallel"),
        name="s5_state",
    )(e, la, lb)


def _s5_out_kernel(rev, u_ref, w_ref, x_ref, q_ref, y_ref):
    pw = w_ref.shape[1]
    npair = w_ref.shape[0]
    x = x_ref[...].astype(BF16)
    for tp in range(npair):
        acc = _dot_nt(x, q_ref[tp * pw:(tp + 1) * pw, :])
        for lp in range(npair - tp if rev else tp + 1):
            sp = tp + lp if rev else tp - lp
            acc = acc + _dot(u_ref[:, sp * pw:(sp + 1) * pw], w_ref[lp])
        y_ref[:, tp * pw:(tp + 1) * pw] = acc


def _s5_out(u8, wpair, xin, qin8, nb, d):
    noct, nch, _ = u8.shape
    npair, pw = wpair.shape[2:4]
    kw = npair * pw
    n2 = qin8.shape[3]
    return pl.pallas_call(
        functools.partial(_s5_out_kernel, d == 1),
        grid=(noct, nb),
        in_specs=[
            pl.BlockSpec((None, nch, kw), lambda o, b: (o, 0, b)),
            pl.BlockSpec((None, None, npair, pw, pw), lambda o, b: (d, o, 0, 0, 0)),
            pl.BlockSpec((None, nch, n2), lambda o, b: (d, 0, o * nb + b)),
            pl.BlockSpec((None, None, kw, n2), lambda o, b: (d, o, 0, 0)),
        ],
        out_specs=pl.BlockSpec((None, nch, kw), lambda o, b: (o, 0, b)),
        out_shape=jax.ShapeDtypeStruct((noct, nch, nb * kw), F32),
        compiler_params=_cparams("parallel", "parallel"),
        name="s5_out",
    )(u8, wpair, xin, qin8)


def _s5_mix(u8, weights, nb, nctc16):
    wpair, pout8, qin8, la, lb = weights
    noct, nch, _ = u8.shape
    e = _s5_local(u8, pout8, nb)
    assert noct * nb == S5_STATE_ROWS
    fl = e.shape[2] // S5_STATE_ROWS
    coef = lambda t: jnp.repeat(t.reshape(2, noct, fl), nb, axis=1)
    xin = _s5_state(e.reshape(2, nch, S5_STATE_ROWS, fl), coef(la), coef(lb), nctc16, nch)
    xin = xin.reshape(e.shape)
    return _s5_out(u8, wpair, xin, qin8, nb, 0), _s5_out(u8, wpair, xin, qin8, nb, 1)


def _mixout_kernel(x_ref, yf_ref, yr_ref, g_ref, bo_ref, y8f_ref, y8r_ref, u_ref, lnw_ref, lnb_ref,
                   bd_ref, dsk_ref, gluw_ref, glub_ref, wout_ref, gate_ref, o_ref, ysn_ref):
    cpt = TM // S5_CHUNK
    for o8 in range(y8f_ref.shape[0]):
        for s in range(S5_CHUNK):
            lanes = slice(s * S5_OCT, (s + 1) * S5_OCT)
            ysn_ref[o8, pl.ds(s, cpt, stride=S5_CHUNK), :] = y8f_ref[o8, :, lanes] + y8r_ref[o8, :, lanes]
    ys = jnp.concatenate([ysn_ref[o8] for o8 in range(y8f_ref.shape[0])], axis=1)
    bd = bd_ref[...]
    inv = 1.0 / HEAD
    rw = None
    for d, y_ref in enumerate((yf_ref, yr_ref)):
        y = y_ref[...]
        mean = _dot_ones(y, bd) * inv
        yc = y - mean
        var = _dot_ones(yc * yc, bd) * inv
        yn = yc * lax.rsqrt(var + GN_EPS) * lnw_ref[...] + lnb_ref[...]
        o = (yn + bo_ref[d].astype(F32)) * g_ref[d].astype(F32)
        rw = o if rw is None else rw + o
    u = u_ref[...].astype(F32)
    ss = ys + dsk_ref[...] * u
    ss = jax.nn.gelu(ss)
    ss = ss * jax.nn.sigmoid(_dot(ss.astype(BF16), gluw_ref[...]) + glub_ref[...])
    w = rw.shape[1]
    mix = _dot(rw.astype(BF16), wout_ref[0:w, :]) + _dot(ss.astype(BF16), wout_ref[w:, :])
    o_ref[...] = x_ref[...] + gate_ref[...] * mix


def _mixout(xcat, yf, yr, g, bo, y8, p, ln_w, ln_b, bd, d_skip, glu_w, glu_b, w_out, gate, nct, t0):
    b, tt, d = xcat.shape
    w = ln_w.shape[1]
    sw = d_skip.shape[1]
    y8f, y8r = y8
    noct = y8f.shape[0]
    cpt = TM // S5_CHUNK
    kw = S5_CHUNK * S5_OCT
    ublk = (p.shape[2] - sw) // sw
    nt = tt // TM - t0
    full = lambda *s: pl.BlockSpec(s, lambda bi, i: (0,) * len(s))
    tok = pl.BlockSpec((None, TM, w), lambda bi, i: (bi, i + t0, 0))
    tok2 = pl.BlockSpec((2, None, TM, w), lambda bi, i: (0, bi, i + t0, 0))
    return pl.pallas_call(
        _mixout_kernel,
        grid=(b, nt),
        in_specs=[
            pl.BlockSpec((None, TM, d), lambda bi, i: (bi, i + t0, 0)),
            tok, tok, tok2, tok2,
            pl.BlockSpec((noct, cpt, kw), lambda bi, i: (0, i + t0, bi)),
            pl.BlockSpec((noct, cpt, kw), lambda bi, i: (0, i + t0, bi)),
            pl.BlockSpec((None, TM, sw), lambda bi, i: (bi, i + t0, ublk)),
            full(1, w), full(1, w), full(w, w), full(1, sw), full(sw, sw), full(1, sw),
            full(w + sw, d),
            pl.BlockSpec((None, None, 1, d), lambda bi, i: (bi, jnp.where(i + t0 < nct, 0, 1), 0, 0)),
        ],
        out_specs=pl.BlockSpec((None, TM, d), lambda bi, i: (bi, i, 0)),
        out_shape=jax.ShapeDtypeStruct((b, nt * TM, d), F32),
        scratch_shapes=[pltpu.VMEM((noct, TM, S5_OCT), F32)],
        compiler_params=_cparams("parallel", "parallel"),
        name="mix_out",
    )(xcat, yf, yr, g, bo, y8f, y8r, p, ln_w, ln_b, bd, d_skip, glu_w, glu_b, w_out, gate)


def _ffn_kernel(x_ref, g_ref, sh_ref, sc_ref, gate_ref, wg_ref, wu_ref, wd_ref, o_ref):
    x = x_ref[...]
    h = _norm_mod(x, g_ref[...], sh_ref[...], sc_ref[...]).astype(BF16)
    a = _dot(h, wg_ref[...])
    a = a * jax.nn.sigmoid(a) * _dot(h, wu_ref[...])
    o_ref[...] = x + gate_ref[...] * _dot(a.astype(BF16), wd_ref[...])


def _ffn(xcat, g, shift, scale, gate, wg, wu, wd, nct):
    b, tt, d = xcat.shape
    ff = wg.shape[1]
    kind = lambda bi, i: (bi, jnp.where(i < nct, 0, 1), 0, 0)
    mod = pl.BlockSpec((None, None, 1, d), kind)
    return pl.pallas_call(
        _ffn_kernel,
        grid=(b, tt // TM),
        in_specs=[
            pl.BlockSpec((None, TM, d), lambda bi, i: (bi, i, 0)),
            pl.BlockSpec((1, d), lambda bi, i: (0, 0)),
            mod, mod, mod,
            pl.BlockSpec((d, ff), lambda bi, i: (0, 0)),
            pl.BlockSpec((d, ff), lambda bi, i: (0, 0)),
            pl.BlockSpec((ff, d), lambda bi, i: (0, 0)),
        ],
        out_specs=pl.BlockSpec((None, TM, d), lambda bi, i: (bi, i, 0)),
        out_shape=jax.ShapeDtypeStruct((b, tt, d), F32),
        compiler_params=_cparams("parallel", "parallel"),
        name="ffn",
    )(xcat, g, shift, scale, gate, wg, wu, wd)


MOE_TR = 1024
MOE_TM = 2048
MOE_TF = 896
MOE_BLK = 256
MOE_SUB = 256


def _route_kernel(ne, x_ref, g_ref, sh_ref, sc_ref, rt_ref, h_o, cmb_o, cnt_o):
    h = _norm_mod(x_ref[...], g_ref[...], sh_ref[...], sc_ref[...])
    h_o[...] = h.astype(BF16)
    logits = lax.dot_general(rt_ref[...], h, (((1,), (1,)), ((), ())), precision=HIGHEST,
                             preferred_element_type=F32)
    sub = lax.broadcasted_iota(jnp.int32, logits.shape, 0).astype(F32)
    none = float(logits.shape[0])
    logits = jnp.where(sub < ne, logits, -jnp.inf)
    m1 = jnp.max(logits, axis=0, keepdims=True)
    i1 = jnp.min(jnp.where(logits == m1, sub, none), axis=0, keepdims=True)
    rest = jnp.where(sub == i1, -jnp.inf, logits)
    m2 = jnp.max(rest, axis=0, keepdims=True)
    i2 = jnp.min(jnp.where(rest == m2, sub, none), axis=0, keepdims=True)
    e2 = jnp.exp(m2 - m1)
    p1 = 1.0 / (1.0 + e2)
    p2 = e2 / (1.0 + e2)
    cmb = jnp.where(sub == i1, p1, 0.0) + jnp.where(sub == i2, p2, 0.0)
    cmb_o[...] = cmb
    cnt = jnp.sum((cmb > 0.0).astype(F32), axis=1, keepdims=True)
    cnt_o[...] = jnp.broadcast_to(cnt, cnt_o.shape).astype(jnp.int32)


def _route(x, g, shift, scale, router_t, ne):
    b, l, d = x.shape
    nr = router_t.shape[0]
    nt = l // MOE_TR
    mod = pl.BlockSpec((None, 1, d), lambda bi, i: (bi, 0, 0))
    return pl.pallas_call(
        functools.partial(_route_kernel, ne),
        grid=(b, nt),
        in_specs=[
            pl.BlockSpec((None, MOE_TR, d), lambda bi, i: (bi, i, 0)),
            pl.BlockSpec((1, d), lambda bi, i: (0, 0)),
            mod, mod,
            pl.BlockSpec((nr, d), lambda bi, i: (0, 0)),
        ],
        out_specs=[
            pl.BlockSpec((MOE_TR, d), lambda bi, i: (bi * nt + i, 0)),
            pl.BlockSpec((nr, MOE_TR), lambda bi, i: (0, bi * nt + i)),
            pl.BlockSpec((None, nr, 128), lambda bi, i: (bi * nt + i, 0, 0)),
        ],
        out_shape=[
            jax.ShapeDtypeStruct((b * l, d), BF16),
            jax.ShapeDtypeStruct((nr, b * l), F32),
            jax.ShapeDtypeStruct((b * nt, nr, 128), jnp.int32),
        ],
        compiler_params=_cparams("parallel", "parallel"),
        name="moe_route",
    )(x, g, shift, scale, router_t)


def _moe_kernel(cnt_ref, h_ref, cmb_ref, tri_ref, wg_ref, wu_ref, wd_ref, o_ref,
                pos_ref, hg_ref, ya_ref):
    t = pl.program_id(0)
    e = pl.program_id(1)
    j = pl.program_id(2)
    tm = h_ref.shape[0]
    nblk = jnp.right_shift(cnt_ref[t, e] + (MOE_BLK - 1), int(math.log2(MOE_BLK)))

    @pl.when((e == 0) & (j == 0))
    def _():
        o_ref[...] = jnp.zeros_like(o_ref)
        asg = (cmb_ref[...] > 0.0).astype(BF16)
        off = jnp.zeros((asg.shape[0], 1), F32)
        for k in range(tm // MOE_SUB):
            blk = asg[:, k * MOE_SUB:(k + 1) * MOE_SUB]
            pos_ref[:, k * MOE_SUB:(k + 1) * MOE_SUB] = _dot(blk, tri_ref[...]) + off
            off = off + jnp.sum(blk.astype(F32), axis=1, keepdims=True)

    sel = lax.broadcasted_iota(jnp.int32, pos_ref.shape, 0) == e
    posrow = jnp.sum(jnp.where(sel, pos_ref[...], 0.0), axis=0, keepdims=True)
    cwrow = jnp.sum(jnp.where(sel, cmb_ref[...], 0.0), axis=0, keepdims=True)
    rowi = lax.broadcasted_iota(jnp.int32, (MOE_BLK, tm), 0).astype(F32)

    def onehot(b):
        slot = rowi + (b * MOE_BLK).astype(F32)
        return (posrow == slot) & (cwrow > 0.0)

    @pl.when(j == 0)
    def _():
        def gather(b, carry):
            sel_b = jnp.where(onehot(b), 1.0, 0.0).astype(BF16)
            hg_ref[b] = _dot(sel_b, h_ref[...]).astype(BF16)
            ya_ref[b] = jnp.zeros(ya_ref.shape[1:], F32)
            return carry
        lax.fori_loop(0, nblk, gather, 0)

    d_model = hg_ref.shape[2]

    def swiglu(hb):
        a = _dot(hb, wg_ref[...])
        a = a * jax.nn.sigmoid(a) * _dot(hb, wu_ref[...])
        return _dot(a.astype(BF16), wd_ref[...])

    def ffn_pair(i, carry):
        rows = pl.ds(2 * i, 2)
        y = swiglu(hg_ref[rows].reshape(2 * MOE_BLK, d_model))
        ya_ref[rows] += y.reshape(2, MOE_BLK, d_model)
        return carry
    npair = jnp.right_shift(nblk, 1)
    lax.fori_loop(0, npair, ffn_pair, 0)

    @pl.when(nblk > 2 * npair)
    def _():
        ya_ref[nblk - 1] += swiglu(hg_ref[nblk - 1])

    @pl.when(j == pl.num_programs(2) - 1)
    def _():
        def scatter(b, carry):
            wsel = jnp.where(onehot(b), cwrow, 0.0).astype(BF16)
            o_ref[...] += _dot_tn(wsel, ya_ref[b].astype(BF16))
            return carry
        lax.fori_loop(0, nblk, scatter, 0)


def _moe(h, cmb, cnt, tri, wg, wu, wd):
    n, d = h.shape
    nr = cmb.shape[0]
    ne, _, ff = wg.shape
    nbmax = MOE_TM // MOE_BLK
    grid_spec = pltpu.PrefetchScalarGridSpec(
        num_scalar_prefetch=1,
        grid=(n // MOE_TM, ne, ff // MOE_TF),
        in_specs=[
            pl.BlockSpec((MOE_TM, d), lambda t, e, j, c: (t, 0)),
            pl.BlockSpec((nr, MOE_TM), lambda t, e, j, c: (0, t)),
            pl.BlockSpec((MOE_SUB, MOE_SUB), lambda t, e, j, c: (0, 0)),
            pl.BlockSpec((None, d, MOE_TF), lambda t, e, j, c: (e, 0, j)),
            pl.BlockSpec((None, d, MOE_TF), lambda t, e, j, c: (e, 0, j)),
            pl.BlockSpec((None, MOE_TF, d), lambda t, e, j, c: (e, j, 0)),
        ],
        out_specs=pl.BlockSpec((MOE_TM, d), lambda t, e, j, c: (t, 0)),
        scratch_shapes=[
            pltpu.VMEM((nr, MOE_TM), F32),
            pltpu.VMEM((nbmax, MOE_BLK, d), BF16),
            pltpu.VMEM((nbmax, MOE_BLK, d), F32),
        ],
    )
    return pl.pallas_call(
        _moe_kernel,
        grid_spec=grid_spec,
        out_shape=jax.ShapeDtypeStruct((n, d), F32),
        compiler_params=_cparams("parallel", "arbitrary", "arbitrary"),
        name="moe",
    )(cnt, h, cmb, tri, wg, wu, wd)


def _final_kernel(x_ref, m_ref, gate_ref, fg_ref, o_ref):
    y = x_ref[...] + gate_ref[...] * m_ref[...]
    ms = jnp.mean(y * y, axis=-1, keepdims=True)
    o_ref[...] = y * lax.rsqrt(ms + NORM_EPS) * fg_ref[...]


def _final(x, m, gate, final_g):
    b, l, d = x.shape
    nt = l // MOE_TR
    return pl.pallas_call(
        _final_kernel,
        grid=(b, nt),
        in_specs=[
            pl.BlockSpec((None, MOE_TR, d), lambda bi, i: (bi, i, 0)),
            pl.BlockSpec((MOE_TR, d), lambda bi, i: (bi * nt + i, 0)),
            pl.BlockSpec((None, 1, d), lambda bi, i: (bi, 0, 0)),
            pl.BlockSpec((1, d), lambda bi, i: (0, 0)),
        ],
        out_specs=pl.BlockSpec((None, MOE_TR, d), lambda bi, i: (bi, i, 0)),
        out_shape=jax.ShapeDtypeStruct((b, l, d), F32),
        compiler_params=_cparams("parallel", "parallel"),
        name="moe_final",
    )(x, m, gate, final_g)


def _shift_masks(mu, ctx_len, seq_len):
    slab = mu.shape[0]
    nct = ctx_len // TM
    tt = ctx_len + seq_len
    t = jnp.arange(tt)
    is_ctx = t < ctx_len
    tl = t - ctx_len
    col = tl % GRID_W
    rows = seq_len // GRID_W
    grow = tl // GRID_W
    left = jnp.where(is_ctx, t != 0, col != 0)
    right = jnp.where(is_ctx, t != ctx_len - 1, col != GRID_W - 1)
    upv = jnp.where(is_ctx, False, grow != 0)
    dnv = jnp.where(is_ctx, False, grow != rows - 1)
    zero = jnp.zeros_like(left)
    rowmask = jnp.stack([left, right, upv, dnv, zero, zero, zero, zero], axis=-1).astype(F32)
    rowmask = rowmask.reshape(tt // TM, TM, 8)
    c = jnp.arange(slab)
    z = jnp.zeros_like(mu)
    lat = jnp.stack([mu * (c % 4 == 0), mu * (c % 4 == 1), mu * (c % 4 == 2), mu * (c % 4 == 3),
                     1.0 - mu, z, z, z])
    ctx = jnp.stack([mu * (c % 2 == 0), mu * (c % 2 == 1), z, z, 1.0 - mu, z, z, z])
    return rowmask, jnp.stack([ctx, lat]).astype(F32)


def _pad_rows(wt):
    z = jnp.zeros_like(wt[0])
    wp = jnp.stack([jnp.concatenate([wt[0], z], axis=0), jnp.concatenate([z, wt[1]], axis=0)])
    hi = wp.astype(BF16)
    lo = (wp - hi.astype(F32)).astype(BF16)
    return jnp.stack([hi, lo], axis=1)


def kernel(x, c, ctx, c_ctx, ada_w, ada_b, norm1_g, norm2_g, w_in, w_out, shift_mu, rwkv_w0, rwkv_w_up, rwkv_a0, rwkv_a_up, rwkv_g_up, rwkv_k_k, rwkv_k_a, rwkv_r_k, rwkv_ln_w, rwkv_ln_b, s5_lam_re, s5_lam_im, s5_log_dt, s5_b_re, s5_b_im, s5_c_re, s5_c_im, s5_d, s5_glu_w, s5_glu_b, ffn_w_gate, ffn_w_up, ffn_w_down, moe_router, moe_w_gate, moe_w_up, moe_w_down, final_g):
    b, l, d = x.shape
    ctx_len = ctx.shape[1]
    depth = ada_w.shape[0]
    slab_w = shift_mu.shape[1]
    rw_w = rwkv_k_k.shape[1]
    assert ctx_len == TM and l % TM == 0 and b + 1 <= 8
    assert rw_w % (WKV_HEADS * HEAD) == 0 and depth == 2
    nct = ctx_len // TM
    nctc = ctx_len // WKV_CHUNK
    nctc16 = ctx_len // S5_CHUNK

    act = jnp.zeros((8, d), F32).at[:b].set(c).at[b].set(c_ctx)
    mods = _ada_mod(act, ada_w, ada_b).reshape(depth, 8, 6, d)

    def mod(i, k):
        cm = jnp.broadcast_to(mods[i, b, k][None, :], (b, d))
        return jnp.stack([cm, mods[i, :b, k]], axis=1)[:, :, None, :]

    hi = lax.broadcasted_iota(jnp.int32, (rw_w, rw_w), 0) // HEAD
    hj = lax.broadcasted_iota(jnp.int32, (rw_w, rw_w), 1) // HEAD
    bd = (hi == hj).astype(BF16)

    ti = lax.broadcasted_iota(jnp.int32, (TM, TM), 0)
    si = lax.broadcasted_iota(jnp.int32, (TM, TM), 1)
    same_chunk = (ti // WKV_CHUNK) == (si // WKV_CHUNK)
    tri = jnp.stack([same_chunk & (si <= ti), same_chunk & (si >= ti)]).astype(BF16)

    xcat = jnp.concatenate([ctx, x], axis=1)
    out = None
    for i in range(depth):
        last = i == depth - 1
        p, u8 = _inproj(xcat, norm1_g[i][None], mod(i, 0), mod(i, 1), w_in[i].astype(BF16), nct,
                        s5_d.shape[1])
        rowmask, lanec = _shift_masks(shift_mu[i], ctx_len, l)
        v, at, rt, bg, kg, ee, g, bo = _rwkv_prep(
            p, rowmask, lanec, rwkv_k_k[i][None], rwkv_k_a[i][None], rwkv_r_k[i].reshape(1, -1),
            rwkv_w0[i], rwkv_a0[i], _pad_rows(rwkv_w_up[i]), _pad_rows(rwkv_a_up[i]),
            _pad_rows(rwkv_g_up[i]), bd, tri, nct, slab_w)
        yf, yr = _wkv_scan(v, at, rt, bg, kg, ee, nctc)
        s5w = _s5_weights(s5_lam_re[i], s5_lam_im[i], s5_log_dt[i], s5_b_re[i], s5_b_im[i],
                          s5_c_re[i], s5_c_im[i])
        ys = _s5_mix(u8, s5w, b, nctc16)
        t0 = nct if last else 0
        xm = _mixout(xcat, yf, yr, g, bo, ys, p, rwkv_ln_w[i].reshape(1, -1), rwkv_ln_b[i].reshape(1, -1),
                     bd, s5_d[i][None], s5_glu_w[i].astype(BF16), s5_glu_b[i][None],
                     w_out[i].astype(BF16), mod(i, 2), nct, t0)
        if not last:
            j = i // 2
            xcat = _ffn(xm, norm2_g[i][None], mod(i, 3), mod(i, 4), mod(i, 5),
                        ffn_w_gate[j].astype(BF16), ffn_w_up[j].astype(BF16),
                        ffn_w_down[j].astype(BF16), nct)
        else:
            j = i // 2
            ne = moe_router.shape[2]
            nr = -(-ne // 8) * 8
            router_t = jnp.zeros((nr, d), F32).at[:ne].set(moe_router[j].T)
            lat = lambda k: mods[i, :b, k][:, None, :]
            h, cmb, cnt = _route(xm, norm2_g[i][None], lat(3), lat(4), router_t, ne)
            cnt = cnt[:, :ne, 0].reshape(-1, MOE_TM // MOE_TR, ne).sum(axis=1)
            ui = lax.broadcasted_iota(jnp.int32, (MOE_SUB, MOE_SUB), 0)
            uj = lax.broadcasted_iota(jnp.int32, (MOE_SUB, MOE_SUB), 1)
            moe = _moe(h, cmb, cnt, (ui < uj).astype(BF16), moe_w_gate[j].astype(BF16),
                       moe_w_up[j].astype(BF16), moe_w_down[j].astype(BF16))
            out = _final(xm, moe, lat(5), final_g[None])
    return out
```

```python
import functools
import math

import jax
import jax.numpy as jnp
from jax import lax
from jax.experimental import pallas as pl
from jax.experimental.pallas import tpu as pltpu

F32 = jnp.float32
BF16 = jnp.bfloat16
HIGHEST = lax.Precision.HIGHEST

GRID_W = 64
HEAD = 64
DECAY_RANK = 64
ICL_RANK = 64
GATE_RANK = 128
S5_GROUP = 16
S5_STATE = 64
NORM_EPS = 1e-6
GN_EPS = 64e-5
L2_EPS = 1e-12
LAM_RE_MAX = -1e-4
TOP_K = 2

TM = 256
WKV_CHUNK = 64
WKV_HEADS = 4
WKV_STEP = 2
S5_CHUNK = 16
S5_OCT = 128
VMEM_LIMIT = 56 * 1024 * 1024


def _cparams(*sem):
    return pltpu.CompilerParams(dimension_semantics=sem, vmem_limit_bytes=VMEM_LIMIT)


def _dot(a, b):
    return jnp.dot(a, b, preferred_element_type=F32)


def _dot32(a, b):
    return jnp.dot(a, b, precision=HIGHEST, preferred_element_type=F32)


def _split2(x):
    hi = x.astype(BF16)
    return hi, (x - hi.astype(F32)).astype(BF16)


def _dot_ones(x, ones_bf):
    hi, lo = _split2(x)
    return _dot(hi, ones_bf) + _dot(lo, ones_bf)


def _dot_w2(x, w2_ref):
    hi, lo = _split2(x)
    return _dot(hi, w2_ref[0]) + _dot(lo, w2_ref[0]) + _dot(hi, w2_ref[1])


def _dot_nt(a, b):
    return lax.dot_general(a, b, (((1,), (1,)), ((), ())), preferred_element_type=F32)


def _dot_tn(a, b):
    return lax.dot_general(a, b, (((0,), (0,)), ((), ())), preferred_element_type=F32)


def _ada_kernel(act_ref, w_ref, b_ref, o_ref):
    a = act_ref[...]
    a = a * jax.nn.sigmoid(a)
    o_ref[...] = _dot32(a, w_ref[...]) + b_ref[...]


def _ada_mod(act, ada_w, ada_b):
    depth, d, n = ada_w.shape
    tn = 1536
    return pl.pallas_call(
        _ada_kernel,
        grid=(depth, n // tn),
        in_specs=[
            pl.BlockSpec((8, d), lambda i, j: (0, 0)),
            pl.BlockSpec((None, d, tn), lambda i, j: (i, 0, j)),
            pl.BlockSpec((None, 1, tn), lambda i, j: (i, 0, j)),
        ],
        out_specs=pl.BlockSpec((None, 8, tn), lambda i, j: (i, 0, j)),
        out_shape=jax.ShapeDtypeStruct((depth, 8, n), F32),
        compiler_params=_cparams("arbitrary", "arbitrary"),
        name="ada_mod",
    )(act, ada_w, ada_b.reshape(depth, 1, n))


def _norm_mod(x, g, shift, scale):
    ms = jnp.mean(x * x, axis=-1, keepdims=True)
    y = x * lax.rsqrt(ms + NORM_EPS) * g
    return y * (1.0 + scale) + shift


def _inproj_kernel(sw, x_ref, g_ref, sh_ref, sc_ref, w_ref, o_ref, u8_ref, us_ref):
    h = _norm_mod(x_ref[...], g_ref[...], sh_ref[...], sc_ref[...])
    p = _dot(h.astype(BF16), w_ref[...])
    o_ref[...] = p.astype(BF16)
    base = p.shape[1] - sw
    cpt = TM // S5_CHUNK
    for o8 in range(sw // S5_OCT):
        us_ref[o8] = p[:, base + o8 * S5_OCT:base + (o8 + 1) * S5_OCT]
    for o8 in range(sw // S5_OCT):
        for s in range(S5_CHUNK):
            u8_ref[o8, :, s * S5_OCT:(s + 1) * S5_OCT] = (
                us_ref[o8, pl.ds(s, cpt, stride=S5_CHUNK), :].astype(BF16))


def _inproj(xcat, g, shift, scale, w_bf, nct, sw):
    b, tt, d = xcat.shape
    n = w_bf.shape[1]
    noct = sw // S5_OCT
    cpt = TM // S5_CHUNK
    kw = S5_CHUNK * S5_OCT
    kind = lambda bi, i: (bi, jnp.where(i < nct, 0, 1), 0, 0)
    return pl.pallas_call(
        functools.partial(_inproj_kernel, sw),
        grid=(b, tt // TM),
        in_specs=[
            pl.BlockSpec((None, TM, d), lambda bi, i: (bi, i, 0)),
            pl.BlockSpec((1, d), lambda bi, i: (0, 0)),
            pl.BlockSpec((None, None, 1, d), kind),
            pl.BlockSpec((None, None, 1, d), kind),
            pl.BlockSpec((d, n), lambda bi, i: (0, 0)),
        ],
        out_specs=[pl.BlockSpec((None, TM, n), lambda bi, i: (bi, i, 0)),
                   pl.BlockSpec((noct, cpt, kw), lambda bi, i: (0, i, bi))],
        out_shape=[jax.ShapeDtypeStruct((b, tt, n), BF16),
                   jax.ShapeDtypeStruct((noct, tt // S5_CHUNK, b * kw), BF16)],
        scratch_shapes=[pltpu.VMEM((noct, TM, S5_OCT), F32)],
        compiler_params=_cparams("parallel", "parallel"),
        name="inproj",
    )(xcat, g, shift, scale, w_bf)


def _split3(x):
    hi = x.astype(BF16)
    r1 = x - hi.astype(F32)
    mid = r1.astype(BF16)
    lo = (r1 - mid.astype(F32)).astype(BF16)
    return hi, mid, lo


def _prep_kernel(p_ref, up_ref, dn_ref, rm_ref, lc_ref, kk_ref, ka_ref, rk_ref, w0_ref, a0_ref,
                 wup_ref, aup_ref, gup_ref, bd_ref, tri_ref,
                 v_o, at_o, rt_o, bg_o, kg_o, ee_o, g_o, bo_o):
    x = p_ref[...].astype(F32)
    rm = rm_ref[...]
    lc = lc_ref[...]
    prev = pltpu.roll(x, 1, 0)
    nxt = pltpu.roll(x, TM - 1, 0)
    up = jnp.concatenate([up_ref[...].astype(F32), x[: TM - GRID_W]], axis=0)
    dn = jnp.concatenate([x[GRID_W:], dn_ref[...].astype(F32)], axis=0)
    slab = (x * lc[4:5]
            + rm[:, 0:1] * (prev * lc[0:1])
            + rm[:, 1:2] * (nxt * lc[1:2])
            + rm[:, 2:3] * (up * lc[2:3])
            + rm[:, 3:4] * (dn * lc[3:4]))
    w = kk_ref.shape[1]
    r = slab[:, 0:w]
    k = slab[:, w:2 * w]
    v = slab[:, 2 * w:3 * w]
    o = 3 * w
    wd = slab[:, o:o + 2 * DECAY_RANK]
    ad = slab[:, o + 2 * DECAY_RANK:o + 2 * DECAY_RANK + 2 * ICL_RANK]
    gd = slab[:, o + 2 * DECAY_RANK + 2 * ICL_RANK:]
    bd = bd_ref[...]
    kk = k * kk_ref[...]
    nrm = jnp.sqrt(_dot_ones(kk * kk, bd))
    kk = kk / jnp.maximum(nrm, L2_EPS)
    v_o[...] = v.astype(BF16)
    twd = jnp.tanh(wd)
    sgd = jax.nn.sigmoid(gd)
    c = WKV_CHUNK
    for d in range(2):
        z = w0_ref[d:d + 1, :] + _dot_w2(twd, wup_ref.at[d])
        w_log = -jax.nn.softplus(-z) - 0.5
        lw = -jnp.exp(w_log)
        a = jax.nn.sigmoid(a0_ref[d:d + 1, :] + _dot_w2(ad, aup_ref.at[d]))
        kt = k * (1.0 + (a - 1.0) * ka_ref[...])
        g_o[d] = _dot_w2(sgd, gup_ref.at[d]).astype(BF16)
        bo_o[d] = (_dot_ones(r * kt * rk_ref[...], bd) * v).astype(BF16)
        tri = tri_ref[d]
        hi, mid, lo = _split3(lw)
        lg_in = _dot(tri, hi) + _dot(tri, mid) + _dot(tri, lo)
        e_neg = jnp.exp(-lg_in)
        at_o[d] = (-kk * jnp.exp(lg_in - lw)).astype(BF16)
        rt_o[d] = (r * jnp.exp(lg_in)).astype(BF16)
        bg_o[d] = (kk * a * e_neg).astype(BF16)
        kg_o[d] = (kt * e_neg).astype(BF16)
        for ci in range(TM // c):
            last = ci * c + (c - 1 if d == 0 else 0)
            ee_o[d, ci] = jnp.exp(lg_in[last:last + 1, :])


def _rwkv_prep(p, rowmask, lanec, k_k, k_a, r_k, w0, a0, wup, aup, gup, bd, tri, nct, slab_w):
    b, tt, _ = p.shape
    w = k_k.shape[1]
    nt = tt // TM
    cpt = TM // WKV_CHUNK
    hb = TM // GRID_W
    nhb = tt // GRID_W
    full = lambda *s: pl.BlockSpec(s, lambda bi, i: (0,) * len(s))
    tok = pl.BlockSpec((None, TM, w), lambda bi, i: (bi, i, 0))
    tok2 = pl.BlockSpec((2, None, TM, w), lambda bi, i: (0, bi, i, 0))
    bf1 = jax.ShapeDtypeStruct((b, tt, w), BF16)
    bf2 = jax.ShapeDtypeStruct((2, b, tt, w), BF16)
    sh2 = jax.ShapeDtypeStruct((2, b, tt, w), F32)
    return pl.pallas_call(
        _prep_kernel,
        grid=(b, nt),
        in_specs=[
            pl.BlockSpec((None, TM, slab_w), lambda bi, i: (bi, i, 0)),
            pl.BlockSpec((None, GRID_W, slab_w), lambda bi, i: (bi, jnp.maximum(i * hb - 1, 0), 0)),
            pl.BlockSpec((None, GRID_W, slab_w),
                         lambda bi, i: (bi, jnp.minimum(i * hb + hb, nhb - 1), 0)),
            pl.BlockSpec((None, TM, 8), lambda bi, i: (i, 0, 0)),
            pl.BlockSpec((None, 8, slab_w), lambda bi, i: (jnp.where(i < nct, 0, 1), 0, 0)),
            full(1, w), full(1, w), full(1, w), full(2, w), full(2, w),
            full(2, 2, 2 * DECAY_RANK, w), full(2, 2, 2 * ICL_RANK, w), full(2, 2, 2 * GATE_RANK, w),
            full(w, w), full(2, TM, TM),
        ],
        out_specs=[tok, tok2, tok2, tok2, tok2,
                   pl.BlockSpec((2, None, cpt, 1, w), lambda bi, i: (0, bi, i, 0, 0)),
                   tok2, tok2],
        out_shape=[bf1, bf2, bf2, bf2, bf2,
                   jax.ShapeDtypeStruct((2, b, tt // WKV_CHUNK, 1, w), F32), bf2, bf2],
        compiler_params=_cparams("parallel", "parallel"),
        name="rwkv_prep",
    )(p, p, p, rowmask, lanec, k_k, k_a, r_k, w0, a0, wup, aup, gup, bd, tri)


def _wkv_kernel(nb, ngrp, v_f, v_r, at_f, at_r, rt_f, rt_r, bg_f, bg_r, kg_f, kg_r, ee_f, ee_r,
                y_f, y_r, ht_ref):
    j = pl.program_id(0)
    c = WKV_CHUNK
    gw = WKV_HEADS * HEAD
    gn = WKV_HEADS * c

    @pl.when(j == 0)
    def _():
        ht_ref[...] = jnp.zeros_like(ht_ref)

    sh = int(math.log2(c))
    row = lax.broadcasted_iota(jnp.int32, (gn, gw), 0)
    col = lax.broadcasted_iota(jnp.int32, (gn, gw), 1)
    same = (row >> sh) == (col >> sh)
    tf = lax.broadcasted_iota(jnp.int32, (c, gn), 0)
    sf = lax.broadcasted_iota(jnp.int32, (c, gn), 1) & (c - 1)
    eye = (tf == sf).astype(F32)

    def stack(x):
        xb = jnp.concatenate([x.astype(BF16)] * WKV_HEADS, axis=0)
        return jnp.where(same, xb, jnp.zeros_like(xb))

    dirs = ((v_f, at_f, rt_f, bg_f, kg_f, ee_f, y_f, sf < tf, sf <= tf),
            (v_r, at_r, rt_r, bg_r, kg_r, ee_r, y_r, sf > tf, sf >= tf))
    nsc = v_f.shape[1] // c
    chains = [(d, bi, q, sc) for d in range(2) for bi in range(nb) for q in range(ngrp)
              for sc in range(nsc)]
    sl = lambda q: slice(q * gw, (q + 1) * gw)
    rows = lambda sc: slice(sc * c, (sc + 1) * c)
    rd = lambda k: [dirs[d][k][bi, rows(sc), sl(q)] for d, bi, q, sc in chains]
    cat0 = lambda xs: jnp.concatenate(xs, axis=0)
    v, at, rt, bg, kg = rd(0), rd(1), rd(2), rd(3), rd(4)
    ee = [dirs[d][5][bi, sc, :, sl(q)] for d, bi, q, sc in chains]
    before = [dirs[ch[0]][7] for ch in chains]
    incl = [dirs[ch[0]][8] for ch in chains]
    n_ch = range(len(chains))

    v_bd = [stack(x) for x in v]
    at_bd = [stack(x) for x in at]
    bk_bd = [cat0([stack(bg[i]), stack(kg[i])]) for i in n_ch]
    a = [_dot_nt(cat0([at[i], rt[i]]), bk_bd[i]) for i in n_ch]
    n = [jnp.where(before[i], a[i][0:c, 0:gn], 0.0) for i in n_ch]
    a_kk = [cat0([jnp.where(before[i], a[i][0:c, gn:], 0.0),
                  jnp.where(incl[i], a[i][c:, gn:], 0.0)]).astype(BF16) for i in n_ch]
    a_rb = [jnp.where(incl[i], a[i][c:, 0:gn], 0.0).astype(BF16) for i in n_ch]
    tm = [eye + x for x in n]
    pw = [_dot(x.astype(BF16), stack(x)) for x in n]
    for lvl in range(1, sh):
        pw_bd = [stack(x) for x in pw]
        if lvl < sh - 1:
            tp = [_dot(cat0([tm[i].astype(BF16), pw[i].astype(BF16)]), pw_bd[i]) for i in n_ch]
            tm = [tm[i] + tp[i][0:c] for i in n_ch]
            pw = [tp[i][c:] for i in n_ch]
        else:
            tm = [tm[i] + _dot(tm[i].astype(BF16), pw_bd[i]) for i in n_ch]
    tm_b = [x.astype(BF16) for x in tm]
    atp = [_dot(tm_b[i], at_bd[i]) for i in n_ch]
    av = [_dot(a_kk[i], v_bd[i]) for i in n_ch]
    wv = [_dot(tm_b[i], stack(av[i][0:c])) for i in n_ch]
    wv_bd = [stack(x) for x in wv]
    atp_bd = [stack(x) for x in atp]
    ar = [_dot(a_rb[i], jnp.concatenate([wv_bd[i], atp_bd[i]], axis=1)) for i in n_ch]
    y0 = [ar[i][:, 0:gw] + av[i][c:] for i in n_ch]
    rtp = [(ar[i][:, gw:] + rt[i].astype(F32)).astype(BF16) for i in n_ch]
    bge_bd = [stack(bg[i].astype(F32) * ee[i]) for i in n_ch]
    kge_bd = [stack(kg[i].astype(F32) * ee[i]) for i in n_ch]
    g = [_dot_tn(bge_bd[i], atp_bd[i]).astype(BF16) for i in n_ch]
    hloc_t = [_dot_tn(cat0([wv_bd[i], v_bd[i]]), cat0([bge_bd[i], kge_bd[i]])) for i in n_ch]
    index = {ch: i for i, ch in enumerate(chains)}
    for d in range(2):
        order = range(nsc) if d == 0 else range(nsc - 1, -1, -1)
        for bi in range(nb):
            for q in range(ngrp):
                ht = ht_ref[d, bi, q]
                for sc in order:
                    i = index[(d, bi, q, sc)]
                    ht_b = ht.astype(BF16)
                    dirs[d][6][bi, rows(sc), sl(q)] = y0[i] + _dot_nt(rtp[i], ht_b)
                    ht = ht * ee[i] + _dot_nt(ht_b, g[i]) + hloc_t[i]
                ht_ref[d, bi, q] = ht


def _wkv_scan(v, at, rt, bg, kg, ee, nctc):
    b, tt, w = v.shape
    rows = WKV_STEP * WKV_CHUNK
    assert nctc % WKV_STEP == 0 and (tt // WKV_CHUNK) % WKV_STEP == 0
    ntot = tt // rows
    nctb = nctc // WKV_STEP
    ngrp = w // (WKV_HEADS * HEAD)
    fwd = lambda j: j
    rev = lambda j: jnp.where(j < nctb, nctb - 1 - j, ntot - 1 + nctb - j)
    tok = lambda cm: pl.BlockSpec((b, rows, w), lambda j: (0, cm(j), 0))
    tok2 = lambda d, cm: pl.BlockSpec((None, b, rows, w), lambda j: (d, 0, cm(j), 0))
    eesp = lambda d, cm: pl.BlockSpec((None, b, WKV_STEP, 1, w), lambda j: (d, 0, cm(j), 0, 0))
    pair = lambda f: [f(0, fwd), f(1, rev)]
    ysh = jax.ShapeDtypeStruct((b, tt, w), F32)
    return pl.pallas_call(
        functools.partial(_wkv_kernel, b, ngrp),
        grid=(ntot,),
        in_specs=[tok(fwd), tok(rev)] + pair(tok2) + pair(tok2) + pair(tok2) + pair(tok2) + pair(eesp),
        out_specs=[tok(fwd), tok(rev)],
        out_shape=[ysh, ysh],
        scratch_shapes=[pltpu.VMEM((2, b, ngrp, WKV_HEADS * HEAD, WKV_HEADS * HEAD), F32)],
        compiler_params=_cparams("arbitrary"),
        name="wkv_scan",
    )(v, v, at, at, rt, rt, bg, bg, kg, kg, ee, ee)


def _s5_weights(lam_re, lam_im, log_dt, b_re, b_im, c_re, c_im):
    tc = S5_CHUNK
    lr = jnp.minimum(lam_re.astype(F32), LAM_RE_MAX)
    li = lam_im.astype(F32)
    dt = jnp.exp(log_dt.astype(F32))[..., None]
    mag = jnp.exp(lr * dt)
    ar = mag * jnp.cos(li * dt)
    ai = mag * jnp.sin(li * dt)
    den = lr * lr + li * li
    xr = ar - 1.0
    cr = (xr * lr + ai * li) / den
    ci = (ai * lr - xr * li) / den
    br = cr[..., None] * b_re - ci[..., None] * b_im
    bi = cr[..., None] * b_im + ci[..., None] * b_re
    pr, pi = [jnp.ones_like(ar)], [jnp.zeros_like(ar)]
    for _ in range(tc):
        pr_n = pr[-1] * ar - pi[-1] * ai
        pi_n = pr[-1] * ai + pi[-1] * ar
        pr.append(pr_n)
        pi.append(pi_n)
    pr = jnp.stack(pr)
    pi = jnp.stack(pi)
    lbr = pr[..., None] * br - pi[..., None] * bi
    lbi = pr[..., None] * bi + pi[..., None] * br
    clr = c_re * pr[:, :, :, None, :] - c_im * pi[:, :, :, None, :]
    cli = c_re * pi[:, :, :, None, :] + c_im * pr[:, :, :, None, :]
    lbr_t = jnp.swapaxes(lbr, -1, -2)
    lbi_t = jnp.swapaxes(lbi, -1, -2)
    kern_t = jnp.sum(lbr_t[..., :, None, :] * c_re[None, :, :, None, :, :]
                     - lbi_t[..., :, None, :] * c_im[None, :, :, None, :, :], axis=-1)
    g = ar.shape[1]
    og = S5_OCT // S5_GROUP
    noct = g // og
    eye = jnp.eye(og, dtype=F32)

    def bdiag(x):
        nt, _, _, a, n = x.shape
        x = x.reshape(nt, 2, noct, og, a, n)
        y = x[:, :, :, :, :, None, :] * eye[None, None, None, :, None, :, None]
        return jnp.transpose(y, (1, 2, 0, 3, 4, 5, 6)).reshape(2, noct, nt, og * a, og * n).astype(BF16)

    kbd = bdiag(kern_t)

    def pair_block(d, lp):
        kd = kbd[d]
        zero = jnp.zeros_like(kd[:, 0])
        k = lambda tau: kd[:, tau] if tau >= 0 else zero
        if d == 0:
            rows = [[k(2 * lp), k(2 * lp + 1)], [k(2 * lp - 1), k(2 * lp)]]
        else:
            rows = [[k(2 * lp), k(2 * lp - 1)], [k(2 * lp + 1), k(2 * lp)]]
        return jnp.concatenate([jnp.concatenate(r, axis=-1) for r in rows], axis=-2)

    wpair = jnp.stack([jnp.stack([pair_block(d, lp) for lp in range(tc // 2)], axis=1)
                       for d in range(2)])
    lbc = jnp.concatenate([lbr_t, lbi_t], axis=-1)
    clc = jnp.concatenate([clr, -cli], axis=-1)
    pout8, qin8 = _s5_expand(lbc, clc)
    la = jnp.concatenate([pr[tc], pr[tc]], axis=-1)
    lb = jnp.concatenate([-pi[tc], pi[tc]], axis=-1)
    return wpair, pout8, qin8, la, lb


def _s5_expand_kernel(p_ref, q_ref, po_ref, qo_ref):
    nt, _, og, a, n = p_ref.shape
    tc = nt - 1
    po_ref[...] = jnp.zeros_like(po_ref)
    qo_ref[...] = jnp.zeros_like(qo_ref)
    for d in range(2):
        for s in range(tc):
            lag_out = tc - 1 - s if d == 0 else s
            lag_in = s + 1 if d == 0 else tc - s
            for gi in range(og):
                rows, lanes = slice(gi * a, (gi + 1) * a), slice(gi * n, (gi + 1) * n)
                po_ref[d, s, rows, lanes] = p_ref[lag_out, d, gi].astype(BF16)
                qo_ref[d, s, rows, lanes] = q_ref[lag_in, d, gi].astype(BF16)


def _s5_expand(lbc, clc):
    nt, _, g, a, n = lbc.shape
    tc = nt - 1
    og = S5_OCT // S5_GROUP
    noct = g // og
    isp = pl.BlockSpec((nt, 2, og, a, n), lambda o: (0, 0, o, 0, 0))
    osp = pl.BlockSpec((2, None, tc, og * a, og * n), lambda o: (0, o, 0, 0, 0))
    osh = jax.ShapeDtypeStruct((2, noct, tc, og * a, og * n), BF16)
    po, qo = pl.pallas_call(
        _s5_expand_kernel,
        grid=(noct,),
        in_specs=[isp, isp],
        out_specs=[osp, osp],
        out_shape=[osh, osh],
        compiler_params=_cparams("parallel"),
        name="s5_expand",
    )(lbc, clc)
    return po.reshape(2, noct, tc * og * a, og * n), qo.reshape(2, noct, tc * og * a, og * n)


def _s5_local_kernel(u_ref, p_ref, e_ref):
    e_ref[...] = _dot(u_ref[...], p_ref[...])


def _s5_local(u8, pout8, nb):
    noct, nch, _ = u8.shape
    kw, n = pout8.shape[2:]
    return pl.pallas_call(
        _s5_local_kernel,
        grid=(noct, nb, 2),
        in_specs=[
            pl.BlockSpec((None, nch, kw), lambda o, b, d: (o, 0, b)),
            pl.BlockSpec((None, None, kw, n), lambda o, b, d: (d, o, 0, 0)),
        ],
        out_specs=pl.BlockSpec((None, nch, n), lambda o, b, d: (d, 0, o * nb + b)),
        out_shape=jax.ShapeDtypeStruct((2, nch, noct * nb * n), F32),
        compiler_params=_cparams("parallel", "parallel", "parallel"),
        name="s5_local",
    )(u8, pout8)


S5_STATE_ROWS = 8
S5_STATE_LANES = 512


def _s5_state_kernel(nctc, ntot, e_ref, la_ref, lb_ref, x_ref, es_ref):
    d = pl.program_id(0)
    la = la_ref[...]
    lb = lb_ref[...]
    nr, wl = la.shape

    def swap(t):
        lane = lax.broadcasted_iota(jnp.int32, t.shape, 1)
        first_half = (lane & (2 * S5_STATE - 1)) < S5_STATE
        return jnp.where(first_half, pltpu.roll(t, wl - S5_STATE, 1), pltpu.roll(t, S5_STATE, 1))

    es_ref[...] = swap(e_ref[...].reshape(ntot * nr, wl)).reshape(ntot, nr, wl)
    lbs = swap(lb)

    def body(j, carry):
        x, xs = carry
        rev_idx = jnp.where(j < nctc, nctc - 1 - j, ntot - 1 + nctc - j)
        c = jnp.where(d == 0, j, rev_idx)
        x_ref[c] = x
        return la * x + lb * xs + e_ref[c], la * xs + lbs * x + es_ref[c]

    zero = jnp.zeros(la.shape, F32)
    lax.fori_loop(0, ntot, body, (zero, zero))


def _s5_state(e, la, lb, nctc, ntot):
    _, nch, nr, lanes = e.shape
    wl = S5_STATE_LANES
    blk = pl.BlockSpec((None, nch, nr, wl), lambda d, i: (d, 0, 0, i))
    cf = pl.BlockSpec((None, nr, wl), lambda d, i: (d, 0, i))
    return pl.pallas_call(
        functools.partial(_s5_state_kernel, nctc, ntot),
        grid=(2, lanes // wl),
        in_specs=[blk, cf, cf],
        out_specs=blk,
        out_shape=jax.ShapeDtypeStruct(e.shape, F32),
        scratch_shapes=[pltpu.VMEM((nch, nr, wl), F32)],
        compiler_params=_cparams("parallel", "parallel"),
        name="s5_state",
    )(e, la, lb)


def _s5_out_kernel(rev, u_ref, w_ref, x_ref, q_ref, y_ref):
    pw = w_ref.shape[1]
    npair = w_ref.shape[0]
    x = x_ref[...].astype(BF16)
    for tp in range(npair):
        acc = _dot_nt(x, q_ref[tp * pw:(tp + 1) * pw, :])
        for lp in range(npair - tp if rev else tp + 1):
            sp = tp + lp if rev else tp - lp
            acc = acc + _dot(u_ref[:, sp * pw:(sp + 1) * pw], w_ref[lp])
        y_ref[:, tp * pw:(tp + 1) * pw] = acc


def _s5_out(u8, wpair, xin, qin8, nb, d):
    noct, nch, _ = u8.shape
    npair, pw = wpair.shape[2:4]
    kw = npair * pw
    n2 = qin8.shape[3]
    return pl.pallas_call(
        functools.partial(_s5_out_kernel, d == 1),
        grid=(noct, nb),
        in_specs=[
            pl.BlockSpec((None, nch, kw), lambda o, b: (o, 0, b)),
            pl.BlockSpec((None, None, npair, pw, pw), lambda o, b: (d, o, 0, 0, 0)),
            pl.BlockSpec((None, nch, n2), lambda o, b: (d, 0, o * nb + b)),
            pl.BlockSpec((None, None, kw, n2), lambda o, b: (d, o, 0, 0)),
        ],
        out_specs=pl.BlockSpec((None, nch, kw), lambda o, b: (o, 0, b)),
        out_shape=jax.ShapeDtypeStruct((noct, nch, nb * kw), F32),
        compiler_params=_cparams("parallel", "parallel"),
        name="s5_out",
    )(u8, wpair, xin, qin8)


def _s5_mix(u8, weights, nb, nctc16):
    wpair, pout8, qin8, la, lb = weights
    noct, nch, _ = u8.shape
    e = _s5_local(u8, pout8, nb)
    assert noct * nb == S5_STATE_ROWS
    fl = e.shape[2] // S5_STATE_ROWS
    coef = lambda t: jnp.repeat(t.reshape(2, noct, fl), nb, axis=1)
    xin = _s5_state(e.reshape(2, nch, S5_STATE_ROWS, fl), coef(la), coef(lb), nctc16, nch)
    xin = xin.reshape(e.shape)
    return _s5_out(u8, wpair, xin, qin8, nb, 0), _s5_out(u8, wpair, xin, qin8, nb, 1)


def _mixout_kernel(x_ref, yf_ref, yr_ref, g_ref, bo_ref, y8f_ref, y8r_ref, u_ref, lnw_ref, lnb_ref,
                   bd_ref, dsk_ref, gluw_ref, glub_ref, wout_ref, gate_ref, o_ref, ysn_ref):
    cpt = TM // S5_CHUNK
    for o8 in range(y8f_ref.shape[0]):
        for s in range(S5_CHUNK):
            lanes = slice(s * S5_OCT, (s + 1) * S5_OCT)
            ysn_ref[o8, pl.ds(s, cpt, stride=S5_CHUNK), :] = y8f_ref[o8, :, lanes] + y8r_ref[o8, :, lanes]
    ys = jnp.concatenate([ysn_ref[o8] for o8 in range(y8f_ref.shape[0])], axis=1)
    bd = bd_ref[...]
    inv = 1.0 / HEAD
    rw = None
    for d, y_ref in enumerate((yf_ref, yr_ref)):
        y = y_ref[...]
        mean = _dot_ones(y, bd) * inv
        yc = y - mean
        var = _dot_ones(yc * yc, bd) * inv
        yn = yc * lax.rsqrt(var + GN_EPS) * lnw_ref[...] + lnb_ref[...]
        o = (yn + bo_ref[d].astype(F32)) * g_ref[d].astype(F32)
        rw = o if rw is None else rw + o
    u = u_ref[...].astype(F32)
    ss = ys + dsk_ref[...] * u
    ss = jax.nn.gelu(ss)
    ss = ss * jax.nn.sigmoid(_dot(ss.astype(BF16), gluw_ref[...]) + glub_ref[...])
    w = rw.shape[1]
    mix = _dot(rw.astype(BF16), wout_ref[0:w, :]) + _dot(ss.astype(BF16), wout_ref[w:, :])
    o_ref[...] = x_ref[...] + gate_ref[...] * mix


def _mixout(xcat, yf, yr, g, bo, y8, p, ln_w, ln_b, bd, d_skip, glu_w, glu_b, w_out, gate, nct, t0):
    b, tt, d = xcat.shape
    w = ln_w.shape[1]
    sw = d_skip.shape[1]
    y8f, y8r = y8
    noct = y8f.shape[0]
    cpt = TM // S5_CHUNK
    kw = S5_CHUNK * S5_OCT
    ublk = (p.shape[2] - sw) // sw
    nt = tt // TM - t0
    full = lambda *s: pl.BlockSpec(s, lambda bi, i: (0,) * len(s))
    tok = pl.BlockSpec((None, TM, w), lambda bi, i: (bi, i + t0, 0))
    tok2 = pl.BlockSpec((2, None, TM, w), lambda bi, i: (0, bi, i + t0, 0))
    return pl.pallas_call(
        _mixout_kernel,
        grid=(b, nt),
        in_specs=[
            pl.BlockSpec((None, TM, d), lambda bi, i: (bi, i + t0, 0)),
            tok, tok, tok2, tok2,
            pl.BlockSpec((noct, cpt, kw), lambda bi, i: (0, i + t0, bi)),
            pl.BlockSpec((noct, cpt, kw), lambda bi, i: (0, i + t0, bi)),
            pl.BlockSpec((None, TM, sw), lambda bi, i: (bi, i + t0, ublk)),
            full(1, w), full(1, w), full(w, w), full(1, sw), full(sw, sw), full(1, sw),
            full(w + sw, d),
            pl.BlockSpec((None, None, 1, d), lambda bi, i: (bi, jnp.where(i + t0 < nct, 0, 1), 0, 0)),
        ],
        out_specs=pl.BlockSpec((None, TM, d), lambda bi, i: (bi, i, 0)),
        out_shape=jax.ShapeDtypeStruct((b, nt * TM, d), F32),
        scratch_shapes=[pltpu.VMEM((noct, TM, S5_OCT), F32)],
        compiler_params=_cparams("parallel", "parallel"),
        name="mix_out",
    )(xcat, yf, yr, g, bo, y8f, y8r, p, ln_w, ln_b, bd, d_skip, glu_w, glu_b, w_out, gate)


def _ffn_kernel(x_ref, g_ref, sh_ref, sc_ref, gate_ref, wg_ref, wu_ref, wd_ref, o_ref):
    x = x_ref[...]
    h = _norm_mod(x, g_ref[...], sh_ref[...], sc_ref[...]).astype(BF16)
    a = _dot(h, wg_ref[...])
    a = a * jax.nn.sigmoid(a) * _dot(h, wu_ref[...])
    o_ref[...] = x + gate_ref[...] * _dot(a.astype(BF16), wd_ref[...])


def _ffn(xcat, g, shift, scale, gate, wg, wu, wd, nct):
    b, tt, d = xcat.shape
    ff = wg.shape[1]
    kind = lambda bi, i: (bi, jnp.where(i < nct, 0, 1), 0, 0)
    mod = pl.BlockSpec((None, None, 1, d), kind)
    return pl.pallas_call(
        _ffn_kernel,
        grid=(b, tt // TM),
        in_specs=[
            pl.BlockSpec((None, TM, d), lambda bi, i: (bi, i, 0)),
            pl.BlockSpec((1, d), lambda bi, i: (0, 0)),
            mod, mod, mod,
            pl.BlockSpec((d, ff), lambda bi, i: (0, 0)),
            pl.BlockSpec((d, ff), lambda bi, i: (0, 0)),
            pl.BlockSpec((ff, d), lambda bi, i: (0, 0)),
        ],
        out_specs=pl.BlockSpec((None, TM, d), lambda bi, i: (bi, i, 0)),
        out_shape=jax.ShapeDtypeStruct((b, tt, d), F32),
        compiler_params=_cparams("parallel", "parallel"),
        name="ffn",
    )(xcat, g, shift, scale, gate, wg, wu, wd)


MOE_TR = 1024
MOE_TM = 2048
MOE_TF = 896
MOE_BLK = 256
MOE_SUB = 256


def _route_kernel(ne, x_ref, g_ref, sh_ref, sc_ref, rt_ref, h_o, cmb_o, cnt_o):
    h = _norm_mod(x_ref[...], g_ref[...], sh_ref[...], sc_ref[...])
    h_o[...] = h.astype(BF16)
    logits = lax.dot_general(rt_ref[...], h, (((1,), (1,)), ((), ())), precision=HIGHEST,
                             preferred_element_type=F32)
    sub = lax.broadcasted_iota(jnp.int32, logits.shape, 0).astype(F32)
    none = float(logits.shape[0])
    logits = jnp.where(sub < ne, logits, -jnp.inf)
    m1 = jnp.max(logits, axis=0, keepdims=True)
    i1 = jnp.min(jnp.where(logits == m1, sub, none), axis=0, keepdims=True)
    rest = jnp.where(sub == i1, -jnp.inf, logits)
    m2 = jnp.max(rest, axis=0, keepdims=True)
    i2 = jnp.min(jnp.where(rest == m2, sub, none), axis=0, keepdims=True)
    e2 = jnp.exp(m2 - m1)
    p1 = 1.0 / (1.0 + e2)
    p2 = e2 / (1.0 + e2)
    cmb = jnp.where(sub == i1, p1, 0.0) + jnp.where(sub == i2, p2, 0.0)
    cmb_o[...] = cmb
    cnt = jnp.sum((cmb > 0.0).astype(F32), axis=1, keepdims=True)
    cnt_o[...] = jnp.broadcast_to(cnt, cnt_o.shape).astype(jnp.int32)


def _route(x, g, shift, scale, router_t, ne):
    b, l, d = x.shape
    nr = router_t.shape[0]
    nt = l // MOE_TR
    mod = pl.BlockSpec((None, 1, d), lambda bi, i: (bi, 0, 0))
    return pl.pallas_call(
        functools.partial(_route_kernel, ne),
        grid=(b, nt),
        in_specs=[
            pl.BlockSpec((None, MOE_TR, d), lambda bi, i: (bi, i, 0)),
            pl.BlockSpec((1, d), lambda bi, i: (0, 0)),
            mod, mod,
            pl.BlockSpec((nr, d), lambda bi, i: (0, 0)),
        ],
        out_specs=[
            pl.BlockSpec((MOE_TR, d), lambda bi, i: (bi * nt + i, 0)),
            pl.BlockSpec((nr, MOE_TR), lambda bi, i: (0, bi * nt + i)),
            pl.BlockSpec((None, nr, 128), lambda bi, i: (bi * nt + i, 0, 0)),
        ],
        out_shape=[
            jax.ShapeDtypeStruct((b * l, d), BF16),
            jax.ShapeDtypeStruct((nr, b * l), F32),
            jax.ShapeDtypeStruct((b * nt, nr, 128), jnp.int32),
        ],
        compiler_params=_cparams("parallel", "parallel"),
        name="moe_route",
    )(x, g, shift, scale, router_t)


def _moe_kernel(cnt_ref, h_ref, cmb_ref, tri_ref, wg_ref, wu_ref, wd_ref, o_ref,
                pos_ref, hg_ref, ya_ref):
    t = pl.program_id(0)
    e = pl.program_id(1)
    j = pl.program_id(2)
    tm = h_ref.shape[0]
    nblk = jnp.right_shift(cnt_ref[t, e] + (MOE_BLK - 1), int(math.log2(MOE_BLK)))

    @pl.when((e == 0) & (j == 0))
    def _():
        o_ref[...] = jnp.zeros_like(o_ref)
        asg = (cmb_ref[...] > 0.0).astype(BF16)
        off = jnp.zeros((asg.shape[0], 1), F32)
        for k in range(tm // MOE_SUB):
            blk = asg[:, k * MOE_SUB:(k + 1) * MOE_SUB]
            pos_ref[:, k * MOE_SUB:(k + 1) * MOE_SUB] = _dot(blk, tri_ref[...]) + off
            off = off + jnp.sum(blk.astype(F32), axis=1, keepdims=True)

    sel = lax.broadcasted_iota(jnp.int32, pos_ref.shape, 0) == e
    posrow = jnp.sum(jnp.where(sel, pos_ref[...], 0.0), axis=0, keepdims=True)
    cwrow = jnp.sum(jnp.where(sel, cmb_ref[...], 0.0), axis=0, keepdims=True)
    rowi = lax.broadcasted_iota(jnp.int32, (MOE_BLK, tm), 0).astype(F32)

    def onehot(b):
        slot = rowi + (b * MOE_BLK).astype(F32)
        return (posrow == slot) & (cwrow > 0.0)

    @pl.when(j == 0)
    def _():
        def gather(b, carry):
            sel_b = jnp.where(onehot(b), 1.0, 0.0).astype(BF16)
            hg_ref[b] = _dot(sel_b, h_ref[...]).astype(BF16)
            ya_ref[b] = jnp.zeros(ya_ref.shape[1:], F32)
            return carry
        lax.fori_loop(0, nblk, gather, 0)

    d_model = hg_ref.shape[2]

    def swiglu(hb):
        a = _dot(hb, wg_ref[...])
        a = a * jax.nn.sigmoid(a) * _dot(hb, wu_ref[...])
        return _dot(a.astype(BF16), wd_ref[...])

    def ffn_pair(i, carry):
        rows = pl.ds(2 * i, 2)
        y = swiglu(hg_ref[rows].reshape(2 * MOE_BLK, d_model))
        ya_ref[rows] += y.reshape(2, MOE_BLK, d_model)
        return carry
    npair = jnp.right_shift(nblk, 1)
    lax.fori_loop(0, npair, ffn_pair, 0)

    @pl.when(nblk > 2 * npair)
    def _():
        ya_ref[nblk - 1] += swiglu(hg_ref[nblk - 1])

    @pl.when(j == pl.num_programs(2) - 1)
    def _():
        def scatter(b, carry):
            wsel = jnp.where(onehot(b), cwrow, 0.0).astype(BF16)
            o_ref[...] += _dot_tn(wsel, ya_ref[b].astype(BF16))
            return carry
        lax.fori_loop(0, nblk, scatter, 0)


def _moe(h, cmb, cnt, tri, wg, wu, wd):
    n, d = h.shape
    nr = cmb.shape[0]
    ne, _, ff = wg.shape
    nbmax = MOE_TM // MOE_BLK
    grid_spec = pltpu.PrefetchScalarGridSpec(
        num_scalar_prefetch=1,
        grid=(n // MOE_TM, ne, ff // MOE_TF),
        in_specs=[
            pl.BlockSpec((MOE_TM, d), lambda t, e, j, c: (t, 0)),
            pl.BlockSpec((nr, MOE_TM), lambda t, e, j, c: (0, t)),
            pl.BlockSpec((MOE_SUB, MOE_SUB), lambda t, e, j, c: (0, 0)),
            pl.BlockSpec((None, d, MOE_TF), lambda t, e, j, c: (e, 0, j)),
            pl.BlockSpec((None, d, MOE_TF), lambda t, e, j, c: (e, 0, j)),
            pl.BlockSpec((None, MOE_TF, d), lambda t, e, j, c: (e, j, 0)),
        ],
        out_specs=pl.BlockSpec((MOE_TM, d), lambda t, e, j, c: (t, 0)),
        scratch_shapes=[
            pltpu.VMEM((nr, MOE_TM), F32),
            pltpu.VMEM((nbmax, MOE_BLK, d), BF16),
            pltpu.VMEM((nbmax, MOE_BLK, d), F32),
        ],
    )
    return pl.pallas_call(
        _moe_kernel,
        grid_spec=grid_spec,
        out_shape=jax.ShapeDtypeStruct((n, d), F32),
        compiler_params=_cparams("parallel", "arbitrary", "arbitrary"),
        name="moe",
    )(cnt, h, cmb, tri, wg, wu, wd)


def _final_kernel(x_ref, m_ref, gate_ref, fg_ref, o_ref):
    y = x_ref[...] + gate_ref[...] * m_ref[...]
    ms = jnp.mean(y * y, axis=-1, keepdims=True)
    o_ref[...] = y * lax.rsqrt(ms + NORM_EPS) * fg_ref[...]


def _final(x, m, gate, final_g):
    b, l, d = x.shape
    nt = l // MOE_TR
    return pl.pallas_call(
        _final_kernel,
        grid=(b, nt),
        in_specs=[
            pl.BlockSpec((None, MOE_TR, d), lambda bi, i: (bi, i, 0)),
            pl.BlockSpec((MOE_TR, d), lambda bi, i: (bi * nt + i, 0)),
            pl.BlockSpec((None, 1, d), lambda bi, i: (bi, 0, 0)),
            pl.BlockSpec((1, d), lambda bi, i: (0, 0)),
        ],
        out_specs=pl.BlockSpec((None, MOE_TR, d), lambda bi, i: (bi, i, 0)),
        out_shape=jax.ShapeDtypeStruct((b, l, d), F32),
        compiler_params=_cparams("parallel", "parallel"),
        name="moe_final",
    )(x, m, gate, final_g)


def _shift_masks(mu, ctx_len, seq_len):
    slab = mu.shape[0]
    nct = ctx_len // TM
    tt = ctx_len + seq_len
    t = jnp.arange(tt)
    is_ctx = t < ctx_len
    tl = t - ctx_len
    col = tl % GRID_W
    rows = seq_len // GRID_W
    grow = tl // GRID_W
    left = jnp.where(is_ctx, t != 0, col != 0)
    right = jnp.where(is_ctx, t != ctx_len - 1, col != GRID_W - 1)
    upv = jnp.where(is_ctx, False, grow != 0)
    dnv = jnp.where(is_ctx, False, grow != rows - 1)
    zero = jnp.zeros_like(left)
    rowmask = jnp.stack([left, right, upv, dnv, zero, zero, zero, zero], axis=-1).astype(F32)
    rowmask = rowmask.reshape(tt // TM, TM, 8)
    c = jnp.arange(slab)
    z = jnp.zeros_like(mu)
    lat = jnp.stack([mu * (c % 4 == 0), mu * (c % 4 == 1), mu * (c % 4 == 2), mu * (c % 4 == 3),
                     1.0 - mu, z, z, z])
    ctx = jnp.stack([mu * (c % 2 == 0), mu * (c % 2 == 1), z, z, 1.0 - mu, z, z, z])
    return rowmask, jnp.stack([ctx, lat]).astype(F32)


def _pad_rows(wt):
    z = jnp.zeros_like(wt[0])
    wp = jnp.stack([jnp.concatenate([wt[0], z], axis=0), jnp.concatenate([z, wt[1]], axis=0)])
    hi = wp.astype(BF16)
    lo = (wp - hi.astype(F32)).astype(BF16)
    return jnp.stack([hi, lo], axis=1)


def kernel(x, c, ctx, c_ctx, ada_w, ada_b, norm1_g, norm2_g, w_in, w_out, shift_mu, rwkv_w0, rwkv_w_up, rwkv_a0, rwkv_a_up, rwkv_g_up, rwkv_k_k, rwkv_k_a, rwkv_r_k, rwkv_ln_w, rwkv_ln_b, s5_lam_re, s5_lam_im, s5_log_dt, s5_b_re, s5_b_im, s5_c_re, s5_c_im, s5_d, s5_glu_w, s5_glu_b, ffn_w_gate, ffn_w_up, ffn_w_down, moe_router, moe_w_gate, moe_w_up, moe_w_down, final_g):
    b, l, d = x.shape
    ctx_len = ctx.shape[1]
    depth = ada_w.shape[0]
    slab_w = shift_mu.shape[1]
    rw_w = rwkv_k_k.shape[1]
    assert ctx_len == TM and l % TM == 0 and b + 1 <= 8
    assert rw_w % (WKV_HEADS * HEAD) == 0 and depth == 2
    nct = ctx_len // TM
    nctc = ctx_len // WKV_CHUNK
    nctc16 = ctx_len // S5_CHUNK

    act = jnp.zeros((8, d), F32).at[:b].set(c).at[b].set(c_ctx)
    mods = _ada_mod(act, ada_w, ada_b).reshape(depth, 8, 6, d)

    def mod(i, k):
        cm = jnp.broadcast_to(mods[i, b, k][None, :], (b, d))
        return jnp.stack([cm, mods[i, :b, k]], axis=1)[:, :, None, :]

    hi = lax.broadcasted_iota(jnp.int32, (rw_w, rw_w), 0) // HEAD
    hj = lax.broadcasted_iota(jnp.int32, (rw_w, rw_w), 1) // HEAD
    bd = (hi == hj).astype(BF16)

    ti = lax.broadcasted_iota(jnp.int32, (TM, TM), 0)
    si = lax.broadcasted_iota(jnp.int32, (TM, TM), 1)
    same_chunk = (ti // WKV_CHUNK) == (si // WKV_CHUNK)
    tri = jnp.stack([same_chunk & (si <= ti), same_chunk & (si >= ti)]).astype(BF16)

    xcat = jnp.concatenate([ctx, x], axis=1)
    out = None
    for i in range(depth):
        last = i == depth - 1
        p, u8 = _inproj(xcat, norm1_g[i][None], mod(i, 0), mod(i, 1), w_in[i].astype(BF16), nct,
                        s5_d.shape[1])
        rowmask, lanec = _shift_masks(shift_mu[i], ctx_len, l)
        v, at, rt, bg, kg, ee, g, bo = _rwkv_prep(
            p, rowmask, lanec, rwkv_k_k[i][None], rwkv_k_a[i][None], rwkv_r_k[i].reshape(1, -1),
            rwkv_w0[i], rwkv_a0[i], _pad_rows(rwkv_w_up[i]), _pad_rows(rwkv_a_up[i]),
            _pad_rows(rwkv_g_up[i]), bd, tri, nct, slab_w)
        yf, yr = _wkv_scan(v, at, rt, bg, kg, ee, nctc)
        s5w = _s5_weights(s5_lam_re[i], s5_lam_im[i], s5_log_dt[i], s5_b_re[i], s5_b_im[i],
                          s5_c_re[i], s5_c_im[i])
        ys = _s5_mix(u8, s5w, b, nctc16)
        t0 = nct if last else 0
        xm = _mixout(xcat, yf, yr, g, bo, ys, p, rwkv_ln_w[i].reshape(1, -1), rwkv_ln_b[i].reshape(1, -1),
                     bd, s5_d[i][None], s5_glu_w[i].astype(BF16), s5_glu_b[i][None],
                     w_out[i].astype(BF16), mod(i, 2), nct, t0)
        if not last:
            j = i // 2
            xcat = _ffn(xm, norm2_g[i][None], mod(i, 3), mod(i, 4), mod(i, 5),
                        ffn_w_gate[j].astype(BF16), ffn_w_up[j].astype(BF16),
                        ffn_w_down[j].astype(BF16), nct)
        else:
            j = i // 2
            ne = moe_router.shape[2]
            nr = -(-ne // 8) * 8
            router_t = jnp.zeros((nr, d), F32).at[:ne].set(moe_router[j].T)
            lat = lambda k: mods[i, :b, k][:, None, :]
            h, cmb, cnt = _route(xm, norm2_g[i][None], lat(3), lat(4), router_t, ne)
            cnt = cnt[:, :ne, 0].reshape(-1, MOE_TM // MOE_TR, ne).sum(axis=1)
            ui = lax.broadcasted_iota(jnp.int32, (MOE_SUB, MOE_SUB), 0)
            uj = lax.broadcasted_iota(jnp.int32, (MOE_SUB, MOE_SUB), 1)
            moe = _moe(h, cmb, cnt, (ui < uj).astype(BF16), moe_w_gate[j].astype(BF16),
                       moe_w_up[j].astype(BF16), moe_w_down[j].astype(BF16))
            out = _final(xm, moe, lat(5), final_g[None])
    return out
```
